```python
import jax, jax.numpy as jnp
from jax import lax
import numpy as np

D_MODEL = 2048
BATCH = 4
SEQ = 2048
DEPTH = 1

N_META = 16
HEAD_DIM = 64
D_ATTN = D_MODEL // 2
ATTN_HEADS = D_ATTN // HEAD_DIM
ATTN_KV_HEADS = ATTN_HEADS // 4
ATTN_GROUP = ATTN_HEADS // ATTN_KV_HEADS
D_KV = ATTN_KV_HEADS * HEAD_DIM
WINDOW = 128
ATTN_BLOCK = 128
REL_BUCKETS = 32
REL_MAX_DIST = 128
SSM_D_INNER = D_MODEL - D_ATTN
SSM_HEAD_DIM = 64
SSM_HEADS = SSM_D_INNER // SSM_HEAD_DIM
SSM_GROUPS = 2
SSM_HPG = SSM_HEADS // SSM_GROUPS
SSM_STATE = 128
CONV_WIDTH = 4
SSD_CHUNK = 128
D_XBC = SSM_D_INNER + 2 * SSM_GROUPS * SSM_STATE
D_IN_PROJ = D_ATTN + 2 * D_KV + SSM_D_INNER + D_XBC + SSM_HEADS
D_MIX = D_ATTN + SSM_D_INNER
IN_SPLITS = (D_ATTN, D_ATTN + D_KV, D_ATTN + 2 * D_KV, D_ATTN + 2 * D_KV + SSM_D_INNER,
             D_ATTN + 2 * D_KV + SSM_D_INNER + D_XBC)
PEER_HEADS = 8
PEER_TOPK = 16
N_KEYS = 128
N_EXPERTS = N_KEYS * N_KEYS
PEER_HALF = 128
PEER_KEY_DIM = 2 * PEER_HALF
PEER_BLOCK = 128
EPS = 1e-6
NEG = -1e30

kernel_name = "hymba_swa_ssd_peer_block"


def rmsnorm(x, w):
    xf = x.astype(jnp.float32)
    y = xf * lax.rsqrt(jnp.mean(xf * xf, axis=-1, keepdims=True) + EPS)
    return (y * w.astype(jnp.float32)).astype(x.dtype)


def t5_bucket(dist):
    n = np.maximum(dist, 0)
    max_exact = REL_BUCKETS // 2
    large = max_exact + (np.log(np.maximum(n, 1) / max_exact) / np.log(REL_MAX_DIST / max_exact)
                         * (REL_BUCKETS - max_exact)).astype(np.int32)
    large = np.minimum(large, REL_BUCKETS - 1)
    return np.where(n < max_exact, n, large).astype(np.int32)


def band_structure(nb):
    blk = np.arange(nb)[:, None]
    kj = np.arange(ATTN_BLOCK)[None, :]
    qpos = N_META + blk[:, :, None] * ATTN_BLOCK + np.arange(ATTN_BLOCK)[None, :, None]
    kpos = np.concatenate([np.broadcast_to(np.arange(N_META)[None, :], (nb, N_META)),
                           N_META + (blk - 1) * ATTN_BLOCK + kj,
                           N_META + blk * ATTN_BLOCK + kj], axis=1)
    exists = np.concatenate([np.ones((nb, N_META), bool),
                             np.broadcast_to(blk > 0, (nb, ATTN_BLOCK)),
                             np.ones((nb, ATTN_BLOCK), bool)], axis=1)
    is_meta = np.concatenate([np.ones((nb, N_META), bool),
                              np.zeros((nb, 2 * ATTN_BLOCK), bool)], axis=1)
    dist = qpos - kpos[:, None, :]
    valid = exists[:, None, :] & (dist >= 0) & ((dist < WINDOW) | is_meta[:, None, :])
    return t5_bucket(dist), valid


def sink_softmax(logits, sink):
    s = sink.astype(jnp.float32)[..., None, None]
    m = jnp.maximum(jnp.max(logits, axis=-1, keepdims=True), s)
    e = jnp.exp(logits - m)
    return e / (jnp.sum(e, axis=-1, keepdims=True) + jnp.exp(s - m))


def swa_sink_attention(q, k, v, sinks, rel_bias):
    b, L, _ = q.shape
    S = L - N_META
    nb = S // ATTN_BLOCK
    q = q.reshape(b, L, ATTN_KV_HEADS, ATTN_GROUP, HEAD_DIM) * (HEAD_DIM ** -0.5)
    k = k.reshape(b, L, ATTN_KV_HEADS, HEAD_DIM)
    v = v.reshape(b, L, ATTN_KV_HEADS, HEAD_DIM)
    bias_tab = rel_bias.astype(jnp.float32).reshape(REL_BUCKETS, ATTN_KV_HEADS, ATTN_GROUP)
    sink = sinks.reshape(ATTN_KV_HEADS, ATTN_GROUP)
    qm, km, vm = q[:, :N_META], k[:, :N_META], v[:, :N_META]
    mi = np.arange(N_META)
    dist_m = mi[:, None] - mi[None, :]
    bias_m = jnp.transpose(bias_tab[t5_bucket(dist_m)], (2, 3, 0, 1))
    lm = jnp.einsum('bqhgd,bkhd->bhgqk', qm, km).astype(jnp.float32) + bias_m
    lm = jnp.where(dist_m >= 0, lm, NEG)
    om = jnp.einsum('bhgqk,bkhd->bqhgd', sink_softmax(lm, sink).astype(v.dtype), vm)
    qr = q[:, N_META:].reshape(b, nb, ATTN_BLOCK, ATTN_KV_HEADS, ATTN_GROUP, HEAD_DIM)
    kr = k[:, N_META:].reshape(b, nb, ATTN_BLOCK, ATTN_KV_HEADS, HEAD_DIM)
    vr = v[:, N_META:].reshape(b, nb, ATTN_BLOCK, ATTN_KV_HEADS, HEAD_DIM)

    def band_keys(t, tm):
        prev = jnp.concatenate([jnp.zeros_like(t[:, :1]), t[:, :-1]], axis=1)
        meta = jnp.broadcast_to(tm[:, None], (b, nb, N_META, ATTN_KV_HEADS, HEAD_DIM))
        return jnp.concatenate([meta, prev, t], axis=2)

    kb, vb = band_keys(kr, km), band_keys(vr, vm)
    bucket, valid = band_structure(nb)
    bias_b = jnp.transpose(bias_tab[bucket], (0, 3, 4, 1, 2))
    lr = jnp.einsum('bnqhgd,bnkhd->bnhgqk', qr, kb).astype(jnp.float32) + bias_b
    lr = jnp.where(valid[:, None, None], lr, NEG)
    orr = jnp.einsum('bnhgqk,bnkhd->bnqhgd', sink_softmax(lr, sink).astype(v.dtype), vb)
    return jnp.concatenate([om.reshape(b, N_META, D_ATTN), orr.reshape(b, S, D_ATTN)], axis=1)


def ssd_mixer(z, xbc, dt_raw, conv_w, conv_b, dt_bias, a_log, d_skip, norm_w):
    b, L, _ = xbc.shape
    xbc = lax.conv_general_dilated(xbc, conv_w[:, None, :], window_strides=(1,),
                                   padding=[(CONV_WIDTH - 1, 0)],
                                   dimension_numbers=('NWC', 'WIO', 'NWC'),
                                   feature_group_count=D_XBC) + conv_b
    xbc = jax.nn.silu(xbc)
    xs, Bm, Cm = jnp.split(xbc, [SSM_D_INNER, SSM_D_INNER + SSM_GROUPS * SSM_STATE], axis=-1)
    dt = jax.nn.softplus(dt_raw.astype(jnp.float32) + dt_bias.astype(jnp.float32))
    A = -jnp.exp(a_log.astype(jnp.float32)).reshape(SSM_GROUPS, SSM_HPG)
    pad = (-N_META) % SSD_CHUNK
    padf = lambda t: jnp.pad(t, ((0, 0), (pad, 0)) + ((0, 0),) * (t.ndim - 2))
    Lp = L + pad
    nc = Lp // SSD_CHUNK
    x = padf(xs).reshape(b, nc, SSD_CHUNK, SSM_GROUPS, SSM_HPG, SSM_HEAD_DIM)
    Bc = padf(Bm).reshape(b, nc, SSD_CHUNK, SSM_GROUPS, SSM_STATE)
    Cc = padf(Cm).reshape(b, nc, SSD_CHUNK, SSM_GROUPS, SSM_STATE)
    dtc = padf(dt).reshape(b, nc, SSD_CHUNK, SSM_GROUPS, SSM_HPG)
    xdt = x * dtc[..., None].astype(x.dtype)
    dA = jnp.transpose(dtc * A, (0, 3, 4, 1, 2))
    A_cs = jnp.cumsum(dA, axis=-1)
    tri = np.tril(np.ones((SSD_CHUNK, SSD_CHUNK), bool))
    Lmat = jnp.exp(jnp.where(tri, A_cs[..., :, None] - A_cs[..., None, :], -jnp.inf))
    CB = jnp.einsum('bclgn,bcsgn->bgcls', Cc, Bc)
    M = CB[:, :, None] * Lmat
    y_diag = jnp.einsum('bghcls,bcsghp->bclghp', M, xdt)
    decay_states = jnp.exp(A_cs[..., -1:] - A_cs)
    states = jnp.einsum('bcsgn,bghcs,bcsghp->bcghpn', Bc, decay_states, xdt).astype(jnp.float32)
    chunk_decay = jnp.exp(A_cs[..., -1])

    def step(h, inp):
        st, dec = inp
        return dec[..., None, None] * h + st, h

    h0 = jnp.zeros((b, SSM_GROUPS, SSM_HPG, SSM_HEAD_DIM, SSM_STATE), jnp.float32)
    _, states_prev = lax.scan(step, h0, (jnp.moveaxis(states, 1, 0), jnp.moveaxis(chunk_decay, -1, 0)))
    y_off = jnp.einsum('bclgn,bghcl,cbghpn->bclghp', Cc, jnp.exp(A_cs), states_prev)
    y = y_diag + y_off + x * d_skip.reshape(SSM_GROUPS, SSM_HPG)[..., None]
    y = y.reshape(b, Lp, SSM_D_INNER)[:, pad:].astype(xs.dtype)
    yg = (y * jax.nn.silu(z)).astype(jnp.float32).reshape(b, L, SSM_GROUPS, -1)
    yg = yg * lax.rsqrt(jnp.mean(yg * yg, axis=-1, keepdims=True) + EPS)
    return (yg.reshape(b, L, SSM_D_INNER) * norm_w.astype(jnp.float32)).astype(z.dtype)


def mixer_layer(hn, w_in, sinks, rel_bias, conv_w, conv_b, dt_bias, a_log, d_skip,
                attn_norm_w, ssm_norm_w, w_out):
    proj = hn @ w_in
    q, k, v, z, xbc, dt_raw = jnp.split(proj, list(IN_SPLITS), axis=-1)
    ya = rmsnorm(swa_sink_attention(q, k, v, sinks, rel_bias), attn_norm_w)
    ys = ssd_mixer(z, xbc, dt_raw, conv_w, conv_b, dt_bias, a_log, d_skip, ssm_norm_w)
    return jnp.concatenate([ya, ys], axis=-1) @ w_out


def peer_ffn(xn, wq, keys, u, v):
    b, t, d = xn.shape
    T = b * t
    xt = xn.reshape(T, d)
    q = (xt @ wq).reshape(T, PEER_HEADS, 2, PEER_HALF)
    s = jnp.einsum('thcd,hckd->thck', q, keys).astype(jnp.float32)
    s1, i1 = lax.top_k(s[:, :, 0], PEER_TOPK)
    s2, i2 = lax.top_k(s[:, :, 1], PEER_TOPK)
    cand = (s1[..., :, None] + s2[..., None, :]).reshape(T, PEER_HEADS, PEER_TOPK * PEER_TOPK)
    top, pos = lax.top_k(cand, PEER_TOPK)
    idx = (jnp.take_along_axis(i1, pos // PEER_TOPK, axis=-1) * N_KEYS
           + jnp.take_along_axis(i2, pos % PEER_TOPK, axis=-1))
    gates = jax.nn.softmax(top, axis=-1)
    nblk = -(-T // PEER_BLOCK)
    padT = nblk * PEER_BLOCK - T
    xb = jnp.pad(xt, ((0, padT), (0, 0))).reshape(nblk, PEER_BLOCK, d)
    ib = jnp.pad(idx, ((0, padT), (0, 0), (0, 0))).reshape(nblk, PEER_BLOCK, PEER_HEADS, PEER_TOPK)
    gb = jnp.pad(gates, ((0, padT), (0, 0), (0, 0))).reshape(nblk, PEER_BLOCK, PEER_HEADS, PEER_TOPK)

    def block(args):
        x_blk, i_blk, g_blk = args
        a = jnp.einsum('thkd,td->thk', u[i_blk], x_blk).astype(jnp.float32)
        w = (g_blk * jax.nn.gelu(a, approximate=False)).astype(x_blk.dtype)
        return jnp.einsum('thk,thkd->td', w, v[i_blk])

    out = lax.map(block, (xb, ib, gb)).reshape(nblk * PEER_BLOCK, d)[:T]
    return out.reshape(b, t, d)


def setup_inputs(seed: int = 0) -> dict:
    key = jax.random.key(seed)
    ks = jax.random.split(key, 20)
    f32 = jnp.float32
    nrm = lambda k, shape, scale: jax.random.normal(k, shape, f32) * scale
    dt0 = jnp.exp(jax.random.uniform(ks[8], (DEPTH, SSM_HEADS), f32, np.log(1e-3), np.log(1e-1)))
    return {
        "x": nrm(ks[0], (BATCH, SEQ, D_MODEL), 1.0),
        "meta_tokens": nrm(ks[1], (N_META, D_MODEL), 1.0),
        "rel_bias": nrm(ks[2], (REL_BUCKETS, ATTN_HEADS), 0.5),
        "ln_mix": 1.0 + nrm(ks[3], (DEPTH, D_MODEL), 0.02),
        "w_in": nrm(ks[4], (DEPTH, D_MODEL, D_IN_PROJ), D_MODEL ** -0.5),
        "attn_sinks": nrm(ks[5], (DEPTH, ATTN_HEADS), 0.5),
        "conv_w": nrm(ks[6], (DEPTH, CONV_WIDTH, D_XBC), CONV_WIDTH ** -0.5),
        "conv_b": nrm(ks[7], (DEPTH, D_XBC), 0.02),
        "dt_bias": dt0 + jnp.log(-jnp.expm1(-dt0)),
        "a_log": jnp.log(jax.random.uniform(ks[9], (DEPTH, SSM_HEADS), f32, 1.0, 16.0)),
        "d_skip": 1.0 + nrm(ks[10], (DEPTH, SSM_HEADS), 0.02),
        "attn_norm_w": 1.0 + nrm(ks[11], (DEPTH, D_ATTN), 0.02),
        "ssm_norm_w": 1.0 + nrm(ks[12], (DEPTH, SSM_D_INNER), 0.02),
        "w_out": nrm(ks[13], (DEPTH, D_MIX, D_MODEL), D_MIX ** -0.5),
        "ln_ffn": 1.0 + nrm(ks[14], (DEPTH, D_MODEL), 0.02),
        "peer_wq": nrm(ks[15], (DEPTH, D_MODEL, PEER_HEADS * PEER_KEY_DIM), D_MODEL ** -0.5),
        "peer_keys": nrm(ks[16], (DEPTH, PEER_HEADS, 2, N_KEYS, PEER_HALF), PEER_HALF ** -0.5),
        "peer_u": nrm(ks[17], (DEPTH, N_EXPERTS, D_MODEL), D_MODEL ** -0.5),
        "peer_v": nrm(ks[18], (DEPTH, N_EXPERTS, D_MODEL), PEER_HEADS ** -0.5),
        "ln_final": 1.0 + nrm(ks[19], (D_MODEL,), 0.02),
    }


def reference(x, meta_tokens, rel_bias, ln_mix, w_in, attn_sinks, conv_w, conv_b, dt_bias, a_log,
              d_skip, attn_norm_w, ssm_norm_w, w_out, ln_ffn, peer_wq, peer_keys, peer_u, peer_v,
              ln_final):
    b = x.shape[0]
    meta = jnp.broadcast_to(meta_tokens[None].astype(x.dtype), (b, N_META, D_MODEL))
    h = jnp.concatenate([meta, x], axis=1)
    for layer in range(DEPTH):
        h = h + mixer_layer(rmsnorm(h, ln_mix[layer]), w_in[layer], attn_sinks[layer], rel_bias,
                            conv_w[layer], conv_b[layer], dt_bias[layer], a_log[layer],
                            d_skip[layer], attn_norm_w[layer], ssm_norm_w[layer], w_out[layer])
        if layer == DEPTH - 1:
            h = h[:, N_META:]
        h = h + peer_ffn(rmsnorm(h, ln_ffn[layer]), peer_wq[layer], peer_keys[layer],
                         peer_u[layer], peer_v[layer])
    return rmsnorm(h, ln_final)
```

```python
import functools

import jax
import jax.numpy as jnp
import numpy as np
from jax import lax
from jax.experimental import pallas as pl
from jax.experimental.pallas import tpu as pltpu

F32 = jnp.float32
BF16 = jnp.bfloat16

D_MODEL = 2048
N_META = 16
HEAD_DIM = 64
D_ATTN = 1024
ATTN_HEADS = 16
ATTN_KV_HEADS = 4
ATTN_GROUP = 4
D_KV = 256
WINDOW = 128
BLK = 128
REL_BUCKETS = 32
REL_MAX_DIST = 128
SSM_D_INNER = 1024
SSM_HEAD_DIM = 64
SSM_HEADS = 16
SSM_GROUPS = 2
SSM_HPG = 8
SSM_STATE = 128
CONV_WIDTH = 4
D_XBC = 1536
D_BC = 2 * SSM_GROUPS * SSM_STATE
PEER_HEADS = 8
PEER_TOPK = 16
N_KEYS = 128
N_EXPERTS = N_KEYS * N_KEYS
PEER_HALF = 128
EPS = 1e-6
NEG = -1e30

COL_Q = 0
COL_Z = 1024
COL_XS = 2048
COL_BC = 3072
COL_K = 3584
COL_V = 3840
COL_DT = 4096
D_PROJ = 4224

VMEM_LIMIT = 56 * 1024 * 1024


def _params(*sem):
    return pltpu.CompilerParams(dimension_semantics=sem, vmem_limit_bytes=VMEM_LIMIT)


def _dot(a, b):
    return jnp.dot(a, b, preferred_element_type=F32)


def _dot_nt(a, b):
    return lax.dot_general(a, b, (((1,), (1,)), ((), ())), preferred_element_type=F32)


def _split3(x):
    hi = x.astype(BF16)
    r = x - hi.astype(F32)
    mid = r.astype(BF16)
    lo = (r - mid.astype(F32)).astype(BF16)
    return hi, mid, lo


def _dot01_left(m01, x):
    hi, mid, lo = _split3(x)
    return _dot(m01, hi) + _dot(m01, mid) + _dot(m01, lo)


def _dot01_right(x, m01):
    hi, mid, lo = _split3(x)
    return _dot(hi, m01) + _dot(mid, m01) + _dot(lo, m01)


def _silu(x):
    return x * jax.nn.sigmoid(x)


def _softplus(x):
    return jnp.maximum(x, 0.0) + jnp.log1p(jnp.exp(-jnp.abs(x)))


def _gelu_exact(x):
    return 0.5 * x * (1.0 + lax.erf(x * np.float32(np.sqrt(0.5))))


def _inproj_kernel(x_ref, g_ref, w_ref, o_ref, xn_ref):
    @pl.when(pl.program_id(1) == 0)
    def _():
        x = x_ref[...]
        ms = jnp.mean(x * x, axis=-1, keepdims=True)
        xn_ref[...] = (x * lax.rsqrt(ms + EPS) * g_ref[...]).astype(BF16)

    o_ref[...] = _dot(xn_ref[...], w_ref[...])


def _inproj(x2d, gain, w_p):
    m = x2d.shape[0]
    tm = min(m, 512)
    tn = D_PROJ // 3
    return pl.pallas_call(
        _inproj_kernel,
        grid=(m // tm, 3),
        in_specs=[
            pl.BlockSpec((tm, D_MODEL), lambda i, j: (i, 0)),
            pl.BlockSpec((1, D_MODEL), lambda i, j: (0, 0)),
            pl.BlockSpec((D_MODEL, tn), lambda i, j: (0, j)),
        ],
        out_specs=pl.BlockSpec((tm, tn), lambda i, j: (i, j)),
        out_shape=jax.ShapeDtypeStruct((m, D_PROJ), F32),
        scratch_shapes=[pltpu.VMEM((tm, D_MODEL), BF16)],
        compiler_params=_params("parallel", "arbitrary"),
        name="inproj",
    )(x2d, gain, w_p)


def _t5_bucket(dist):
    n = np.maximum(dist, 0)
    max_exact = REL_BUCKETS // 2
    large = max_exact + (np.log(np.maximum(n, 1) / max_exact) / np.log(REL_MAX_DIST / max_exact)
                         * (REL_BUCKETS - max_exact)).astype(np.int32)
    large = np.minimum(large, REL_BUCKETS - 1)
    return np.where(n < max_exact, n, large).astype(np.int32)


def _band_tables(nb):
    q = np.arange(BLK)[:, None]
    kj = np.arange(BLK)[None, :]
    m = np.arange(N_META)[None, :]
    buckets, valids = [], []
    for n in range(nb):
        d_prev = q - kj + BLK
        d_own = q - kj
        d_meta = N_META + n * BLK + q - m
        dist = np.concatenate([d_prev, d_own, d_meta], axis=1)
        valid = np.concatenate([(d_prev < WINDOW) & (n > 0), d_own >= 0,
                                np.ones((BLK, N_META), bool)], axis=1)
        buckets.append(_t5_bucket(dist))
        valids.append(valid)
    for n in range(2, nb):
        assert (buckets[n] == buckets[1]).all() and (valids[n] == valids[1]).all()
    last = min(1, nb - 1)
    return np.stack([buckets[0], buckets[last]]), np.stack([valids[0], valids[last]])


def _attn_kernel(sink_ref, q_ref, kp_ref, ko_ref, km_ref, vp_ref, vo_ref, vm_ref,
                 bias_ref, nw_ref, o_ref, acc_ref):
    q = q_ref[...] * np.float32(HEAD_DIM ** -0.5)
    for j in range(ATTN_KV_HEADS):
        ks = slice(j * HEAD_DIM, (j + 1) * HEAD_DIM)
        kcat = jnp.concatenate([kp_ref[:, ks], ko_ref[:, ks], km_ref[:, ks]], axis=0).astype(BF16)
        vcat = jnp.concatenate([vp_ref[:, ks], vo_ref[:, ks], vm_ref[:, ks]], axis=0).astype(BF16)
        heads = [j * ATTN_GROUP + g for g in range(ATTN_GROUP)]
        q4 = jnp.concatenate([q[:, h * HEAD_DIM:(h + 1) * HEAD_DIM] for h in heads], axis=0)
        logits = _dot_nt(q4.astype(BF16), kcat)
        logits = logits + jnp.concatenate([bias_ref[0, h] for h in heads], axis=0)
        sink = jnp.concatenate([jnp.full((BLK, 1), sink_ref[h], F32) for h in heads], axis=0)
        mx = jnp.maximum(jnp.max(logits, axis=-1, keepdims=True), sink)
        e = jnp.exp(logits - mx)
        p = e / (jnp.sum(e, axis=-1, keepdims=True) + jnp.exp(sink - mx))
        o = _dot(p.astype(BF16), vcat)
        for g, h in enumerate(heads):
            acc_ref[:, h * HEAD_DIM:(h + 1) * HEAD_DIM] = o[g * BLK:(g + 1) * BLK]
    y = acc_ref[...]
    ms = jnp.mean(y * y, axis=-1, keepdims=True)
    o_ref[...] = (y * lax.rsqrt(ms + EPS) * nw_ref[...]).astype(o_ref.dtype)


def _attention(proj, proj_meta, sinks, bias, norm_w, batch, nb, ymix_dtype):
    rows = batch * nb * BLK
    kcol, vcol = COL_K // D_KV, COL_V // D_KV
    meta_blk = (BLK - N_META) // N_META

    def cur(col):
        return lambda b, n: (b * nb + n, col)

    def prev(col):
        return lambda b, n: (b * nb + jnp.maximum(n - 1, 0), col)

    return pl.pallas_call(
        _attn_kernel,
        grid=(batch, nb),
        in_specs=[
            pl.BlockSpec(memory_space=pltpu.SMEM),
            pl.BlockSpec((BLK, D_ATTN), cur(0)),
            pl.BlockSpec((BLK, D_KV), prev(kcol)),
            pl.BlockSpec((BLK, D_KV), cur(kcol)),
            pl.BlockSpec((N_META, D_KV), lambda b, n: (meta_blk, kcol)),
            pl.BlockSpec((BLK, D_KV), prev(vcol)),
            pl.BlockSpec((BLK, D_KV), cur(vcol)),
            pl.BlockSpec((N_META, D_KV), lambda b, n: (meta_blk, vcol)),
            pl.BlockSpec((1, ATTN_HEADS, BLK, 2 * BLK + N_META),
                         lambda b, n: (jnp.minimum(n, 1), 0, 0, 0)),
            pl.BlockSpec((1, D_ATTN), lambda b, n: (0, 0)),
        ],
        out_specs=pl.BlockSpec((BLK, D_ATTN), cur(0)),
        out_shape=jax.ShapeDtypeStruct((rows, D_ATTN), ymix_dtype),
        scratch_shapes=[pltpu.VMEM((BLK, D_ATTN), F32)],
        compiler_params=_params("parallel", "arbitrary"),
        name="swa_attention",
    )(sinks, proj, proj, proj, proj_meta, proj, proj, proj_meta, bias, norm_w)


def _ssd_chunk(xs_raw, bc_raw, tail_xs, tail_bc, dt_raw, cw, cb, dtb, alog, row_mask, state):
    def conv(blk, tail, w, b):
        ext = jnp.concatenate([tail, blk], axis=0)
        acc = b
        for k in range(CONV_WIDTH):
            acc = acc + w[k:k + 1, :] * ext[5 + k:5 + k + BLK, :]
        return _silu(acc)

    xs = conv(xs_raw, tail_xs, cw[:, :SSM_D_INNER], cb[:, :SSM_D_INNER])
    bc = conv(bc_raw, tail_bc, cw[:, SSM_D_INNER:], cb[:, SSM_D_INNER:])
    dt = _softplus(dt_raw + dtb)
    if row_mask is not None:
        xs = jnp.where(row_mask, xs, 0.0)
        bc = jnp.where(row_mask, bc, 0.0)
        dt = jnp.where(row_mask, dt, 0.0)
    a_neg = -jnp.exp(alog)
    d_a = dt * a_neg

    r = lax.broadcasted_iota(jnp.int32, (BLK, BLK), 0)
    c = lax.broadcasted_iota(jnp.int32, (BLK, BLK), 1)
    tri = r >= c
    cs = _dot01_left(tri.astype(BF16), d_a)
    cs_t = cs.T
    hh = lax.broadcasted_iota(jnp.int32, (BLK, SSM_D_INNER), 0)
    cc = lax.broadcasted_iota(jnp.int32, (BLK, SSM_D_INNER), 1)
    expand = (cc // SSM_HEAD_DIM == hh).astype(BF16)
    dt_rep = _dot01_right(dt, expand)
    ecs_rep = _dot01_right(jnp.exp(cs), expand)
    dec_rep = _dot01_right(jnp.exp(cs[BLK - 1:BLK, :] - cs), expand)

    xdt = xs * dt_rep
    xdtd = (xdt * dec_rep).astype(BF16)
    xdt_b = xdt.astype(BF16)
    chunk_decay = ecs_rep[BLK - 1:BLK, :]

    y_parts, new_state = [], []
    for g in range(SSM_GROUPS):
        b_g = bc[:, g * SSM_STATE:(g + 1) * SSM_STATE]
        c_g = bc[:, (SSM_GROUPS + g) * SSM_STATE:(SSM_GROUPS + g + 1) * SSM_STATE]
        cols = slice(g * 512, (g + 1) * 512)
        cb_g = _dot_nt(c_g.astype(BF16), b_g.astype(BF16))
        y_off = _dot(c_g.astype(BF16), state[g].astype(BF16)) * ecs_rep[:, cols]
        new_state.append(chunk_decay[:, cols] * state[g] + _dot(b_g.T.astype(BF16), xdtd[:, cols]))
        y_diag = []
        for hp in range(SSM_HPG):
            h = g * SSM_HPG + hp
            seg = cs[:, h:h + 1] - cs_t[h:h + 1, :]
            lmat = jnp.exp(jnp.where(tri, seg, -jnp.inf))
            m = (cb_g * lmat).astype(BF16)
            y_diag.append(_dot(m, xdt_b[:, h * SSM_HEAD_DIM:(h + 1) * SSM_HEAD_DIM]))
        y_parts.append(jnp.concatenate(y_diag, axis=1) + y_off)
    return jnp.concatenate(y_parts, axis=1), xs, new_state


def _ssd_meta_kernel(xs_ref, bc_ref, dt_ref, cw_ref, cb_ref, dtb_ref, alog_ref, st_ref):
    rows = lax.broadcasted_iota(jnp.int32, (BLK, 1), 0)
    zero_state = [jnp.zeros((SSM_STATE, 512), F32) for _ in range(SSM_GROUPS)]
    _, _, st = _ssd_chunk(xs_ref[...], bc_ref[...], jnp.zeros((8, SSM_D_INNER), F32),
                          jnp.zeros((8, D_BC), F32), dt_ref[...], cw_ref[...], cb_ref[...],
                          dtb_ref[...], alog_ref[...], rows >= BLK - N_META, zero_state)
    for g in range(SSM_GROUPS):
        st_ref[g] = st[g]


def _ssd_kernel(xs_ref, bc_ref, z_ref, dt_ref, txs_ref, tbc_ref, mxs_ref, mbc_ref, st0_ref,
                cw_ref, cb_ref, dtb_ref, alog_ref, dsk_ref, nw_ref, o_ref, st_ref):
    first = pl.program_id(1) == 0

    @pl.when(first)
    def _():
        st_ref[...] = st0_ref[...]

    tail_xs = jnp.where(first, mxs_ref[...], txs_ref[...])
    tail_bc = jnp.where(first, mbc_ref[...], tbc_ref[...])
    state = [st_ref[g] for g in range(SSM_GROUPS)]
    y, xs, new_state = _ssd_chunk(xs_ref[...], bc_ref[...], tail_xs, tail_bc, dt_ref[...],
                                  cw_ref[...], cb_ref[...], dtb_ref[...], alog_ref[...], None, state)
    for g in range(SSM_GROUPS):
        st_ref[g] = new_state[g]
    y = y + xs * dsk_ref[...]
    yg = y * _silu(z_ref[...])
    outs = []
    for g in range(SSM_GROUPS):
        part = yg[:, g * 512:(g + 1) * 512]
        ms = jnp.mean(part * part, axis=-1, keepdims=True)
        outs.append(part * lax.rsqrt(ms + EPS))
    o_ref[...] = (jnp.concatenate(outs, axis=1) * nw_ref[...]).astype(o_ref.dtype)


def _pad_lanes(v, n=BLK):
    v = v.reshape(1, -1)
    return jnp.pad(v, ((0, 0), (0, n - v.shape[1])))


def _ssd(proj, proj_meta, conv_w, conv_b, dt_bias, a_log, d_skip, norm_w, batch, nc, ymix_dtype):
    rows = batch * nc * BLK
    cb = conv_b.reshape(1, D_XBC)
    dtb, alog = _pad_lanes(dt_bias), _pad_lanes(a_log)
    dsk = jnp.repeat(d_skip, SSM_HEAD_DIM).reshape(1, SSM_D_INNER)
    xs_c, z_c, bc_c, dt_c = COL_XS // 1024, COL_Z // 1024, COL_BC // D_BC, COL_DT // BLK
    full = lambda shape: pl.BlockSpec(shape, lambda *_: (0,) * len(shape))

    state0 = pl.pallas_call(
        _ssd_meta_kernel,
        grid=(1,),
        in_specs=[
            pl.BlockSpec((BLK, SSM_D_INNER), lambda i: (0, xs_c)),
            pl.BlockSpec((BLK, D_BC), lambda i: (0, bc_c)),
            pl.BlockSpec((BLK, BLK), lambda i: (0, dt_c)),
            full((CONV_WIDTH, D_XBC)), full((1, D_XBC)), full((1, BLK)), full((1, BLK)),
        ],
        out_specs=full((SSM_GROUPS, SSM_STATE, 512)),
        out_shape=jax.ShapeDtypeStruct((SSM_GROUPS, SSM_STATE, 512), F32),
        compiler_params=_params("arbitrary"),
        name="ssd_meta_state",
    )(proj_meta, proj_meta, proj_meta, conv_w, cb, dtb, alog)

    def cur(col):
        return lambda b, c: (b * nc + c, col)

    def tail(col):
        return lambda b, c: (jnp.maximum((b * nc + c) * (BLK // 8) - 1, 0), col)

    return pl.pallas_call(
        _ssd_kernel,
        grid=(batch, nc),
        in_specs=[
            pl.BlockSpec((BLK, SSM_D_INNER), cur(xs_c)),
            pl.BlockSpec((BLK, D_BC), cur(bc_c)),
            pl.BlockSpec((BLK, SSM_D_INNER), cur(z_c)),
            pl.BlockSpec((BLK, BLK), cur(dt_c)),
            pl.BlockSpec((8, SSM_D_INNER), tail(xs_c)),
            pl.BlockSpec((8, D_BC), tail(bc_c)),
            pl.BlockSpec((8, SSM_D_INNER), lambda b, c: (BLK // 8 - 1, xs_c)),
            pl.BlockSpec((8, D_BC), lambda b, c: (BLK // 8 - 1, bc_c)),
            full((SSM_GROUPS, SSM_STATE, 512)),
            full((CONV_WIDTH, D_XBC)), full((1, D_XBC)), full((1, BLK)), full((1, BLK)),
            full((1, SSM_D_INNER)), full((1, SSM_D_INNER)),
        ],
        out_specs=pl.BlockSpec((BLK, SSM_D_INNER), cur(0)),
        out_shape=jax.ShapeDtypeStruct((rows, SSM_D_INNER), ymix_dtype),
        scratch_shapes=[pltpu.VMEM((SSM_GROUPS, SSM_STATE, 512), F32)],
        compiler_params=_params("parallel", "arbitrary"),
        name="ssd_mixer",
    )(proj, proj, proj, proj, proj, proj, proj_meta, proj_meta, state0,
      conv_w, cb, dtb, alog, dsk, norm_w)


def _outproj_kernel(ya_ref, ys_ref, x_ref, w_ref, g_ref, h_ref, xn_ref):
    y = jnp.concatenate([ya_ref[...], ys_ref[...]], axis=1).astype(BF16)
    h = x_ref[...] + _dot(y, w_ref[...])
    h_ref[...] = h
    ms = jnp.mean(h * h, axis=-1, keepdims=True)
    xn_ref[...] = (h * lax.rsqrt(ms + EPS) * g_ref[...]).astype(BF16)


def _outproj(ya, ys, x2d, w_out, gain):
    m = x2d.shape[0]
    tm = min(m, 256)
    return pl.pallas_call(
        _outproj_kernel,
        grid=(m // tm,),
        in_specs=[
            pl.BlockSpec((tm, D_ATTN), lambda i: (i, 0)),
            pl.BlockSpec((tm, SSM_D_INNER), lambda i: (i, 0)),
            pl.BlockSpec((tm, D_MODEL), lambda i: (i, 0)),
            pl.BlockSpec((D_MODEL, D_MODEL), lambda i: (0, 0)),
            pl.BlockSpec((1, D_MODEL), lambda i: (0, 0)),
        ],
        out_specs=[pl.BlockSpec((tm, D_MODEL), lambda i: (i, 0)),
                   pl.BlockSpec((tm, D_MODEL), lambda i: (i, 0))],
        out_shape=[jax.ShapeDtypeStruct((m, D_MODEL), F32),
                   jax.ShapeDtypeStruct((m, D_MODEL), BF16)],
        compiler_params=_params("parallel"),
        name="outproj",
    )(ya, ys, x2d, w_out, gain)


def _top16_rows(s):
    vals = []
    for r in range(PEER_TOPK):
        m = jnp.max(s, axis=0, keepdims=True)
        vals.append(m)
        if r < PEER_TOPK - 1:
            s = jnp.where(s == m, -jnp.inf, s)
    return vals


def _route_kernel(xn_ref, wq_ref, keys_ref, s1_ref, e1_ref, s2_ref, e2_ref, tau_ref, v_ref):
    q = _dot(xn_ref[...], wq_ref[...]).astype(BF16)
    for h in range(PEER_HEADS):
        s, tops = [], []
        for c in range(2):
            qs = q[:, (2 * h + c) * PEER_HALF:(2 * h + c + 1) * PEER_HALF]
            sc = _dot_nt(keys_ref[h, c], qs)
            s.append(sc)
            for r, m in enumerate(_top16_rows(sc)):
                v_ref[c, r:r + 1, :] = m
            tops.append(v_ref[c])
        v1, v2 = tops
        blocks = [v1[0:1] + v2, v1[1:2] + v2[0:8]]
        blocks += [v1[a:a + 1] + v2[0:8] for a in range(2, 8)]
        blocks.append(v1[8:16] + v2[0:1])
        cand = jnp.concatenate(blocks, axis=0)
        top = v1[0:1] + v2[0:1]
        rem = cand
        for _ in range(PEER_TOPK - 1):
            m = jnp.max(rem, axis=0, keepdims=True)
            rem = jnp.where(rem == m, -jnp.inf, rem)
        tau = jnp.max(rem, axis=0, keepdims=True)
        z = jnp.sum(jnp.where(cand >= tau, jnp.exp(cand - top), 0.0), axis=0, keepdims=True)
        s1_ref[h] = s[0]
        s2_ref[h] = s[1]
        e1_ref[h] = jnp.exp(s[0] - v1[0:1]) / z
        e2_ref[h] = jnp.exp(s[1] - v2[0:1])
        tau_ref[h:h + 1, :] = tau


def _route(xn, wq, keys):
    t = xn.shape[0]
    tm = min(t, 256)
    big = pl.BlockSpec((PEER_HEADS, N_KEYS, tm), lambda i: (0, 0, i))
    big_shape = jax.ShapeDtypeStruct((PEER_HEADS, N_KEYS, t), F32)
    return pl.pallas_call(
        _route_kernel,
        grid=(t // tm,),
        in_specs=[
            pl.BlockSpec((tm, D_MODEL), lambda i: (i, 0)),
            pl.BlockSpec((D_MODEL, D_MODEL), lambda i: (0, 0)),
            pl.BlockSpec((PEER_HEADS, 2, N_KEYS, PEER_HALF), lambda i: (0, 0, 0, 0)),
        ],
        out_specs=[big, big, big, big, pl.BlockSpec((PEER_HEADS, tm), lambda i: (0, i))],
        out_shape=[big_shape] * 4 + [jax.ShapeDtypeStruct((PEER_HEADS, t), F32)],
        scratch_shapes=[pltpu.VMEM((2, PEER_TOPK, tm), F32)],
        compiler_params=_params("parallel"),
        name="peer_route",
    )(xn, wq, keys)


PEER_TB = 512
PEER_EB = 1024
PEER_CH = 16


def _peer_kernel(xn_ref, u_ref, vt_ref, s1_ref, e1_ref, s2_ref, e2_ref, tau_ref, o_ref,
                 a_ref, w_ref):
    @pl.when(pl.program_id(1) == 0)
    def _():
        o_ref[...] = jnp.zeros_like(o_ref)

    a_ref[...] = _dot_nt(u_ref[...], xn_ref[...])
    per_i = N_KEYS // PEER_CH

    def body(k, carry):
        ii = k // per_i
        j0 = pl.multiple_of((k % per_i) * PEER_CH, PEER_CH)
        acc = jnp.zeros((PEER_CH, PEER_TB), F32)
        for h in range(PEER_HEADS):
            c = s1_ref[h, pl.ds(ii, 1), :] + s2_ref[h, pl.ds(j0, PEER_CH), :]
            gate = e1_ref[h, pl.ds(ii, 1), :] * e2_ref[h, pl.ds(j0, PEER_CH), :]
            acc = acc + jnp.where(c >= tau_ref[h:h + 1, :], gate, 0.0)
        r0 = pl.multiple_of(k * PEER_CH, PEER_CH)
        w_ref[pl.ds(r0, PEER_CH), :] = (acc * _gelu_exact(a_ref[pl.ds(r0, PEER_CH), :])).astype(BF16)
        return carry

    lax.fori_loop(0, PEER_EB // PEER_CH, body, 0)
    o_ref[...] += _dot(vt_ref[...], w_ref[...])


def _peer(xn, u_b, vt_b, s1, e1, s2, e2, tau):
    t = xn.shape[0]
    tb = min(t, PEER_TB)
    assert tb == PEER_TB
    ni = PEER_EB // N_KEYS
    small = pl.BlockSpec((PEER_HEADS, ni, tb), lambda i, e: (0, e, i))
    big = pl.BlockSpec((PEER_HEADS, N_KEYS, tb), lambda i, e: (0, 0, i))
    return pl.pallas_call(
        _peer_kernel,
        grid=(t // tb, N_EXPERTS // PEER_EB),
        in_specs=[
            pl.BlockSpec((tb, D_MODEL), lambda i, e: (i, 0)),
            pl.BlockSpec((PEER_EB, D_MODEL), lambda i, e: (e, 0)),
            pl.BlockSpec((D_MODEL, PEER_EB), lambda i, e: (0, e)),
            small, small, big, big,
            pl.BlockSpec((PEER_HEADS, tb), lambda i, e: (0, i)),
        ],
        out_specs=pl.BlockSpec((D_MODEL, tb), lambda i, e: (0, i)),
        out_shape=jax.ShapeDtypeStruct((D_MODEL, t), F32),
        scratch_shapes=[pltpu.VMEM((PEER_EB, tb), F32), pltpu.VMEM((PEER_EB, tb), BF16)],
        compiler_params=_params("parallel", "arbitrary"),
        name="peer_experts",
    )(xn, u_b, vt_b, s1, e1, s2, e2, tau)


def _final_kernel(h_ref, pt_ref, g_ref, o_ref):
    h = h_ref[...] + pt_ref[...].T
    ms = jnp.mean(h * h, axis=-1, keepdims=True)
    o_ref[...] = h * lax.rsqrt(ms + EPS) * g_ref[...]


def _final(h1, peer_t, gain):
    t = h1.shape[0]
    tm = min(t, 256)
    return pl.pallas_call(
        _final_kernel,
        grid=(t // tm,),
        in_specs=[
            pl.BlockSpec((tm, D_MODEL), lambda i: (i, 0)),
            pl.BlockSpec((D_MODEL, tm), lambda i: (0, i)),
            pl.BlockSpec((1, D_MODEL), lambda i: (0, 0)),
        ],
        out_specs=pl.BlockSpec((tm, D_MODEL), lambda i: (i, 0)),
        out_shape=jax.ShapeDtypeStruct((t, D_MODEL), F32),
        compiler_params=_params("parallel"),
        name="final_norm",
    )(h1, peer_t, gain)


def _mixer(x2d, batch, seq, meta_tokens, rel_bias, ln_mix, w_in, sinks, conv_w, conv_b, dt_bias,
           a_log, d_skip, attn_norm_w, ssm_norm_w):
    nb = seq // BLK
    w_p = jnp.concatenate(
        [w_in[:, 0:1024], w_in[:, 1536:2560], w_in[:, 2560:4096], w_in[:, 1024:1280],
         w_in[:, 1280:1536], w_in[:, 4096:4112],
         jnp.zeros((D_MODEL, D_PROJ - 4112), w_in.dtype)], axis=1).astype(BF16)
    gain = ln_mix.reshape(1, D_MODEL)
    meta_pad = jnp.concatenate([jnp.zeros((BLK - N_META, D_MODEL), F32), meta_tokens.astype(F32)], axis=0)
    proj = _inproj(x2d, gain, w_p)
    proj_meta = _inproj(meta_pad, gain, w_p)

    bucket, valid = _band_tables(nb)
    bias = jnp.transpose(rel_bias.astype(F32)[bucket], (0, 3, 1, 2))
    bias = jnp.where(valid[:, None], bias, NEG)
    ya = _attention(proj, proj_meta, sinks.astype(F32), bias, attn_norm_w.reshape(1, D_ATTN),
                    batch, nb, F32)
    ys = _ssd(proj, proj_meta, conv_w, conv_b, dt_bias, a_log, d_skip,
                   ssm_norm_w.reshape(1, SSM_D_INNER), batch, nb, F32)
    return ya, ys


def kernel(x, meta_tokens, rel_bias, ln_mix, w_in, attn_sinks, conv_w, conv_b, dt_bias, a_log, d_skip,
           attn_norm_w, ssm_norm_w, w_out, ln_ffn, peer_wq, peer_keys, peer_u, peer_v, ln_final):
    batch, seq, _ = x.shape
    x2d = x.reshape(batch * seq, D_MODEL)
    ya, ys = _mixer(x2d, batch, seq, meta_tokens, rel_bias, ln_mix[0], w_in[0], attn_sinks[0],
                         conv_w[0], conv_b[0], dt_bias[0], a_log[0], d_skip[0], attn_norm_w[0],
                         ssm_norm_w[0])
    h1, xn = _outproj(ya, ys, x2d, w_out[0].astype(BF16), ln_ffn[0].reshape(1, D_MODEL))
    s1, e1, s2, e2, tau = _route(xn, peer_wq[0].astype(BF16), peer_keys[0].astype(BF16))
    peer_t = _peer(xn, peer_u[0].astype(BF16), peer_v[0].T.astype(BF16), s1, e1, s2, e2, tau)
    out = _final(h1, peer_t, ln_final.reshape(1, D_MODEL))
    return out.reshape(batch, seq, D_MODEL)
```

```python
import functools

import jax
import jax.numpy as jnp
import numpy as np
from jax import lax
from jax.experimental import pallas as pl
from jax.experimental.pallas import tpu as pltpu

F32 = jnp.float32
BF16 = jnp.bfloat16

D_MODEL = 2048
N_META = 16
HEAD_DIM = 64
D_ATTN = 1024
ATTN_HEADS = 16
ATTN_KV_HEADS = 4
ATTN_GROUP = 4
D_KV = 256
WINDOW = 128
BLK = 128
REL_BUCKETS = 32
REL_MAX_DIST = 128
SSM_D_INNER = 1024
SSM_HEAD_DIM = 64
SSM_HEADS = 16
SSM_GROUPS = 2
SSM_HPG = 8
SSM_STATE = 128
CONV_WIDTH = 4
D_XBC = 1536
D_BC = 2 * SSM_GROUPS * SSM_STATE
PEER_HEADS = 8
PEER_TOPK = 16
N_KEYS = 128
N_EXPERTS = N_KEYS * N_KEYS
PEER_HALF = 128
EPS = 1e-6
NEG = -1e30

COL_Q = 0
COL_Z = 1024
COL_XS = 2048
COL_BC = 3072
COL_K = 3584
COL_V = 3840
COL_DT = 4096
D_PROJ = 4224

VMEM_LIMIT = 56 * 1024 * 1024


def _params(*sem, flags=None):
    return pltpu.CompilerParams(dimension_semantics=sem, vmem_limit_bytes=VMEM_LIMIT, flags=flags)


def _dot(a, b):
    return jnp.dot(a, b, preferred_element_type=F32)


def _dot_nt(a, b):
    return lax.dot_general(a, b, (((1,), (1,)), ((), ())), preferred_element_type=F32)


def _split3(x):
    hi = x.astype(BF16)
    r = x - hi.astype(F32)
    mid = r.astype(BF16)
    lo = (r - mid.astype(F32)).astype(BF16)
    return hi, mid, lo


def _dot01_left(m01, x):
    hi, mid, lo = _split3(x)
    return _dot(m01, hi) + _dot(m01, mid) + _dot(m01, lo)


def _dot01_right(x, m01):
    hi, mid, lo = _split3(x)
    return _dot(hi, m01) + _dot(mid, m01) + _dot(lo, m01)


def _silu(x):
    return x * jax.nn.sigmoid(x)


def _softplus(x):
    return jnp.maximum(x, 0.0) + jnp.log1p(jnp.exp(-jnp.abs(x)))


def _gelu_exact(x):
    return 0.5 * x * (1.0 + lax.erf(x * np.float32(np.sqrt(0.5))))


def _inproj_kernel(x_ref, g_ref, w_ref, o_ref, xn_ref):
    @pl.when(pl.program_id(1) == 0)
    def _():
        x = x_ref[...]
        ms = jnp.mean(x * x, axis=-1, keepdims=True)
        xn_ref[...] = (x * lax.rsqrt(ms + EPS) * g_ref[...]).astype(BF16)

    o_ref[...] = _dot(xn_ref[...], w_ref[...])


def _inproj(x2d, gain, w_p):
    m = x2d.shape[0]
    tm = min(m, 512)
    tn = D_PROJ // 3
    return pl.pallas_call(
        _inproj_kernel,
        grid=(m // tm, 3),
        in_specs=[
            pl.BlockSpec((tm, D_MODEL), lambda i, j: (i, 0)),
            pl.BlockSpec((1, D_MODEL), lambda i, j: (0, 0)),
            pl.BlockSpec((D_MODEL, tn), lambda i, j: (0, j)),
        ],
        out_specs=pl.BlockSpec((tm, tn), lambda i, j: (i, j)),
        out_shape=jax.ShapeDtypeStruct((m, D_PROJ), F32),
        scratch_shapes=[pltpu.VMEM((tm, D_MODEL), BF16)],
        compiler_params=_params("parallel", "arbitrary"),
        name="inproj",
    )(x2d, gain, w_p)


def _t5_bucket(dist):
    n = np.maximum(dist, 0)
    max_exact = REL_BUCKETS // 2
    large = max_exact + (np.log(np.maximum(n, 1) / max_exact) / np.log(REL_MAX_DIST / max_exact)
                         * (REL_BUCKETS - max_exact)).astype(np.int32)
    large = np.minimum(large, REL_BUCKETS - 1)
    return np.where(n < max_exact, n, large).astype(np.int32)


def _band_tables(nb):
    q = np.arange(BLK)[:, None]
    kj = np.arange(BLK)[None, :]
    m = np.arange(N_META)[None, :]
    buckets, valids = [], []
    for n in range(nb):
        d_prev = q - kj + BLK
        d_own = q - kj
        d_meta = N_META + n * BLK + q - m
        dist = np.concatenate([d_prev, d_own, d_meta], axis=1)
        valid = np.concatenate([(d_prev < WINDOW) & (n > 0), d_own >= 0,
                                np.ones((BLK, N_META), bool)], axis=1)
        buckets.append(_t5_bucket(dist))
        valids.append(valid)
    for n in range(2, nb):
        assert (buckets[n] == buckets[1]).all() and (valids[n] == valids[1]).all()
    last = min(1, nb - 1)
    return np.stack([buckets[0], buckets[last]]), np.stack([valids[0], valids[last]])


def _attn_kernel(sink_ref, q_ref, kp_ref, ko_ref, km_ref, vp_ref, vo_ref, vm_ref,
                 bias_ref, nw_ref, o_ref, acc_ref):
    q = q_ref[...] * np.float32(HEAD_DIM ** -0.5)
    for j in range(ATTN_KV_HEADS):
        ks = slice(j * HEAD_DIM, (j + 1) * HEAD_DIM)
        kcat = jnp.concatenate([kp_ref[:, ks], ko_ref[:, ks], km_ref[:, ks]], axis=0).astype(BF16)
        vcat = jnp.concatenate([vp_ref[:, ks], vo_ref[:, ks], vm_ref[:, ks]], axis=0).astype(BF16)
        heads = [j * ATTN_GROUP + g for g in range(ATTN_GROUP)]
        q4 = jnp.concatenate([q[:, h * HEAD_DIM:(h + 1) * HEAD_DIM] for h in heads], axis=0)
        logits = _dot_nt(q4.astype(BF16), kcat)
        logits = logits + jnp.concatenate([bias_ref[0, h] for h in heads], axis=0)
        sink = jnp.concatenate([jnp.full((BLK, 1), sink_ref[h], F32) for h in heads], axis=0)
        mx = jnp.maximum(jnp.max(logits, axis=-1, keepdims=True), sink)
        e = jnp.exp(logits - mx)
        p = e / (jnp.sum(e, axis=-1, keepdims=True) + jnp.exp(sink - mx))
        o = _dot(p.astype(BF16), vcat)
        for g, h in enumerate(heads):
            acc_ref[:, h * HEAD_DIM:(h + 1) * HEAD_DIM] = o[g * BLK:(g + 1) * BLK]
    y = acc_ref[...]
    ms = jnp.mean(y * y, axis=-1, keepdims=True)
    o_ref[...] = (y * lax.rsqrt(ms + EPS) * nw_ref[...]).astype(o_ref.dtype)


def _attention(proj, proj_meta, sinks, bias, norm_w, batch, nb, ymix_dtype):
    rows = batch * nb * BLK
    kcol, vcol = COL_K // D_KV, COL_V // D_KV
    meta_blk = (BLK - N_META) // N_META

    def cur(col):
        return lambda b, n: (b * nb + n, col)

    def prev(col):
        return lambda b, n: (b * nb + jnp.maximum(n - 1, 0), col)

    return pl.pallas_call(
        _attn_kernel,
        grid=(batch, nb),
        in_specs=[
            pl.BlockSpec(memory_space=pltpu.SMEM),
            pl.BlockSpec((BLK, D_ATTN), cur(0)),
            pl.BlockSpec((BLK, D_KV), prev(kcol)),
            pl.BlockSpec((BLK, D_KV), cur(kcol)),
            pl.BlockSpec((N_META, D_KV), lambda b, n: (meta_blk, kcol)),
            pl.BlockSpec((BLK, D_KV), prev(vcol)),
            pl.BlockSpec((BLK, D_KV), cur(vcol)),
            pl.BlockSpec((N_META, D_KV), lambda b, n: (meta_blk, vcol)),
            pl.BlockSpec((1, ATTN_HEADS, BLK, 2 * BLK + N_META),
                         lambda b, n: (jnp.minimum(n, 1), 0, 0, 0)),
            pl.BlockSpec((1, D_ATTN), lambda b, n: (0, 0)),
        ],
        out_specs=pl.BlockSpec((BLK, D_ATTN), cur(0)),
        out_shape=jax.ShapeDtypeStruct((rows, D_ATTN), ymix_dtype),
        scratch_shapes=[pltpu.VMEM((BLK, D_ATTN), F32)],
        compiler_params=_params("parallel", "arbitrary"),
        name="swa_attention",
    )(sinks, proj, proj, proj, proj_meta, proj, proj, proj_meta, bias, norm_w)


def _ssd_chunk(xs_raw, bc_raw, tail_xs, tail_bc, dt_raw, cw, cb, dtb, alog, row_mask, state):
    def conv(blk, tail, w, b):
        ext = jnp.concatenate([tail, blk], axis=0)
        acc = b
        for k in range(CONV_WIDTH):
            acc = acc + w[k:k + 1, :] * ext[5 + k:5 + k + BLK, :]
        return _silu(acc)

    xs = conv(xs_raw, tail_xs, cw[:, :SSM_D_INNER], cb[:, :SSM_D_INNER])
    bc = conv(bc_raw, tail_bc, cw[:, SSM_D_INNER:], cb[:, SSM_D_INNER:])
    dt = _softplus(dt_raw + dtb)
    if row_mask is not None:
        xs = jnp.where(row_mask, xs, 0.0)
        bc = jnp.where(row_mask, bc, 0.0)
        dt = jnp.where(row_mask, dt, 0.0)
    a_neg = -jnp.exp(alog)
    d_a = dt * a_neg

    r = lax.broadcasted_iota(jnp.int32, (BLK, BLK), 0)
    c = lax.broadcasted_iota(jnp.int32, (BLK, BLK), 1)
    tri = r >= c
    cs = _dot01_left(tri.astype(BF16), d_a)
    cs_t = cs.T
    hh = lax.broadcasted_iota(jnp.int32, (BLK, SSM_D_INNER), 0)
    cc = lax.broadcasted_iota(jnp.int32, (BLK, SSM_D_INNER), 1)
    expand = (cc // SSM_HEAD_DIM == hh).astype(BF16)
    dt_rep = _dot01_right(dt, expand)
    ecs_rep = _dot01_right(jnp.exp(cs), expand)
    dec_rep = _dot01_right(jnp.exp(cs[BLK - 1:BLK, :] - cs), expand)

    xdt = xs * dt_rep
    xdtd = (xdt * dec_rep).astype(BF16)
    xdt_b = xdt.astype(BF16)
    chunk_decay = ecs_rep[BLK - 1:BLK, :]

    y_parts, new_state = [], []
    for g in range(SSM_GROUPS):
        b_g = bc[:, g * SSM_STATE:(g + 1) * SSM_STATE]
        c_g = bc[:, (SSM_GROUPS + g) * SSM_STATE:(SSM_GROUPS + g + 1) * SSM_STATE]
        cols = slice(g * 512, (g + 1) * 512)
        cb_g = _dot_nt(c_g.astype(BF16), b_g.astype(BF16))
        y_off = _dot(c_g.astype(BF16), state[g].astype(BF16)) * ecs_rep[:, cols]
        new_state.append(chunk_decay[:, cols] * state[g] + _dot(b_g.T.astype(BF16), xdtd[:, cols]))
        y_diag = []
        for hp in range(SSM_HPG):
            h = g * SSM_HPG + hp
            seg = cs[:, h:h + 1] - cs_t[h:h + 1, :]
            lmat = jnp.exp(jnp.where(tri, seg, -jnp.inf))
            m = (cb_g * lmat).astype(BF16)
            y_diag.append(_dot(m, xdt_b[:, h * SSM_HEAD_DIM:(h + 1) * SSM_HEAD_DIM]))
        y_parts.append(jnp.concatenate(y_diag, axis=1) + y_off)
    return jnp.concatenate(y_parts, axis=1), xs, new_state


def _ssd_meta_kernel(xs_ref, bc_ref, dt_ref, cw_ref, cb_ref, dtb_ref, alog_ref, st_ref):
    rows = lax.broadcasted_iota(jnp.int32, (BLK, 1), 0)
    zero_state = [jnp.zeros((SSM_STATE, 512), F32) for _ in range(SSM_GROUPS)]
    _, _, st = _ssd_chunk(xs_ref[...], bc_ref[...], jnp.zeros((8, SSM_D_INNER), F32),
                          jnp.zeros((8, D_BC), F32), dt_ref[...], cw_ref[...], cb_ref[...],
                          dtb_ref[...], alog_ref[...], rows >= BLK - N_META, zero_state)
    for g in range(SSM_GROUPS):
        st_ref[g] = st[g]


def _ssd_kernel(xs_ref, bc_ref, z_ref, dt_ref, txs_ref, tbc_ref, mxs_ref, mbc_ref, st0_ref,
                cw_ref, cb_ref, dtb_ref, alog_ref, dsk_ref, nw_ref, o_ref, st_ref):
    first = pl.program_id(1) == 0

    @pl.when(first)
    def _():
        st_ref[...] = st0_ref[...]

    tail_xs = jnp.where(first, mxs_ref[...], txs_ref[...])
    tail_bc = jnp.where(first, mbc_ref[...], tbc_ref[...])
    state = [st_ref[g] for g in range(SSM_GROUPS)]
    y, xs, new_state = _ssd_chunk(xs_ref[...], bc_ref[...], tail_xs, tail_bc, dt_ref[...],
                                  cw_ref[...], cb_ref[...], dtb_ref[...], alog_ref[...], None, state)
    for g in range(SSM_GROUPS):
        st_ref[g] = new_state[g]
    y = y + xs * dsk_ref[...]
    yg = y * _silu(z_ref[...])
    outs = []
    for g in range(SSM_GROUPS):
        part = yg[:, g * 512:(g + 1) * 512]
        ms = jnp.mean(part * part, axis=-1, keepdims=True)
        outs.append(part * lax.rsqrt(ms + EPS))
    o_ref[...] = (jnp.concatenate(outs, axis=1) * nw_ref[...]).astype(o_ref.dtype)


def _pad_lanes(v, n=BLK):
    v = v.reshape(1, -1)
    return jnp.pad(v, ((0, 0), (0, n - v.shape[1])))


def _ssd(proj, proj_meta, conv_w, conv_b, dt_bias, a_log, d_skip, norm_w, batch, nc, ymix_dtype):
    rows = batch * nc * BLK
    cb = conv_b.reshape(1, D_XBC)
    dtb, alog = _pad_lanes(dt_bias), _pad_lanes(a_log)
    dsk = jnp.repeat(d_skip, SSM_HEAD_DIM).reshape(1, SSM_D_INNER)
    xs_c, z_c, bc_c, dt_c = COL_XS // 1024, COL_Z // 1024, COL_BC // D_BC, COL_DT // BLK
    full = lambda shape: pl.BlockSpec(shape, lambda *_: (0,) * len(shape))

    state0 = pl.pallas_call(
        _ssd_meta_kernel,
        grid=(1,),
        in_specs=[
            pl.BlockSpec((BLK, SSM_D_INNER), lambda i: (0, xs_c)),
            pl.BlockSpec((BLK, D_BC), lambda i: (0, bc_c)),
            pl.BlockSpec((BLK, BLK), lambda i: (0, dt_c)),
            full((CONV_WIDTH, D_XBC)), full((1, D_XBC)), full((1, BLK)), full((1, BLK)),
        ],
        out_specs=full((SSM_GROUPS, SSM_STATE, 512)),
        out_shape=jax.ShapeDtypeStruct((SSM_GROUPS, SSM_STATE, 512), F32),
        compiler_params=_params("arbitrary"),
        name="ssd_meta_state",
    )(proj_meta, proj_meta, proj_meta, conv_w, cb, dtb, alog)

    def cur(col):
        return lambda b, c: (b * nc + c, col)

    def tail(col):
        return lambda b, c: (jnp.maximum((b * nc + c) * (BLK // 8) - 1, 0), col)

    return pl.pallas_call(
        _ssd_kernel,
        grid=(batch, nc),
        in_specs=[
            pl.BlockSpec((BLK, SSM_D_INNER), cur(xs_c)),
            pl.BlockSpec((BLK, D_BC), cur(bc_c)),
            pl.BlockSpec((BLK, SSM_D_INNER), cur(z_c)),
            pl.BlockSpec((BLK, BLK), cur(dt_c)),
            pl.BlockSpec((8, SSM_D_INNER), tail(xs_c)),
            pl.BlockSpec((8, D_BC), tail(bc_c)),
            pl.BlockSpec((8, SSM_D_INNER), lambda b, c: (BLK // 8 - 1, xs_c)),
            pl.BlockSpec((8, D_BC), lambda b, c: (BLK // 8 - 1, bc_c)),
            full((SSM_GROUPS, SSM_STATE, 512)),
            full((CONV_WIDTH, D_XBC)), full((1, D_XBC)), full((1, BLK)), full((1, BLK)),
            full((1, SSM_D_INNER)), full((1, SSM_D_INNER)),
        ],
        out_specs=pl.BlockSpec((BLK, SSM_D_INNER), cur(0)),
        out_shape=jax.ShapeDtypeStruct((rows, SSM_D_INNER), ymix_dtype),
        scratch_shapes=[pltpu.VMEM((SSM_GROUPS, SSM_STATE, 512), F32)],
        compiler_params=_params("parallel", "arbitrary"),
        name="ssd_mixer",
    )(proj, proj, proj, proj, proj, proj, proj_meta, proj_meta, state0,
      conv_w, cb, dtb, alog, dsk, norm_w)


def _outproj_kernel(ya_ref, ys_ref, x_ref, w_ref, g_ref, h_ref, xn_ref):
    y = jnp.concatenate([ya_ref[...], ys_ref[...]], axis=1).astype(BF16)
    h = x_ref[...] + _dot(y, w_ref[...])
    h_ref[...] = h
    ms = jnp.mean(h * h, axis=-1, keepdims=True)
    xn_ref[...] = (h * lax.rsqrt(ms + EPS) * g_ref[...]).astype(BF16)


def _outproj(ya, ys, x2d, w_out, gain):
    m = x2d.shape[0]
    tm = min(m, 256)
    return pl.pallas_call(
        _outproj_kernel,
        grid=(m // tm,),
        in_specs=[
            pl.BlockSpec((tm, D_ATTN), lambda i: (i, 0)),
            pl.BlockSpec((tm, SSM_D_INNER), lambda i: (i, 0)),
            pl.BlockSpec((tm, D_MODEL), lambda i: (i, 0)),
            pl.BlockSpec((D_MODEL, D_MODEL), lambda i: (0, 0)),
            pl.BlockSpec((1, D_MODEL), lambda i: (0, 0)),
        ],
        out_specs=[pl.BlockSpec((tm, D_MODEL), lambda i: (i, 0)),
                   pl.BlockSpec((tm, D_MODEL), lambda i: (i, 0))],
        out_shape=[jax.ShapeDtypeStruct((m, D_MODEL), F32),
                   jax.ShapeDtypeStruct((m, D_MODEL), BF16)],
        compiler_params=_params("parallel"),
        name="outproj",
    )(ya, ys, x2d, w_out, gain)


def _top16_rows(s):
    vals = []
    for r in range(PEER_TOPK):
        m = jnp.max(s, axis=0, keepdims=True)
        vals.append(m)
        if r < PEER_TOPK - 1:
            s = jnp.where(s == m, -jnp.inf, s)
    return vals


def _route_kernel(xn_ref, wq_ref, keys_ref, th_ref, e1_ref, s2_ref, e2_ref, v_ref):
    q = _dot(xn_ref[...], wq_ref[...]).astype(BF16)
    for h in range(PEER_HEADS):
        s, tops = [], []
        for c in range(2):
            qs = q[:, (2 * h + c) * PEER_HALF:(2 * h + c + 1) * PEER_HALF]
            sc = _dot_nt(keys_ref[h, c], qs)
            s.append(sc)
            for r, m in enumerate(_top16_rows(sc)):
                v_ref[c, r:r + 1, :] = m
            tops.append(v_ref[c])
        v1, v2 = tops
        blocks = [v1[0:1] + v2, v1[1:2] + v2[0:8]]
        blocks += [v1[a:a + 1] + v2[0:8] for a in range(2, 8)]
        blocks.append(v1[8:16] + v2[0:1])
        cand = jnp.concatenate(blocks, axis=0)
        top = v1[0:1] + v2[0:1]
        rem = cand
        for _ in range(PEER_TOPK - 1):
            m = jnp.max(rem, axis=0, keepdims=True)
            rem = jnp.where(rem == m, -jnp.inf, rem)
        tau = jnp.max(rem, axis=0, keepdims=True)
        z = jnp.sum(jnp.where(cand >= tau, jnp.exp(cand - top), 0.0), axis=0, keepdims=True)
        widths = [PEER_TOPK, 8] + [8] * 6
        thr = [jnp.min(jnp.where(blocks[a] >= tau, v2[0:widths[a]], jnp.inf), axis=0, keepdims=True)
               for a in range(8)]
        thr_hi = jnp.where(blocks[8] >= tau, v2[0:1], jnp.inf)
        theta = jnp.full_like(s[0], jnp.inf)
        for a in range(PEER_TOPK):
            t_a = thr[a] if a < 8 else thr_hi[a - 8:a - 7]
            theta = jnp.where(s[0] == v1[a:a + 1], t_a, theta)
        th_ref[h] = theta
        e1_ref[h] = jnp.exp(s[0] - v1[0:1]) / z
        e2 = jnp.exp(s[1] - v2[0:1])
        for lt in range(s[1].shape[1] // BLK):
            s2_ref[h, lt] = s[1][:, lt * BLK:(lt + 1) * BLK]
            e2_ref[h, lt] = e2[:, lt * BLK:(lt + 1) * BLK]


def _route(xn, wq, keys):
    t = xn.shape[0]
    tm = min(t, 256)
    big = pl.BlockSpec((PEER_HEADS, N_KEYS, tm), lambda i: (0, 0, i))
    big_shape = jax.ShapeDtypeStruct((PEER_HEADS, N_KEYS, t), F32)
    tiled = pl.BlockSpec((PEER_HEADS, tm // BLK, N_KEYS, BLK), lambda i: (0, i, 0, 0))
    tiled_shape = jax.ShapeDtypeStruct((PEER_HEADS, t // BLK, N_KEYS, BLK), F32)
    return pl.pallas_call(
        _route_kernel,
        grid=(t // tm,),
        in_specs=[
            pl.BlockSpec((tm, D_MODEL), lambda i: (i, 0)),
            pl.BlockSpec((D_MODEL, D_MODEL), lambda i: (0, 0)),
            pl.BlockSpec((PEER_HEADS, 2, N_KEYS, PEER_HALF), lambda i: (0, 0, 0, 0)),
        ],
        out_specs=[big, big, tiled, tiled],
        out_shape=[big_shape, big_shape, tiled_shape, tiled_shape],
        scratch_shapes=[pltpu.VMEM((2, PEER_TOPK, tm), F32)],
        compiler_params=_params("parallel"),
        name="peer_route",
    )(xn, wq, keys)


PEER_TB = 512
PEER_EB = 1024
PEER_JCH = 64


def _peer_kernel(xn_ref, u_ref, vt_ref, th_ref, e1_ref, s2_ref, e2_ref, o_ref, a_ref, w_ref):
    @pl.when(pl.program_id(1) == 0)
    def _():
        o_ref[...] = jnp.zeros_like(o_ref)

    a_ref[...] = _dot_nt(u_ref[...], xn_ref[...])
    tb = xn_ref.shape[0]

    def per_key(ii, carry):
        row0 = pl.multiple_of(ii * N_KEYS, N_KEYS)
        th_rows = [th_ref[h, pl.ds(ii, 1), :] for h in range(PEER_HEADS)]
        e1_rows = [e1_ref[h, pl.ds(ii, 1), :] for h in range(PEER_HEADS)]
        for lt in range(tb // BLK):
            lanes = slice(lt * BLK, (lt + 1) * BLK)
            for j0 in range(0, N_KEYS, PEER_JCH):
                js = slice(j0, j0 + PEER_JCH)
                acc = None
                for h in range(PEER_HEADS):
                    keep = s2_ref[h, lt, js, :] >= th_rows[h][:, lanes]
                    gate = jnp.where(keep, e2_ref[h, lt, js, :], 0.0) * e1_rows[h][:, lanes]
                    acc = gate if acc is None else acc + gate
                rows = pl.ds(row0 + j0, PEER_JCH)
                w_ref[rows, lanes] = (acc * _gelu_exact(a_ref[rows, lanes])).astype(BF16)
        return carry

    lax.fori_loop(0, PEER_EB // N_KEYS, per_key, 0)
    o_ref[...] += _dot(vt_ref[...], w_ref[...])


def _peer(xn, u_b, vt_b, th, e1, s2, e2):
    t = xn.shape[0]
    tb = min(t, PEER_TB)
    assert tb == PEER_TB
    ni = PEER_EB // N_KEYS
    small = pl.BlockSpec((PEER_HEADS, ni, tb), lambda i, e: (0, e, i))
    big = pl.BlockSpec((PEER_HEADS, tb // BLK, N_KEYS, BLK), lambda i, e: (0, i, 0, 0))
    return pl.pallas_call(
        _peer_kernel,
        grid=(t // tb, N_EXPERTS // PEER_EB),
        in_specs=[
            pl.BlockSpec((tb, D_MODEL), lambda i, e: (i, 0)),
            pl.BlockSpec((PEER_EB, D_MODEL), lambda i, e: (e, 0)),
            pl.BlockSpec((D_MODEL, PEER_EB), lambda i, e: (0, e)),
            small, small, big, big,
        ],
        out_specs=pl.BlockSpec((D_MODEL, tb), lambda i, e: (0, i)),
        out_shape=jax.ShapeDtypeStruct((D_MODEL, t), F32),
        scratch_shapes=[pltpu.VMEM((PEER_EB, tb), F32), pltpu.VMEM((PEER_EB, tb), BF16)],
        compiler_params=_params("parallel", "arbitrary"),
        name="peer_experts",
    )(xn, u_b, vt_b, th, e1, s2, e2)


def _final_kernel(h_ref, pt_ref, g_ref, o_ref):
    h = h_ref[...] + pt_ref[...].T
    ms = jnp.mean(h * h, axis=-1, keepdims=True)
    o_ref[...] = h * lax.rsqrt(ms + EPS) * g_ref[...]


def _final(h1, peer_t, gain):
    t = h1.shape[0]
    tm = min(t, 256)
    return pl.pallas_call(
        _final_kernel,
        grid=(t // tm,),
        in_specs=[
            pl.BlockSpec((tm, D_MODEL), lambda i: (i, 0)),
            pl.BlockSpec((D_MODEL, tm), lambda i: (0, i)),
            pl.BlockSpec((1, D_MODEL), lambda i: (0, 0)),
        ],
        out_specs=pl.BlockSpec((tm, D_MODEL), lambda i: (i, 0)),
        out_shape=jax.ShapeDtypeStruct((t, D_MODEL), F32),
        compiler_params=_params("parallel"),
        name="final_norm",
    )(h1, peer_t, gain)


def _mixer(x2d, batch, seq, meta_tokens, rel_bias, ln_mix, w_in, sinks, conv_w, conv_b, dt_bias,
           a_log, d_skip, attn_norm_w, ssm_norm_w):
    nb = seq // BLK
    w_p = jnp.concatenate(
        [w_in[:, 0:1024], w_in[:, 1536:2560], w_in[:, 2560:4096], w_in[:, 1024:1280],
         w_in[:, 1280:1536], w_in[:, 4096:4112],
         jnp.zeros((D_MODEL, D_PROJ - 4112), w_in.dtype)], axis=1).astype(BF16)
    gain = ln_mix.reshape(1, D_MODEL)
    meta_pad = jnp.concatenate([jnp.zeros((BLK - N_META, D_MODEL), F32), meta_tokens.astype(F32)], axis=0)
    proj = _inproj(x2d, gain, w_p)
    proj_meta = _inproj(meta_pad, gain, w_p)

    bucket, valid = _band_tables(nb)
    tab = rel_bias.astype(F32)
    bias = jnp.full((2, ATTN_HEADS) + bucket.shape[1:], NEG, F32)
    for b in range(REL_BUCKETS):
        bias = jnp.where((valid & (bucket == b))[:, None], tab[b][None, :, None, None], bias)
    ya = _attention(proj, proj_meta, sinks.astype(F32), bias, attn_norm_w.reshape(1, D_ATTN),
                    batch, nb, F32)
    ys = _ssd(proj, proj_meta, conv_w, conv_b, dt_bias, a_log, d_skip,
                   ssm_norm_w.reshape(1, SSM_D_INNER), batch, nb, F32)
    return ya, ys


def kernel(x, meta_tokens, rel_bias, ln_mix, w_in, attn_sinks, conv_w, conv_b, dt_bias, a_log, d_skip,
           attn_norm_w, ssm_norm_w, w_out, ln_ffn, peer_wq, peer_keys, peer_u, peer_v, ln_final):
    batch, seq, _ = x.shape
    x2d = x.reshape(batch * seq, D_MODEL)
    ya, ys = _mixer(x2d, batch, seq, meta_tokens, rel_bias, ln_mix[0], w_in[0], attn_sinks[0],
                         conv_w[0], conv_b[0], dt_bias[0], a_log[0], d_skip[0], attn_norm_w[0],
                         ssm_norm_w[0])
    h1, xn = _outproj(ya, ys, x2d, w_out[0].astype(BF16), ln_ffn[0].reshape(1, D_MODEL))
    th, e1, s2, e2 = _route(xn, peer_wq[0].astype(BF16), peer_keys[0].astype(BF16))
    peer_t = _peer(xn, peer_u[0].astype(BF16), peer_v[0].T.astype(BF16), th, e1, s2, e2)
    out = _final(h1, peer_t, ln_final.reshape(1, D_MODEL))
    return out.reshape(batch, seq, D_MODEL)
```

```python
import functools

import jax
import jax.numpy as jnp
import numpy as np
from jax import lax
from jax.experimental import pallas as pl
from jax.experimental.pallas import tpu as pltpu

F32 = jnp.float32
BF16 = jnp.bfloat16

D_MODEL = 2048
N_META = 16
HEAD_DIM = 64
D_ATTN = 1024
ATTN_HEADS = 16
ATTN_KV_HEADS = 4
ATTN_GROUP = 4
D_KV = 256
WINDOW = 128
BLK = 128
REL_BUCKETS = 32
REL_MAX_DIST = 128
SSM_D_INNER = 1024
SSM_HEAD_DIM = 64
SSM_HEADS = 16
SSM_GROUPS = 2
SSM_HPG = 8
SSM_STATE = 128
CONV_WIDTH = 4
D_XBC = 1536
D_BC = 2 * SSM_GROUPS * SSM_STATE
PEER_HEADS = 8
PEER_TOPK = 16
N_KEYS = 128
N_EXPERTS = N_KEYS * N_KEYS
PEER_HALF = 128
EPS = 1e-6
NEG = -1e30

COL_Q = 0
COL_Z = 1024
COL_XS = 2048
COL_BC = 3072
COL_K = 3584
COL_V = 3840
COL_DT = 4096
D_PROJ = 4224

VMEM_LIMIT = 56 * 1024 * 1024


def _params(*sem, flags=None):
    return pltpu.CompilerParams(dimension_semantics=sem, vmem_limit_bytes=VMEM_LIMIT, flags=flags)


def _dot(a, b):
    return jnp.dot(a, b, preferred_element_type=F32)


def _dot_nt(a, b):
    return lax.dot_general(a, b, (((1,), (1,)), ((), ())), preferred_element_type=F32)


def _split3(x):
    hi = x.astype(BF16)
    r = x - hi.astype(F32)
    mid = r.astype(BF16)
    lo = (r - mid.astype(F32)).astype(BF16)
    return hi, mid, lo


def _dot01_left(m01, x):
    hi, mid, lo = _split3(x)
    return _dot(m01, hi) + _dot(m01, mid) + _dot(m01, lo)


def _dot01_right(x, m01):
    hi, mid, lo = _split3(x)
    return _dot(hi, m01) + _dot(mid, m01) + _dot(lo, m01)


def _silu(x):
    return x * jax.nn.sigmoid(x)


def _softplus(x):
    return jnp.maximum(x, 0.0) + jnp.log1p(jnp.exp(-jnp.abs(x)))


def _gelu_exact(x):
    return 0.5 * x * (1.0 + lax.erf(x * np.float32(np.sqrt(0.5))))


def _inproj_kernel(x_ref, g_ref, w_ref, o_ref, xn_ref):
    @pl.when(pl.program_id(1) == 0)
    def _():
        x = x_ref[...]
        ms = jnp.mean(x * x, axis=-1, keepdims=True)
        xn_ref[...] = (x * lax.rsqrt(ms + EPS) * g_ref[...]).astype(BF16)

    o_ref[...] = _dot(xn_ref[...], w_ref[...])


def _inproj(x2d, gain, w_p):
    m = x2d.shape[0]
    tm = min(m, 512)
    tn = D_PROJ // 3
    return pl.pallas_call(
        _inproj_kernel,
        grid=(m // tm, 3),
        in_specs=[
            pl.BlockSpec((tm, D_MODEL), lambda i, j: (i, 0)),
            pl.BlockSpec((1, D_MODEL), lambda i, j: (0, 0)),
            pl.BlockSpec((D_MODEL, tn), lambda i, j: (0, j)),
        ],
        out_specs=pl.BlockSpec((tm, tn), lambda i, j: (i, j)),
        out_shape=jax.ShapeDtypeStruct((m, D_PROJ), F32),
        scratch_shapes=[pltpu.VMEM((tm, D_MODEL), BF16)],
        compiler_params=_params("parallel", "arbitrary"),
        name="inproj",
    )(x2d, gain, w_p)


def _t5_bucket(dist):
    n = np.maximum(dist, 0)
    max_exact = REL_BUCKETS // 2
    large = max_exact + (np.log(np.maximum(n, 1) / max_exact) / np.log(REL_MAX_DIST / max_exact)
                         * (REL_BUCKETS - max_exact)).astype(np.int32)
    large = np.minimum(large, REL_BUCKETS - 1)
    return np.where(n < max_exact, n, large).astype(np.int32)


N_BAND = BLK + N_META


def _band_tables(nb):
    r = np.arange(BLK)[:, None]
    q = np.arange(BLK)[None, :]
    m = np.arange(N_META)[:, None]
    buckets, valids = [], []
    for n in range(nb):
        upper = r > q
        d_band = np.where(upper, q - r + BLK, q - r)
        d_meta = N_META + n * BLK + q - m
        assert (d_band[upper] < WINDOW).all() and (d_band >= 0).all() and (d_meta >= 0).all()
        buckets.append(_t5_bucket(np.concatenate([d_band, d_meta], axis=0)))
        valids.append(np.concatenate([~upper | (n > 0), np.ones((N_META, BLK), bool)], axis=0))
    for n in range(2, nb):
        assert (buckets[n] == buckets[1]).all() and (valids[n] == valids[1]).all()
    last = min(1, nb - 1)
    return np.stack([buckets[0], buckets[last]]), np.stack([valids[0], valids[last]])


def _attn_kernel(sink_ref, q_ref, kp_ref, ko_ref, km_ref, vp_ref, vo_ref, vmt_ref,
                 bias_ref, nw_ref, o_ref, yt_ref):
    q = (q_ref[...] * np.float32(HEAD_DIM ** -0.5)).astype(BF16)
    upper = (lax.broadcasted_iota(jnp.int32, (BLK, BLK), 0)
             > lax.broadcasted_iota(jnp.int32, (BLK, BLK), 1))
    vpt = vp_ref[...].T.astype(BF16)
    vot = vo_ref[...].T.astype(BF16)
    vmt = vmt_ref[...].astype(BF16)
    for j in range(ATTN_KV_HEADS):
        ks = slice(j * HEAD_DIM, (j + 1) * HEAD_DIM)
        heads = [j * ATTN_GROUP + g for g in range(ATTN_GROUP)]
        q4 = jnp.concatenate([q[:, h * HEAD_DIM:(h + 1) * HEAD_DIM] for h in heads], axis=0)
        lp = _dot_nt(kp_ref[:, ks].astype(BF16), q4)
        lo = _dot_nt(ko_ref[:, ks].astype(BF16), q4)
        lm = _dot_nt(km_ref[:, ks].astype(BF16), q4)
        e_prev, e_own, e_meta, inv = [], [], [], []
        for g, h in enumerate(heads):
            cols = slice(g * BLK, (g + 1) * BLK)
            band = jnp.where(upper, lp[:, cols], lo[:, cols]) + bias_ref[0, h, 0:BLK, :]
            meta = lm[:, cols] + bias_ref[0, h, BLK:N_BAND, :]
            sink = sink_ref[h]
            mx = jnp.maximum(jnp.maximum(jnp.max(band, axis=0, keepdims=True),
                                         jnp.max(meta, axis=0, keepdims=True)), sink)
            eb = jnp.exp(band - mx)
            em = jnp.exp(meta - mx)
            denom = (jnp.sum(eb, axis=0, keepdims=True) + jnp.sum(em, axis=0, keepdims=True)
                     + jnp.exp(sink - mx))
            inv.append(1.0 / denom)
            e_prev.append(jnp.where(upper, eb, 0.0).astype(BF16))
            e_own.append(jnp.where(upper, 0.0, eb).astype(BF16))
            e_meta.append(em.astype(BF16))
        cat = lambda parts: jnp.concatenate(parts, axis=1)
        ot = (_dot(vpt[ks, :], cat(e_prev)) + _dot(vot[ks, :], cat(e_own))
              + _dot(vmt[ks, :], cat(e_meta))) * cat(inv)
        for g, h in enumerate(heads):
            yt_ref[h * HEAD_DIM:(h + 1) * HEAD_DIM, :] = ot[:, g * BLK:(g + 1) * BLK]
    yt = yt_ref[...]
    ms = jnp.mean(yt * yt, axis=0, keepdims=True)
    o_ref[...] = ((yt * lax.rsqrt(ms + EPS)).T * nw_ref[...]).astype(o_ref.dtype)


def _attention(proj, proj_meta, sinks, bias, norm_w, batch, nb, ymix_dtype):
    rows = batch * nb * BLK
    kcol, vcol = COL_K // D_KV, COL_V // D_KV
    meta_blk = (BLK - N_META) // N_META
    vm_t = proj_meta[BLK - N_META:, COL_V:COL_V + D_KV].T

    def cur(col):
        return lambda b, n: (b * nb + n, col)

    def prev(col):
        return lambda b, n: (b * nb + jnp.maximum(n - 1, 0), col)

    return pl.pallas_call(
        _attn_kernel,
        grid=(batch, nb),
        in_specs=[
            pl.BlockSpec(memory_space=pltpu.SMEM),
            pl.BlockSpec((BLK, D_ATTN), cur(0)),
            pl.BlockSpec((BLK, D_KV), prev(kcol)),
            pl.BlockSpec((BLK, D_KV), cur(kcol)),
            pl.BlockSpec((N_META, D_KV), lambda b, n: (meta_blk, kcol)),
            pl.BlockSpec((BLK, D_KV), prev(vcol)),
            pl.BlockSpec((BLK, D_KV), cur(vcol)),
            pl.BlockSpec((D_KV, N_META), lambda b, n: (0, 0)),
            pl.BlockSpec((1, ATTN_HEADS, N_BAND, BLK), lambda b, n: (jnp.minimum(n, 1), 0, 0, 0)),
            pl.BlockSpec((1, D_ATTN), lambda b, n: (0, 0)),
        ],
        out_specs=pl.BlockSpec((BLK, D_ATTN), cur(0)),
        out_shape=jax.ShapeDtypeStruct((rows, D_ATTN), ymix_dtype),
        scratch_shapes=[pltpu.VMEM((D_ATTN, BLK), F32)],
        compiler_params=_params("parallel", "arbitrary"),
        name="swa_attention",
    )(sinks, proj, proj, proj, proj_meta, proj, proj, vm_t, bias, norm_w)


def _ssd_chunk(xs_raw, bc_raw, tail_xs, tail_bc, dt_raw, cw, cb, dtb, alog, row_mask, state):
    def conv(blk, tail, w, b):
        ext = jnp.concatenate([tail, blk], axis=0)
        acc = b
        for k in range(CONV_WIDTH):
            acc = acc + w[k:k + 1, :] * ext[5 + k:5 + k + BLK, :]
        return _silu(acc)

    xs = conv(xs_raw, tail_xs, cw[:, :SSM_D_INNER], cb[:, :SSM_D_INNER])
    bc = conv(bc_raw, tail_bc, cw[:, SSM_D_INNER:], cb[:, SSM_D_INNER:])
    dt = _softplus(dt_raw + dtb)
    if row_mask is not None:
        xs = jnp.where(row_mask, xs, 0.0)
        bc = jnp.where(row_mask, bc, 0.0)
        dt = jnp.where(row_mask, dt, 0.0)
    a_neg = -jnp.exp(alog)
    d_a = dt * a_neg

    r = lax.broadcasted_iota(jnp.int32, (BLK, BLK), 0)
    c = lax.broadcasted_iota(jnp.int32, (BLK, BLK), 1)
    tri = r >= c
    cs = _dot01_left(tri.astype(BF16), d_a)
    cs_t = cs.T
    hh = lax.broadcasted_iota(jnp.int32, (BLK, SSM_D_INNER), 0)
    cc = lax.broadcasted_iota(jnp.int32, (BLK, SSM_D_INNER), 1)
    expand = (cc // SSM_HEAD_DIM == hh).astype(BF16)
    dt_rep = _dot01_right(dt, expand)
    ecs_rep = _dot01_right(jnp.exp(cs), expand)
    dec_rep = _dot01_right(jnp.exp(cs[BLK - 1:BLK, :] - cs), expand)

    xdt = xs * dt_rep
    xdtd = (xdt * dec_rep).astype(BF16)
    xdt_b = xdt.astype(BF16)
    chunk_decay = ecs_rep[BLK - 1:BLK, :]

    y_parts, new_state = [], []
    for g in range(SSM_GROUPS):
        b_g = bc[:, g * SSM_STATE:(g + 1) * SSM_STATE]
        c_g = bc[:, (SSM_GROUPS + g) * SSM_STATE:(SSM_GROUPS + g + 1) * SSM_STATE]
        cols = slice(g * 512, (g + 1) * 512)
        cb_g = _dot_nt(c_g.astype(BF16), b_g.astype(BF16))
        y_off = _dot(c_g.astype(BF16), state[g].astype(BF16)) * ecs_rep[:, cols]
        new_state.append(chunk_decay[:, cols] * state[g] + _dot(b_g.T.astype(BF16), xdtd[:, cols]))
        y_diag = []
        for hp in range(SSM_HPG):
            h = g * SSM_HPG + hp
            seg = cs[:, h:h + 1] - cs_t[h:h + 1, :]
            lmat = jnp.exp(jnp.where(tri, seg, -jnp.inf))
            m = (cb_g * lmat).astype(BF16)
            y_diag.append(_dot(m, xdt_b[:, h * SSM_HEAD_DIM:(h + 1) * SSM_HEAD_DIM]))
        y_parts.append(jnp.concatenate(y_diag, axis=1) + y_off)
    return jnp.concatenate(y_parts, axis=1), xs, new_state


def _ssd_meta_kernel(xs_ref, bc_ref, dt_ref, cw_ref, cb_ref, dtb_ref, alog_ref, st_ref):
    rows = lax.broadcasted_iota(jnp.int32, (BLK, 1), 0)
    zero_state = [jnp.zeros((SSM_STATE, 512), F32) for _ in range(SSM_GROUPS)]
    _, _, st = _ssd_chunk(xs_ref[...], bc_ref[...], jnp.zeros((8, SSM_D_INNER), F32),
                          jnp.zeros((8, D_BC), F32), dt_ref[...], cw_ref[...], cb_ref[...],
                          dtb_ref[...], alog_ref[...], rows >= BLK - N_META, zero_state)
    for g in range(SSM_GROUPS):
        st_ref[g] = st[g]


def _ssd_kernel(xs_ref, bc_ref, z_ref, dt_ref, txs_ref, tbc_ref, mxs_ref, mbc_ref, st0_ref,
                cw_ref, cb_ref, dtb_ref, alog_ref, dsk_ref, nw_ref, o_ref, st_ref):
    first = pl.program_id(1) == 0

    @pl.when(first)
    def _():
        st_ref[...] = st0_ref[...]

    tail_xs = jnp.where(first, mxs_ref[...], txs_ref[...])
    tail_bc = jnp.where(first, mbc_ref[...], tbc_ref[...])
    state = [st_ref[g] for g in range(SSM_GROUPS)]
    y, xs, new_state = _ssd_chunk(xs_ref[...], bc_ref[...], tail_xs, tail_bc, dt_ref[...],
                                  cw_ref[...], cb_ref[...], dtb_ref[...], alog_ref[...], None, state)
    for g in range(SSM_GROUPS):
        st_ref[g] = new_state[g]
    y = y + xs * dsk_ref[...]
    yg = y * _silu(z_ref[...])
    outs = []
    for g in range(SSM_GROUPS):
        part = yg[:, g * 512:(g + 1) * 512]
        ms = jnp.mean(part * part, axis=-1, keepdims=True)
        outs.append(part * lax.rsqrt(ms + EPS))
    o_ref[...] = (jnp.concatenate(outs, axis=1) * nw_ref[...]).astype(o_ref.dtype)


def _pad_lanes(v, n=BLK):
    v = v.reshape(1, -1)
    return jnp.pad(v, ((0, 0), (0, n - v.shape[1])))


def _ssd(proj, proj_meta, conv_w, conv_b, dt_bias, a_log, d_skip, norm_w, batch, nc, ymix_dtype):
    rows = batch * nc * BLK
    cb = conv_b.reshape(1, D_XBC)
    dtb, alog = _pad_lanes(dt_bias), _pad_lanes(a_log)
    dsk = jnp.repeat(d_skip, SSM_HEAD_DIM).reshape(1, SSM_D_INNER)
    xs_c, z_c, bc_c, dt_c = COL_XS // 1024, COL_Z // 1024, COL_BC // D_BC, COL_DT // BLK
    full = lambda shape: pl.BlockSpec(shape, lambda *_: (0,) * len(shape))

    state0 = pl.pallas_call(
        _ssd_meta_kernel,
        grid=(1,),
        in_specs=[
            pl.BlockSpec((BLK, SSM_D_INNER), lambda i: (0, xs_c)),
            pl.BlockSpec((BLK, D_BC), lambda i: (0, bc_c)),
            pl.BlockSpec((BLK, BLK), lambda i: (0, dt_c)),
            full((CONV_WIDTH, D_XBC)), full((1, D_XBC)), full((1, BLK)), full((1, BLK)),
        ],
        out_specs=full((SSM_GROUPS, SSM_STATE, 512)),
        out_shape=jax.ShapeDtypeStruct((SSM_GROUPS, SSM_STATE, 512), F32),
        compiler_params=_params("arbitrary"),
        name="ssd_meta_state",
    )(proj_meta, proj_meta, proj_meta, conv_w, cb, dtb, alog)

    def cur(col):
        return lambda b, c: (b * nc + c, col)

    def tail(col):
        return lambda b, c: (jnp.maximum((b * nc + c) * (BLK // 8) - 1, 0), col)

    return pl.pallas_call(
        _ssd_kernel,
        grid=(batch, nc),
        in_specs=[
            pl.BlockSpec((BLK, SSM_D_INNER), cur(xs_c)),
            pl.BlockSpec((BLK, D_BC), cur(bc_c)),
            pl.BlockSpec((BLK, SSM_D_INNER), cur(z_c)),
            pl.BlockSpec((BLK, BLK), cur(dt_c)),
            pl.BlockSpec((8, SSM_D_INNER), tail(xs_c)),
            pl.BlockSpec((8, D_BC), tail(bc_c)),
            pl.BlockSpec((8, SSM_D_INNER), lambda b, c: (BLK // 8 - 1, xs_c)),
            pl.BlockSpec((8, D_BC), lambda b, c: (BLK // 8 - 1, bc_c)),
            full((SSM_GROUPS, SSM_STATE, 512)),
            full((CONV_WIDTH, D_XBC)), full((1, D_XBC)), full((1, BLK)), full((1, BLK)),
            full((1, SSM_D_INNER)), full((1, SSM_D_INNER)),
        ],
        out_specs=pl.BlockSpec((BLK, SSM_D_INNER), cur(0)),
        out_shape=jax.ShapeDtypeStruct((rows, SSM_D_INNER), ymix_dtype),
        scratch_shapes=[pltpu.VMEM((SSM_GROUPS, SSM_STATE, 512), F32)],
        compiler_params=_params("parallel", "arbitrary"),
        name="ssd_mixer",
    )(proj, proj, proj, proj, proj, proj, proj_meta, proj_meta, state0,
      conv_w, cb, dtb, alog, dsk, norm_w)


def _outproj_kernel(ya_ref, ys_ref, x_ref, w_ref, g_ref, h_ref, xn_ref):
    y = jnp.concatenate([ya_ref[...], ys_ref[...]], axis=1).astype(BF16)
    h = x_ref[...] + _dot(y, w_ref[...])
    h_ref[...] = h
    ms = jnp.mean(h * h, axis=-1, keepdims=True)
    xn_ref[...] = (h * lax.rsqrt(ms + EPS) * g_ref[...]).astype(BF16)


def _outproj(ya, ys, x2d, w_out, gain):
    m = x2d.shape[0]
    tm = min(m, 256)
    return pl.pallas_call(
        _outproj_kernel,
        grid=(m // tm,),
        in_specs=[
            pl.BlockSpec((tm, D_ATTN), lambda i: (i, 0)),
            pl.BlockSpec((tm, SSM_D_INNER), lambda i: (i, 0)),
            pl.BlockSpec((tm, D_MODEL), lambda i: (i, 0)),
            pl.BlockSpec((D_MODEL, D_MODEL), lambda i: (0, 0)),
            pl.BlockSpec((1, D_MODEL), lambda i: (0, 0)),
        ],
        out_specs=[pl.BlockSpec((tm, D_MODEL), lambda i: (i, 0)),
                   pl.BlockSpec((tm, D_MODEL), lambda i: (i, 0))],
        out_shape=[jax.ShapeDtypeStruct((m, D_MODEL), F32),
                   jax.ShapeDtypeStruct((m, D_MODEL), BF16)],
        compiler_params=_params("parallel"),
        name="outproj",
    )(ya, ys, x2d, w_out, gain)


def _top16_rows(s):
    vals = []
    for r in range(PEER_TOPK):
        m = jnp.max(s, axis=0, keepdims=True)
        vals.append(m)
        if r < PEER_TOPK - 1:
            s = jnp.where(s == m, -jnp.inf, s)
    return vals


def _route_kernel(xn_ref, wq_ref, keys_ref, th_ref, e1_ref, s2_ref, e2_ref, v_ref):
    q = _dot(xn_ref[...], wq_ref[...]).astype(BF16)
    for h in range(PEER_HEADS):
        s, tops = [], []
        for c in range(2):
            qs = q[:, (2 * h + c) * PEER_HALF:(2 * h + c + 1) * PEER_HALF]
            sc = _dot_nt(keys_ref[h, c], qs)
            s.append(sc)
            for r, m in enumerate(_top16_rows(sc)):
                v_ref[c, r:r + 1, :] = m
            tops.append(v_ref[c])
        v1, v2 = tops
        blocks = [v1[0:1] + v2, v1[1:2] + v2[0:8]]
        blocks += [v1[a:a + 1] + v2[0:8] for a in range(2, 8)]
        blocks.append(v1[8:16] + v2[0:1])
        cand = jnp.concatenate(blocks, axis=0)
        top = v1[0:1] + v2[0:1]
        rem = cand
        for _ in range(PEER_TOPK - 1):
            m = jnp.max(rem, axis=0, keepdims=True)
            rem = jnp.where(rem == m, -jnp.inf, rem)
        tau = jnp.max(rem, axis=0, keepdims=True)
        z = jnp.sum(jnp.where(cand >= tau, jnp.exp(cand - top), 0.0), axis=0, keepdims=True)
        widths = [PEER_TOPK, 8] + [8] * 6
        thr = [jnp.min(jnp.where(blocks[a] >= tau, v2[0:widths[a]], jnp.inf), axis=0, keepdims=True)
               for a in range(8)]
        thr_hi = jnp.where(blocks[8] >= tau, v2[0:1], jnp.inf)
        theta = jnp.full_like(s[0], jnp.inf)
        for a in range(PEER_TOPK):
            t_a = thr[a] if a < 8 else thr_hi[a - 8:a - 7]
            theta = jnp.where(s[0] == v1[a:a + 1], t_a, theta)
        th_ref[h] = theta
        e1_ref[h] = jnp.exp(s[0] - v1[0:1]) / z
        e2 = jnp.exp(s[1] - v2[0:1])
        for lt in range(s[1].shape[1] // BLK):
            s2_ref[h, lt] = s[1][:, lt * BLK:(lt + 1) * BLK]
            e2_ref[h, lt] = e2[:, lt * BLK:(lt + 1) * BLK]


def _route(xn, wq, keys):
    t = xn.shape[0]
    tm = min(t, 256)
    big = pl.BlockSpec((PEER_HEADS, N_KEYS, tm), lambda i: (0, 0, i))
    big_shape = jax.ShapeDtypeStruct((PEER_HEADS, N_KEYS, t), F32)
    tiled = pl.BlockSpec((PEER_HEADS, tm // BLK, N_KEYS, BLK), lambda i: (0, i, 0, 0))
    tiled_shape = jax.ShapeDtypeStruct((PEER_HEADS, t // BLK, N_KEYS, BLK), F32)
    return pl.pallas_call(
        _route_kernel,
        grid=(t // tm,),
        in_specs=[
            pl.BlockSpec((tm, D_MODEL), lambda i: (i, 0)),
            pl.BlockSpec((D_MODEL, D_MODEL), lambda i: (0, 0)),
            pl.BlockSpec((PEER_HEADS, 2, N_KEYS, PEER_HALF), lambda i: (0, 0, 0, 0)),
        ],
        out_specs=[big, big, tiled, tiled],
        out_shape=[big_shape, big_shape, tiled_shape, tiled_shape],
        scratch_shapes=[pltpu.VMEM((2, PEER_TOPK, tm), F32)],
        compiler_params=_params("parallel"),
        name="peer_route",
    )(xn, wq, keys)


PEER_TB = 512
PEER_EB = 1024
PEER_JCH = 64


def _peer_kernel(xn_ref, u_ref, vt_ref, th_ref, e1_ref, s2_ref, e2_ref, o_ref, a_ref, w_ref):
    @pl.when(pl.program_id(1) == 0)
    def _():
        o_ref[...] = jnp.zeros_like(o_ref)

    a_ref[...] = _dot_nt(u_ref[...], xn_ref[...])
    tb = xn_ref.shape[0]

    def per_key(ii, carry):
        row0 = pl.multiple_of(ii * N_KEYS, N_KEYS)
        th_rows = [th_ref[h, pl.ds(ii, 1), :] for h in range(PEER_HEADS)]
        e1_rows = [e1_ref[h, pl.ds(ii, 1), :] for h in range(PEER_HEADS)]
        for lt in range(tb // BLK):
            lanes = slice(lt * BLK, (lt + 1) * BLK)
            for j0 in range(0, N_KEYS, PEER_JCH):
                js = slice(j0, j0 + PEER_JCH)
                acc = None
                for h in range(PEER_HEADS):
                    keep = s2_ref[h, lt, js, :] >= th_rows[h][:, lanes]
                    gate = jnp.where(keep, e2_ref[h, lt, js, :], 0.0) * e1_rows[h][:, lanes]
                    acc = gate if acc is None else acc + gate
                rows = pl.ds(row0 + j0, PEER_JCH)
                w_ref[rows, lanes] = (acc * _gelu_exact(a_ref[rows, lanes])).astype(BF16)
        return carry

    lax.fori_loop(0, PEER_EB // N_KEYS, per_key, 0)
    o_ref[...] += _dot(vt_ref[...], w_ref[...])


def _peer(xn, u_b, vt_b, th, e1, s2, e2):
    t = xn.shape[0]
    tb = min(t, PEER_TB)
    assert tb == PEER_TB
    ni = PEER_EB // N_KEYS
    small = pl.BlockSpec((PEER_HEADS, ni, tb), lambda i, e: (0, e, i))
    big = pl.BlockSpec((PEER_HEADS, tb // BLK, N_KEYS, BLK), lambda i, e: (0, i, 0, 0))
    return pl.pallas_call(
        _peer_kernel,
        grid=(t // tb, N_EXPERTS // PEER_EB),
        in_specs=[
            pl.BlockSpec((tb, D_MODEL), lambda i, e: (i, 0)),
            pl.BlockSpec((PEER_EB, D_MODEL), lambda i, e: (e, 0)),
            pl.BlockSpec((D_MODEL, PEER_EB), lambda i, e: (0, e)),
            small, small, big, big,
        ],
        out_specs=pl.BlockSpec((D_MODEL, tb), lambda i, e: (0, i)),
        out_shape=jax.ShapeDtypeStruct((D_MODEL, t), F32),
        scratch_shapes=[pltpu.VMEM((PEER_EB, tb), F32), pltpu.VMEM((PEER_EB, tb), BF16)],
        compiler_params=_params("parallel", "arbitrary"),
        name="peer_experts",
    )(xn, u_b, vt_b, th, e1, s2, e2)


def _final_kernel(h_ref, pt_ref, g_ref, o_ref):
    h = h_ref[...] + pt_ref[...].T
    ms = jnp.mean(h * h, axis=-1, keepdims=True)
    o_ref[...] = h * lax.rsqrt(ms + EPS) * g_ref[...]


def _final(h1, peer_t, gain):
    t = h1.shape[0]
    tm = min(t, 256)
    return pl.pallas_call(
        _final_kernel,
        grid=(t // tm,),
        in_specs=[
            pl.BlockSpec((tm, D_MODEL), lambda i: (i, 0)),
            pl.BlockSpec((D_MODEL, tm), lambda i: (0, i)),
            pl.BlockSpec((1, D_MODEL), lambda i: (0, 0)),
        ],
        out_specs=pl.BlockSpec((tm, D_MODEL), lambda i: (i, 0)),
        out_shape=jax.ShapeDtypeStruct((t, D_MODEL), F32),
        compiler_params=_params("parallel"),
        name="final_norm",
    )(h1, peer_t, gain)


def _mixer(x2d, batch, seq, meta_tokens, rel_bias, ln_mix, w_in, sinks, conv_w, conv_b, dt_bias,
           a_log, d_skip, attn_norm_w, ssm_norm_w):
    nb = seq // BLK
    w_p = jnp.concatenate(
        [w_in[:, 0:1024], w_in[:, 1536:2560], w_in[:, 2560:4096], w_in[:, 1024:1280],
         w_in[:, 1280:1536], w_in[:, 4096:4112],
         jnp.zeros((D_MODEL, D_PROJ - 4112), w_in.dtype)], axis=1).astype(BF16)
    gain = ln_mix.reshape(1, D_MODEL)
    meta_pad = jnp.concatenate([jnp.zeros((BLK - N_META, D_MODEL), F32), meta_tokens.astype(F32)], axis=0)
    proj = _inproj(x2d, gain, w_p)
    proj_meta = _inproj(meta_pad, gain, w_p)

    bucket, valid = _band_tables(nb)
    tab = rel_bias.astype(F32)
    bias = jnp.full((2, ATTN_HEADS) + bucket.shape[1:], NEG, F32)
    for b in range(REL_BUCKETS):
        bias = jnp.where((valid & (bucket == b))[:, None], tab[b][None, :, None, None], bias)
    ya = _attention(proj, proj_meta, sinks.astype(F32), bias, attn_norm_w.reshape(1, D_ATTN),
                    batch, nb, F32)
    ys = _ssd(proj, proj_meta, conv_w, conv_b, dt_bias, a_log, d_skip,
                   ssm_norm_w.reshape(1, SSM_D_INNER), batch, nb, F32)
    return ya, ys


def kernel(x, meta_tokens, rel_bias, ln_mix, w_in, attn_sinks, conv_w, conv_b, dt_bias, a_log, d_skip,
           attn_norm_w, ssm_norm_w, w_out, ln_ffn, peer_wq, peer_keys, peer_u, peer_v, ln_final):
    batch, seq, _ = x.shape
    x2d = x.reshape(batch * seq, D_MODEL)
    ya, ys = _mixer(x2d, batch, seq, meta_tokens, rel_bias, ln_mix[0], w_in[0], attn_sinks[0],
                         conv_w[0], conv_b[0], dt_bias[0], a_log[0], d_skip[0], attn_norm_w[0],
                         ssm_norm_w[0])
    h1, xn = _outproj(ya, ys, x2d, w_out[0].astype(BF16), ln_ffn[0].reshape(1, D_MODEL))
    th, e1, s2, e2 = _route(xn, peer_wq[0].astype(BF16), peer_keys[0].astype(BF16))
    peer_t = _peer(xn, peer_u[0].astype(BF16), peer_v[0].T.astype(BF16), th, e1, s2, e2)
    out = _final(h1, peer_t, ln_final.reshape(1, D_MODEL))
    return out.reshape(batch, seq, D_MODEL)
```

```python
import functools

import jax
import jax.numpy as jnp
import numpy as np
from jax import lax
from jax.experimental import pallas as pl
from jax.experimental.pallas import tpu as pltpu

F32 = jnp.float32
BF16 = jnp.bfloat16

D_MODEL = 2048
N_META = 16
HEAD_DIM = 64
D_ATTN = 1024
ATTN_HEADS = 16
ATTN_KV_HEADS = 4
ATTN_GROUP = 4
D_KV = 256
WINDOW = 128
BLK = 128
REL_BUCKETS = 32
REL_MAX_DIST = 128
SSM_D_INNER = 1024
SSM_HEAD_DIM = 64
SSM_HEADS = 16
SSM_GROUPS = 2
SSM_HPG = 8
SSM_STATE = 128
CONV_WIDTH = 4
D_XBC = 1536
D_BC = 2 * SSM_GROUPS * SSM_STATE
PEER_HEADS = 8
PEER_TOPK = 16
N_KEYS = 128
N_EXPERTS = N_KEYS * N_KEYS
PEER_HALF = 128
EPS = 1e-6
NEG = -1e30

COL_Q = 0
COL_Z = 1024
COL_XS = 2048
COL_BC = 3072
COL_K = 3584
COL_V = 3840
COL_DT = 4096
D_PROJ = 4224

VMEM_LIMIT = 56 * 1024 * 1024


def _params(*sem, flags=None):
    return pltpu.CompilerParams(dimension_semantics=sem, vmem_limit_bytes=VMEM_LIMIT, flags=flags)


def _dot(a, b):
    return jnp.dot(a, b, preferred_element_type=F32)


def _dot_nt(a, b):
    return lax.dot_general(a, b, (((1,), (1,)), ((), ())), preferred_element_type=F32)


def _split3(x):
    hi = x.astype(BF16)
    r = x - hi.astype(F32)
    mid = r.astype(BF16)
    lo = (r - mid.astype(F32)).astype(BF16)
    return hi, mid, lo


def _dot01_left(m01, x):
    hi, mid, lo = _split3(x)
    return _dot(m01, hi) + _dot(m01, mid) + _dot(m01, lo)


def _dot01_right(x, m01):
    hi, mid, lo = _split3(x)
    return _dot(hi, m01) + _dot(mid, m01) + _dot(lo, m01)


def _silu(x):
    return x * jax.nn.sigmoid(x)


def _softplus(x):
    return jnp.maximum(x, 0.0) + jnp.log1p(jnp.exp(-jnp.abs(x)))


def _gelu_exact(x):
    return 0.5 * x * (1.0 + lax.erf(x * np.float32(np.sqrt(0.5))))


def _inproj_kernel(x_ref, g_ref, w_ref, o_ref, xn_ref):
    @pl.when(pl.program_id(1) == 0)
    def _():
        x = x_ref[...]
        ms = jnp.mean(x * x, axis=-1, keepdims=True)
        xn_ref[...] = (x * lax.rsqrt(ms + EPS) * g_ref[...]).astype(BF16)

    o_ref[...] = _dot(xn_ref[...], w_ref[...])


def _inproj(x2d, gain, w_p):
    m = x2d.shape[0]
    tm = min(m, 512)
    tn = D_PROJ // 3
    return pl.pallas_call(
        _inproj_kernel,
        grid=(m // tm, 3),
        in_specs=[
            pl.BlockSpec((tm, D_MODEL), lambda i, j: (i, 0)),
            pl.BlockSpec((1, D_MODEL), lambda i, j: (0, 0)),
            pl.BlockSpec((D_MODEL, tn), lambda i, j: (0, j)),
        ],
        out_specs=pl.BlockSpec((tm, tn), lambda i, j: (i, j)),
        out_shape=jax.ShapeDtypeStruct((m, D_PROJ), F32),
        scratch_shapes=[pltpu.VMEM((tm, D_MODEL), BF16)],
        compiler_params=_params("parallel", "arbitrary"),
        name="inproj",
    )(x2d, gain, w_p)


def _t5_bucket(dist):
    n = np.maximum(dist, 0)
    max_exact = REL_BUCKETS // 2
    large = max_exact + (np.log(np.maximum(n, 1) / max_exact) / np.log(REL_MAX_DIST / max_exact)
                         * (REL_BUCKETS - max_exact)).astype(np.int32)
    large = np.minimum(large, REL_BUCKETS - 1)
    return np.where(n < max_exact, n, large).astype(np.int32)


N_BAND = BLK + N_META


def _band_tables(nb):
    r = np.arange(BLK)[:, None]
    q = np.arange(BLK)[None, :]
    m = np.arange(N_META)[:, None]
    buckets, valids = [], []
    for n in range(nb):
        upper = r > q
        d_band = np.where(upper, q - r + BLK, q - r)
        d_meta = N_META + n * BLK + q - m
        assert (d_band[upper] < WINDOW).all() and (d_band >= 0).all() and (d_meta >= 0).all()
        buckets.append(_t5_bucket(np.concatenate([d_band, d_meta], axis=0)))
        valids.append(np.concatenate([~upper | (n > 0), np.ones((N_META, BLK), bool)], axis=0))
    for n in range(2, nb):
        assert (buckets[n] == buckets[1]).all() and (valids[n] == valids[1]).all()
    last = min(1, nb - 1)
    return np.stack([buckets[0], buckets[last]]), np.stack([valids[0], valids[last]])


def _attn_kernel(sink_ref, q_ref, kp_ref, ko_ref, km_ref, vp_ref, vo_ref, vmt_ref,
                 bias_ref, nw_ref, o_ref, yt_ref):
    q = (q_ref[...] * np.float32(HEAD_DIM ** -0.5)).astype(BF16)
    upper = (lax.broadcasted_iota(jnp.int32, (BLK, BLK), 0)
             > lax.broadcasted_iota(jnp.int32, (BLK, BLK), 1))
    vpt = vp_ref[...].T.astype(BF16)
    vot = vo_ref[...].T.astype(BF16)
    vmt = vmt_ref[...].astype(BF16)
    for j in range(ATTN_KV_HEADS):
        ks = slice(j * HEAD_DIM, (j + 1) * HEAD_DIM)
        heads = [j * ATTN_GROUP + g for g in range(ATTN_GROUP)]
        q4 = jnp.concatenate([q[:, h * HEAD_DIM:(h + 1) * HEAD_DIM] for h in heads], axis=0)
        lp = _dot_nt(kp_ref[:, ks].astype(BF16), q4)
        lo = _dot_nt(ko_ref[:, ks].astype(BF16), q4)
        lm = _dot_nt(km_ref[:, ks].astype(BF16), q4)
        e_prev, e_own, e_meta, inv = [], [], [], []
        for g, h in enumerate(heads):
            cols = slice(g * BLK, (g + 1) * BLK)
            band = jnp.where(upper, lp[:, cols], lo[:, cols]) + bias_ref[0, h, 0:BLK, :]
            meta = lm[:, cols] + bias_ref[0, h, BLK:N_BAND, :]
            sink = sink_ref[h]
            mx = jnp.maximum(jnp.maximum(jnp.max(band, axis=0, keepdims=True),
                                         jnp.max(meta, axis=0, keepdims=True)), sink)
            eb = jnp.exp(band - mx)
            em = jnp.exp(meta - mx)
            denom = (jnp.sum(eb, axis=0, keepdims=True) + jnp.sum(em, axis=0, keepdims=True)
                     + jnp.exp(sink - mx))
            inv.append(1.0 / denom)
            e_prev.append(jnp.where(upper, eb, 0.0).astype(BF16))
            e_own.append(jnp.where(upper, 0.0, eb).astype(BF16))
            e_meta.append(em.astype(BF16))
        cat = lambda parts: jnp.concatenate(parts, axis=1)
        ot = (_dot(vpt[ks, :], cat(e_prev)) + _dot(vot[ks, :], cat(e_own))
              + _dot(vmt[ks, :], cat(e_meta))) * cat(inv)
        for g, h in enumerate(heads):
            yt_ref[h * HEAD_DIM:(h + 1) * HEAD_DIM, :] = ot[:, g * BLK:(g + 1) * BLK]
    yt = yt_ref[...]
    ms = jnp.mean(yt * yt, axis=0, keepdims=True)
    o_ref[...] = ((yt * lax.rsqrt(ms + EPS)).T * nw_ref[...]).astype(o_ref.dtype)


def _attention(proj, proj_meta, sinks, bias, norm_w, batch, nb, ymix_dtype):
    rows = batch * nb * BLK
    kcol, vcol = COL_K // D_KV, COL_V // D_KV
    meta_blk = (BLK - N_META) // N_META
    vm_t = proj_meta[BLK - N_META:, COL_V:COL_V + D_KV].T

    def cur(col):
        return lambda b, n: (b * nb + n, col)

    def prev(col):
        return lambda b, n: (b * nb + jnp.maximum(n - 1, 0), col)

    return pl.pallas_call(
        _attn_kernel,
        grid=(batch, nb),
        in_specs=[
            pl.BlockSpec(memory_space=pltpu.SMEM),
            pl.BlockSpec((BLK, D_ATTN), cur(0)),
            pl.BlockSpec((BLK, D_KV), prev(kcol)),
            pl.BlockSpec((BLK, D_KV), cur(kcol)),
            pl.BlockSpec((N_META, D_KV), lambda b, n: (meta_blk, kcol)),
            pl.BlockSpec((BLK, D_KV), prev(vcol)),
            pl.BlockSpec((BLK, D_KV), cur(vcol)),
            pl.BlockSpec((D_KV, N_META), lambda b, n: (0, 0)),
            pl.BlockSpec((1, ATTN_HEADS, N_BAND, BLK), lambda b, n: (jnp.minimum(n, 1), 0, 0, 0)),
            pl.BlockSpec((1, D_ATTN), lambda b, n: (0, 0)),
        ],
        out_specs=pl.BlockSpec((BLK, D_ATTN), cur(0)),
        out_shape=jax.ShapeDtypeStruct((rows, D_ATTN), ymix_dtype),
        scratch_shapes=[pltpu.VMEM((D_ATTN, BLK), F32)],
        compiler_params=_params("parallel", "arbitrary"),
        name="swa_attention",
    )(sinks, proj, proj, proj, proj_meta, proj, proj, vm_t, bias, norm_w)


def _ssd_chunk(xs_raw, bc_raw, tail_xs, tail_bc, dt_raw, cw, cb, dtb, alog, row_mask, state):
    def conv(blk, tail, w, b):
        ext = jnp.concatenate([tail, blk], axis=0)
        acc = b
        for k in range(CONV_WIDTH):
            acc = acc + w[k:k + 1, :] * ext[5 + k:5 + k + BLK, :]
        return _silu(acc)

    xs = conv(xs_raw, tail_xs, cw[:, :SSM_D_INNER], cb[:, :SSM_D_INNER])
    bc = conv(bc_raw, tail_bc, cw[:, SSM_D_INNER:], cb[:, SSM_D_INNER:])
    dt = _softplus(dt_raw + dtb)
    if row_mask is not None:
        xs = jnp.where(row_mask, xs, 0.0)
        bc = jnp.where(row_mask, bc, 0.0)
        dt = jnp.where(row_mask, dt, 0.0)
    a_neg = -jnp.exp(alog)
    d_a = dt * a_neg

    r = lax.broadcasted_iota(jnp.int32, (BLK, BLK), 0)
    c = lax.broadcasted_iota(jnp.int32, (BLK, BLK), 1)
    tri = r >= c
    cs = _dot01_left(tri.astype(BF16), d_a)
    cs_t = cs.T
    hh = lax.broadcasted_iota(jnp.int32, (BLK, SSM_D_INNER), 0)
    cc = lax.broadcasted_iota(jnp.int32, (BLK, SSM_D_INNER), 1)
    expand = (cc // SSM_HEAD_DIM == hh).astype(BF16)
    dt_rep = _dot01_right(dt, expand)
    ecs_rep = _dot01_right(jnp.exp(cs), expand)
    dec_rep = _dot01_right(jnp.exp(cs[BLK - 1:BLK, :] - cs), expand)

    xdt = xs * dt_rep
    xdtd = (xdt * dec_rep).astype(BF16)
    xdt_b = xdt.astype(BF16)
    chunk_decay = ecs_rep[BLK - 1:BLK, :]

    y_parts, new_state = [], []
    for g in range(SSM_GROUPS):
        b_g = bc[:, g * SSM_STATE:(g + 1) * SSM_STATE]
        c_g = bc[:, (SSM_GROUPS + g) * SSM_STATE:(SSM_GROUPS + g + 1) * SSM_STATE]
        cols = slice(g * 512, (g + 1) * 512)
        cb_g = _dot_nt(c_g.astype(BF16), b_g.astype(BF16))
        y_off = _dot(c_g.astype(BF16), state[g].astype(BF16)) * ecs_rep[:, cols]
        new_state.append(chunk_decay[:, cols] * state[g] + _dot(b_g.T.astype(BF16), xdtd[:, cols]))
        y_diag = []
        for hp in range(SSM_HPG):
            h = g * SSM_HPG + hp
            seg = cs[:, h:h + 1] - cs_t[h:h + 1, :]
            lmat = jnp.exp(jnp.where(tri, seg, -jnp.inf))
            m = (cb_g * lmat).astype(BF16)
            y_diag.append(_dot(m, xdt_b[:, h * SSM_HEAD_DIM:(h + 1) * SSM_HEAD_DIM]))
        y_parts.append(jnp.concatenate(y_diag, axis=1) + y_off)
    return jnp.concatenate(y_parts, axis=1), xs, new_state


def _ssd_meta_kernel(xs_ref, bc_ref, dt_ref, cw_ref, cb_ref, dtb_ref, alog_ref, st_ref):
    rows = lax.broadcasted_iota(jnp.int32, (BLK, 1), 0)
    zero_state = [jnp.zeros((SSM_STATE, 512), F32) for _ in range(SSM_GROUPS)]
    _, _, st = _ssd_chunk(xs_ref[...], bc_ref[...], jnp.zeros((8, SSM_D_INNER), F32),
                          jnp.zeros((8, D_BC), F32), dt_ref[...], cw_ref[...], cb_ref[...],
                          dtb_ref[...], alog_ref[...], rows >= BLK - N_META, zero_state)
    for g in range(SSM_GROUPS):
        st_ref[g] = st[g]


def _ssd_kernel(xs_ref, bc_ref, z_ref, dt_ref, txs_ref, tbc_ref, mxs_ref, mbc_ref, st0_ref,
                cw_ref, cb_ref, dtb_ref, alog_ref, dsk_ref, nw_ref, o_ref, st_ref):
    first = pl.program_id(1) == 0

    @pl.when(first)
    def _():
        st_ref[...] = st0_ref[...]

    tail_xs = jnp.where(first, mxs_ref[...], txs_ref[...])
    tail_bc = jnp.where(first, mbc_ref[...], tbc_ref[...])
    state = [st_ref[g] for g in range(SSM_GROUPS)]
    y, xs, new_state = _ssd_chunk(xs_ref[...], bc_ref[...], tail_xs, tail_bc, dt_ref[...],
                                  cw_ref[...], cb_ref[...], dtb_ref[...], alog_ref[...], None, state)
    for g in range(SSM_GROUPS):
        st_ref[g] = new_state[g]
    y = y + xs * dsk_ref[...]
    yg = y * _silu(z_ref[...])
    outs = []
    for g in range(SSM_GROUPS):
        part = yg[:, g * 512:(g + 1) * 512]
        ms = jnp.mean(part * part, axis=-1, keepdims=True)
        outs.append(part * lax.rsqrt(ms + EPS))
    o_ref[...] = (jnp.concatenate(outs, axis=1) * nw_ref[...]).astype(o_ref.dtype)


def _pad_lanes(v, n=BLK):
    v = v.reshape(1, -1)
    return jnp.pad(v, ((0, 0), (0, n - v.shape[1])))


def _ssd(proj, proj_meta, conv_w, conv_b, dt_bias, a_log, d_skip, norm_w, batch, nc, ymix_dtype):
    rows = batch * nc * BLK
    cb = conv_b.reshape(1, D_XBC)
    dtb, alog = _pad_lanes(dt_bias), _pad_lanes(a_log)
    dsk = jnp.repeat(d_skip, SSM_HEAD_DIM).reshape(1, SSM_D_INNER)
    xs_c, z_c, bc_c, dt_c = COL_XS // 1024, COL_Z // 1024, COL_BC // D_BC, COL_DT // BLK
    full = lambda shape: pl.BlockSpec(shape, lambda *_: (0,) * len(shape))

    state0 = pl.pallas_call(
        _ssd_meta_kernel,
        grid=(1,),
        in_specs=[
            pl.BlockSpec((BLK, SSM_D_INNER), lambda i: (0, xs_c)),
            pl.BlockSpec((BLK, D_BC), lambda i: (0, bc_c)),
            pl.BlockSpec((BLK, BLK), lambda i: (0, dt_c)),
            full((CONV_WIDTH, D_XBC)), full((1, D_XBC)), full((1, BLK)), full((1, BLK)),
        ],
        out_specs=full((SSM_GROUPS, SSM_STATE, 512)),
        out_shape=jax.ShapeDtypeStruct((SSM_GROUPS, SSM_STATE, 512), F32),
        compiler_params=_params("arbitrary"),
        name="ssd_meta_state",
    )(proj_meta, proj_meta, proj_meta, conv_w, cb, dtb, alog)

    def cur(col):
        return lambda b, c: (b * nc + c, col)

    def tail(col):
        return lambda b, c: (jnp.maximum((b * nc + c) * (BLK // 8) - 1, 0), col)

    return pl.pallas_call(
        _ssd_kernel,
        grid=(batch, nc),
        in_specs=[
            pl.BlockSpec((BLK, SSM_D_INNER), cur(xs_c)),
            pl.BlockSpec((BLK, D_BC), cur(bc_c)),
            pl.BlockSpec((BLK, SSM_D_INNER), cur(z_c)),
            pl.BlockSpec((BLK, BLK), cur(dt_c)),
            pl.BlockSpec((8, SSM_D_INNER), tail(xs_c)),
            pl.BlockSpec((8, D_BC), tail(bc_c)),
            pl.BlockSpec((8, SSM_D_INNER), lambda b, c: (BLK // 8 - 1, xs_c)),
            pl.BlockSpec((8, D_BC), lambda b, c: (BLK // 8 - 1, bc_c)),
            full((SSM_GROUPS, SSM_STATE, 512)),
            full((CONV_WIDTH, D_XBC)), full((1, D_XBC)), full((1, BLK)), full((1, BLK)),
            full((1, SSM_D_INNER)), full((1, SSM_D_INNER)),
        ],
        out_specs=pl.BlockSpec((BLK, SSM_D_INNER), cur(0)),
        out_shape=jax.ShapeDtypeStruct((rows, SSM_D_INNER), ymix_dtype),
        scratch_shapes=[pltpu.VMEM((SSM_GROUPS, SSM_STATE, 512), F32)],
        compiler_params=_params("parallel", "arbitrary"),
        name="ssd_mixer",
    )(proj, proj, proj, proj, proj, proj, proj_meta, proj_meta, state0,
      conv_w, cb, dtb, alog, dsk, norm_w)


def _outproj_kernel(ya_ref, ys_ref, x_ref, w_ref, g_ref, h_ref, xn_ref):
    y = jnp.concatenate([ya_ref[...], ys_ref[...]], axis=1).astype(BF16)
    h = x_ref[...] + _dot(y, w_ref[...])
    h_ref[...] = h
    ms = jnp.mean(h * h, axis=-1, keepdims=True)
    xn_ref[...] = (h * lax.rsqrt(ms + EPS) * g_ref[...]).astype(BF16)


def _outproj(ya, ys, x2d, w_out, gain):
    m = x2d.shape[0]
    tm = min(m, 256)
    return pl.pallas_call(
        _outproj_kernel,
        grid=(m // tm,),
        in_specs=[
            pl.BlockSpec((tm, D_ATTN), lambda i: (i, 0)),
            pl.BlockSpec((tm, SSM_D_INNER), lambda i: (i, 0)),
            pl.BlockSpec((tm, D_MODEL), lambda i: (i, 0)),
            pl.BlockSpec((D_MODEL, D_MODEL), lambda i: (0, 0)),
            pl.BlockSpec((1, D_MODEL), lambda i: (0, 0)),
        ],
        out_specs=[pl.BlockSpec((tm, D_MODEL), lambda i: (i, 0)),
                   pl.BlockSpec((tm, D_MODEL), lambda i: (i, 0))],
        out_shape=[jax.ShapeDtypeStruct((m, D_MODEL), F32),
                   jax.ShapeDtypeStruct((m, D_MODEL), BF16)],
        compiler_params=_params("parallel"),
        name="outproj",
    )(ya, ys, x2d, w_out, gain)


def _top16_rows(s):
    vals = []
    rank = jnp.full_like(s, float(PEER_TOPK))
    for r in range(PEER_TOPK):
        m = jnp.max(s, axis=0, keepdims=True)
        vals.append(m)
        hit = s == m
        rank = jnp.where(hit, float(r), rank)
        if r < PEER_TOPK - 1:
            s = jnp.where(hit, -jnp.inf, s)
    return vals, rank


def _route_kernel(xn_ref, wq_ref, keys_ref, n_ref, e1_ref, r2_ref, e2_ref, v_ref):
    q = _dot(xn_ref[...], wq_ref[...]).astype(BF16)
    for h in range(PEER_HEADS):
        s, tops, ranks = [], [], []
        for c in range(2):
            qs = q[:, (2 * h + c) * PEER_HALF:(2 * h + c + 1) * PEER_HALF]
            sc = _dot_nt(keys_ref[h, c], qs)
            s.append(sc)
            vals, rank = _top16_rows(sc)
            ranks.append(rank)
            for r, m in enumerate(vals):
                v_ref[c, r:r + 1, :] = m
            tops.append(v_ref[c])
        v1, v2 = tops
        blocks = [v1[0:1] + v2, v1[1:2] + v2[0:8]]
        blocks += [v1[a:a + 1] + v2[0:8] for a in range(2, 8)]
        blocks.append(v1[8:16] + v2[0:1])
        cand = jnp.concatenate(blocks, axis=0)
        top = v1[0:1] + v2[0:1]
        rem = cand
        for _ in range(PEER_TOPK - 1):
            m = jnp.max(rem, axis=0, keepdims=True)
            rem = jnp.where(rem == m, -jnp.inf, rem)
        tau = jnp.max(rem, axis=0, keepdims=True)
        z = jnp.sum(jnp.where(cand >= tau, jnp.exp(cand - top), 0.0), axis=0, keepdims=True)
        cnt = [jnp.sum(jnp.where(blocks[a] >= tau, 1.0, 0.0), axis=0, keepdims=True) for a in range(8)]
        cnt_hi = jnp.where(blocks[8] >= tau, 1.0, 0.0)
        n_sel = jnp.zeros_like(s[0])
        for a in range(PEER_TOPK):
            n_a = cnt[a] if a < 8 else cnt_hi[a - 8:a - 7]
            n_sel = jnp.where(ranks[0] == float(a), n_a, n_sel)
        n_ref[h] = n_sel
        e1_ref[h] = jnp.exp(s[0] - v1[0:1]) / z
        e2 = jnp.exp(s[1] - v2[0:1]).astype(BF16)
        r2 = ranks[1].astype(BF16)
        for lt in range(r2.shape[1] // BLK):
            lanes = slice(lt * BLK, (lt + 1) * BLK)
            r2_ref[h, lt] = pltpu.bitcast(r2[:, lanes], jnp.uint32)
            e2_ref[h, lt] = pltpu.bitcast(e2[:, lanes], jnp.uint32)


def _route(xn, wq, keys):
    t = xn.shape[0]
    tm = min(t, 256)
    big = pl.BlockSpec((PEER_HEADS, N_KEYS, tm), lambda i: (0, 0, i))
    big_shape = jax.ShapeDtypeStruct((PEER_HEADS, N_KEYS, t), F32)
    tiled = pl.BlockSpec((PEER_HEADS, tm // BLK, N_KEYS // 2, BLK), lambda i: (0, i, 0, 0))
    tiled_shape = jax.ShapeDtypeStruct((PEER_HEADS, t // BLK, N_KEYS // 2, BLK), jnp.uint32)
    return pl.pallas_call(
        _route_kernel,
        grid=(t // tm,),
        in_specs=[
            pl.BlockSpec((tm, D_MODEL), lambda i: (i, 0)),
            pl.BlockSpec((D_MODEL, D_MODEL), lambda i: (0, 0)),
            pl.BlockSpec((PEER_HEADS, 2, N_KEYS, PEER_HALF), lambda i: (0, 0, 0, 0)),
        ],
        out_specs=[big, big, tiled, tiled],
        out_shape=[big_shape, big_shape, tiled_shape, tiled_shape],
        scratch_shapes=[pltpu.VMEM((2, PEER_TOPK, tm), F32)],
        compiler_params=_params("parallel"),
        name="peer_route",
    )(xn, wq, keys)


PEER_TB = 512
PEER_EB = 1024
PEER_JCH = 128


def _peer_kernel(xn_ref, u_ref, vt_ref, n_ref, e1_ref, r2_ref, e2_ref, o_ref, a_ref, w_ref):
    @pl.when(pl.program_id(1) == 0)
    def _():
        o_ref[...] = jnp.zeros_like(o_ref)

    a_ref[...] = _dot_nt(u_ref[...], xn_ref[...])
    tb = xn_ref.shape[0]

    def per_key(ii, carry):
        row0 = pl.multiple_of(ii * N_KEYS, N_KEYS)
        n_rows = [n_ref[h, pl.ds(ii, 1), :] for h in range(PEER_HEADS)]
        e1_rows = [e1_ref[h, pl.ds(ii, 1), :] for h in range(PEER_HEADS)]
        for lt in range(tb // BLK):
            lanes = slice(lt * BLK, (lt + 1) * BLK)
            wide = lambda row: jnp.broadcast_to(row[:, lanes], (PEER_JCH, BLK)).astype(BF16)
            n_b = [wide(r) for r in n_rows]
            e1_b = [wide(r) for r in e1_rows]
            for j0 in range(0, N_KEYS, PEER_JCH):
                js = slice(j0 // 2, (j0 + PEER_JCH) // 2)
                acc = None
                for h in range(PEER_HEADS):
                    keep = pltpu.bitcast(r2_ref[h, lt, js, :], BF16) < n_b[h]
                    gate = jnp.where(keep, pltpu.bitcast(e2_ref[h, lt, js, :], BF16), 0.0) * e1_b[h]
                    acc = gate if acc is None else acc + gate
                rows = pl.ds(row0 + j0, PEER_JCH)
                w_ref[rows, lanes] = acc * _gelu_exact(a_ref[rows, lanes]).astype(BF16)
        return carry

    lax.fori_loop(0, PEER_EB // N_KEYS, per_key, 0)
    o_ref[...] += _dot(vt_ref[...], w_ref[...])


def _peer(xn, u_b, vt_b, th, e1, s2, e2):
    t = xn.shape[0]
    tb = min(t, PEER_TB)
    assert tb == PEER_TB
    ni = PEER_EB // N_KEYS
    small = pl.BlockSpec((PEER_HEADS, ni, tb), lambda i, e: (0, e, i))
    big = pl.BlockSpec((PEER_HEADS, tb // BLK, N_KEYS // 2, BLK), lambda i, e: (0, i, 0, 0))
    return pl.pallas_call(
        _peer_kernel,
        grid=(t // tb, N_EXPERTS // PEER_EB),
        in_specs=[
            pl.BlockSpec((tb, D_MODEL), lambda i, e: (i, 0)),
            pl.BlockSpec((PEER_EB, D_MODEL), lambda i, e: (e, 0)),
            pl.BlockSpec((D_MODEL, PEER_EB), lambda i, e: (0, e)),
            small, small, big, big,
        ],
        out_specs=pl.BlockSpec((D_MODEL, tb), lambda i, e: (0, i)),
        out_shape=jax.ShapeDtypeStruct((D_MODEL, t), F32),
        scratch_shapes=[pltpu.VMEM((PEER_EB, tb), F32), pltpu.VMEM((PEER_EB, tb), BF16)],
        compiler_params=_params("parallel", "arbitrary"),
        name="peer_experts",
    )(xn, u_b, vt_b, th, e1, s2, e2)


def _final_kernel(h_ref, pt_ref, g_ref, o_ref):
    h = h_ref[...] + pt_ref[...].T
    ms = jnp.mean(h * h, axis=-1, keepdims=True)
    o_ref[...] = h * lax.rsqrt(ms + EPS) * g_ref[...]


def _final(h1, peer_t, gain):
    t = h1.shape[0]
    tm = min(t, 256)
    return pl.pallas_call(
        _final_kernel,
        grid=(t // tm,),
        in_specs=[
            pl.BlockSpec((tm, D_MODEL), lambda i: (i, 0)),
            pl.BlockSpec((D_MODEL, tm), lambda i: (0, i)),
            pl.BlockSpec((1, D_MODEL), lambda i: (0, 0)),
        ],
        out_specs=pl.BlockSpec((tm, D_MODEL), lambda i: (i, 0)),
        out_shape=jax.ShapeDtypeStruct((t, D_MODEL), F32),
        compiler_params=_params("parallel"),
        name="final_norm",
    )(h1, peer_t, gain)


def _mixer(x2d, batch, seq, meta_tokens, rel_bias, ln_mix, w_in, sinks, conv_w, conv_b, dt_bias,
           a_log, d_skip, attn_norm_w, ssm_norm_w):
    nb = seq // BLK
    w_p = jnp.concatenate(
        [w_in[:, 0:1024], w_in[:, 1536:2560], w_in[:, 2560:4096], w_in[:, 1024:1280],
         w_in[:, 1280:1536], w_in[:, 4096:4112],
         jnp.zeros((D_MODEL, D_PROJ - 4112), w_in.dtype)], axis=1).astype(BF16)
    gain = ln_mix.reshape(1, D_MODEL)
    meta_pad = jnp.concatenate([jnp.zeros((BLK - N_META, D_MODEL), F32), meta_tokens.astype(F32)], axis=0)
    proj = _inproj(x2d, gain, w_p)
    proj_meta = _inproj(meta_pad, gain, w_p)

    bucket, valid = _band_tables(nb)
    tab = rel_bias.astype(F32)
    bias = jnp.full((2, ATTN_HEADS) + bucket.shape[1:], NEG, F32)
    for b in range(REL_BUCKETS):
        bias = jnp.where((valid & (bucket == b))[:, None], tab[b][None, :, None, None], bias)
    ya = _attention(proj, proj_meta, sinks.astype(F32), bias, attn_norm_w.reshape(1, D_ATTN),
                    batch, nb, F32)
    ys = _ssd(proj, proj_meta, conv_w, conv_b, dt_bias, a_log, d_skip,
                   ssm_norm_w.reshape(1, SSM_D_INNER), batch, nb, F32)
    return ya, ys


def kernel(x, meta_tokens, rel_bias, ln_mix, w_in, attn_sinks, conv_w, conv_b, dt_bias, a_log, d_skip,
           attn_norm_w, ssm_norm_w, w_out, ln_ffn, peer_wq, peer_keys, peer_u, peer_v, ln_final):
    batch, seq, _ = x.shape
    x2d = x.reshape(batch * seq, D_MODEL)
    ya, ys = _mixer(x2d, batch, seq, meta_tokens, rel_bias, ln_mix[0], w_in[0], attn_sinks[0],
                         conv_w[0], conv_b[0], dt_bias[0], a_log[0], d_skip[0], attn_norm_w[0],
                         ssm_norm_w[0])
    h1, xn = _outproj(ya, ys, x2d, w_out[0].astype(BF16), ln_ffn[0].reshape(1, D_MODEL))
    th, e1, s2, e2 = _route(xn, peer_wq[0].astype(BF16), peer_keys[0].astype(BF16))
    peer_t = _peer(xn, peer_u[0].astype(BF16), peer_v[0].T.astype(BF16), th, e1, s2, e2)
    out = _final(h1, peer_t, ln_final.reshape(1, D_MODEL))
    return out.reshape(batch, seq, D_MODEL)
```

```python
import functools

import jax
import jax.numpy as jnp
import numpy as np
from jax import lax
from jax.experimental import pallas as pl
from jax.experimental.pallas import tpu as pltpu

F32 = jnp.float32
BF16 = jnp.bfloat16

D_MODEL = 2048
N_META = 16
HEAD_DIM = 64
D_ATTN = 1024
ATTN_HEADS = 16
ATTN_KV_HEADS = 4
ATTN_GROUP = 4
D_KV = 256
WINDOW = 128
BLK = 128
REL_BUCKETS = 32
REL_MAX_DIST = 128
SSM_D_INNER = 1024
SSM_HEAD_DIM = 64
SSM_HEADS = 16
SSM_GROUPS = 2
SSM_HPG = 8
SSM_STATE = 128
CONV_WIDTH = 4
D_XBC = 1536
D_BC = 2 * SSM_GROUPS * SSM_STATE
PEER_HEADS = 8
PEER_TOPK = 16
N_KEYS = 128
N_EXPERTS = N_KEYS * N_KEYS
PEER_HALF = 128
EPS = 1e-6
NEG = -1e30

COL_Q = 0
COL_K = 1024
COL_V = 1280
COL_Z = 1536
COL_XS = 2560
COL_BC = 3584
COL_DT = 4096
D_IN = 4112
D_PROJ = 4224
HALF = 512

VMEM_LIMIT = 56 * 1024 * 1024


def _params(*sem, flags=None):
    return pltpu.CompilerParams(dimension_semantics=sem, vmem_limit_bytes=VMEM_LIMIT, flags=flags)


def _dot(a, b):
    return jnp.dot(a, b, preferred_element_type=F32)


def _dot_nt(a, b):
    return lax.dot_general(a, b, (((1,), (1,)), ((), ())), preferred_element_type=F32)


def _split3(x):
    hi = x.astype(BF16)
    r = x - hi.astype(F32)
    mid = r.astype(BF16)
    lo = (r - mid.astype(F32)).astype(BF16)
    return hi, mid, lo


def _dot01_left(m01, x):
    hi, mid, lo = _split3(x)
    return _dot(m01, hi) + _dot(m01, mid) + _dot(m01, lo)


def _dot01_right(x, m01):
    hi, mid, lo = _split3(x)
    return _dot(hi, m01) + _dot(mid, m01) + _dot(lo, m01)


def _silu(x):
    return x * jax.nn.sigmoid(x)


def _softplus(x):
    return jnp.maximum(x, 0.0) + jnp.log1p(jnp.exp(-jnp.abs(x)))


def _gelu_exact(x):
    return 0.5 * x * (1.0 + lax.erf(x * np.float32(np.sqrt(0.5))))


def _inproj_kernel(x_ref, g_ref, w_ref, o_ref, xn_ref):
    @pl.when(pl.program_id(1) == 0)
    def _():
        x = x_ref[...]
        ms = jnp.mean(x * x, axis=-1, keepdims=True)
        xn_ref[...] = (x * lax.rsqrt(ms + EPS) * g_ref[...]).astype(BF16)

    o_ref[...] = _dot(xn_ref[...], w_ref[...])


def _inproj(x2d, gain, w_p):
    m = x2d.shape[0]
    tm = min(m, 512)
    tn = D_PROJ // 3
    return pl.pallas_call(
        _inproj_kernel,
        grid=(m // tm, 3),
        in_specs=[
            pl.BlockSpec((tm, D_MODEL), lambda i, j: (i, 0)),
            pl.BlockSpec((1, D_MODEL), lambda i, j: (0, 0)),
            pl.BlockSpec((D_MODEL, tn), lambda i, j: (0, j)),
        ],
        out_specs=pl.BlockSpec((tm, tn), lambda i, j: (i, j)),
        out_shape=jax.ShapeDtypeStruct((m, D_PROJ), F32),
        scratch_shapes=[pltpu.VMEM((tm, D_MODEL), BF16)],
        compiler_params=_params("parallel", "arbitrary"),
        name="inproj",
    )(x2d, gain, w_p)


def _t5_bucket(dist):
    n = np.maximum(dist, 0)
    max_exact = REL_BUCKETS // 2
    large = max_exact + (np.log(np.maximum(n, 1) / max_exact) / np.log(REL_MAX_DIST / max_exact)
                         * (REL_BUCKETS - max_exact)).astype(np.int32)
    large = np.minimum(large, REL_BUCKETS - 1)
    return np.where(n < max_exact, n, large).astype(np.int32)


N_BAND = BLK + N_META


def _band_tables(nb):
    r = np.arange(BLK)[:, None]
    q = np.arange(BLK)[None, :]
    m = np.arange(N_META)[:, None]
    buckets, valids = [], []
    for n in range(nb):
        upper = r > q
        d_band = np.where(upper, q - r + BLK, q - r)
        d_meta = N_META + n * BLK + q - m
        assert (d_band[upper] < WINDOW).all() and (d_band >= 0).all() and (d_meta >= 0).all()
        buckets.append(_t5_bucket(np.concatenate([d_band, d_meta], axis=0)))
        valids.append(np.concatenate([~upper | (n > 0), np.ones((N_META, BLK), bool)], axis=0))
    for n in range(2, nb):
        assert (buckets[n] == buckets[1]).all() and (valids[n] == valids[1]).all()
    last = min(1, nb - 1)
    return np.stack([buckets[0], buckets[last]]), np.stack([valids[0], valids[last]])


def _attn_kernel(sink_ref, q_ref, kp_ref, ko_ref, km_ref, vp_ref, vo_ref, vmt_ref,
                 bias_ref, nw_ref, o_ref, yt_ref):
    q = (q_ref[...] * np.float32(HEAD_DIM ** -0.5)).astype(BF16)
    upper = (lax.broadcasted_iota(jnp.int32, (BLK, BLK), 0)
             > lax.broadcasted_iota(jnp.int32, (BLK, BLK), 1))
    vpt = vp_ref[...].T.astype(BF16)
    vot = vo_ref[...].T.astype(BF16)
    vmt = vmt_ref[...].astype(BF16)
    for j in range(ATTN_KV_HEADS):
        ks = slice(j * HEAD_DIM, (j + 1) * HEAD_DIM)
        heads = [j * ATTN_GROUP + g for g in range(ATTN_GROUP)]
        q4 = jnp.concatenate([q[:, h * HEAD_DIM:(h + 1) * HEAD_DIM] for h in heads], axis=0)
        lp = _dot_nt(kp_ref[:, ks].astype(BF16), q4)
        lo = _dot_nt(ko_ref[:, ks].astype(BF16), q4)
        lm = _dot_nt(km_ref[:, ks].astype(BF16), q4)
        e_prev, e_own, e_meta, inv = [], [], [], []
        for g, h in enumerate(heads):
            cols = slice(g * BLK, (g + 1) * BLK)
            band = jnp.where(upper, lp[:, cols], lo[:, cols]) + bias_ref[0, h, 0:BLK, :]
            meta = lm[:, cols] + bias_ref[0, h, BLK:N_BAND, :]
            sink = sink_ref[h]
            mx = jnp.maximum(jnp.maximum(jnp.max(band, axis=0, keepdims=True),
                                         jnp.max(meta, axis=0, keepdims=True)), sink)
            eb = jnp.exp(band - mx)
            em = jnp.exp(meta - mx)
            denom = (jnp.sum(eb, axis=0, keepdims=True) + jnp.sum(em, axis=0, keepdims=True)
                     + jnp.exp(sink - mx))
            inv.append(1.0 / denom)
            e_prev.append(jnp.where(upper, eb, 0.0).astype(BF16))
            e_own.append(jnp.where(upper, 0.0, eb).astype(BF16))
            e_meta.append(em.astype(BF16))
        cat = lambda parts: jnp.concatenate(parts, axis=1)
        ot = (_dot(vpt[ks, :], cat(e_prev)) + _dot(vot[ks, :], cat(e_own))
              + _dot(vmt[ks, :], cat(e_meta))) * cat(inv)
        for g, h in enumerate(heads):
            yt_ref[h * HEAD_DIM:(h + 1) * HEAD_DIM, :] = ot[:, g * BLK:(g + 1) * BLK]
    yt = yt_ref[...]
    ms = jnp.mean(yt * yt, axis=0, keepdims=True)
    o_ref[...] = ((yt * lax.rsqrt(ms + EPS)).T * nw_ref[...]).astype(o_ref.dtype)


def _attention(proj, proj_meta, sinks, bias, norm_w, batch, nb, ymix_dtype):
    rows = batch * nb * BLK
    kcol, vcol = COL_K // D_KV, COL_V // D_KV
    meta_blk = (BLK - N_META) // N_META
    vm_t = proj_meta[BLK - N_META:, COL_V:COL_V + D_KV].T

    def cur(col):
        return lambda b, n: (b * nb + n, col)

    def prev(col):
        return lambda b, n: (b * nb + jnp.maximum(n - 1, 0), col)

    return pl.pallas_call(
        _attn_kernel,
        grid=(batch, nb),
        in_specs=[
            pl.BlockSpec(memory_space=pltpu.SMEM),
            pl.BlockSpec((BLK, D_ATTN), cur(0)),
            pl.BlockSpec((BLK, D_KV), prev(kcol)),
            pl.BlockSpec((BLK, D_KV), cur(kcol)),
            pl.BlockSpec((N_META, D_KV), lambda b, n: (meta_blk, kcol)),
            pl.BlockSpec((BLK, D_KV), prev(vcol)),
            pl.BlockSpec((BLK, D_KV), cur(vcol)),
            pl.BlockSpec((D_KV, N_META), lambda b, n: (0, 0)),
            pl.BlockSpec((1, ATTN_HEADS, N_BAND, BLK), lambda b, n: (jnp.minimum(n, 1), 0, 0, 0)),
            pl.BlockSpec((1, D_ATTN), lambda b, n: (0, 0)),
        ],
        out_specs=pl.BlockSpec((BLK, D_ATTN), cur(0)),
        out_shape=jax.ShapeDtypeStruct((rows, D_ATTN), ymix_dtype),
        scratch_shapes=[pltpu.VMEM((D_ATTN, BLK), F32)],
        compiler_params=_params("parallel", "arbitrary"),
        name="swa_attention",
    )(sinks, proj, proj, proj, proj_meta, proj, proj, vm_t, bias, norm_w)


def _ssd_chunk(xs_raw, bc_raw, tail_xs, tail_bc, dt_raw, cw, cb, dtb, alog, row_mask, state):
    def conv(blk, tail, w, b):
        ext = jnp.concatenate([tail, blk], axis=0)
        acc = b
        for k in range(CONV_WIDTH):
            acc = acc + w[k:k + 1, :] * ext[5 + k:5 + k + BLK, :]
        return _silu(acc)

    xs = conv(xs_raw, tail_xs, cw[:, :SSM_D_INNER], cb[:, :SSM_D_INNER])
    bc = conv(bc_raw, tail_bc, cw[:, SSM_D_INNER:], cb[:, SSM_D_INNER:])
    dt = _softplus(dt_raw + dtb)
    if row_mask is not None:
        xs = jnp.where(row_mask, xs, 0.0)
        bc = jnp.where(row_mask, bc, 0.0)
        dt = jnp.where(row_mask, dt, 0.0)
    a_neg = -jnp.exp(alog)
    d_a = dt * a_neg

    r = lax.broadcasted_iota(jnp.int32, (BLK, BLK), 0)
    c = lax.broadcasted_iota(jnp.int32, (BLK, BLK), 1)
    tri = r >= c
    cs = _dot01_left(tri.astype(BF16), d_a)
    cs_t = cs.T
    hh = lax.broadcasted_iota(jnp.int32, (BLK, SSM_D_INNER), 0)
    cc = lax.broadcasted_iota(jnp.int32, (BLK, SSM_D_INNER), 1)
    expand = (cc // SSM_HEAD_DIM == hh).astype(BF16)
    dt_rep = _dot01_right(dt, expand)
    ecs_rep = _dot01_right(jnp.exp(cs), expand)
    dec_rep = _dot01_right(jnp.exp(cs[BLK - 1:BLK, :] - cs), expand)

    xdt = xs * dt_rep
    xdtd = (xdt * dec_rep).astype(BF16)
    xdt_b = xdt.astype(BF16)
    chunk_decay = ecs_rep[BLK - 1:BLK, :]

    y_parts, new_state = [], []
    for g in range(SSM_GROUPS):
        b_g = bc[:, g * SSM_STATE:(g + 1) * SSM_STATE]
        c_g = bc[:, (SSM_GROUPS + g) * SSM_STATE:(SSM_GROUPS + g + 1) * SSM_STATE]
        cols = slice(g * 512, (g + 1) * 512)
        cb_g = _dot_nt(c_g.astype(BF16), b_g.astype(BF16))
        y_off = _dot(c_g.astype(BF16), state[g].astype(BF16)) * ecs_rep[:, cols]
        new_state.append(chunk_decay[:, cols] * state[g] + _dot(b_g.T.astype(BF16), xdtd[:, cols]))
        y_diag = []
        for hp in range(SSM_HPG):
            h = g * SSM_HPG + hp
            seg = cs[:, h:h + 1] - cs_t[h:h + 1, :]
            lmat = jnp.exp(jnp.where(tri, seg, -jnp.inf))
            m = (cb_g * lmat).astype(BF16)
            y_diag.append(_dot(m, xdt_b[:, h * SSM_HEAD_DIM:(h + 1) * SSM_HEAD_DIM]))
        y_parts.append(jnp.concatenate(y_diag, axis=1) + y_off)
    return jnp.concatenate(y_parts, axis=1), xs, new_state


def _halves(lo_ref, hi_ref):
    return jnp.concatenate([lo_ref[...], hi_ref[...]], axis=1)


def _ssd_meta_kernel(xs0_ref, xs1_ref, bc_ref, dt_ref, cw_ref, cb_ref, dtb_ref, alog_ref, st_ref):
    rows = lax.broadcasted_iota(jnp.int32, (BLK, 1), 0)
    zero_state = [jnp.zeros((SSM_STATE, 512), F32) for _ in range(SSM_GROUPS)]
    _, _, st = _ssd_chunk(_halves(xs0_ref, xs1_ref), bc_ref[...], jnp.zeros((8, SSM_D_INNER), F32),
                          jnp.zeros((8, D_BC), F32), dt_ref[...], cw_ref[...], cb_ref[...],
                          dtb_ref[...], alog_ref[...], rows >= BLK - N_META, zero_state)
    for g in range(SSM_GROUPS):
        st_ref[g] = st[g]


def _ssd_kernel(xs0_ref, xs1_ref, bc_ref, z0_ref, z1_ref, dt_ref, txs0_ref, txs1_ref, tbc_ref,
                mxs0_ref, mxs1_ref, mbc_ref, st0_ref,
                cw_ref, cb_ref, dtb_ref, alog_ref, dsk_ref, nw_ref, o_ref, st_ref):
    first = pl.program_id(1) == 0

    @pl.when(first)
    def _():
        st_ref[...] = st0_ref[...]

    tail_xs = jnp.where(first, _halves(mxs0_ref, mxs1_ref), _halves(txs0_ref, txs1_ref))
    tail_bc = jnp.where(first, mbc_ref[...], tbc_ref[...])
    state = [st_ref[g] for g in range(SSM_GROUPS)]
    y, xs, new_state = _ssd_chunk(_halves(xs0_ref, xs1_ref), bc_ref[...], tail_xs, tail_bc, dt_ref[...],
                                  cw_ref[...], cb_ref[...], dtb_ref[...], alog_ref[...], None, state)
    for g in range(SSM_GROUPS):
        st_ref[g] = new_state[g]
    y = y + xs * dsk_ref[...]
    yg = y * _silu(_halves(z0_ref, z1_ref))
    outs = []
    for g in range(SSM_GROUPS):
        part = yg[:, g * 512:(g + 1) * 512]
        ms = jnp.mean(part * part, axis=-1, keepdims=True)
        outs.append(part * lax.rsqrt(ms + EPS))
    o_ref[...] = (jnp.concatenate(outs, axis=1) * nw_ref[...]).astype(o_ref.dtype)


def _pad_lanes(v, n=BLK):
    v = v.reshape(1, -1)
    return jnp.pad(v, ((0, 0), (0, n - v.shape[1])))


def _ssd(proj, proj_meta, conv_w, conv_b, dt_bias, a_log, d_skip, norm_w, batch, nc, ymix_dtype):
    rows = batch * nc * BLK
    cb = conv_b.reshape(1, D_XBC)
    dtb, alog = _pad_lanes(dt_bias), _pad_lanes(a_log)
    dsk = jnp.repeat(d_skip, SSM_HEAD_DIM).reshape(1, SSM_D_INNER)
    xs_c, z_c, bc_c, dt_c = COL_XS // HALF, COL_Z // HALF, COL_BC // D_BC, COL_DT // BLK
    full = lambda shape: pl.BlockSpec(shape, lambda *_: (0,) * len(shape))

    state0 = pl.pallas_call(
        _ssd_meta_kernel,
        grid=(1,),
        in_specs=[
            pl.BlockSpec((BLK, HALF), lambda i: (0, xs_c)),
            pl.BlockSpec((BLK, HALF), lambda i: (0, xs_c + 1)),
            pl.BlockSpec((BLK, D_BC), lambda i: (0, bc_c)),
            pl.BlockSpec((BLK, BLK), lambda i: (0, dt_c)),
            full((CONV_WIDTH, D_XBC)), full((1, D_XBC)), full((1, BLK)), full((1, BLK)),
        ],
        out_specs=full((SSM_GROUPS, SSM_STATE, 512)),
        out_shape=jax.ShapeDtypeStruct((SSM_GROUPS, SSM_STATE, 512), F32),
        compiler_params=_params("arbitrary"),
        name="ssd_meta_state",
    )(proj_meta, proj_meta, proj_meta, proj_meta, conv_w, cb, dtb, alog)

    def cur(col, width):
        return pl.BlockSpec((BLK, width), lambda b, c: (b * nc + c, col))

    def tail(col, width):
        return pl.BlockSpec((8, width), lambda b, c: (jnp.maximum((b * nc + c) * (BLK // 8) - 1, 0), col))

    def meta_tail(col, width):
        return pl.BlockSpec((8, width), lambda b, c: (BLK // 8 - 1, col))

    return pl.pallas_call(
        _ssd_kernel,
        grid=(batch, nc),
        in_specs=[
            cur(xs_c, HALF), cur(xs_c + 1, HALF), cur(bc_c, D_BC), cur(z_c, HALF), cur(z_c + 1, HALF),
            cur(dt_c, BLK),
            tail(xs_c, HALF), tail(xs_c + 1, HALF), tail(bc_c, D_BC),
            meta_tail(xs_c, HALF), meta_tail(xs_c + 1, HALF), meta_tail(bc_c, D_BC),
            full((SSM_GROUPS, SSM_STATE, 512)),
            full((CONV_WIDTH, D_XBC)), full((1, D_XBC)), full((1, BLK)), full((1, BLK)),
            full((1, SSM_D_INNER)), full((1, SSM_D_INNER)),
        ],
        out_specs=pl.BlockSpec((BLK, SSM_D_INNER), lambda b, c: (b * nc + c, 0)),
        out_shape=jax.ShapeDtypeStruct((rows, SSM_D_INNER), ymix_dtype),
        scratch_shapes=[pltpu.VMEM((SSM_GROUPS, SSM_STATE, 512), F32)],
        compiler_params=_params("parallel", "arbitrary"),
        name="ssd_mixer",
    )(proj, proj, proj, proj, proj, proj, proj, proj, proj, proj_meta, proj_meta, proj_meta, state0,
      conv_w, cb, dtb, alog, dsk, norm_w)


def _outproj_kernel(ya_ref, ys_ref, x_ref, w_ref, g_ref, h_ref, xn_ref):
    y = jnp.concatenate([ya_ref[...], ys_ref[...]], axis=1).astype(BF16)
    h = x_ref[...] + _dot(y, w_ref[...])
    h_ref[...] = h
    ms = jnp.mean(h * h, axis=-1, keepdims=True)
    xn_ref[...] = (h * lax.rsqrt(ms + EPS) * g_ref[...]).astype(BF16)


def _outproj(ya, ys, x2d, w_out, gain):
    m = x2d.shape[0]
    tm = min(m, 256)
    return pl.pallas_call(
        _outproj_kernel,
        grid=(m // tm,),
        in_specs=[
            pl.BlockSpec((tm, D_ATTN), lambda i: (i, 0)),
            pl.BlockSpec((tm, SSM_D_INNER), lambda i: (i, 0)),
            pl.BlockSpec((tm, D_MODEL), lambda i: (i, 0)),
            pl.BlockSpec((D_MODEL, D_MODEL), lambda i: (0, 0)),
            pl.BlockSpec((1, D_MODEL), lambda i: (0, 0)),
        ],
        out_specs=[pl.BlockSpec((tm, D_MODEL), lambda i: (i, 0)),
                   pl.BlockSpec((tm, D_MODEL), lambda i: (i, 0))],
        out_shape=[jax.ShapeDtypeStruct((m, D_MODEL), F32),
                   jax.ShapeDtypeStruct((m, D_MODEL), BF16)],
        compiler_params=_params("parallel"),
        name="outproj",
    )(ya, ys, x2d, w_out, gain)


def _top16_rows(s):
    vals = []
    rank = jnp.full_like(s, float(PEER_TOPK))
    for r in range(PEER_TOPK):
        m = jnp.max(s, axis=0, keepdims=True)
        vals.append(m)
        hit = s == m
        rank = jnp.where(hit, float(r), rank)
        if r < PEER_TOPK - 1:
            s = jnp.where(hit, -jnp.inf, s)
    return vals, rank


def _route_kernel(xn_ref, wq_ref, keys_ref, n_ref, e1_ref, r2_ref, e2_ref, v_ref):
    q = _dot(xn_ref[...], wq_ref[...]).astype(BF16)
    for h in range(PEER_HEADS):
        s, tops, ranks = [], [], []
        for c in range(2):
            qs = q[:, (2 * h + c) * PEER_HALF:(2 * h + c + 1) * PEER_HALF]
            sc = _dot_nt(keys_ref[h, c], qs)
            s.append(sc)
            vals, rank = _top16_rows(sc)
            ranks.append(rank)
            for r, m in enumerate(vals):
                v_ref[c, r:r + 1, :] = m
            tops.append(v_ref[c])
        v1, v2 = tops
        blocks = [v1[0:1] + v2, v1[1:2] + v2[0:8]]
        blocks += [v1[a:a + 1] + v2[0:8] for a in range(2, 8)]
        blocks.append(v1[8:16] + v2[0:1])
        cand = jnp.concatenate(blocks, axis=0)
        top = v1[0:1] + v2[0:1]
        rem = cand
        for _ in range(PEER_TOPK - 1):
            m = jnp.max(rem, axis=0, keepdims=True)
            rem = jnp.where(rem == m, -jnp.inf, rem)
        tau = jnp.max(rem, axis=0, keepdims=True)
        z = jnp.sum(jnp.where(cand >= tau, jnp.exp(cand - top), 0.0), axis=0, keepdims=True)
        cnt = [jnp.sum(jnp.where(blocks[a] >= tau, 1.0, 0.0), axis=0, keepdims=True) for a in range(8)]
        cnt_hi = jnp.where(blocks[8] >= tau, 1.0, 0.0)
        n_sel = jnp.zeros_like(s[0])
        for a in range(PEER_TOPK):
            n_a = cnt[a] if a < 8 else cnt_hi[a - 8:a - 7]
            n_sel = jnp.where(ranks[0] == float(a), n_a, n_sel)
        n_ref[h] = n_sel
        e1_ref[h] = jnp.exp(s[0] - v1[0:1]) / z
        e2 = jnp.exp(s[1] - v2[0:1]).astype(BF16)
        r2 = ranks[1].astype(BF16)
        for lt in range(r2.shape[1] // BLK):
            lanes = slice(lt * BLK, (lt + 1) * BLK)
            r2_ref[h, lt] = pltpu.bitcast(r2[:, lanes], jnp.uint32)
            e2_ref[h, lt] = pltpu.bitcast(e2[:, lanes], jnp.uint32)


def _route(xn, wq, keys):
    t = xn.shape[0]
    tm = min(t, 256)
    big = pl.BlockSpec((PEER_HEADS, N_KEYS, tm), lambda i: (0, 0, i))
    big_shape = jax.ShapeDtypeStruct((PEER_HEADS, N_KEYS, t), F32)
    tiled = pl.BlockSpec((PEER_HEADS, tm // BLK, N_KEYS // 2, BLK), lambda i: (0, i, 0, 0))
    tiled_shape = jax.ShapeDtypeStruct((PEER_HEADS, t // BLK, N_KEYS // 2, BLK), jnp.uint32)
    return pl.pallas_call(
        _route_kernel,
        grid=(t // tm,),
        in_specs=[
            pl.BlockSpec((tm, D_MODEL), lambda i: (i, 0)),
            pl.BlockSpec((D_MODEL, D_MODEL), lambda i: (0, 0)),
            pl.BlockSpec((PEER_HEADS, 2, N_KEYS, PEER_HALF), lambda i: (0, 0, 0, 0)),
        ],
        out_specs=[big, big, tiled, tiled],
        out_shape=[big_shape, big_shape, tiled_shape, tiled_shape],
        scratch_shapes=[pltpu.VMEM((2, PEER_TOPK, tm), F32)],
        compiler_params=_params("parallel"),
        name="peer_route",
    )(xn, wq, keys)


PEER_TB = 512
PEER_EB = 1024
PEER_JCH = 128


def _peer_kernel(xn_ref, u_ref, vt_ref, n_ref, e1_ref, r2_ref, e2_ref, o_ref, a_ref, w_ref):
    @pl.when(pl.program_id(1) == 0)
    def _():
        o_ref[...] = jnp.zeros_like(o_ref)

    a_ref[...] = _dot_nt(u_ref[...].astype(BF16), xn_ref[...])
    tb = xn_ref.shape[0]

    def per_key(ii, carry):
        row0 = pl.multiple_of(ii * N_KEYS, N_KEYS)
        n_rows = [n_ref[h, pl.ds(ii, 1), :] for h in range(PEER_HEADS)]
        e1_rows = [e1_ref[h, pl.ds(ii, 1), :] for h in range(PEER_HEADS)]
        for lt in range(tb // BLK):
            lanes = slice(lt * BLK, (lt + 1) * BLK)
            wide = lambda row: jnp.broadcast_to(row[:, lanes], (PEER_JCH, BLK)).astype(BF16)
            n_b = [wide(r) for r in n_rows]
            e1_b = [wide(r) for r in e1_rows]
            for j0 in range(0, N_KEYS, PEER_JCH):
                js = slice(j0 // 2, (j0 + PEER_JCH) // 2)
                acc = None
                for h in range(PEER_HEADS):
                    keep = pltpu.bitcast(r2_ref[h, lt, js, :], BF16) < n_b[h]
                    gate = jnp.where(keep, pltpu.bitcast(e2_ref[h, lt, js, :], BF16), 0.0) * e1_b[h]
                    acc = gate if acc is None else acc + gate
                rows = pl.ds(row0 + j0, PEER_JCH)
                w_ref[rows, lanes] = acc * _gelu_exact(a_ref[rows, lanes]).astype(BF16)
        return carry

    lax.fori_loop(0, PEER_EB // N_KEYS, per_key, 0)
    o_ref[...] += _dot(vt_ref[...], w_ref[...])


def _peer(xn, u, vt_b, n_sel, e1, r2, e2):
    t = xn.shape[0]
    tb = min(t, PEER_TB)
    assert tb == PEER_TB
    ni = PEER_EB // N_KEYS
    small = pl.BlockSpec((PEER_HEADS, ni, tb), lambda i, e: (0, e, i))
    big = pl.BlockSpec((PEER_HEADS, tb // BLK, N_KEYS // 2, BLK), lambda i, e: (0, i, 0, 0))
    return pl.pallas_call(
        _peer_kernel,
        grid=(t // tb, N_EXPERTS // PEER_EB),
        in_specs=[
            pl.BlockSpec((tb, D_MODEL), lambda i, e: (i, 0)),
            pl.BlockSpec((PEER_EB, D_MODEL), lambda i, e: (e, 0)),
            pl.BlockSpec((D_MODEL, PEER_EB), lambda i, e: (0, e)),
            small, small, big, big,
        ],
        out_specs=pl.BlockSpec((D_MODEL, tb), lambda i, e: (0, i)),
        out_shape=jax.ShapeDtypeStruct((D_MODEL, t), F32),
        scratch_shapes=[pltpu.VMEM((PEER_EB, tb), F32), pltpu.VMEM((PEER_EB, tb), BF16)],
        compiler_params=_params("parallel", "arbitrary"),
        name="peer_experts",
    )(xn, u, vt_b, n_sel, e1, r2, e2)


def _final_kernel(h_ref, pt_ref, g_ref, o_ref):
    h = h_ref[...] + pt_ref[...].T
    ms = jnp.mean(h * h, axis=-1, keepdims=True)
    o_ref[...] = h * lax.rsqrt(ms + EPS) * g_ref[...]


def _final(h1, peer_t, gain):
    t = h1.shape[0]
    tm = min(t, 256)
    rows = pl.BlockSpec((tm, D_MODEL), lambda i: (i, 0))
    return pl.pallas_call(
        _final_kernel,
        grid=(t // tm,),
        in_specs=[rows, pl.BlockSpec((D_MODEL, tm), lambda i: (0, i)),
                  pl.BlockSpec((1, D_MODEL), lambda i: (0, 0))],
        out_specs=rows,
        out_shape=jax.ShapeDtypeStruct((t, D_MODEL), F32),
        compiler_params=_params("parallel"),
        name="final_norm",
    )(h1, peer_t, gain)


def _mixer(x2d, batch, seq, meta_tokens, rel_bias, ln_mix, w_in, sinks, conv_w, conv_b, dt_bias,
           a_log, d_skip, attn_norm_w, ssm_norm_w):
    nb = seq // BLK
    w_p = jnp.pad(w_in.astype(BF16), ((0, 0), (0, D_PROJ - D_IN)))
    gain = ln_mix.reshape(1, D_MODEL)
    meta_pad = jnp.concatenate([jnp.zeros((BLK - N_META, D_MODEL), F32), meta_tokens.astype(F32)], axis=0)
    proj = _inproj(x2d, gain, w_p)
    proj_meta = _inproj(meta_pad, gain, w_p)

    bucket, valid = _band_tables(nb)
    tab = rel_bias.astype(F32)
    bias = jnp.full((2, ATTN_HEADS) + bucket.shape[1:], NEG, F32)
    for b in range(REL_BUCKETS):
        bias = jnp.where((valid & (bucket == b))[:, None], tab[b][None, :, None, None], bias)
    ya = _attention(proj, proj_meta, sinks.astype(F32), bias, attn_norm_w.reshape(1, D_ATTN),
                    batch, nb, F32)
    ys = _ssd(proj, proj_meta, conv_w, conv_b, dt_bias, a_log, d_skip,
                   ssm_norm_w.reshape(1, SSM_D_INNER), batch, nb, F32)
    return ya, ys


def kernel(x, meta_tokens, rel_bias, ln_mix, w_in, attn_sinks, conv_w, conv_b, dt_bias, a_log, d_skip,
           attn_norm_w, ssm_norm_w, w_out, ln_ffn, peer_wq, peer_keys, peer_u, peer_v, ln_final):
    batch, seq, _ = x.shape
    x2d = x.reshape(batch * seq, D_MODEL)
    ya, ys = _mixer(x2d, batch, seq, meta_tokens, rel_bias, ln_mix[0], w_in[0], attn_sinks[0],
                         conv_w[0], conv_b[0], dt_bias[0], a_log[0], d_skip[0], attn_norm_w[0],
                         ssm_norm_w[0])
    h1, xn = _outproj(ya, ys, x2d, w_out[0].astype(BF16), ln_ffn[0].reshape(1, D_MODEL))
    n_sel, e1, r2, e2 = _route(xn, peer_wq[0].astype(BF16), peer_keys[0].astype(BF16))
    peer_t = _peer(xn, peer_u[0], peer_v[0].T.astype(BF16), n_sel, e1, r2, e2)
    out = _final(h1, peer_t, ln_final.reshape(1, D_MODEL))
    return out.reshape(batch, seq, D_MODEL)
```

```python
import functools

import jax
import jax.numpy as jnp
import numpy as np
from jax import lax
from jax.experimental import pallas as pl
from jax.experimental.pallas import tpu as pltpu

F32 = jnp.float32
BF16 = jnp.bfloat16

D_MODEL = 2048
N_META = 16
HEAD_DIM = 64
D_ATTN = 1024
ATTN_HEADS = 16
ATTN_KV_HEADS = 4
ATTN_GROUP = 4
D_KV = 256
WINDOW = 128
BLK = 128
REL_BUCKETS = 32
REL_MAX_DIST = 128
SSM_D_INNER = 1024
SSM_HEAD_DIM = 64
SSM_HEADS = 16
SSM_GROUPS = 2
SSM_HPG = 8
SSM_STATE = 128
CONV_WIDTH = 4
D_XBC = 1536
D_BC = 2 * SSM_GROUPS * SSM_STATE
PEER_HEADS = 8
PEER_TOPK = 16
N_KEYS = 128
N_EXPERTS = N_KEYS * N_KEYS
PEER_HALF = 128
EPS = 1e-6
NEG = -1e30

COL_Q = 0
COL_K = 1024
COL_V = 1280
COL_Z = 1536
COL_XS = 2560
COL_BC = 3584
COL_DT = 4096
D_IN = 4112
D_PROJ = 4224
HALF = 512

VMEM_LIMIT = 56 * 1024 * 1024


def _params(*sem, flags=None):
    return pltpu.CompilerParams(dimension_semantics=sem, vmem_limit_bytes=VMEM_LIMIT, flags=flags)


def _dot(a, b):
    return jnp.dot(a, b, preferred_element_type=F32)


def _dot_nt(a, b):
    return lax.dot_general(a, b, (((1,), (1,)), ((), ())), preferred_element_type=F32)


def _split3(x):
    hi = x.astype(BF16)
    r = x - hi.astype(F32)
    mid = r.astype(BF16)
    lo = (r - mid.astype(F32)).astype(BF16)
    return hi, mid, lo


def _dot01_left(m01, x):
    hi, mid, lo = _split3(x)
    return _dot(m01, hi) + _dot(m01, mid) + _dot(m01, lo)


def _dot01_right(x, m01):
    hi, mid, lo = _split3(x)
    return _dot(hi, m01) + _dot(mid, m01) + _dot(lo, m01)


def _silu(x):
    return x * jax.nn.sigmoid(x)


def _softplus(x):
    return jnp.maximum(x, 0.0) + jnp.log1p(jnp.exp(-jnp.abs(x)))


def _gelu_exact(x):
    return 0.5 * x * (1.0 + lax.erf(x * np.float32(np.sqrt(0.5))))


def _inproj_kernel(x_ref, g_ref, w_ref, o_ref):
    x = x_ref[...]
    ms = jnp.mean(x * x, axis=-1, keepdims=True)
    xn = (x * lax.rsqrt(ms + EPS) * g_ref[...]).astype(BF16)
    o_ref[...] = _dot(xn, w_ref[...])


def _inproj(x2d, gain, w_p):
    m = x2d.shape[0]
    tm = min(m, 512)
    return pl.pallas_call(
        _inproj_kernel,
        grid=(m // tm,),
        in_specs=[
            pl.BlockSpec((tm, D_MODEL), lambda i: (i, 0)),
            pl.BlockSpec((1, D_MODEL), lambda i: (0, 0)),
            pl.BlockSpec((D_MODEL, D_PROJ), lambda i: (0, 0), pipeline_mode=pl.Buffered(1)),
        ],
        out_specs=pl.BlockSpec((tm, D_PROJ), lambda i: (i, 0)),
        out_shape=jax.ShapeDtypeStruct((m, D_PROJ), F32),
        compiler_params=_params("parallel"),
        name="inproj",
    )(x2d, gain, w_p)


def _t5_bucket(dist):
    n = np.maximum(dist, 0)
    max_exact = REL_BUCKETS // 2
    large = max_exact + (np.log(np.maximum(n, 1) / max_exact) / np.log(REL_MAX_DIST / max_exact)
                         * (REL_BUCKETS - max_exact)).astype(np.int32)
    large = np.minimum(large, REL_BUCKETS - 1)
    return np.where(n < max_exact, n, large).astype(np.int32)


N_BAND = BLK + N_META


def _band_tables(nb):
    r = np.arange(BLK)[:, None]
    q = np.arange(BLK)[None, :]
    m = np.arange(N_META)[:, None]
    buckets, valids = [], []
    for n in range(nb):
        upper = r > q
        d_band = np.where(upper, q - r + BLK, q - r)
        d_meta = N_META + n * BLK + q - m
        assert (d_band[upper] < WINDOW).all() and (d_band >= 0).all() and (d_meta >= 0).all()
        buckets.append(_t5_bucket(np.concatenate([d_band, d_meta], axis=0)))
        valids.append(np.concatenate([~upper | (n > 0), np.ones((N_META, BLK), bool)], axis=0))
    for n in range(2, nb):
        assert (buckets[n] == buckets[1]).all() and (valids[n] == valids[1]).all()
    last = min(1, nb - 1)
    return np.stack([buckets[0], buckets[last]]), np.stack([valids[0], valids[last]])


def _attn_kernel(sink_ref, q_ref, kp_ref, ko_ref, km_ref, vp_ref, vo_ref, vmt_ref,
                 bias_ref, nw_ref, o_ref, yt_ref):
    q = (q_ref[...] * np.float32(HEAD_DIM ** -0.5)).astype(BF16)
    upper = (lax.broadcasted_iota(jnp.int32, (BLK, BLK), 0)
             > lax.broadcasted_iota(jnp.int32, (BLK, BLK), 1))
    vpt = vp_ref[...].T.astype(BF16)
    vot = vo_ref[...].T.astype(BF16)
    vmt = vmt_ref[...].astype(BF16)
    for j in range(ATTN_KV_HEADS):
        ks = slice(j * HEAD_DIM, (j + 1) * HEAD_DIM)
        heads = [j * ATTN_GROUP + g for g in range(ATTN_GROUP)]
        q4 = jnp.concatenate([q[:, h * HEAD_DIM:(h + 1) * HEAD_DIM] for h in heads], axis=0)
        lp = _dot_nt(kp_ref[:, ks].astype(BF16), q4)
        lo = _dot_nt(ko_ref[:, ks].astype(BF16), q4)
        lm = _dot_nt(km_ref[:, ks].astype(BF16), q4)
        e_prev, e_own, e_meta, inv = [], [], [], []
        for g, h in enumerate(heads):
            cols = slice(g * BLK, (g + 1) * BLK)
            band = jnp.where(upper, lp[:, cols], lo[:, cols]) + bias_ref[0, h, 0:BLK, :]
            meta = lm[:, cols] + bias_ref[0, h, BLK:N_BAND, :]
            sink = sink_ref[h]
            mx = jnp.maximum(jnp.maximum(jnp.max(band, axis=0, keepdims=True),
                                         jnp.max(meta, axis=0, keepdims=True)), sink)
            eb = jnp.exp(band - mx)
            em = jnp.exp(meta - mx)
            denom = (jnp.sum(eb, axis=0, keepdims=True) + jnp.sum(em, axis=0, keepdims=True)
                     + jnp.exp(sink - mx))
            inv.append(1.0 / denom)
            e_prev.append(jnp.where(upper, eb, 0.0).astype(BF16))
            e_own.append(jnp.where(upper, 0.0, eb).astype(BF16))
            e_meta.append(em.astype(BF16))
        cat = lambda parts: jnp.concatenate(parts, axis=1)
        ot = (_dot(vpt[ks, :], cat(e_prev)) + _dot(vot[ks, :], cat(e_own))
              + _dot(vmt[ks, :], cat(e_meta))) * cat(inv)
        for g, h in enumerate(heads):
            yt_ref[h * HEAD_DIM:(h + 1) * HEAD_DIM, :] = ot[:, g * BLK:(g + 1) * BLK]
    yt = yt_ref[...]
    ms = jnp.mean(yt * yt, axis=0, keepdims=True)
    o_ref[...] = ((yt * lax.rsqrt(ms + EPS)).T * nw_ref[...]).astype(o_ref.dtype)


def _attention(proj, proj_meta, sinks, bias, norm_w, batch, nb, ymix_dtype):
    rows = batch * nb * BLK
    kcol, vcol = COL_K // D_KV, COL_V // D_KV
    meta_blk = (BLK - N_META) // N_META
    vm_t = proj_meta[BLK - N_META:, COL_V:COL_V + D_KV].T

    def cur(col):
        return lambda b, n: (b * nb + n, col)

    def prev(col):
        return lambda b, n: (b * nb + jnp.maximum(n - 1, 0), col)

    return pl.pallas_call(
        _attn_kernel,
        grid=(batch, nb),
        in_specs=[
            pl.BlockSpec(memory_space=pltpu.SMEM),
            pl.BlockSpec((BLK, D_ATTN), cur(0)),
            pl.BlockSpec((BLK, D_KV), prev(kcol)),
            pl.BlockSpec((BLK, D_KV), cur(kcol)),
            pl.BlockSpec((N_META, D_KV), lambda b, n: (meta_blk, kcol)),
            pl.BlockSpec((BLK, D_KV), prev(vcol)),
            pl.BlockSpec((BLK, D_KV), cur(vcol)),
            pl.BlockSpec((D_KV, N_META), lambda b, n: (0, 0)),
            pl.BlockSpec((1, ATTN_HEADS, N_BAND, BLK), lambda b, n: (jnp.minimum(n, 1), 0, 0, 0)),
            pl.BlockSpec((1, D_ATTN), lambda b, n: (0, 0)),
        ],
        out_specs=pl.BlockSpec((BLK, D_ATTN), cur(0)),
        out_shape=jax.ShapeDtypeStruct((rows, D_ATTN), ymix_dtype),
        scratch_shapes=[pltpu.VMEM((D_ATTN, BLK), F32)],
        compiler_params=_params("parallel", "arbitrary"),
        name="swa_attention",
    )(sinks, proj, proj, proj, proj_meta, proj, proj, vm_t, bias, norm_w)


def _ssd_chunk(xs_raw, bc_raw, tail_xs, tail_bc, dt_raw, cw, cb, dtb, alog, row_mask, state):
    def conv(blk, tail, w, b):
        ext = jnp.concatenate([tail, blk], axis=0)
        acc = b
        for k in range(CONV_WIDTH):
            acc = acc + w[k:k + 1, :] * ext[5 + k:5 + k + BLK, :]
        return _silu(acc)

    xs = conv(xs_raw, tail_xs, cw[:, :SSM_D_INNER], cb[:, :SSM_D_INNER])
    bc = conv(bc_raw, tail_bc, cw[:, SSM_D_INNER:], cb[:, SSM_D_INNER:])
    dt = _softplus(dt_raw + dtb)
    if row_mask is not None:
        xs = jnp.where(row_mask, xs, 0.0)
        bc = jnp.where(row_mask, bc, 0.0)
        dt = jnp.where(row_mask, dt, 0.0)
    a_neg = -jnp.exp(alog)
    d_a = dt * a_neg

    r = lax.broadcasted_iota(jnp.int32, (BLK, BLK), 0)
    c = lax.broadcasted_iota(jnp.int32, (BLK, BLK), 1)
    tri = r >= c
    cs = _dot01_left(tri.astype(BF16), d_a)
    cs_t = cs.T
    hh = lax.broadcasted_iota(jnp.int32, (BLK, SSM_D_INNER), 0)
    cc = lax.broadcasted_iota(jnp.int32, (BLK, SSM_D_INNER), 1)
    expand = (cc // SSM_HEAD_DIM == hh).astype(BF16)
    dt_rep = _dot01_right(dt, expand)
    ecs_rep = _dot01_right(jnp.exp(cs), expand)
    dec_rep = _dot01_right(jnp.exp(cs[BLK - 1:BLK, :] - cs), expand)

    xdt = xs * dt_rep
    xdtd = (xdt * dec_rep).astype(BF16)
    xdt_b = xdt.astype(BF16)
    chunk_decay = ecs_rep[BLK - 1:BLK, :]

    y_parts, new_state = [], []
    for g in range(SSM_GROUPS):
        b_g = bc[:, g * SSM_STATE:(g + 1) * SSM_STATE]
        c_g = bc[:, (SSM_GROUPS + g) * SSM_STATE:(SSM_GROUPS + g + 1) * SSM_STATE]
        cols = slice(g * 512, (g + 1) * 512)
        cb_g = _dot_nt(c_g.astype(BF16), b_g.astype(BF16))
        y_off = _dot(c_g.astype(BF16), state[g].astype(BF16)) * ecs_rep[:, cols]
        new_state.append(chunk_decay[:, cols] * state[g] + _dot(b_g.T.astype(BF16), xdtd[:, cols]))
        y_diag = []
        for hp in range(SSM_HPG):
            h = g * SSM_HPG + hp
            seg = cs[:, h:h + 1] - cs_t[h:h + 1, :]
            lmat = jnp.exp(jnp.where(tri, seg, -jnp.inf))
            m = (cb_g * lmat).astype(BF16)
            y_diag.append(_dot(m, xdt_b[:, h * SSM_HEAD_DIM:(h + 1) * SSM_HEAD_DIM]))
        y_parts.append(jnp.concatenate(y_diag, axis=1) + y_off)
    return jnp.concatenate(y_parts, axis=1), xs, new_state


def _halves(lo_ref, hi_ref):
    return jnp.concatenate([lo_ref[...], hi_ref[...]], axis=1)


def _ssd_meta_kernel(xs0_ref, xs1_ref, bc_ref, dt_ref, cw_ref, cb_ref, dtb_ref, alog_ref, st_ref):
    rows = lax.broadcasted_iota(jnp.int32, (BLK, 1), 0)
    zero_state = [jnp.zeros((SSM_STATE, 512), F32) for _ in range(SSM_GROUPS)]
    _, _, st = _ssd_chunk(_halves(xs0_ref, xs1_ref), bc_ref[...], jnp.zeros((8, SSM_D_INNER), F32),
                          jnp.zeros((8, D_BC), F32), dt_ref[...], cw_ref[...], cb_ref[...],
                          dtb_ref[...], alog_ref[...], rows >= BLK - N_META, zero_state)
    for g in range(SSM_GROUPS):
        st_ref[g] = st[g]


def _ssd_kernel(xs0_ref, xs1_ref, bc_ref, z0_ref, z1_ref, dt_ref, txs0_ref, txs1_ref, tbc_ref,
                mxs0_ref, mxs1_ref, mbc_ref, st0_ref,
                cw_ref, cb_ref, dtb_ref, alog_ref, dsk_ref, nw_ref, o_ref, st_ref):
    first = pl.program_id(1) == 0

    @pl.when(first)
    def _():
        st_ref[...] = st0_ref[...]

    tail_xs = jnp.where(first, _halves(mxs0_ref, mxs1_ref), _halves(txs0_ref, txs1_ref))
    tail_bc = jnp.where(first, mbc_ref[...], tbc_ref[...])
    state = [st_ref[g] for g in range(SSM_GROUPS)]
    y, xs, new_state = _ssd_chunk(_halves(xs0_ref, xs1_ref), bc_ref[...], tail_xs, tail_bc, dt_ref[...],
                                  cw_ref[...], cb_ref[...], dtb_ref[...], alog_ref[...], None, state)
    for g in range(SSM_GROUPS):
        st_ref[g] = new_state[g]
    y = y + xs * dsk_ref[...]
    yg = y * _silu(_halves(z0_ref, z1_ref))
    outs = []
    for g in range(SSM_GROUPS):
        part = yg[:, g * 512:(g + 1) * 512]
        ms = jnp.mean(part * part, axis=-1, keepdims=True)
        outs.append(part * lax.rsqrt(ms + EPS))
    o_ref[...] = (jnp.concatenate(outs, axis=1) * nw_ref[...]).astype(o_ref.dtype)


def _pad_lanes(v, n=BLK):
    v = v.reshape(1, -1)
    return jnp.pad(v, ((0, 0), (0, n - v.shape[1])))


def _ssd(proj, proj_meta, conv_w, conv_b, dt_bias, a_log, d_skip, norm_w, batch, nc, ymix_dtype):
    rows = batch * nc * BLK
    cb = conv_b.reshape(1, D_XBC)
    dtb, alog = _pad_lanes(dt_bias), _pad_lanes(a_log)
    dsk = jnp.repeat(d_skip, SSM_HEAD_DIM).reshape(1, SSM_D_INNER)
    xs_c, z_c, bc_c, dt_c = COL_XS // HALF, COL_Z // HALF, COL_BC // D_BC, COL_DT // BLK
    full = lambda shape: pl.BlockSpec(shape, lambda *_: (0,) * len(shape))

    state0 = pl.pallas_call(
        _ssd_meta_kernel,
        grid=(1,),
        in_specs=[
            pl.BlockSpec((BLK, HALF), lambda i: (0, xs_c)),
            pl.BlockSpec((BLK, HALF), lambda i: (0, xs_c + 1)),
            pl.BlockSpec((BLK, D_BC), lambda i: (0, bc_c)),
            pl.BlockSpec((BLK, BLK), lambda i: (0, dt_c)),
            full((CONV_WIDTH, D_XBC)), full((1, D_XBC)), full((1, BLK)), full((1, BLK)),
        ],
        out_specs=full((SSM_GROUPS, SSM_STATE, 512)),
        out_shape=jax.ShapeDtypeStruct((SSM_GROUPS, SSM_STATE, 512), F32),
        compiler_params=_params("arbitrary"),
        name="ssd_meta_state",
    )(proj_meta, proj_meta, proj_meta, proj_meta, conv_w, cb, dtb, alog)

    def cur(col, width):
        return pl.BlockSpec((BLK, width), lambda b, c: (b * nc + c, col))

    def tail(col, width):
        return pl.BlockSpec((8, width), lambda b, c: (jnp.maximum((b * nc + c) * (BLK // 8) - 1, 0), col))

    def meta_tail(col, width):
        return pl.BlockSpec((8, width), lambda b, c: (BLK // 8 - 1, col))

    return pl.pallas_call(
        _ssd_kernel,
        grid=(batch, nc),
        in_specs=[
            cur(xs_c, HALF), cur(xs_c + 1, HALF), cur(bc_c, D_BC), cur(z_c, HALF), cur(z_c + 1, HALF),
            cur(dt_c, BLK),
            tail(xs_c, HALF), tail(xs_c + 1, HALF), tail(bc_c, D_BC),
            meta_tail(xs_c, HALF), meta_tail(xs_c + 1, HALF), meta_tail(bc_c, D_BC),
            full((SSM_GROUPS, SSM_STATE, 512)),
            full((CONV_WIDTH, D_XBC)), full((1, D_XBC)), full((1, BLK)), full((1, BLK)),
            full((1, SSM_D_INNER)), full((1, SSM_D_INNER)),
        ],
        out_specs=pl.BlockSpec((BLK, SSM_D_INNER), lambda b, c: (b * nc + c, 0)),
        out_shape=jax.ShapeDtypeStruct((rows, SSM_D_INNER), ymix_dtype),
        scratch_shapes=[pltpu.VMEM((SSM_GROUPS, SSM_STATE, 512), F32)],
        compiler_params=_params("parallel", "arbitrary"),
        name="ssd_mixer",
    )(proj, proj, proj, proj, proj, proj, proj, proj, proj, proj_meta, proj_meta, proj_meta, state0,
      conv_w, cb, dtb, alog, dsk, norm_w)


def _outproj_kernel(ya_ref, ys_ref, x_ref, w_ref, g_ref, h_ref, xn_ref):
    y = jnp.concatenate([ya_ref[...], ys_ref[...]], axis=1).astype(BF16)
    h = x_ref[...] + _dot(y, w_ref[...])
    h_ref[...] = h
    ms = jnp.mean(h * h, axis=-1, keepdims=True)
    xn_ref[...] = (h * lax.rsqrt(ms + EPS) * g_ref[...]).astype(BF16)


def _outproj(ya, ys, x2d, w_out, gain):
    m = x2d.shape[0]
    tm = min(m, 512)
    return pl.pallas_call(
        _outproj_kernel,
        grid=(m // tm,),
        in_specs=[
            pl.BlockSpec((tm, D_ATTN), lambda i: (i, 0)),
            pl.BlockSpec((tm, SSM_D_INNER), lambda i: (i, 0)),
            pl.BlockSpec((tm, D_MODEL), lambda i: (i, 0)),
            pl.BlockSpec((D_MODEL, D_MODEL), lambda i: (0, 0), pipeline_mode=pl.Buffered(1)),
            pl.BlockSpec((1, D_MODEL), lambda i: (0, 0)),
        ],
        out_specs=[pl.BlockSpec((tm, D_MODEL), lambda i: (i, 0)),
                   pl.BlockSpec((tm, D_MODEL), lambda i: (i, 0))],
        out_shape=[jax.ShapeDtypeStruct((m, D_MODEL), F32),
                   jax.ShapeDtypeStruct((m, D_MODEL), BF16)],
        compiler_params=_params("parallel"),
        name="outproj",
    )(ya, ys, x2d, w_out, gain)


def _top16_rows(s, want_rank):
    vals = []
    rank = jnp.full_like(s, float(PEER_TOPK)) if want_rank else None
    for r in range(PEER_TOPK):
        m = jnp.max(s, axis=0, keepdims=True)
        vals.append(m)
        if want_rank or r < PEER_TOPK - 1:
            hit = s == m
        if want_rank:
            rank = jnp.where(hit, float(r), rank)
        if r < PEER_TOPK - 1:
            s = jnp.where(hit, -jnp.inf, s)
    return vals, rank


def _route_kernel(xn_ref, wq_ref, keys_ref, n_ref, e1_ref, r2_ref, e2_ref, v_ref):
    q = _dot(xn_ref[...], wq_ref[...]).astype(BF16)
    for h in range(PEER_HEADS):
        s, tops, ranks = [], [], []
        for c in range(2):
            qs = q[:, (2 * h + c) * PEER_HALF:(2 * h + c + 1) * PEER_HALF]
            sc = _dot_nt(keys_ref[h, c], qs)
            s.append(sc)
            vals, rank = _top16_rows(sc, want_rank=(c == 1))
            ranks.append(rank)
            for r, m in enumerate(vals):
                v_ref[c, r:r + 1, :] = m
            tops.append(v_ref[c])
        v1, v2 = tops
        blocks = [v1[0:1] + v2, v1[1:2] + v2[0:8]]
        blocks += [v1[a:a + 1] + v2[0:8] for a in range(2, 8)]
        blocks.append(v1[8:16] + v2[0:1])
        cand = jnp.concatenate(blocks, axis=0)
        top = v1[0:1] + v2[0:1]
        rem = cand
        for _ in range(PEER_TOPK - 1):
            m = jnp.max(rem, axis=0, keepdims=True)
            rem = jnp.where(rem == m, -jnp.inf, rem)
        tau = jnp.max(rem, axis=0, keepdims=True)
        z = jnp.sum(jnp.where(cand >= tau, jnp.exp(cand - top), 0.0), axis=0, keepdims=True)
        cnt = [jnp.sum(jnp.where(blocks[a] >= tau, 1.0, 0.0), axis=0, keepdims=True) for a in range(8)]
        cnt_hi = jnp.where(blocks[8] >= tau, 1.0, 0.0)
        n_sel = jnp.zeros_like(s[0])
        for a in range(PEER_TOPK):
            n_a = cnt[a] if a < 8 else cnt_hi[a - 8:a - 7]
            n_sel = jnp.where(s[0] == v1[a:a + 1], n_a, n_sel)
        n_ref[h] = n_sel
        e1_ref[h] = jnp.exp(s[0] - v1[0:1]) / z
        e2 = jnp.exp(s[1] - v2[0:1]).astype(BF16)
        r2 = ranks[1].astype(BF16)
        for lt in range(r2.shape[1] // BLK):
            lanes = slice(lt * BLK, (lt + 1) * BLK)
            r2_ref[h, lt] = pltpu.bitcast(r2[:, lanes], jnp.uint32)
            e2_ref[h, lt] = pltpu.bitcast(e2[:, lanes], jnp.uint32)


def _route(xn, wq, keys):
    t = xn.shape[0]
    tm = min(t, 256)
    big = pl.BlockSpec((PEER_HEADS, N_KEYS, tm), lambda i: (0, 0, i))
    big_shape = jax.ShapeDtypeStruct((PEER_HEADS, N_KEYS, t), F32)
    tiled = pl.BlockSpec((PEER_HEADS, tm // BLK, N_KEYS // 2, BLK), lambda i: (0, i, 0, 0))
    tiled_shape = jax.ShapeDtypeStruct((PEER_HEADS, t // BLK, N_KEYS // 2, BLK), jnp.uint32)
    return pl.pallas_call(
        _route_kernel,
        grid=(t // tm,),
        in_specs=[
            pl.BlockSpec((tm, D_MODEL), lambda i: (i, 0)),
            pl.BlockSpec((D_MODEL, D_MODEL), lambda i: (0, 0)),
            pl.BlockSpec((PEER_HEADS, 2, N_KEYS, PEER_HALF), lambda i: (0, 0, 0, 0)),
        ],
        out_specs=[big, big, tiled, tiled],
        out_shape=[big_shape, big_shape, tiled_shape, tiled_shape],
        scratch_shapes=[pltpu.VMEM((2, PEER_TOPK, tm), F32)],
        compiler_params=_params("parallel"),
        name="peer_route",
    )(xn, wq, keys)


PEER_TB = 512
PEER_EB = 1024
PEER_JCH = 128


def _peer_kernel(xn_ref, u_ref, vt_ref, n_ref, e1_ref, r2_ref, e2_ref, o_ref, a_ref, w_ref):
    @pl.when(pl.program_id(1) == 0)
    def _():
        o_ref[...] = jnp.zeros_like(o_ref)

    a_ref[...] = _dot_nt(u_ref[...].astype(BF16), xn_ref[...])
    tb = xn_ref.shape[0]

    def per_key(ii, carry):
        row0 = pl.multiple_of(ii * N_KEYS, N_KEYS)
        n_rows = [n_ref[h, pl.ds(ii, 1), :] for h in range(PEER_HEADS)]
        e1_rows = [e1_ref[h, pl.ds(ii, 1), :] for h in range(PEER_HEADS)]
        for lt in range(tb // BLK):
            lanes = slice(lt * BLK, (lt + 1) * BLK)
            wide = lambda row: jnp.broadcast_to(row[:, lanes], (PEER_JCH, BLK)).astype(BF16)
            n_b = [wide(r) for r in n_rows]
            e1_b = [wide(r) for r in e1_rows]
            for j0 in range(0, N_KEYS, PEER_JCH):
                js = slice(j0 // 2, (j0 + PEER_JCH) // 2)
                acc = None
                for h in range(PEER_HEADS):
                    keep = pltpu.bitcast(r2_ref[h, lt, js, :], BF16) < n_b[h]
                    gate = jnp.where(keep, pltpu.bitcast(e2_ref[h, lt, js, :], BF16), 0.0) * e1_b[h]
                    acc = gate if acc is None else acc + gate
                rows = pl.ds(row0 + j0, PEER_JCH)
                w_ref[rows, lanes] = acc * _gelu_exact(a_ref[rows, lanes]).astype(BF16)
        return carry

    lax.fori_loop(0, PEER_EB // N_KEYS, per_key, 0)
    o_ref[...] += _dot(vt_ref[...], w_ref[...])


def _peer(xn, u, vt_b, n_sel, e1, r2, e2):
    t = xn.shape[0]
    tb = min(t, PEER_TB)
    assert tb == PEER_TB
    ni = PEER_EB // N_KEYS
    small = pl.BlockSpec((PEER_HEADS, ni, tb), lambda i, e: (0, e, i))
    big = pl.BlockSpec((PEER_HEADS, tb // BLK, N_KEYS // 2, BLK), lambda i, e: (0, i, 0, 0))
    return pl.pallas_call(
        _peer_kernel,
        grid=(t // tb, N_EXPERTS // PEER_EB),
        in_specs=[
            pl.BlockSpec((tb, D_MODEL), lambda i, e: (i, 0)),
            pl.BlockSpec((PEER_EB, D_MODEL), lambda i, e: (e, 0)),
            pl.BlockSpec((D_MODEL, PEER_EB), lambda i, e: (0, e)),
            small, small, big, big,
        ],
        out_specs=pl.BlockSpec((D_MODEL, tb), lambda i, e: (0, i)),
        out_shape=jax.ShapeDtypeStruct((D_MODEL, t), F32),
        scratch_shapes=[pltpu.VMEM((PEER_EB, tb), F32), pltpu.VMEM((PEER_EB, tb), BF16)],
        compiler_params=_params("parallel", "arbitrary"),
        name="peer_experts",
    )(xn, u, vt_b, n_sel, e1, r2, e2)


def _final_kernel(h_ref, pt_ref, g_ref, o_ref):
    h = h_ref[...] + pt_ref[...].T
    ms = jnp.mean(h * h, axis=-1, keepdims=True)
    o_ref[...] = h * lax.rsqrt(ms + EPS) * g_ref[...]


def _final(h1, peer_t, gain):
    t = h1.shape[0]
    tm = min(t, 512)
    rows = pl.BlockSpec((tm, D_MODEL), lambda i: (i, 0))
    return pl.pallas_call(
        _final_kernel,
        grid=(t // tm,),
        in_specs=[rows, pl.BlockSpec((D_MODEL, tm), lambda i: (0, i)),
                  pl.BlockSpec((1, D_MODEL), lambda i: (0, 0))],
        out_specs=rows,
        out_shape=jax.ShapeDtypeStruct((t, D_MODEL), F32),
        compiler_params=_params("parallel"),
        name="final_norm",
    )(h1, peer_t, gain)


def _mixer(x2d, batch, seq, meta_tokens, rel_bias, ln_mix, w_in, sinks, conv_w, conv_b, dt_bias,
           a_log, d_skip, attn_norm_w, ssm_norm_w):
    nb = seq // BLK
    w_p = jnp.pad(w_in.astype(BF16), ((0, 0), (0, D_PROJ - D_IN)))
    gain = ln_mix.reshape(1, D_MODEL)
    meta_pad = jnp.concatenate([jnp.zeros((BLK - N_META, D_MODEL), F32), meta_tokens.astype(F32)], axis=0)
    proj = _inproj(x2d, gain, w_p)
    proj_meta = _inproj(meta_pad, gain, w_p)

    bucket, valid = _band_tables(nb)
    tab = rel_bias.astype(F32)
    bias = jnp.full((2, ATTN_HEADS) + bucket.shape[1:], NEG, F32)
    for b in range(REL_BUCKETS):
        bias = jnp.where((valid & (bucket == b))[:, None], tab[b][None, :, None, None], bias)
    ya = _attention(proj, proj_meta, sinks.astype(F32), bias, attn_norm_w.reshape(1, D_ATTN),
                    batch, nb, F32)
    ys = _ssd(proj, proj_meta, conv_w, conv_b, dt_bias, a_log, d_skip,
                   ssm_norm_w.reshape(1, SSM_D_INNER), batch, nb, F32)
    return ya, ys


def kernel(x, meta_tokens, rel_bias, ln_mix, w_in, attn_sinks, conv_w, conv_b, dt_bias, a_log, d_skip,
           attn_norm_w, ssm_norm_w, w_out, ln_ffn, peer_wq, peer_keys, peer_u, peer_v, ln_final):
    batch, seq, _ = x.shape
    x2d = x.reshape(batch * seq, D_MODEL)
    ya, ys = _mixer(x2d, batch, seq, meta_tokens, rel_bias, ln_mix[0], w_in[0], attn_sinks[0],
                         conv_w[0], conv_b[0], dt_bias[0], a_log[0], d_skip[0], attn_norm_w[0],
                         ssm_norm_w[0])
    h1, xn = _outproj(ya, ys, x2d, w_out[0].astype(BF16), ln_ffn[0].reshape(1, D_MODEL))
    n_sel, e1, r2, e2 = _route(xn, peer_wq[0].astype(BF16), peer_keys[0].astype(BF16))
    peer_t = _peer(xn, peer_u[0], peer_v[0].T.astype(BF16), n_sel, e1, r2, e2)
    out = _final(h1, peer_t, ln_final.reshape(1, D_MODEL))
    return out.reshape(batch, seq, D_MODEL)
```

```python
import functools

import jax
import jax.numpy as jnp
import numpy as np
from jax import lax
from jax.experimental import pallas as pl
from jax.experimental.pallas import tpu as pltpu

F32 = jnp.float32
BF16 = jnp.bfloat16

D_MODEL = 2048
N_META = 16
HEAD_DIM = 64
D_ATTN = 1024
ATTN_HEADS = 16
ATTN_KV_HEADS = 4
ATTN_GROUP = 4
D_KV = 256
WINDOW = 128
BLK = 128
REL_BUCKETS = 32
REL_MAX_DIST = 128
SSM_D_INNER = 1024
SSM_HEAD_DIM = 64
SSM_HEADS = 16
SSM_GROUPS = 2
SSM_HPG = 8
SSM_STATE = 128
CONV_WIDTH = 4
D_XBC = 1536
D_BC = 2 * SSM_GROUPS * SSM_STATE
PEER_HEADS = 8
PEER_TOPK = 16
N_KEYS = 128
N_EXPERTS = N_KEYS * N_KEYS
PEER_HALF = 128
EPS = 1e-6
NEG = -1e30

COL_Q = 0
COL_K = 1024
COL_V = 1280
COL_Z = 1536
COL_XS = 2560
COL_BC = 3584
COL_DT = 4096
D_IN = 4112
D_PROJ = 4224
HALF = 512

VMEM_LIMIT = 56 * 1024 * 1024


def _params(*sem, flags=None):
    return pltpu.CompilerParams(dimension_semantics=sem, vmem_limit_bytes=VMEM_LIMIT, flags=flags)


def _dot(a, b):
    return jnp.dot(a, b, preferred_element_type=F32)


def _dot_nt(a, b):
    return lax.dot_general(a, b, (((1,), (1,)), ((), ())), preferred_element_type=F32)


def _split3(x):
    hi = x.astype(BF16)
    r = x - hi.astype(F32)
    mid = r.astype(BF16)
    lo = (r - mid.astype(F32)).astype(BF16)
    return hi, mid, lo


def _dot01_left(m01, x):
    hi, mid, lo = _split3(x)
    return _dot(m01, hi) + _dot(m01, mid) + _dot(m01, lo)


def _dot01_right(x, m01):
    hi, mid, lo = _split3(x)
    return _dot(hi, m01) + _dot(mid, m01) + _dot(lo, m01)


def _silu(x):
    return x * jax.nn.sigmoid(x)


def _softplus(x):
    return jnp.maximum(x, 0.0) + jnp.log1p(jnp.exp(-jnp.abs(x)))


def _gelu_exact(x):
    return 0.5 * x * (1.0 + lax.erf(x * np.float32(np.sqrt(0.5))))


def _inproj_kernel(x_ref, g_ref, w_ref, o_ref):
    x = x_ref[...]
    ms = jnp.mean(x * x, axis=-1, keepdims=True)
    xn = (x * lax.rsqrt(ms + EPS) * g_ref[...]).astype(BF16)
    o_ref[...] = _dot(xn, w_ref[...])


def _inproj(x2d, gain, w_p):
    m = x2d.shape[0]
    tm = min(m, 512)
    return pl.pallas_call(
        _inproj_kernel,
        grid=(m // tm,),
        in_specs=[
            pl.BlockSpec((tm, D_MODEL), lambda i: (i, 0)),
            pl.BlockSpec((1, D_MODEL), lambda i: (0, 0)),
            pl.BlockSpec((D_MODEL, D_PROJ), lambda i: (0, 0), pipeline_mode=pl.Buffered(1)),
        ],
        out_specs=pl.BlockSpec((tm, D_PROJ), lambda i: (i, 0)),
        out_shape=jax.ShapeDtypeStruct((m, D_PROJ), F32),
        compiler_params=_params("parallel"),
        name="inproj",
    )(x2d, gain, w_p)


def _t5_bucket(dist):
    n = np.maximum(dist, 0)
    max_exact = REL_BUCKETS // 2
    large = max_exact + (np.log(np.maximum(n, 1) / max_exact) / np.log(REL_MAX_DIST / max_exact)
                         * (REL_BUCKETS - max_exact)).astype(np.int32)
    large = np.minimum(large, REL_BUCKETS - 1)
    return np.where(n < max_exact, n, large).astype(np.int32)


N_BAND = BLK + N_META


def _band_tables(nb):
    r = np.arange(BLK)[:, None]
    q = np.arange(BLK)[None, :]
    m = np.arange(N_META)[:, None]
    buckets, valids = [], []
    for n in range(nb):
        upper = r > q
        d_band = np.where(upper, q - r + BLK, q - r)
        d_meta = N_META + n * BLK + q - m
        assert (d_band[upper] < WINDOW).all() and (d_band >= 0).all() and (d_meta >= 0).all()
        buckets.append(_t5_bucket(np.concatenate([d_band, d_meta], axis=0)))
        valids.append(np.concatenate([~upper | (n > 0), np.ones((N_META, BLK), bool)], axis=0))
    for n in range(2, nb):
        assert (buckets[n] == buckets[1]).all() and (valids[n] == valids[1]).all()
    last = min(1, nb - 1)
    return np.stack([buckets[0], buckets[last]]), np.stack([valids[0], valids[last]])


def _attn_kernel(sink_ref, q_ref, kp_ref, ko_ref, km_ref, vp_ref, vo_ref, vmt_ref,
                 bias_ref, nw_ref, o_ref, yt_ref):
    q = (q_ref[...] * np.float32(HEAD_DIM ** -0.5)).astype(BF16)
    upper = (lax.broadcasted_iota(jnp.int32, (BLK, BLK), 0)
             > lax.broadcasted_iota(jnp.int32, (BLK, BLK), 1))
    vpt = vp_ref[...].T.astype(BF16)
    vot = vo_ref[...].T.astype(BF16)
    vmt = vmt_ref[...].astype(BF16)
    for j in range(ATTN_KV_HEADS):
        ks = slice(j * HEAD_DIM, (j + 1) * HEAD_DIM)
        heads = [j * ATTN_GROUP + g for g in range(ATTN_GROUP)]
        q4 = jnp.concatenate([q[:, h * HEAD_DIM:(h + 1) * HEAD_DIM] for h in heads], axis=0)
        lp = _dot_nt(kp_ref[:, ks].astype(BF16), q4)
        lo = _dot_nt(ko_ref[:, ks].astype(BF16), q4)
        lm = _dot_nt(km_ref[:, ks].astype(BF16), q4)
        e_prev, e_own, e_meta, inv = [], [], [], []
        for g, h in enumerate(heads):
            cols = slice(g * BLK, (g + 1) * BLK)
            band = jnp.where(upper, lp[:, cols], lo[:, cols]) + bias_ref[0, h, 0:BLK, :]
            meta = lm[:, cols] + bias_ref[0, h, BLK:N_BAND, :]
            sink = sink_ref[h]
            mx = jnp.maximum(jnp.maximum(jnp.max(band, axis=0, keepdims=True),
                                         jnp.max(meta, axis=0, keepdims=True)), sink)
            eb = jnp.exp(band - mx)
            em = jnp.exp(meta - mx)
            denom = (jnp.sum(eb, axis=0, keepdims=True) + jnp.sum(em, axis=0, keepdims=True)
                     + jnp.exp(sink - mx))
            inv.append(1.0 / denom)
            e_prev.append(jnp.where(upper, eb, 0.0).astype(BF16))
            e_own.append(jnp.where(upper, 0.0, eb).astype(BF16))
            e_meta.append(em.astype(BF16))
        cat = lambda parts: jnp.concatenate(parts, axis=1)
        ot = (_dot(vpt[ks, :], cat(e_prev)) + _dot(vot[ks, :], cat(e_own))
              + _dot(vmt[ks, :], cat(e_meta))) * cat(inv)
        for g, h in enumerate(heads):
            yt_ref[h * HEAD_DIM:(h + 1) * HEAD_DIM, :] = ot[:, g * BLK:(g + 1) * BLK]
    yt = yt_ref[...]
    ms = jnp.mean(yt * yt, axis=0, keepdims=True)
    o_ref[...] = ((yt * lax.rsqrt(ms + EPS)).T * nw_ref[...]).astype(o_ref.dtype)


def _attention(proj, proj_meta, sinks, bias, norm_w, batch, nb, ymix_dtype):
    rows = batch * nb * BLK
    kcol, vcol = COL_K // D_KV, COL_V // D_KV
    meta_blk = (BLK - N_META) // N_META
    vm_t = proj_meta[BLK - N_META:, COL_V:COL_V + D_KV].T

    def cur(col):
        return lambda b, n: (b * nb + n, col)

    def prev(col):
        return lambda b, n: (b * nb + jnp.maximum(n - 1, 0), col)

    return pl.pallas_call(
        _attn_kernel,
        grid=(batch, nb),
        in_specs=[
            pl.BlockSpec(memory_space=pltpu.SMEM),
            pl.BlockSpec((BLK, D_ATTN), cur(0)),
            pl.BlockSpec((BLK, D_KV), prev(kcol)),
            pl.BlockSpec((BLK, D_KV), cur(kcol)),
            pl.BlockSpec((N_META, D_KV), lambda b, n: (meta_blk, kcol)),
            pl.BlockSpec((BLK, D_KV), prev(vcol)),
            pl.BlockSpec((BLK, D_KV), cur(vcol)),
            pl.BlockSpec((D_KV, N_META), lambda b, n: (0, 0)),
            pl.BlockSpec((1, ATTN_HEADS, N_BAND, BLK), lambda b, n: (jnp.minimum(n, 1), 0, 0, 0)),
            pl.BlockSpec((1, D_ATTN), lambda b, n: (0, 0)),
        ],
        out_specs=pl.BlockSpec((BLK, D_ATTN), cur(0)),
        out_shape=jax.ShapeDtypeStruct((rows, D_ATTN), ymix_dtype),
        scratch_shapes=[pltpu.VMEM((D_ATTN, BLK), F32)],
        compiler_params=_params("parallel", "arbitrary"),
        name="swa_attention",
    )(sinks, proj, proj, proj, proj_meta, proj, proj, vm_t, bias, norm_w)


def _ssd_chunk(xs_raw, bc_raw, tail_xs, tail_bc, dt_raw, cw, cb, dtb, alog, row_mask, state):
    def conv(blk, tail, w, b):
        ext = jnp.concatenate([tail, blk], axis=0)
        acc = b
        for k in range(CONV_WIDTH):
            acc = acc + w[k:k + 1, :] * ext[5 + k:5 + k + BLK, :]
        return _silu(acc)

    xs = conv(xs_raw, tail_xs, cw[:, :SSM_D_INNER], cb[:, :SSM_D_INNER])
    bc = conv(bc_raw, tail_bc, cw[:, SSM_D_INNER:], cb[:, SSM_D_INNER:])
    dt = _softplus(dt_raw + dtb)
    if row_mask is not None:
        xs = jnp.where(row_mask, xs, 0.0)
        bc = jnp.where(row_mask, bc, 0.0)
        dt = jnp.where(row_mask, dt, 0.0)
    a_neg = -jnp.exp(alog)
    d_a = dt * a_neg

    r = lax.broadcasted_iota(jnp.int32, (BLK, BLK), 0)
    c = lax.broadcasted_iota(jnp.int32, (BLK, BLK), 1)
    tri = r >= c
    cs = _dot01_left(tri.astype(BF16), d_a)
    cs_t = cs.T
    hh = lax.broadcasted_iota(jnp.int32, (BLK, SSM_D_INNER), 0)
    cc = lax.broadcasted_iota(jnp.int32, (BLK, SSM_D_INNER), 1)
    expand = (cc // SSM_HEAD_DIM == hh).astype(BF16)
    dt_rep = _dot01_right(dt, expand)
    ecs_rep = _dot01_right(jnp.exp(cs), expand)
    dec_rep = _dot01_right(jnp.exp(cs[BLK - 1:BLK, :] - cs), expand)

    xdt = xs * dt_rep
    xdtd = (xdt * dec_rep).astype(BF16)
    xdt_b = xdt.astype(BF16)
    chunk_decay = ecs_rep[BLK - 1:BLK, :]

    y_parts, new_state = [], []
    for g in range(SSM_GROUPS):
        b_g = bc[:, g * SSM_STATE:(g + 1) * SSM_STATE]
        c_g = bc[:, (SSM_GROUPS + g) * SSM_STATE:(SSM_GROUPS + g + 1) * SSM_STATE]
        cols = slice(g * 512, (g + 1) * 512)
        cb_g = _dot_nt(c_g.astype(BF16), b_g.astype(BF16))
        y_off = _dot(c_g.astype(BF16), state[g].astype(BF16)) * ecs_rep[:, cols]
        new_state.append(chunk_decay[:, cols] * state[g] + _dot(b_g.T.astype(BF16), xdtd[:, cols]))
        y_diag = []
        for hp in range(SSM_HPG):
            h = g * SSM_HPG + hp
            seg = cs[:, h:h + 1] - cs_t[h:h + 1, :]
            lmat = jnp.exp(jnp.where(tri, seg, -jnp.inf))
            m = (cb_g * lmat).astype(BF16)
            y_diag.append(_dot(m, xdt_b[:, h * SSM_HEAD_DIM:(h + 1) * SSM_HEAD_DIM]))
        y_parts.append(jnp.concatenate(y_diag, axis=1) + y_off)
    return jnp.concatenate(y_parts, axis=1), xs, new_state


def _halves(lo_ref, hi_ref):
    return jnp.concatenate([lo_ref[...], hi_ref[...]], axis=1)


def _ssd_meta_kernel(xs0_ref, xs1_ref, bc_ref, dt_ref, cw_ref, cb_ref, dtb_ref, alog_ref, st_ref):
    rows = lax.broadcasted_iota(jnp.int32, (BLK, 1), 0)
    zero_state = [jnp.zeros((SSM_STATE, 512), F32) for _ in range(SSM_GROUPS)]
    _, _, st = _ssd_chunk(_halves(xs0_ref, xs1_ref), bc_ref[...], jnp.zeros((8, SSM_D_INNER), F32),
                          jnp.zeros((8, D_BC), F32), dt_ref[...], cw_ref[...], cb_ref[...],
                          dtb_ref[...], alog_ref[...], rows >= BLK - N_META, zero_state)
    for g in range(SSM_GROUPS):
        st_ref[g] = st[g]


def _ssd_kernel(xs0_ref, xs1_ref, bc_ref, z0_ref, z1_ref, dt_ref, txs0_ref, txs1_ref, tbc_ref,
                mxs0_ref, mxs1_ref, mbc_ref, st0_ref,
                cw_ref, cb_ref, dtb_ref, alog_ref, dsk_ref, nw_ref, o_ref, st_ref):
    first = pl.program_id(1) == 0

    @pl.when(first)
    def _():
        st_ref[...] = st0_ref[...]

    tail_xs = jnp.where(first, _halves(mxs0_ref, mxs1_ref), _halves(txs0_ref, txs1_ref))
    tail_bc = jnp.where(first, mbc_ref[...], tbc_ref[...])
    state = [st_ref[g] for g in range(SSM_GROUPS)]
    y, xs, new_state = _ssd_chunk(_halves(xs0_ref, xs1_ref), bc_ref[...], tail_xs, tail_bc, dt_ref[...],
                                  cw_ref[...], cb_ref[...], dtb_ref[...], alog_ref[...], None, state)
    for g in range(SSM_GROUPS):
        st_ref[g] = new_state[g]
    y = y + xs * dsk_ref[...]
    yg = y * _silu(_halves(z0_ref, z1_ref))
    outs = []
    for g in range(SSM_GROUPS):
        part = yg[:, g * 512:(g + 1) * 512]
        ms = jnp.mean(part * part, axis=-1, keepdims=True)
        outs.append(part * lax.rsqrt(ms + EPS))
    o_ref[...] = (jnp.concatenate(outs, axis=1) * nw_ref[...]).astype(o_ref.dtype)


def _pad_lanes(v, n=BLK):
    v = v.reshape(1, -1)
    return jnp.pad(v, ((0, 0), (0, n - v.shape[1])))


def _ssd(proj, proj_meta, conv_w, conv_b, dt_bias, a_log, d_skip, norm_w, batch, nc, ymix_dtype):
    rows = batch * nc * BLK
    cb = conv_b.reshape(1, D_XBC)
    dtb, alog = _pad_lanes(dt_bias), _pad_lanes(a_log)
    dsk = jnp.repeat(d_skip, SSM_HEAD_DIM).reshape(1, SSM_D_INNER)
    xs_c, z_c, bc_c, dt_c = COL_XS // HALF, COL_Z // HALF, COL_BC // D_BC, COL_DT // BLK
    full = lambda shape: pl.BlockSpec(shape, lambda *_: (0,) * len(shape))

    state0 = pl.pallas_call(
        _ssd_meta_kernel,
        grid=(1,),
        in_specs=[
            pl.BlockSpec((BLK, HALF), lambda i: (0, xs_c)),
            pl.BlockSpec((BLK, HALF), lambda i: (0, xs_c + 1)),
            pl.BlockSpec((BLK, D_BC), lambda i: (0, bc_c)),
            pl.BlockSpec((BLK, BLK), lambda i: (0, dt_c)),
            full((CONV_WIDTH, D_XBC)), full((1, D_XBC)), full((1, BLK)), full((1, BLK)),
        ],
        out_specs=full((SSM_GROUPS, SSM_STATE, 512)),
        out_shape=jax.ShapeDtypeStruct((SSM_GROUPS, SSM_STATE, 512), F32),
        compiler_params=_params("arbitrary"),
        name="ssd_meta_state",
    )(proj_meta, proj_meta, proj_meta, proj_meta, conv_w, cb, dtb, alog)

    def cur(col, width):
        return pl.BlockSpec((BLK, width), lambda b, c: (b * nc + c, col))

    def tail(col, width):
        return pl.BlockSpec((8, width), lambda b, c: (jnp.maximum((b * nc + c) * (BLK // 8) - 1, 0), col))

    def meta_tail(col, width):
        return pl.BlockSpec((8, width), lambda b, c: (BLK // 8 - 1, col))

    return pl.pallas_call(
        _ssd_kernel,
        grid=(batch, nc),
        in_specs=[
            cur(xs_c, HALF), cur(xs_c + 1, HALF), cur(bc_c, D_BC), cur(z_c, HALF), cur(z_c + 1, HALF),
            cur(dt_c, BLK),
            tail(xs_c, HALF), tail(xs_c + 1, HALF), tail(bc_c, D_BC),
            meta_tail(xs_c, HALF), meta_tail(xs_c + 1, HALF), meta_tail(bc_c, D_BC),
            full((SSM_GROUPS, SSM_STATE, 512)),
            full((CONV_WIDTH, D_XBC)), full((1, D_XBC)), full((1, BLK)), full((1, BLK)),
            full((1, SSM_D_INNER)), full((1, SSM_D_INNER)),
        ],
        out_specs=pl.BlockSpec((BLK, SSM_D_INNER), lambda b, c: (b * nc + c, 0)),
        out_shape=jax.ShapeDtypeStruct((rows, SSM_D_INNER), ymix_dtype),
        scratch_shapes=[pltpu.VMEM((SSM_GROUPS, SSM_STATE, 512), F32)],
        compiler_params=_params("parallel", "arbitrary"),
        name="ssd_mixer",
    )(proj, proj, proj, proj, proj, proj, proj, proj, proj, proj_meta, proj_meta, proj_meta, state0,
      conv_w, cb, dtb, alog, dsk, norm_w)


def _outproj_kernel(ya_ref, ys_ref, x_ref, w_ref, g_ref, h_ref, xn_ref):
    y = jnp.concatenate([ya_ref[...], ys_ref[...]], axis=1).astype(BF16)
    h = x_ref[...] + _dot(y, w_ref[...])
    h_ref[...] = h
    ms = jnp.mean(h * h, axis=-1, keepdims=True)
    xn_ref[...] = (h * lax.rsqrt(ms + EPS) * g_ref[...]).astype(BF16)


def _outproj(ya, ys, x2d, w_out, gain):
    m = x2d.shape[0]
    tm = min(m, 512)
    return pl.pallas_call(
        _outproj_kernel,
        grid=(m // tm,),
        in_specs=[
            pl.BlockSpec((tm, D_ATTN), lambda i: (i, 0)),
            pl.BlockSpec((tm, SSM_D_INNER), lambda i: (i, 0)),
            pl.BlockSpec((tm, D_MODEL), lambda i: (i, 0)),
            pl.BlockSpec((D_MODEL, D_MODEL), lambda i: (0, 0), pipeline_mode=pl.Buffered(1)),
            pl.BlockSpec((1, D_MODEL), lambda i: (0, 0)),
        ],
        out_specs=[pl.BlockSpec((tm, D_MODEL), lambda i: (i, 0)),
                   pl.BlockSpec((tm, D_MODEL), lambda i: (i, 0))],
        out_shape=[jax.ShapeDtypeStruct((m, D_MODEL), F32),
                   jax.ShapeDtypeStruct((m, D_MODEL), BF16)],
        compiler_params=_params("parallel"),
        name="outproj",
    )(ya, ys, x2d, w_out, gain)


def _oddeven_sort_pairs(n):
    pairs = []
    p = 1
    while p < n:
        k = p
        while k >= 1:
            for j in range(k % p, n - k, 2 * k):
                for i in range(min(k, n - j - k)):
                    if (i + j) // (2 * p) == (i + j + k) // (2 * p):
                        pairs.append((i + j, i + j + k))
            k //= 2
        p *= 2
    return pairs


_SORT16 = _oddeven_sort_pairs(PEER_TOPK)


def _top16_tile(x):
    x = list(x)
    for i, j in _SORT16:
        x[i], x[j] = jnp.maximum(x[i], x[j]), jnp.minimum(x[i], x[j])
    for shift in (4, 2, 1):
        other = [pltpu.roll(v, shift, 0) for v in x]
        x = [jnp.maximum(x[r], other[PEER_TOPK - 1 - r]) for r in range(PEER_TOPK)]
        dist = PEER_TOPK // 2
        while dist >= 1:
            for i in range(PEER_TOPK):
                if i & dist == 0:
                    x[i], x[i + dist] = jnp.maximum(x[i], x[i + dist]), jnp.minimum(x[i], x[i + dist])
            dist //= 2
    return x


def _route_kernel(xn_ref, wq_ref, keys_ref, n_ref, e1_ref, r2_ref, e2_ref, v_ref):
    q = _dot(xn_ref[...], wq_ref[...]).astype(BF16)
    n_tiles = xn_ref.shape[0] // BLK
    for h in range(PEER_HEADS):
        s, tops = [], []
        for c in range(2):
            qs = q[:, (2 * h + c) * PEER_HALF:(2 * h + c + 1) * PEER_HALF]
            sc = _dot_nt(keys_ref[h, c], qs)
            s.append(sc)
            for lt in range(n_tiles):
                lanes = slice(lt * BLK, (lt + 1) * BLK)
                keys8 = [sc[8 * r:8 * r + 8, lanes] for r in range(N_KEYS // 8)]
                best = _top16_tile(keys8)
                for r in range(PEER_TOPK):
                    v_ref[c, r:r + 1, lanes] = best[r][0:1, :]
                if c == 1:
                    ranks = []
                    for x in keys8:
                        rank = jnp.full_like(x, float(PEER_TOPK))
                        for a in range(PEER_TOPK):
                            rank = jnp.where(x == best[a], float(a), rank)
                        ranks.append(rank)
                    r2 = jnp.concatenate(ranks, axis=0).astype(BF16)
                    e2 = jnp.exp(sc[:, lanes] - best[0][0:1, :]).astype(BF16)
                    r2_ref[h, lt] = pltpu.bitcast(r2, jnp.uint32)
                    e2_ref[h, lt] = pltpu.bitcast(e2, jnp.uint32)
            tops.append(v_ref[c])
        v1, v2 = tops
        blocks = [v1[0:1] + v2, v1[1:2] + v2[0:8]]
        blocks += [v1[a:a + 1] + v2[0:8] for a in range(2, 8)]
        blocks.append(v1[8:16] + v2[0:1])
        cand = jnp.concatenate(blocks, axis=0)
        top = v1[0:1] + v2[0:1]
        rem = cand
        for _ in range(PEER_TOPK - 1):
            m = jnp.max(rem, axis=0, keepdims=True)
            rem = jnp.where(rem == m, -jnp.inf, rem)
        tau = jnp.max(rem, axis=0, keepdims=True)
        z = jnp.sum(jnp.where(cand >= tau, jnp.exp(cand - top), 0.0), axis=0, keepdims=True)
        cnt = [jnp.sum(jnp.where(blocks[a] >= tau, 1.0, 0.0), axis=0, keepdims=True) for a in range(8)]
        cnt_hi = jnp.where(blocks[8] >= tau, 1.0, 0.0)
        n_sel = jnp.zeros_like(s[0])
        for a in range(PEER_TOPK):
            n_a = cnt[a] if a < 8 else cnt_hi[a - 8:a - 7]
            n_sel = jnp.where(s[0] == v1[a:a + 1], n_a, n_sel)
        n_ref[h] = n_sel
        e1_ref[h] = jnp.exp(s[0] - v1[0:1]) / z


def _route(xn, wq, keys):
    t = xn.shape[0]
    tm = min(t, 256)
    big = pl.BlockSpec((PEER_HEADS, N_KEYS, tm), lambda i: (0, 0, i))
    big_shape = jax.ShapeDtypeStruct((PEER_HEADS, N_KEYS, t), F32)
    tiled = pl.BlockSpec((PEER_HEADS, tm // BLK, N_KEYS // 2, BLK), lambda i: (0, i, 0, 0))
    tiled_shape = jax.ShapeDtypeStruct((PEER_HEADS, t // BLK, N_KEYS // 2, BLK), jnp.uint32)
    return pl.pallas_call(
        _route_kernel,
        grid=(t // tm,),
        in_specs=[
            pl.BlockSpec((tm, D_MODEL), lambda i: (i, 0)),
            pl.BlockSpec((D_MODEL, D_MODEL), lambda i: (0, 0)),
            pl.BlockSpec((PEER_HEADS, 2, N_KEYS, PEER_HALF), lambda i: (0, 0, 0, 0)),
        ],
        out_specs=[big, big, tiled, tiled],
        out_shape=[big_shape, big_shape, tiled_shape, tiled_shape],
        scratch_shapes=[pltpu.VMEM((2, PEER_TOPK, tm), F32)],
        compiler_params=_params("parallel"),
        name="peer_route",
    )(xn, wq, keys)


PEER_TB = 512
PEER_EB = 1024
PEER_JCH = 128


def _peer_kernel(xn_ref, u_ref, vt_ref, n_ref, e1_ref, r2_ref, e2_ref, o_ref, a_ref, w_ref):
    @pl.when(pl.program_id(1) == 0)
    def _():
        o_ref[...] = jnp.zeros_like(o_ref)

    a_ref[...] = _dot_nt(u_ref[...].astype(BF16), xn_ref[...])
    tb = xn_ref.shape[0]

    def per_key(ii, carry):
        row0 = pl.multiple_of(ii * N_KEYS, N_KEYS)
        n_rows = [n_ref[h, pl.ds(ii, 1), :] for h in range(PEER_HEADS)]
        e1_rows = [e1_ref[h, pl.ds(ii, 1), :] for h in range(PEER_HEADS)]
        for lt in range(tb // BLK):
            lanes = slice(lt * BLK, (lt + 1) * BLK)
            wide = lambda row: jnp.broadcast_to(row[:, lanes], (PEER_JCH, BLK)).astype(BF16)
            n_b = [wide(r) for r in n_rows]
            e1_b = [wide(r) for r in e1_rows]
            for j0 in range(0, N_KEYS, PEER_JCH):
                js = slice(j0 // 2, (j0 + PEER_JCH) // 2)
                acc = None
                for h in range(PEER_HEADS):
                    keep = pltpu.bitcast(r2_ref[h, lt, js, :], BF16) < n_b[h]
                    gate = jnp.where(keep, pltpu.bitcast(e2_ref[h, lt, js, :], BF16), 0.0) * e1_b[h]
                    acc = gate if acc is None else acc + gate
                rows = pl.ds(row0 + j0, PEER_JCH)
                w_ref[rows, lanes] = acc * _gelu_exact(a_ref[rows, lanes]).astype(BF16)
        return carry

    lax.fori_loop(0, PEER_EB // N_KEYS, per_key, 0)
    o_ref[...] += _dot(vt_ref[...], w_ref[...])


def _peer(xn, u, vt_b, n_sel, e1, r2, e2):
    t = xn.shape[0]
    tb = min(t, PEER_TB)
    assert tb == PEER_TB
    ni = PEER_EB // N_KEYS
    small = pl.BlockSpec((PEER_HEADS, ni, tb), lambda i, e: (0, e, i))
    big = pl.BlockSpec((PEER_HEADS, tb // BLK, N_KEYS // 2, BLK), lambda i, e: (0, i, 0, 0))
    return pl.pallas_call(
        _peer_kernel,
        grid=(t // tb, N_EXPERTS // PEER_EB),
        in_specs=[
            pl.BlockSpec((tb, D_MODEL), lambda i, e: (i, 0)),
            pl.BlockSpec((PEER_EB, D_MODEL), lambda i, e: (e, 0)),
            pl.BlockSpec((D_MODEL, PEER_EB), lambda i, e: (0, e)),
            small, small, big, big,
        ],
        out_specs=pl.BlockSpec((D_MODEL, tb), lambda i, e: (0, i)),
        out_shape=jax.ShapeDtypeStruct((D_MODEL, t), F32),
        scratch_shapes=[pltpu.VMEM((PEER_EB, tb), F32), pltpu.VMEM((PEER_EB, tb), BF16)],
        compiler_params=_params("parallel", "arbitrary"),
        name="peer_experts",
    )(xn, u, vt_b, n_sel, e1, r2, e2)


def _final_kernel(h_ref, pt_ref, g_ref, o_ref):
    h = h_ref[...] + pt_ref[...].T
    ms = jnp.mean(h * h, axis=-1, keepdims=True)
    o_ref[...] = h * lax.rsqrt(ms + EPS) * g_ref[...]


def _final(h1, peer_t, gain):
    t = h1.shape[0]
    tm = min(t, 512)
    rows = pl.BlockSpec((tm, D_MODEL), lambda i: (i, 0))
    return pl.pallas_call(
        _final_kernel,
        grid=(t // tm,),
        in_specs=[rows, pl.BlockSpec((D_MODEL, tm), lambda i: (0, i)),
                  pl.BlockSpec((1, D_MODEL), lambda i: (0, 0))],
        out_specs=rows,
        out_shape=jax.ShapeDtypeStruct((t, D_MODEL), F32),
        compiler_params=_params("parallel"),
        name="final_norm",
    )(h1, peer_t, gain)


def _mixer(x2d, batch, seq, meta_tokens, rel_bias, ln_mix, w_in, sinks, conv_w, conv_b, dt_bias,
           a_log, d_skip, attn_norm_w, ssm_norm_w):
    nb = seq // BLK
    w_p = jnp.pad(w_in.astype(BF16), ((0, 0), (0, D_PROJ - D_IN)))
    gain = ln_mix.reshape(1, D_MODEL)
    meta_pad = jnp.concatenate([jnp.zeros((BLK - N_META, D_MODEL), F32), meta_tokens.astype(F32)], axis=0)
    proj = _inproj(x2d, gain, w_p)
    proj_meta = _inproj(meta_pad, gain, w_p)

    bucket, valid = _band_tables(nb)
    tab = rel_bias.astype(F32)
    bias = jnp.full((2, ATTN_HEADS) + bucket.shape[1:], NEG, F32)
    for b in range(REL_BUCKETS):
        bias = jnp.where((valid & (bucket == b))[:, None], tab[b][None, :, None, None], bias)
    ya = _attention(proj, proj_meta, sinks.astype(F32), bias, attn_norm_w.reshape(1, D_ATTN),
                    batch, nb, F32)
    ys = _ssd(proj, proj_meta, conv_w, conv_b, dt_bias, a_log, d_skip,
                   ssm_norm_w.reshape(1, SSM_D_INNER), batch, nb, F32)
    return ya, ys


def kernel(x, meta_tokens, rel_bias, ln_mix, w_in, attn_sinks, conv_w, conv_b, dt_bias, a_log, d_skip,
           attn_norm_w, ssm_norm_w, w_out, ln_ffn, peer_wq, peer_keys, peer_u, peer_v, ln_final):
    batch, seq, _ = x.shape
    x2d = x.reshape(batch * seq, D_MODEL)
    ya, ys = _mixer(x2d, batch, seq, meta_tokens, rel_bias, ln_mix[0], w_in[0], attn_sinks[0],
                         conv_w[0], conv_b[0], dt_bias[0], a_log[0], d_skip[0], attn_norm_w[0],
                         ssm_norm_w[0])
    h1, xn = _outproj(ya, ys, x2d, w_out[0].astype(BF16), ln_ffn[0].reshape(1, D_MODEL))
    n_sel, e1, r2, e2 = _route(xn, peer_wq[0].astype(BF16), peer_keys[0].astype(BF16))
    peer_t = _peer(xn, peer_u[0], peer_v[0].T.astype(BF16), n_sel, e1, r2, e2)
    out = _final(h1, peer_t, ln_final.reshape(1, D_MODEL))
    return out.reshape(batch, seq, D_MODEL)
```

```python
import functools

import jax
import jax.numpy as jnp
import numpy as np
from jax import lax
from jax.experimental import pallas as pl
from jax.experimental.pallas import tpu as pltpu

F32 = jnp.float32
BF16 = jnp.bfloat16

D_MODEL = 2048
N_META = 16
HEAD_DIM = 64
D_ATTN = 1024
ATTN_HEADS = 16
ATTN_KV_HEADS = 4
ATTN_GROUP = 4
D_KV = 256
WINDOW = 128
BLK = 128
REL_BUCKETS = 32
REL_MAX_DIST = 128
SSM_D_INNER = 1024
SSM_HEAD_DIM = 64
SSM_HEADS = 16
SSM_GROUPS = 2
SSM_HPG = 8
SSM_STATE = 128
CONV_WIDTH = 4
D_XBC = 1536
D_BC = 2 * SSM_GROUPS * SSM_STATE
PEER_HEADS = 8
PEER_TOPK = 16
N_KEYS = 128
N_EXPERTS = N_KEYS * N_KEYS
PEER_HALF = 128
EPS = 1e-6
NEG = -1e30

COL_Q = 0
COL_K = 1024
COL_V = 1280
COL_Z = 1536
COL_XS = 2560
COL_BC = 3584
COL_DT = 4096
D_IN = 4112
D_PROJ = 4224
HALF = 512

VMEM_LIMIT = 56 * 1024 * 1024


def _params(*sem, flags=None):
    return pltpu.CompilerParams(dimension_semantics=sem, vmem_limit_bytes=VMEM_LIMIT, flags=flags)


def _dot(a, b):
    return jnp.dot(a, b, preferred_element_type=F32)


def _dot_nt(a, b):
    return lax.dot_general(a, b, (((1,), (1,)), ((), ())), preferred_element_type=F32)


def _split3(x):
    hi = x.astype(BF16)
    r = x - hi.astype(F32)
    mid = r.astype(BF16)
    lo = (r - mid.astype(F32)).astype(BF16)
    return hi, mid, lo


def _dot01_left(m01, x):
    hi, mid, lo = _split3(x)
    return _dot(m01, hi) + _dot(m01, mid) + _dot(m01, lo)


def _dot01_right(x, m01):
    hi, mid, lo = _split3(x)
    return _dot(hi, m01) + _dot(mid, m01) + _dot(lo, m01)


def _silu(x):
    return x * jax.nn.sigmoid(x)


def _softplus(x):
    return jnp.maximum(x, 0.0) + jnp.log1p(jnp.exp(-jnp.abs(x)))


def _gelu_exact(x):
    return 0.5 * x * (1.0 + lax.erf(x * np.float32(np.sqrt(0.5))))


def _inproj_kernel(x_ref, g_ref, w_ref, o_ref):
    x = x_ref[...]
    ms = jnp.mean(x * x, axis=-1, keepdims=True)
    xn = (x * lax.rsqrt(ms + EPS) * g_ref[...]).astype(BF16)
    o_ref[...] = _dot(xn, w_ref[...])


def _inproj(x2d, gain, w_p):
    m = x2d.shape[0]
    tm = min(m, 512)
    return pl.pallas_call(
        _inproj_kernel,
        grid=(m // tm,),
        in_specs=[
            pl.BlockSpec((tm, D_MODEL), lambda i: (i, 0)),
            pl.BlockSpec((1, D_MODEL), lambda i: (0, 0)),
            pl.BlockSpec((D_MODEL, D_PROJ), lambda i: (0, 0), pipeline_mode=pl.Buffered(1)),
        ],
        out_specs=pl.BlockSpec((tm, D_PROJ), lambda i: (i, 0)),
        out_shape=jax.ShapeDtypeStruct((m, D_PROJ), F32),
        compiler_params=_params("parallel"),
        name="inproj",
    )(x2d, gain, w_p)


def _t5_bucket(dist):
    n = np.maximum(dist, 0)
    max_exact = REL_BUCKETS // 2
    large = max_exact + (np.log(np.maximum(n, 1) / max_exact) / np.log(REL_MAX_DIST / max_exact)
                         * (REL_BUCKETS - max_exact)).astype(np.int32)
    large = np.minimum(large, REL_BUCKETS - 1)
    return np.where(n < max_exact, n, large).astype(np.int32)


N_BAND = BLK + N_META


def _band_tables(nb):
    r = np.arange(BLK)[:, None]
    q = np.arange(BLK)[None, :]
    m = np.arange(N_META)[:, None]
    buckets, valids = [], []
    for n in range(nb):
        upper = r > q
        d_band = np.where(upper, q - r + BLK, q - r)
        d_meta = N_META + n * BLK + q - m
        assert (d_band[upper] < WINDOW).all() and (d_band >= 0).all() and (d_meta >= 0).all()
        buckets.append(_t5_bucket(np.concatenate([d_band, d_meta], axis=0)))
        valids.append(np.concatenate([~upper | (n > 0), np.ones((N_META, BLK), bool)], axis=0))
    for n in range(2, nb):
        assert (buckets[n] == buckets[1]).all() and (valids[n] == valids[1]).all()
    last = min(1, nb - 1)
    return np.stack([buckets[0], buckets[last]]), np.stack([valids[0], valids[last]])


def _attn_kernel(sink_ref, q_ref, kp_ref, ko_ref, km_ref, vp_ref, vo_ref, vmt_ref,
                 bias_ref, nw_ref, o_ref, yt_ref):
    q = (q_ref[...] * np.float32(HEAD_DIM ** -0.5)).astype(BF16)
    upper = (lax.broadcasted_iota(jnp.int32, (BLK, BLK), 0)
             > lax.broadcasted_iota(jnp.int32, (BLK, BLK), 1))
    vpt = vp_ref[...].T.astype(BF16)
    vot = vo_ref[...].T.astype(BF16)
    vmt = vmt_ref[...].astype(BF16)
    for j in range(ATTN_KV_HEADS):
        ks = slice(j * HEAD_DIM, (j + 1) * HEAD_DIM)
        heads = [j * ATTN_GROUP + g for g in range(ATTN_GROUP)]
        q4 = jnp.concatenate([q[:, h * HEAD_DIM:(h + 1) * HEAD_DIM] for h in heads], axis=0)
        lp = _dot_nt(kp_ref[:, ks].astype(BF16), q4)
        lo = _dot_nt(ko_ref[:, ks].astype(BF16), q4)
        lm = _dot_nt(km_ref[:, ks].astype(BF16), q4)
        e_prev, e_own, e_meta, inv = [], [], [], []
        for g, h in enumerate(heads):
            cols = slice(g * BLK, (g + 1) * BLK)
            band = jnp.where(upper, lp[:, cols], lo[:, cols]) + bias_ref[0, h, 0:BLK, :]
            meta = lm[:, cols] + bias_ref[0, h, BLK:N_BAND, :]
            sink = sink_ref[h]
            mx = jnp.maximum(jnp.maximum(jnp.max(band, axis=0, keepdims=True),
                                         jnp.max(meta, axis=0, keepdims=True)), sink)
            eb = jnp.exp(band - mx)
            em = jnp.exp(meta - mx)
            denom = (jnp.sum(eb, axis=0, keepdims=True) + jnp.sum(em, axis=0, keepdims=True)
                     + jnp.exp(sink - mx))
            inv.append(1.0 / denom)
            e_prev.append(jnp.where(upper, eb, 0.0).astype(BF16))
            e_own.append(jnp.where(upper, 0.0, eb).astype(BF16))
            e_meta.append(em.astype(BF16))
        cat = lambda parts: jnp.concatenate(parts, axis=1)
        ot = (_dot(vpt[ks, :], cat(e_prev)) + _dot(vot[ks, :], cat(e_own))
              + _dot(vmt[ks, :], cat(e_meta))) * cat(inv)
        for g, h in enumerate(heads):
            yt_ref[h * HEAD_DIM:(h + 1) * HEAD_DIM, :] = ot[:, g * BLK:(g + 1) * BLK]
    yt = yt_ref[...]
    ms = jnp.mean(yt * yt, axis=0, keepdims=True)
    o_ref[...] = ((yt * lax.rsqrt(ms + EPS)).T * nw_ref[...]).astype(o_ref.dtype)


def _attention(proj, proj_meta, sinks, bias, norm_w, batch, nb, ymix_dtype):
    rows = batch * nb * BLK
    kcol, vcol = COL_K // D_KV, COL_V // D_KV
    meta_blk = (BLK - N_META) // N_META
    vm_t = proj_meta[BLK - N_META:, COL_V:COL_V + D_KV].T

    def cur(col):
        return lambda b, n: (b * nb + n, col)

    def prev(col):
        return lambda b, n: (b * nb + jnp.maximum(n - 1, 0), col)

    return pl.pallas_call(
        _attn_kernel,
        grid=(batch, nb),
        in_specs=[
            pl.BlockSpec(memory_space=pltpu.SMEM),
            pl.BlockSpec((BLK, D_ATTN), cur(0)),
            pl.BlockSpec((BLK, D_KV), prev(kcol)),
            pl.BlockSpec((BLK, D_KV), cur(kcol)),
            pl.BlockSpec((N_META, D_KV), lambda b, n: (meta_blk, kcol)),
            pl.BlockSpec((BLK, D_KV), prev(vcol)),
            pl.BlockSpec((BLK, D_KV), cur(vcol)),
            pl.BlockSpec((D_KV, N_META), lambda b, n: (0, 0)),
            pl.BlockSpec((1, ATTN_HEADS, N_BAND, BLK), lambda b, n: (jnp.minimum(n, 1), 0, 0, 0)),
            pl.BlockSpec((1, D_ATTN), lambda b, n: (0, 0)),
        ],
        out_specs=pl.BlockSpec((BLK, D_ATTN), cur(0)),
        out_shape=jax.ShapeDtypeStruct((rows, D_ATTN), ymix_dtype),
        scratch_shapes=[pltpu.VMEM((D_ATTN, BLK), F32)],
        compiler_params=_params("parallel", "arbitrary"),
        name="swa_attention",
    )(sinks, proj, proj, proj, proj_meta, proj, proj, vm_t, bias, norm_w)


def _ssd_chunk(xs_raw, bc_raw, tail_xs, tail_bc, dt_raw, cw, cb, dtb, alog, row_mask, state):
    def conv(blk, tail, w, b):
        ext = jnp.concatenate([tail, blk], axis=0)
        acc = b
        for k in range(CONV_WIDTH):
            acc = acc + w[k:k + 1, :] * ext[5 + k:5 + k + BLK, :]
        return _silu(acc)

    xs = conv(xs_raw, tail_xs, cw[:, :SSM_D_INNER], cb[:, :SSM_D_INNER])
    bc = conv(bc_raw, tail_bc, cw[:, SSM_D_INNER:], cb[:, SSM_D_INNER:])
    dt = _softplus(dt_raw + dtb)
    if row_mask is not None:
        xs = jnp.where(row_mask, xs, 0.0)
        bc = jnp.where(row_mask, bc, 0.0)
        dt = jnp.where(row_mask, dt, 0.0)
    a_neg = -jnp.exp(alog)
    d_a = dt * a_neg

    r = lax.broadcasted_iota(jnp.int32, (BLK, BLK), 0)
    c = lax.broadcasted_iota(jnp.int32, (BLK, BLK), 1)
    tri = r >= c
    cs = _dot01_left(tri.astype(BF16), d_a)
    cs_t = cs.T
    hh = lax.broadcasted_iota(jnp.int32, (BLK, SSM_D_INNER), 0)
    cc = lax.broadcasted_iota(jnp.int32, (BLK, SSM_D_INNER), 1)
    expand = (cc // SSM_HEAD_DIM == hh).astype(BF16)
    dt_rep = _dot01_right(dt, expand)
    ecs_rep = _dot01_right(jnp.exp(cs), expand)
    dec_rep = _dot01_right(jnp.exp(cs[BLK - 1:BLK, :] - cs), expand)

    xdt = xs * dt_rep
    xdtd = (xdt * dec_rep).astype(BF16)
    xdt_b = xdt.astype(BF16)
    chunk_decay = ecs_rep[BLK - 1:BLK, :]

    y_parts, new_state = [], []
    for g in range(SSM_GROUPS):
        b_g = bc[:, g * SSM_STATE:(g + 1) * SSM_STATE]
        c_g = bc[:, (SSM_GROUPS + g) * SSM_STATE:(SSM_GROUPS + g + 1) * SSM_STATE]
        cols = slice(g * 512, (g + 1) * 512)
        cb_g = _dot_nt(c_g.astype(BF16), b_g.astype(BF16))
        y_off = _dot(c_g.astype(BF16), state[g].astype(BF16)) * ecs_rep[:, cols]
        new_state.append(chunk_decay[:, cols] * state[g] + _dot(b_g.T.astype(BF16), xdtd[:, cols]))
        y_diag = []
        for hp in range(SSM_HPG):
            h = g * SSM_HPG + hp
            seg = cs[:, h:h + 1] - cs_t[h:h + 1, :]
            lmat = jnp.exp(jnp.where(tri, seg, -jnp.inf))
            m = (cb_g * lmat).astype(BF16)
            y_diag.append(_dot(m, xdt_b[:, h * SSM_HEAD_DIM:(h + 1) * SSM_HEAD_DIM]))
        y_parts.append(jnp.concatenate(y_diag, axis=1) + y_off)
    return jnp.concatenate(y_parts, axis=1), xs, new_state


def _halves(lo_ref, hi_ref):
    return jnp.concatenate([lo_ref[...], hi_ref[...]], axis=1)


def _ssd_meta_kernel(xs0_ref, xs1_ref, bc_ref, dt_ref, cw_ref, cb_ref, dtb_ref, alog_ref, st_ref):
    rows = lax.broadcasted_iota(jnp.int32, (BLK, 1), 0)
    zero_state = [jnp.zeros((SSM_STATE, 512), F32) for _ in range(SSM_GROUPS)]
    _, _, st = _ssd_chunk(_halves(xs0_ref, xs1_ref), bc_ref[...], jnp.zeros((8, SSM_D_INNER), F32),
                          jnp.zeros((8, D_BC), F32), dt_ref[...], cw_ref[...], cb_ref[...],
                          dtb_ref[...], alog_ref[...], rows >= BLK - N_META, zero_state)
    for g in range(SSM_GROUPS):
        st_ref[g] = st[g]


def _ssd_kernel(xs0_ref, xs1_ref, bc_ref, z0_ref, z1_ref, dt_ref, txs0_ref, txs1_ref, tbc_ref,
                mxs0_ref, mxs1_ref, mbc_ref, st0_ref,
                cw_ref, cb_ref, dtb_ref, alog_ref, dsk_ref, nw_ref, o_ref, st_ref):
    first = pl.program_id(1) == 0

    @pl.when(first)
    def _():
        st_ref[...] = st0_ref[...]

    tail_xs = jnp.where(first, _halves(mxs0_ref, mxs1_ref), _halves(txs0_ref, txs1_ref))
    tail_bc = jnp.where(first, mbc_ref[...], tbc_ref[...])
    state = [st_ref[g] for g in range(SSM_GROUPS)]
    y, xs, new_state = _ssd_chunk(_halves(xs0_ref, xs1_ref), bc_ref[...], tail_xs, tail_bc, dt_ref[...],
                                  cw_ref[...], cb_ref[...], dtb_ref[...], alog_ref[...], None, state)
    for g in range(SSM_GROUPS):
        st_ref[g] = new_state[g]
    y = y + xs * dsk_ref[...]
    yg = y * _silu(_halves(z0_ref, z1_ref))
    outs = []
    for g in range(SSM_GROUPS):
        part = yg[:, g * 512:(g + 1) * 512]
        ms = jnp.mean(part * part, axis=-1, keepdims=True)
        outs.append(part * lax.rsqrt(ms + EPS))
    o_ref[...] = (jnp.concatenate(outs, axis=1) * nw_ref[...]).astype(o_ref.dtype)


def _pad_lanes(v, n=BLK):
    v = v.reshape(1, -1)
    return jnp.pad(v, ((0, 0), (0, n - v.shape[1])))


def _ssd(proj, proj_meta, conv_w, conv_b, dt_bias, a_log, d_skip, norm_w, batch, nc, ymix_dtype):
    rows = batch * nc * BLK
    cb = conv_b.reshape(1, D_XBC)
    dtb, alog = _pad_lanes(dt_bias), _pad_lanes(a_log)
    dsk = jnp.repeat(d_skip, SSM_HEAD_DIM).reshape(1, SSM_D_INNER)
    xs_c, z_c, bc_c, dt_c = COL_XS // HALF, COL_Z // HALF, COL_BC // D_BC, COL_DT // BLK
    full = lambda shape: pl.BlockSpec(shape, lambda *_: (0,) * len(shape))

    state0 = pl.pallas_call(
        _ssd_meta_kernel,
        grid=(1,),
        in_specs=[
            pl.BlockSpec((BLK, HALF), lambda i: (0, xs_c)),
            pl.BlockSpec((BLK, HALF), lambda i: (0, xs_c + 1)),
            pl.BlockSpec((BLK, D_BC), lambda i: (0, bc_c)),
            pl.BlockSpec((BLK, BLK), lambda i: (0, dt_c)),
            full((CONV_WIDTH, D_XBC)), full((1, D_XBC)), full((1, BLK)), full((1, BLK)),
        ],
        out_specs=full((SSM_GROUPS, SSM_STATE, 512)),
        out_shape=jax.ShapeDtypeStruct((SSM_GROUPS, SSM_STATE, 512), F32),
        compiler_params=_params("arbitrary"),
        name="ssd_meta_state",
    )(proj_meta, proj_meta, proj_meta, proj_meta, conv_w, cb, dtb, alog)

    def cur(col, width):
        return pl.BlockSpec((BLK, width), lambda b, c: (b * nc + c, col))

    def tail(col, width):
        return pl.BlockSpec((8, width), lambda b, c: (jnp.maximum((b * nc + c) * (BLK // 8) - 1, 0), col))

    def meta_tail(col, width):
        return pl.BlockSpec((8, width), lambda b, c: (BLK // 8 - 1, col))

    return pl.pallas_call(
        _ssd_kernel,
        grid=(batch, nc),
        in_specs=[
            cur(xs_c, HALF), cur(xs_c + 1, HALF), cur(bc_c, D_BC), cur(z_c, HALF), cur(z_c + 1, HALF),
            cur(dt_c, BLK),
            tail(xs_c, HALF), tail(xs_c + 1, HALF), tail(bc_c, D_BC),
            meta_tail(xs_c, HALF), meta_tail(xs_c + 1, HALF), meta_tail(bc_c, D_BC),
            full((SSM_GROUPS, SSM_STATE, 512)),
            full((CONV_WIDTH, D_XBC)), full((1, D_XBC)), full((1, BLK)), full((1, BLK)),
            full((1, SSM_D_INNER)), full((1, SSM_D_INNER)),
        ],
        out_specs=pl.BlockSpec((BLK, SSM_D_INNER), lambda b, c: (b * nc + c, 0)),
        out_shape=jax.ShapeDtypeStruct((rows, SSM_D_INNER), ymix_dtype),
        scratch_shapes=[pltpu.VMEM((SSM_GROUPS, SSM_STATE, 512), F32)],
        compiler_params=_params("parallel", "arbitrary"),
        name="ssd_mixer",
    )(proj, proj, proj, proj, proj, proj, proj, proj, proj, proj_meta, proj_meta, proj_meta, state0,
      conv_w, cb, dtb, alog, dsk, norm_w)


def _outproj_kernel(ya_ref, ys_ref, x_ref, w_ref, g_ref, h_ref, xn_ref):
    y = jnp.concatenate([ya_ref[...], ys_ref[...]], axis=1).astype(BF16)
    h = x_ref[...] + _dot(y, w_ref[...])
    h_ref[...] = h
    ms = jnp.mean(h * h, axis=-1, keepdims=True)
    xn_ref[...] = (h * lax.rsqrt(ms + EPS) * g_ref[...]).astype(BF16)


def _outproj(ya, ys, x2d, w_out, gain):
    m = x2d.shape[0]
    tm = min(m, 512)
    return pl.pallas_call(
        _outproj_kernel,
        grid=(m // tm,),
        in_specs=[
            pl.BlockSpec((tm, D_ATTN), lambda i: (i, 0)),
            pl.BlockSpec((tm, SSM_D_INNER), lambda i: (i, 0)),
            pl.BlockSpec((tm, D_MODEL), lambda i: (i, 0)),
            pl.BlockSpec((D_MODEL, D_MODEL), lambda i: (0, 0), pipeline_mode=pl.Buffered(1)),
            pl.BlockSpec((1, D_MODEL), lambda i: (0, 0)),
        ],
        out_specs=[pl.BlockSpec((tm, D_MODEL), lambda i: (i, 0)),
                   pl.BlockSpec((tm, D_MODEL), lambda i: (i, 0))],
        out_shape=[jax.ShapeDtypeStruct((m, D_MODEL), F32),
                   jax.ShapeDtypeStruct((m, D_MODEL), BF16)],
        compiler_params=_params("parallel"),
        name="outproj",
    )(ya, ys, x2d, w_out, gain)


def _oddeven_sort_pairs(n):
    pairs = []
    p = 1
    while p < n:
        k = p
        while k >= 1:
            for j in range(k % p, n - k, 2 * k):
                for i in range(min(k, n - j - k)):
                    if (i + j) // (2 * p) == (i + j + k) // (2 * p):
                        pairs.append((i + j, i + j + k))
            k //= 2
        p *= 2
    return pairs


_SORT16 = _oddeven_sort_pairs(PEER_TOPK)


def _top16_tile(x):
    x = list(x)
    for i, j in _SORT16:
        x[i], x[j] = jnp.maximum(x[i], x[j]), jnp.minimum(x[i], x[j])
    for shift in (4, 2, 1):
        other = [pltpu.roll(v, shift, 0) for v in x]
        x = [jnp.maximum(x[r], other[PEER_TOPK - 1 - r]) for r in range(PEER_TOPK)]
        dist = PEER_TOPK // 2
        while dist >= 1:
            for i in range(PEER_TOPK):
                if i & dist == 0:
                    x[i], x[i + dist] = jnp.maximum(x[i], x[i + dist]), jnp.minimum(x[i], x[i + dist])
            dist //= 2
    return x


def _route_kernel(xn_ref, wq_ref, keys_ref, n_ref, e1_ref, r2_ref, e2_ref, v_ref):
    q = _dot(xn_ref[...], wq_ref[...]).astype(BF16)
    n_tiles = xn_ref.shape[0] // BLK
    for h in range(PEER_HEADS):
        s, tops = [], []
        for c in range(2):
            qs = q[:, (2 * h + c) * PEER_HALF:(2 * h + c + 1) * PEER_HALF]
            sc = _dot_nt(keys_ref[h, c], qs)
            s.append(sc)
            for lt in range(n_tiles):
                lanes = slice(lt * BLK, (lt + 1) * BLK)
                keys8 = [sc[8 * r:8 * r + 8, lanes] for r in range(N_KEYS // 8)]
                best = _top16_tile(keys8)
                for r in range(PEER_TOPK):
                    v_ref[c, r:r + 1, lanes] = best[r][0:1, :]
                if c == 1:
                    ranks = []
                    for x in keys8:
                        rank = jnp.full_like(x, float(PEER_TOPK))
                        for a in range(PEER_TOPK):
                            rank = jnp.where(x == best[a], float(a), rank)
                        ranks.append(rank)
                    r2 = jnp.concatenate(ranks, axis=0).astype(BF16)
                    e2 = jnp.exp(sc[:, lanes] - best[0][0:1, :]).astype(BF16)
                    r2_ref[h, lt] = pltpu.bitcast(r2, jnp.uint32)
                    e2_ref[h, lt] = pltpu.bitcast(e2, jnp.uint32)
            tops.append(v_ref[c])
        v1, v2 = tops
        blocks = [v1[0:1] + v2, v1[1:2] + v2[0:8]]
        blocks += [v1[a:a + 1] + v2[0:8] for a in range(2, 8)]
        blocks.append(v1[8:16] + v2[0:1])
        cand = jnp.concatenate(blocks, axis=0)
        top = v1[0:1] + v2[0:1]
        rem = cand
        for _ in range(PEER_TOPK - 1):
            m = jnp.max(rem, axis=0, keepdims=True)
            rem = jnp.where(rem == m, -jnp.inf, rem)
        tau = jnp.max(rem, axis=0, keepdims=True)
        z = jnp.sum(jnp.where(cand >= tau, jnp.exp(cand - top), 0.0), axis=0, keepdims=True)
        cnt = [jnp.sum(jnp.where(blocks[a] >= tau, 1.0, 0.0), axis=0, keepdims=True) for a in range(8)]
        cnt_hi = jnp.where(blocks[8] >= tau, 1.0, 0.0)
        n_sel = jnp.zeros_like(s[0])
        for a in range(PEER_TOPK):
            n_a = cnt[a] if a < 8 else cnt_hi[a - 8:a - 7]
            n_sel = jnp.where(s[0] == v1[a:a + 1], n_a, n_sel)
        n_ref[h] = n_sel
        e1_ref[h] = jnp.exp(s[0] - v1[0:1]) / z


def _route(xn, wq, keys):
    t = xn.shape[0]
    tm = min(t, 256)
    big = pl.BlockSpec((PEER_HEADS, N_KEYS, tm), lambda i: (0, 0, i))
    big_shape = jax.ShapeDtypeStruct((PEER_HEADS, N_KEYS, t), F32)
    tiled = pl.BlockSpec((PEER_HEADS, tm // BLK, N_KEYS // 2, BLK), lambda i: (0, i, 0, 0))
    tiled_shape = jax.ShapeDtypeStruct((PEER_HEADS, t // BLK, N_KEYS // 2, BLK), jnp.uint32)
    return pl.pallas_call(
        _route_kernel,
        grid=(t // tm,),
        in_specs=[
            pl.BlockSpec((tm, D_MODEL), lambda i: (i, 0)),
            pl.BlockSpec((D_MODEL, D_MODEL), lambda i: (0, 0)),
            pl.BlockSpec((PEER_HEADS, 2, N_KEYS, PEER_HALF), lambda i: (0, 0, 0, 0)),
        ],
        out_specs=[big, big, tiled, tiled],
        out_shape=[big_shape, big_shape, tiled_shape, tiled_shape],
        scratch_shapes=[pltpu.VMEM((2, PEER_TOPK, tm), F32)],
        compiler_params=_params("parallel"),
        name="peer_route",
    )(xn, wq, keys)


PEER_TB = 512
PEER_EB = 1024
PEER_JCH = 128


def _peer_kernel(xn_ref, u_ref, vt_ref, n_ref, e1_ref, r2_ref, e2_ref, o_ref, a_ref, w_ref):
    @pl.when(pl.program_id(1) == 0)
    def _():
        o_ref[...] = jnp.zeros_like(o_ref)

    a_ref[...] = _gelu_exact(_dot_nt(u_ref[...].astype(BF16), xn_ref[...])).astype(BF16)
    tb = xn_ref.shape[0]

    def per_key(ii, carry):
        row0 = pl.multiple_of(ii * N_KEYS, N_KEYS)
        n_rows = [n_ref[h, pl.ds(ii, 1), :] for h in range(PEER_HEADS)]
        e1_rows = [e1_ref[h, pl.ds(ii, 1), :] for h in range(PEER_HEADS)]
        for lt in range(tb // BLK):
            lanes = slice(lt * BLK, (lt + 1) * BLK)
            wide = lambda row: jnp.broadcast_to(row[:, lanes], (PEER_JCH, BLK)).astype(BF16)
            n_b = [wide(r) for r in n_rows]
            e1_b = [wide(r) for r in e1_rows]
            for j0 in range(0, N_KEYS, PEER_JCH):
                js = slice(j0 // 2, (j0 + PEER_JCH) // 2)
                acc = None
                for h in range(PEER_HEADS):
                    keep = pltpu.bitcast(r2_ref[h, lt, js, :], BF16) < n_b[h]
                    gate = jnp.where(keep, pltpu.bitcast(e2_ref[h, lt, js, :], BF16), 0.0) * e1_b[h]
                    acc = gate if acc is None else acc + gate
                rows = pl.ds(row0 + j0, PEER_JCH)
                w_ref[rows, lanes] = acc * a_ref[rows, lanes]
        return carry

    lax.fori_loop(0, PEER_EB // N_KEYS, per_key, 0)
    o_ref[...] += _dot(vt_ref[...], w_ref[...])


def _peer(xn, u, vt_b, n_sel, e1, r2, e2):
    t = xn.shape[0]
    tb = min(t, PEER_TB)
    assert tb == PEER_TB
    ni = PEER_EB // N_KEYS
    small = pl.BlockSpec((PEER_HEADS, ni, tb), lambda i, e: (0, e, i))
    big = pl.BlockSpec((PEER_HEADS, tb // BLK, N_KEYS // 2, BLK), lambda i, e: (0, i, 0, 0))
    return pl.pallas_call(
        _peer_kernel,
        grid=(t // tb, N_EXPERTS // PEER_EB),
        in_specs=[
            pl.BlockSpec((tb, D_MODEL), lambda i, e: (i, 0)),
            pl.BlockSpec((PEER_EB, D_MODEL), lambda i, e: (e, 0)),
            pl.BlockSpec((D_MODEL, PEER_EB), lambda i, e: (0, e)),
            small, small, big, big,
        ],
        out_specs=pl.BlockSpec((D_MODEL, tb), lambda i, e: (0, i)),
        out_shape=jax.ShapeDtypeStruct((D_MODEL, t), F32),
        scratch_shapes=[pltpu.VMEM((PEER_EB, tb), BF16), pltpu.VMEM((PEER_EB, tb), BF16)],
        compiler_params=_params("parallel", "arbitrary"),
        name="peer_experts",
    )(xn, u, vt_b, n_sel, e1, r2, e2)


def _final_kernel(h_ref, pt_ref, g_ref, o_ref):
    h = h_ref[...] + pt_ref[...].T
    ms = jnp.mean(h * h, axis=-1, keepdims=True)
    o_ref[...] = h * lax.rsqrt(ms + EPS) * g_ref[...]


def _final(h1, peer_t, gain):
    t = h1.shape[0]
    tm = min(t, 512)
    rows = pl.BlockSpec((tm, D_MODEL), lambda i: (i, 0))
    return pl.pallas_call(
        _final_kernel,
        grid=(t // tm,),
        in_specs=[rows, pl.BlockSpec((D_MODEL, tm), lambda i: (0, i)),
                  pl.BlockSpec((1, D_MODEL), lambda i: (0, 0))],
        out_specs=rows,
        out_shape=jax.ShapeDtypeStruct((t, D_MODEL), F32),
        compiler_params=_params("parallel"),
        name="final_norm",
    )(h1, peer_t, gain)


def _mixer(x2d, batch, seq, meta_tokens, rel_bias, ln_mix, w_in, sinks, conv_w, conv_b, dt_bias,
           a_log, d_skip, attn_norm_w, ssm_norm_w):
    nb = seq // BLK
    w_p = jnp.pad(w_in.astype(BF16), ((0, 0), (0, D_PROJ - D_IN)))
    gain = ln_mix.reshape(1, D_MODEL)
    meta_pad = jnp.concatenate([jnp.zeros((BLK - N_META, D_MODEL), F32), meta_tokens.astype(F32)], axis=0)
    proj = _inproj(x2d, gain, w_p)
    proj_meta = _inproj(meta_pad, gain, w_p)

    bucket, valid = _band_tables(nb)
    tab = rel_bias.astype(F32)
    bias = jnp.full((2, ATTN_HEADS) + bucket.shape[1:], NEG, F32)
    for b in range(REL_BUCKETS):
        bias = jnp.where((valid & (bucket == b))[:, None], tab[b][None, :, None, None], bias)
    ya = _attention(proj, proj_meta, sinks.astype(F32), bias, attn_norm_w.reshape(1, D_ATTN),
                    batch, nb, F32)
    ys = _ssd(proj, proj_meta, conv_w, conv_b, dt_bias, a_log, d_skip,
                   ssm_norm_w.reshape(1, SSM_D_INNER), batch, nb, F32)
    return ya, ys


def kernel(x, meta_tokens, rel_bias, ln_mix, w_in, attn_sinks, conv_w, conv_b, dt_bias, a_log, d_skip,
           attn_norm_w, ssm_norm_w, w_out, ln_ffn, peer_wq, peer_keys, peer_u, peer_v, ln_final):
    batch, seq, _ = x.shape
    x2d = x.reshape(batch * seq, D_MODEL)
    ya, ys = _mixer(x2d, batch, seq, meta_tokens, rel_bias, ln_mix[0], w_in[0], attn_sinks[0],
                         conv_w[0], conv_b[0], dt_bias[0], a_log[0], d_skip[0], attn_norm_w[0],
                         ssm_norm_w[0])
    h1, xn = _outproj(ya, ys, x2d, w_out[0].astype(BF16), ln_ffn[0].reshape(1, D_MODEL))
    n_sel, e1, r2, e2 = _route(xn, peer_wq[0].astype(BF16), peer_keys[0].astype(BF16))
    peer_t = _peer(xn, peer_u[0], peer_v[0].T.astype(BF16), n_sel, e1, r2, e2)
    out = _final(h1, peer_t, ln_final.reshape(1, D_MODEL))
    return out.reshape(batch, seq, D_MODEL)
```

```python
import functools

import jax
import jax.numpy as jnp
import numpy as np
from jax import lax
from jax.experimental import pallas as pl
from jax.experimental.pallas import tpu as pltpu

F32 = jnp.float32
BF16 = jnp.bfloat16

D_MODEL = 2048
N_META = 16
HEAD_DIM = 64
D_ATTN = 1024
ATTN_HEADS = 16
ATTN_KV_HEADS = 4
ATTN_GROUP = 4
D_KV = 256
WINDOW = 128
BLK = 128
REL_BUCKETS = 32
REL_MAX_DIST = 128
SSM_D_INNER = 1024
SSM_HEAD_DIM = 64
SSM_HEADS = 16
SSM_GROUPS = 2
SSM_HPG = 8
SSM_STATE = 128
CONV_WIDTH = 4
D_XBC = 1536
D_BC = 2 * SSM_GROUPS * SSM_STATE
PEER_HEADS = 8
PEER_TOPK = 16
N_KEYS = 128
N_EXPERTS = N_KEYS * N_KEYS
PEER_HALF = 128
EPS = 1e-6
NEG = -1e30

COL_Q = 0
COL_K = 1024
COL_V = 1280
COL_Z = 1536
COL_XS = 2560
COL_BC = 3584
COL_DT = 4096
D_IN = 4112
D_PROJ = 4224
HALF = 512

VMEM_LIMIT = 56 * 1024 * 1024


def _params(*sem, flags=None):
    return pltpu.CompilerParams(dimension_semantics=sem, vmem_limit_bytes=VMEM_LIMIT, flags=flags)


def _dot(a, b):
    return jnp.dot(a, b, preferred_element_type=F32)


def _dot_nt(a, b):
    return lax.dot_general(a, b, (((1,), (1,)), ((), ())), preferred_element_type=F32)


def _split3(x):
    hi = x.astype(BF16)
    r = x - hi.astype(F32)
    mid = r.astype(BF16)
    lo = (r - mid.astype(F32)).astype(BF16)
    return hi, mid, lo


def _dot01_left(m01, x):
    hi, mid, lo = _split3(x)
    return _dot(m01, hi) + _dot(m01, mid) + _dot(m01, lo)


def _dot01_right(x, m01):
    hi, mid, lo = _split3(x)
    return _dot(hi, m01) + _dot(mid, m01) + _dot(lo, m01)


def _silu(x):
    h = 0.5 * x
    return h + h * jnp.tanh(h)


def _softplus(x):
    return jnp.maximum(x, 0.0) + jnp.log1p(jnp.exp(-jnp.abs(x)))


def _gelu_exact(x):
    return 0.5 * x * (1.0 + lax.erf(x * np.float32(np.sqrt(0.5))))


def _inproj_kernel(x_ref, g_ref, w_ref, o_ref):
    x = x_ref[...]
    ms = jnp.mean(x * x, axis=-1, keepdims=True)
    xn = (x * lax.rsqrt(ms + EPS) * g_ref[...]).astype(BF16)
    o_ref[...] = _dot(xn, w_ref[...])


def _inproj(x2d, gain, w_p):
    m = x2d.shape[0]
    tm = min(m, 512)
    return pl.pallas_call(
        _inproj_kernel,
        grid=(m // tm,),
        in_specs=[
            pl.BlockSpec((tm, D_MODEL), lambda i: (i, 0)),
            pl.BlockSpec((1, D_MODEL), lambda i: (0, 0)),
            pl.BlockSpec((D_MODEL, D_PROJ), lambda i: (0, 0), pipeline_mode=pl.Buffered(1)),
        ],
        out_specs=pl.BlockSpec((tm, D_PROJ), lambda i: (i, 0)),
        out_shape=jax.ShapeDtypeStruct((m, D_PROJ), F32),
        compiler_params=_params("parallel"),
        name="inproj",
    )(x2d, gain, w_p)


def _t5_bucket(dist):
    n = np.maximum(dist, 0)
    max_exact = REL_BUCKETS // 2
    large = max_exact + (np.log(np.maximum(n, 1) / max_exact) / np.log(REL_MAX_DIST / max_exact)
                         * (REL_BUCKETS - max_exact)).astype(np.int32)
    large = np.minimum(large, REL_BUCKETS - 1)
    return np.where(n < max_exact, n, large).astype(np.int32)


N_BAND = BLK + N_META


def _band_tables(nb):
    r = np.arange(BLK)[:, None]
    q = np.arange(BLK)[None, :]
    m = np.arange(N_META)[:, None]
    buckets, valids = [], []
    for n in range(nb):
        upper = r > q
        d_band = np.where(upper, q - r + BLK, q - r)
        d_meta = N_META + n * BLK + q - m
        assert (d_band[upper] < WINDOW).all() and (d_band >= 0).all() and (d_meta >= 0).all()
        buckets.append(_t5_bucket(np.concatenate([d_band, d_meta], axis=0)))
        valids.append(np.concatenate([~upper | (n > 0), np.ones((N_META, BLK), bool)], axis=0))
    for n in range(2, nb):
        assert (buckets[n] == buckets[1]).all() and (valids[n] == valids[1]).all()
    last = min(1, nb - 1)
    return np.stack([buckets[0], buckets[last]]), np.stack([valids[0], valids[last]])


def _attn_kernel(sink_ref, q_ref, kp_ref, ko_ref, km_ref, vp_ref, vo_ref, vmt_ref,
                 bias_ref, nw_ref, o_ref, yt_ref):
    q = (q_ref[...] * np.float32(HEAD_DIM ** -0.5)).astype(BF16)
    upper = (lax.broadcasted_iota(jnp.int32, (BLK, BLK), 0)
             > lax.broadcasted_iota(jnp.int32, (BLK, BLK), 1))
    vpt = vp_ref[...].T.astype(BF16)
    vot = vo_ref[...].T.astype(BF16)
    vmt = vmt_ref[...].astype(BF16)
    for j in range(ATTN_KV_HEADS):
        ks = slice(j * HEAD_DIM, (j + 1) * HEAD_DIM)
        heads = [j * ATTN_GROUP + g for g in range(ATTN_GROUP)]
        q4 = jnp.concatenate([q[:, h * HEAD_DIM:(h + 1) * HEAD_DIM] for h in heads], axis=0)
        lp = _dot_nt(kp_ref[:, ks].astype(BF16), q4)
        lo = _dot_nt(ko_ref[:, ks].astype(BF16), q4)
        lm = _dot_nt(km_ref[:, ks].astype(BF16), q4)
        e_prev, e_own, e_meta, inv = [], [], [], []
        for g, h in enumerate(heads):
            cols = slice(g * BLK, (g + 1) * BLK)
            band = jnp.where(upper, lp[:, cols], lo[:, cols]) + bias_ref[0, h, 0:BLK, :]
            meta = lm[:, cols] + bias_ref[0, h, BLK:N_BAND, :]
            sink = sink_ref[h]
            mx = jnp.maximum(jnp.maximum(jnp.max(band, axis=0, keepdims=True),
                                         jnp.max(meta, axis=0, keepdims=True)), sink)
            eb = jnp.exp(band - mx)
            em = jnp.exp(meta - mx)
            denom = (jnp.sum(eb, axis=0, keepdims=True) + jnp.sum(em, axis=0, keepdims=True)
                     + jnp.exp(sink - mx))
            inv.append(1.0 / denom)
            e_prev.append(jnp.where(upper, eb, 0.0).astype(BF16))
            e_own.append(jnp.where(upper, 0.0, eb).astype(BF16))
            e_meta.append(em.astype(BF16))
        cat = lambda parts: jnp.concatenate(parts, axis=1)
        ot = (_dot(vpt[ks, :], cat(e_prev)) + _dot(vot[ks, :], cat(e_own))
              + _dot(vmt[ks, :], cat(e_meta))) * cat(inv)
        for g, h in enumerate(heads):
            yt_ref[h * HEAD_DIM:(h + 1) * HEAD_DIM, :] = ot[:, g * BLK:(g + 1) * BLK]
    yt = yt_ref[...]
    ms = jnp.mean(yt * yt, axis=0, keepdims=True)
    o_ref[...] = ((yt * lax.rsqrt(ms + EPS)).T * nw_ref[...]).astype(o_ref.dtype)


def _attention(proj, proj_meta, sinks, bias, norm_w, batch, nb, ymix_dtype):
    rows = batch * nb * BLK
    kcol, vcol = COL_K // D_KV, COL_V // D_KV
    meta_blk = (BLK - N_META) // N_META
    vm_t = proj_meta[BLK - N_META:, COL_V:COL_V + D_KV].T

    def cur(col):
        return lambda b, n: (b * nb + n, col)

    def prev(col):
        return lambda b, n: (b * nb + jnp.maximum(n - 1, 0), col)

    return pl.pallas_call(
        _attn_kernel,
        grid=(batch, nb),
        in_specs=[
            pl.BlockSpec(memory_space=pltpu.SMEM),
            pl.BlockSpec((BLK, D_ATTN), cur(0)),
            pl.BlockSpec((BLK, D_KV), prev(kcol)),
            pl.BlockSpec((BLK, D_KV), cur(kcol)),
            pl.BlockSpec((N_META, D_KV), lambda b, n: (meta_blk, kcol)),
            pl.BlockSpec((BLK, D_KV), prev(vcol)),
            pl.BlockSpec((BLK, D_KV), cur(vcol)),
            pl.BlockSpec((D_KV, N_META), lambda b, n: (0, 0)),
            pl.BlockSpec((1, ATTN_HEADS, N_BAND, BLK), lambda b, n: (jnp.minimum(n, 1), 0, 0, 0)),
            pl.BlockSpec((1, D_ATTN), lambda b, n: (0, 0)),
        ],
        out_specs=pl.BlockSpec((BLK, D_ATTN), cur(0)),
        out_shape=jax.ShapeDtypeStruct((rows, D_ATTN), ymix_dtype),
        scratch_shapes=[pltpu.VMEM((D_ATTN, BLK), F32)],
        compiler_params=_params("parallel", "arbitrary"),
        name="swa_attention",
    )(sinks, proj, proj, proj, proj_meta, proj, proj, vm_t, bias, norm_w)


def _ssd_chunk(xs_raw, bc_raw, tail_xs, tail_bc, dt_raw, cw, cb, dtb, alog, row_mask, state):
    def conv(blk, tail, w, b):
        ext = jnp.concatenate([tail, blk], axis=0)
        acc = b + w[CONV_WIDTH - 1:CONV_WIDTH, :] * blk
        for back in range(1, CONV_WIDTH):
            shifted = pltpu.roll(ext, back, 0)[8:, :]
            acc = acc + w[CONV_WIDTH - 1 - back:CONV_WIDTH - back, :] * shifted
        return _silu(acc)

    xs = conv(xs_raw, tail_xs, cw[:, :SSM_D_INNER], cb[:, :SSM_D_INNER])
    bc = conv(bc_raw, tail_bc, cw[:, SSM_D_INNER:], cb[:, SSM_D_INNER:])
    dt = _softplus(dt_raw + dtb)
    if row_mask is not None:
        xs = jnp.where(row_mask, xs, 0.0)
        bc = jnp.where(row_mask, bc, 0.0)
        dt = jnp.where(row_mask, dt, 0.0)
    a_neg = -jnp.exp(alog)
    d_a = dt * a_neg

    r = lax.broadcasted_iota(jnp.int32, (BLK, BLK), 0)
    c = lax.broadcasted_iota(jnp.int32, (BLK, BLK), 1)
    tri = r >= c
    cs = _dot01_left(tri.astype(BF16), d_a)
    cs_t = cs.T
    hh = lax.broadcasted_iota(jnp.int32, (BLK, SSM_D_INNER), 0)
    cc = lax.broadcasted_iota(jnp.int32, (BLK, SSM_D_INNER), 1)
    expand = (cc // SSM_HEAD_DIM == hh).astype(BF16)
    dt_rep = _dot01_right(dt, expand)
    ecs_rep = _dot01_right(jnp.exp(cs), expand)
    dec_rep = _dot01_right(jnp.exp(cs[BLK - 1:BLK, :] - cs), expand)

    xdt = xs * dt_rep
    xdtd = (xdt * dec_rep).astype(BF16)
    xdt_b = xdt.astype(BF16)
    chunk_decay = ecs_rep[BLK - 1:BLK, :]

    y_parts, new_state = [], []
    for g in range(SSM_GROUPS):
        b_g = bc[:, g * SSM_STATE:(g + 1) * SSM_STATE]
        c_g = bc[:, (SSM_GROUPS + g) * SSM_STATE:(SSM_GROUPS + g + 1) * SSM_STATE]
        cols = slice(g * 512, (g + 1) * 512)
        cb_g = _dot_nt(c_g.astype(BF16), b_g.astype(BF16))
        y_off = _dot(c_g.astype(BF16), state[g].astype(BF16)) * ecs_rep[:, cols]
        new_state.append(chunk_decay[:, cols] * state[g] + _dot(b_g.T.astype(BF16), xdtd[:, cols]))
        y_diag = []
        for hp in range(SSM_HPG):
            h = g * SSM_HPG + hp
            seg = cs[:, h:h + 1] - cs_t[h:h + 1, :]
            lmat = jnp.exp(jnp.where(tri, seg, -jnp.inf))
            m = (cb_g * lmat).astype(BF16)
            y_diag.append(_dot(m, xdt_b[:, h * SSM_HEAD_DIM:(h + 1) * SSM_HEAD_DIM]))
        y_parts.append(jnp.concatenate(y_diag, axis=1) + y_off)
    return jnp.concatenate(y_parts, axis=1), xs, new_state


def _halves(lo_ref, hi_ref):
    return jnp.concatenate([lo_ref[...], hi_ref[...]], axis=1)


def _ssd_meta_kernel(xs0_ref, xs1_ref, bc_ref, dt_ref, cw_ref, cb_ref, dtb_ref, alog_ref, st_ref):
    rows = lax.broadcasted_iota(jnp.int32, (BLK, 1), 0)
    zero_state = [jnp.zeros((SSM_STATE, 512), F32) for _ in range(SSM_GROUPS)]
    _, _, st = _ssd_chunk(_halves(xs0_ref, xs1_ref), bc_ref[...], jnp.zeros((8, SSM_D_INNER), F32),
                          jnp.zeros((8, D_BC), F32), dt_ref[...], cw_ref[...], cb_ref[...],
                          dtb_ref[...], alog_ref[...], rows >= BLK - N_META, zero_state)
    for g in range(SSM_GROUPS):
        st_ref[g] = st[g]


def _ssd_kernel(xs0_ref, xs1_ref, bc_ref, z0_ref, z1_ref, dt_ref, txs0_ref, txs1_ref, tbc_ref,
                mxs0_ref, mxs1_ref, mbc_ref, st0_ref,
                cw_ref, cb_ref, dtb_ref, alog_ref, dsk_ref, nw_ref, o_ref, st_ref):
    first = pl.program_id(1) == 0

    @pl.when(first)
    def _():
        st_ref[...] = st0_ref[...]

    tail_xs = jnp.where(first, _halves(mxs0_ref, mxs1_ref), _halves(txs0_ref, txs1_ref))
    tail_bc = jnp.where(first, mbc_ref[...], tbc_ref[...])
    state = [st_ref[g] for g in range(SSM_GROUPS)]
    y, xs, new_state = _ssd_chunk(_halves(xs0_ref, xs1_ref), bc_ref[...], tail_xs, tail_bc, dt_ref[...],
                                  cw_ref[...], cb_ref[...], dtb_ref[...], alog_ref[...], None, state)
    for g in range(SSM_GROUPS):
        st_ref[g] = new_state[g]
    y = y + xs * dsk_ref[...]
    yg = y * _silu(_halves(z0_ref, z1_ref))
    outs = []
    for g in range(SSM_GROUPS):
        part = yg[:, g * 512:(g + 1) * 512]
        ms = jnp.mean(part * part, axis=-1, keepdims=True)
        outs.append(part * lax.rsqrt(ms + EPS))
    o_ref[...] = (jnp.concatenate(outs, axis=1) * nw_ref[...]).astype(o_ref.dtype)


def _pad_lanes(v, n=BLK):
    v = v.reshape(1, -1)
    return jnp.pad(v, ((0, 0), (0, n - v.shape[1])))


def _ssd(proj, proj_meta, conv_w, conv_b, dt_bias, a_log, d_skip, norm_w, batch, nc, ymix_dtype):
    rows = batch * nc * BLK
    cb = conv_b.reshape(1, D_XBC)
    dtb, alog = _pad_lanes(dt_bias), _pad_lanes(a_log)
    dsk = jnp.repeat(d_skip, SSM_HEAD_DIM).reshape(1, SSM_D_INNER)
    xs_c, z_c, bc_c, dt_c = COL_XS // HALF, COL_Z // HALF, COL_BC // D_BC, COL_DT // BLK
    full = lambda shape: pl.BlockSpec(shape, lambda *_: (0,) * len(shape))

    state0 = pl.pallas_call(
        _ssd_meta_kernel,
        grid=(1,),
        in_specs=[
            pl.BlockSpec((BLK, HALF), lambda i: (0, xs_c)),
            pl.BlockSpec((BLK, HALF), lambda i: (0, xs_c + 1)),
            pl.BlockSpec((BLK, D_BC), lambda i: (0, bc_c)),
            pl.BlockSpec((BLK, BLK), lambda i: (0, dt_c)),
            full((CONV_WIDTH, D_XBC)), full((1, D_XBC)), full((1, BLK)), full((1, BLK)),
        ],
        out_specs=full((SSM_GROUPS, SSM_STATE, 512)),
        out_shape=jax.ShapeDtypeStruct((SSM_GROUPS, SSM_STATE, 512), F32),
        compiler_params=_params("arbitrary"),
        name="ssd_meta_state",
    )(proj_meta, proj_meta, proj_meta, proj_meta, conv_w, cb, dtb, alog)

    def cur(col, width):
        return pl.BlockSpec((BLK, width), lambda b, c: (b * nc + c, col))

    def tail(col, width):
        return pl.BlockSpec((8, width), lambda b, c: (jnp.maximum((b * nc + c) * (BLK // 8) - 1, 0), col))

    def meta_tail(col, width):
        return pl.BlockSpec((8, width), lambda b, c: (BLK // 8 - 1, col))

    return pl.pallas_call(
        _ssd_kernel,
        grid=(batch, nc),
        in_specs=[
            cur(xs_c, HALF), cur(xs_c + 1, HALF), cur(bc_c, D_BC), cur(z_c, HALF), cur(z_c + 1, HALF),
            cur(dt_c, BLK),
            tail(xs_c, HALF), tail(xs_c + 1, HALF), tail(bc_c, D_BC),
            meta_tail(xs_c, HALF), meta_tail(xs_c + 1, HALF), meta_tail(bc_c, D_BC),
            full((SSM_GROUPS, SSM_STATE, 512)),
            full((CONV_WIDTH, D_XBC)), full((1, D_XBC)), full((1, BLK)), full((1, BLK)),
            full((1, SSM_D_INNER)), full((1, SSM_D_INNER)),
        ],
        out_specs=pl.BlockSpec((BLK, SSM_D_INNER), lambda b, c: (b * nc + c, 0)),
        out_shape=jax.ShapeDtypeStruct((rows, SSM_D_INNER), ymix_dtype),
        scratch_shapes=[pltpu.VMEM((SSM_GROUPS, SSM_STATE, 512), F32)],
        compiler_params=_params("parallel", "arbitrary"),
        name="ssd_mixer",
    )(proj, proj, proj, proj, proj, proj, proj, proj, proj, proj_meta, proj_meta, proj_meta, state0,
      conv_w, cb, dtb, alog, dsk, norm_w)


def _outproj_kernel(ya_ref, ys_ref, x_ref, w_ref, g_ref, h_ref, xn_ref):
    y = jnp.concatenate([ya_ref[...], ys_ref[...]], axis=1).astype(BF16)
    h = x_ref[...] + _dot(y, w_ref[...])
    h_ref[...] = h
    ms = jnp.mean(h * h, axis=-1, keepdims=True)
    xn_ref[...] = (h * lax.rsqrt(ms + EPS) * g_ref[...]).astype(BF16)


def _outproj(ya, ys, x2d, w_out, gain):
    m = x2d.shape[0]
    tm = min(m, 512)
    return pl.pallas_call(
        _outproj_kernel,
        grid=(m // tm,),
        in_specs=[
            pl.BlockSpec((tm, D_ATTN), lambda i: (i, 0)),
            pl.BlockSpec((tm, SSM_D_INNER), lambda i: (i, 0)),
            pl.BlockSpec((tm, D_MODEL), lambda i: (i, 0)),
            pl.BlockSpec((D_MODEL, D_MODEL), lambda i: (0, 0), pipeline_mode=pl.Buffered(1)),
            pl.BlockSpec((1, D_MODEL), lambda i: (0, 0)),
        ],
        out_specs=[pl.BlockSpec((tm, D_MODEL), lambda i: (i, 0)),
                   pl.BlockSpec((tm, D_MODEL), lambda i: (i, 0))],
        out_shape=[jax.ShapeDtypeStruct((m, D_MODEL), F32),
                   jax.ShapeDtypeStruct((m, D_MODEL), BF16)],
        compiler_params=_params("parallel"),
        name="outproj",
    )(ya, ys, x2d, w_out, gain)


def _oddeven_sort_pairs(n):
    pairs = []
    p = 1
    while p < n:
        k = p
        while k >= 1:
            for j in range(k % p, n - k, 2 * k):
                for i in range(min(k, n - j - k)):
                    if (i + j) // (2 * p) == (i + j + k) // (2 * p):
                        pairs.append((i + j, i + j + k))
            k //= 2
        p *= 2
    return pairs


_SORT16 = _oddeven_sort_pairs(PEER_TOPK)


def _top16_tile(x):
    x = list(x)
    for i, j in _SORT16:
        x[i], x[j] = jnp.maximum(x[i], x[j]), jnp.minimum(x[i], x[j])
    for shift in (4, 2, 1):
        other = [pltpu.roll(v, shift, 0) for v in x]
        x = [jnp.maximum(x[r], other[PEER_TOPK - 1 - r]) for r in range(PEER_TOPK)]
        dist = PEER_TOPK // 2
        while dist >= 1:
            for i in range(PEER_TOPK):
                if i & dist == 0:
                    x[i], x[i + dist] = jnp.maximum(x[i], x[i + dist]), jnp.minimum(x[i], x[i + dist])
            dist //= 2
    return x


def _route_kernel(xn_ref, wq_ref, keys_ref, n_ref, e1_ref, r2_ref, e2_ref, v_ref):
    q = _dot(xn_ref[...], wq_ref[...]).astype(BF16)
    n_tiles = xn_ref.shape[0] // BLK
    for h in range(PEER_HEADS):
        s, tops = [], []
        for c in range(2):
            qs = q[:, (2 * h + c) * PEER_HALF:(2 * h + c + 1) * PEER_HALF]
            sc = _dot_nt(keys_ref[h, c], qs)
            s.append(sc)
            for lt in range(n_tiles):
                lanes = slice(lt * BLK, (lt + 1) * BLK)
                keys8 = [sc[8 * r:8 * r + 8, lanes] for r in range(N_KEYS // 8)]
                best = _top16_tile(keys8)
                for r in range(PEER_TOPK):
                    v_ref[c, r:r + 1, lanes] = best[r][0:1, :]
                if c == 1:
                    ranks = []
                    for x in keys8:
                        rank = jnp.full_like(x, float(PEER_TOPK))
                        for a in range(PEER_TOPK):
                            rank = jnp.where(x == best[a], float(a), rank)
                        ranks.append(rank)
                    r2 = jnp.concatenate(ranks, axis=0).astype(BF16)
                    e2 = jnp.exp(sc[:, lanes] - best[0][0:1, :]).astype(BF16)
                    r2_ref[h, lt] = pltpu.bitcast(r2, jnp.uint32)
                    e2_ref[h, lt] = pltpu.bitcast(e2, jnp.uint32)
            tops.append(v_ref[c])
        v1, v2 = tops
        blocks = [v1[0:1] + v2, v1[1:2] + v2[0:8]]
        blocks += [v1[a:a + 1] + v2[0:8] for a in range(2, 8)]
        blocks.append(v1[8:16] + v2[0:1])
        cand = jnp.concatenate(blocks, axis=0)
        top = v1[0:1] + v2[0:1]
        rem = cand
        for _ in range(PEER_TOPK - 1):
            m = jnp.max(rem, axis=0, keepdims=True)
            rem = jnp.where(rem == m, -jnp.inf, rem)
        tau = jnp.max(rem, axis=0, keepdims=True)
        z = jnp.sum(jnp.where(cand >= tau, jnp.exp(cand - top), 0.0), axis=0, keepdims=True)
        cnt = [jnp.sum(jnp.where(blocks[a] >= tau, 1.0, 0.0), axis=0, keepdims=True) for a in range(8)]
        cnt_hi = jnp.where(blocks[8] >= tau, 1.0, 0.0)
        n_sel = jnp.zeros_like(s[0])
        for a in range(PEER_TOPK):
            n_a = cnt[a] if a < 8 else cnt_hi[a - 8:a - 7]
            n_sel = jnp.where(s[0] == v1[a:a + 1], n_a, n_sel)
        n_ref[h] = n_sel
        e1_ref[h] = jnp.exp(s[0] - v1[0:1]) / z


def _route(xn, wq, keys):
    t = xn.shape[0]
    tm = min(t, 256)
    big = pl.BlockSpec((PEER_HEADS, N_KEYS, tm), lambda i: (0, 0, i))
    big_shape = jax.ShapeDtypeStruct((PEER_HEADS, N_KEYS, t), F32)
    tiled = pl.BlockSpec((PEER_HEADS, tm // BLK, N_KEYS // 2, BLK), lambda i: (0, i, 0, 0))
    tiled_shape = jax.ShapeDtypeStruct((PEER_HEADS, t // BLK, N_KEYS // 2, BLK), jnp.uint32)
    return pl.pallas_call(
        _route_kernel,
        grid=(t // tm,),
        in_specs=[
            pl.BlockSpec((tm, D_MODEL), lambda i: (i, 0)),
            pl.BlockSpec((D_MODEL, D_MODEL), lambda i: (0, 0)),
            pl.BlockSpec((PEER_HEADS, 2, N_KEYS, PEER_HALF), lambda i: (0, 0, 0, 0)),
        ],
        out_specs=[big, big, tiled, tiled],
        out_shape=[big_shape, big_shape, tiled_shape, tiled_shape],
        scratch_shapes=[pltpu.VMEM((2, PEER_TOPK, tm), F32)],
        compiler_params=_params("parallel"),
        name="peer_route",
    )(xn, wq, keys)


PEER_TB = 512
PEER_EB = 1024
PEER_JCH = 128


def _peer_kernel(xn_ref, u_ref, vt_ref, n_ref, e1_ref, r2_ref, e2_ref, o_ref, a_ref, w_ref):
    @pl.when(pl.program_id(1) == 0)
    def _():
        o_ref[...] = jnp.zeros_like(o_ref)

    a_ref[...] = _gelu_exact(_dot_nt(u_ref[...].astype(BF16), xn_ref[...])).astype(BF16)
    tb = xn_ref.shape[0]

    def per_key(ii, carry):
        row0 = pl.multiple_of(ii * N_KEYS, N_KEYS)
        n_rows = [n_ref[h, pl.ds(ii, 1), :] for h in range(PEER_HEADS)]
        e1_rows = [e1_ref[h, pl.ds(ii, 1), :] for h in range(PEER_HEADS)]
        for lt in range(tb // BLK):
            lanes = slice(lt * BLK, (lt + 1) * BLK)
            wide = lambda row: jnp.broadcast_to(row[:, lanes], (PEER_JCH, BLK)).astype(BF16)
            n_b = [wide(r) for r in n_rows]
            e1_b = [wide(r) for r in e1_rows]
            for j0 in range(0, N_KEYS, PEER_JCH):
                js = slice(j0 // 2, (j0 + PEER_JCH) // 2)
                acc = None
                for h in range(PEER_HEADS):
                    keep = pltpu.bitcast(r2_ref[h, lt, js, :], BF16) < n_b[h]
                    gate = jnp.where(keep, pltpu.bitcast(e2_ref[h, lt, js, :], BF16), 0.0) * e1_b[h]
                    acc = gate if acc is None else acc + gate
                rows = pl.ds(row0 + j0, PEER_JCH)
                w_ref[rows, lanes] = acc * a_ref[rows, lanes]
        return carry

    lax.fori_loop(0, PEER_EB // N_KEYS, per_key, 0)
    o_ref[...] += _dot(vt_ref[...], w_ref[...])


def _transpose_bf16_kernel(x_ref, o_ref):
    o_ref[...] = x_ref[...].T.astype(BF16)


def _transpose_bf16(x, rows=512):
    r, c = x.shape
    return pl.pallas_call(
        _transpose_bf16_kernel,
        grid=(r // rows,),
        in_specs=[pl.BlockSpec((rows, c), lambda i: (i, 0))],
        out_specs=pl.BlockSpec((c, rows), lambda i: (0, i)),
        out_shape=jax.ShapeDtypeStruct((c, r), BF16),
        compiler_params=_params("parallel"),
        name="transpose_v",
    )(x)


def _peer(xn, u, vt_b, n_sel, e1, r2, e2):
    t = xn.shape[0]
    tb = min(t, PEER_TB)
    assert tb == PEER_TB
    ni = PEER_EB // N_KEYS
    small = pl.BlockSpec((PEER_HEADS, ni, tb), lambda i, e: (0, e, i))
    big = pl.BlockSpec((PEER_HEADS, tb // BLK, N_KEYS // 2, BLK), lambda i, e: (0, i, 0, 0))
    return pl.pallas_call(
        _peer_kernel,
        grid=(t // tb, N_EXPERTS // PEER_EB),
        in_specs=[
            pl.BlockSpec((tb, D_MODEL), lambda i, e: (i, 0)),
            pl.BlockSpec((PEER_EB, D_MODEL), lambda i, e: (e, 0)),
            pl.BlockSpec((D_MODEL, PEER_EB), lambda i, e: (0, e)),
            small, small, big, big,
        ],
        out_specs=pl.BlockSpec((D_MODEL, tb), lambda i, e: (0, i)),
        out_shape=jax.ShapeDtypeStruct((D_MODEL, t), F32),
        scratch_shapes=[pltpu.VMEM((PEER_EB, tb), BF16), pltpu.VMEM((PEER_EB, tb), BF16)],
        compiler_params=_params("parallel", "arbitrary"),
        name="peer_experts",
    )(xn, u, vt_b, n_sel, e1, r2, e2)


def _final_kernel(h_ref, pt_ref, g_ref, o_ref):
    h = h_ref[...] + pt_ref[...].T
    ms = jnp.mean(h * h, axis=-1, keepdims=True)
    o_ref[...] = h * lax.rsqrt(ms + EPS) * g_ref[...]


def _final(h1, peer_t, gain):
    t = h1.shape[0]
    tm = min(t, 512)
    rows = pl.BlockSpec((tm, D_MODEL), lambda i: (i, 0))
    return pl.pallas_call(
        _final_kernel,
        grid=(t // tm,),
        in_specs=[rows, pl.BlockSpec((D_MODEL, tm), lambda i: (0, i)),
                  pl.BlockSpec((1, D_MODEL), lambda i: (0, 0))],
        out_specs=rows,
        out_shape=jax.ShapeDtypeStruct((t, D_MODEL), F32),
        compiler_params=_params("parallel"),
        name="final_norm",
    )(h1, peer_t, gain)


def _mixer(x2d, batch, seq, meta_tokens, rel_bias, ln_mix, w_in, sinks, conv_w, conv_b, dt_bias,
           a_log, d_skip, attn_norm_w, ssm_norm_w):
    nb = seq // BLK
    w_p = jnp.pad(w_in.astype(BF16), ((0, 0), (0, D_PROJ - D_IN)))
    gain = ln_mix.reshape(1, D_MODEL)
    meta_pad = jnp.concatenate([jnp.zeros((BLK - N_META, D_MODEL), F32), meta_tokens.astype(F32)], axis=0)
    proj = _inproj(x2d, gain, w_p)
    proj_meta = _inproj(meta_pad, gain, w_p)

    bucket, valid = _band_tables(nb)
    tab = rel_bias.astype(F32)
    bias = jnp.full((2, ATTN_HEADS) + bucket.shape[1:], NEG, F32)
    for b in range(REL_BUCKETS):
        bias = jnp.where((valid & (bucket == b))[:, None], tab[b][None, :, None, None], bias)
    ya = _attention(proj, proj_meta, sinks.astype(F32), bias, attn_norm_w.reshape(1, D_ATTN),
                    batch, nb, F32)
    ys = _ssd(proj, proj_meta, conv_w, conv_b, dt_bias, a_log, d_skip,
                   ssm_norm_w.reshape(1, SSM_D_INNER), batch, nb, F32)
    return ya, ys


def kernel(x, meta_tokens, rel_bias, ln_mix, w_in, attn_sinks, conv_w, conv_b, dt_bias, a_log, d_skip,
           attn_norm_w, ssm_norm_w, w_out, ln_ffn, peer_wq, peer_keys, peer_u, peer_v, ln_final):
    batch, seq, _ = x.shape
    x2d = x.reshape(batch * seq, D_MODEL)
    ya, ys = _mixer(x2d, batch, seq, meta_tokens, rel_bias, ln_mix[0], w_in[0], attn_sinks[0],
                         conv_w[0], conv_b[0], dt_bias[0], a_log[0], d_skip[0], attn_norm_w[0],
                         ssm_norm_w[0])
    h1, xn = _outproj(ya, ys, x2d, w_out[0].astype(BF16), ln_ffn[0].reshape(1, D_MODEL))
    n_sel, e1, r2, e2 = _route(xn, peer_wq[0].astype(BF16), peer_keys[0].astype(BF16))
    peer_t = _peer(xn, peer_u[0], _transpose_bf16(peer_v[0]), n_sel, e1, r2, e2)
    out = _final(h1, peer_t, ln_final.reshape(1, D_MODEL))
    return out.reshape(batch, seq, D_MODEL)
```

```python
import functools

import jax
import jax.numpy as jnp
import numpy as np
from jax import lax
from jax.experimental import pallas as pl
from jax.experimental.pallas import tpu as pltpu

F32 = jnp.float32
BF16 = jnp.bfloat16

D_MODEL = 2048
N_META = 16
HEAD_DIM = 64
D_ATTN = 1024
ATTN_HEADS = 16
ATTN_KV_HEADS = 4
ATTN_GROUP = 4
D_KV = 256
WINDOW = 128
BLK = 128
REL_BUCKETS = 32
REL_MAX_DIST = 128
SSM_D_INNER = 1024
SSM_HEAD_DIM = 64
SSM_HEADS = 16
SSM_GROUPS = 2
SSM_HPG = 8
SSM_STATE = 128
CONV_WIDTH = 4
D_XBC = 1536
D_BC = 2 * SSM_GROUPS * SSM_STATE
PEER_HEADS = 8
PEER_TOPK = 16
N_KEYS = 128
N_EXPERTS = N_KEYS * N_KEYS
PEER_HALF = 128
EPS = 1e-6
NEG = -1e30

COL_Q = 0
COL_K = 1024
COL_V = 1280
COL_Z = 1536
COL_XS = 2560
COL_BC = 3584
COL_DT = 4096
D_IN = 4112
D_PROJ = 4224
HALF = 512

VMEM_LIMIT = 56 * 1024 * 1024


def _params(*sem, flags=None):
    return pltpu.CompilerParams(dimension_semantics=sem, vmem_limit_bytes=VMEM_LIMIT, flags=flags)


def _dot(a, b):
    return jnp.dot(a, b, preferred_element_type=F32)


def _dot_nt(a, b):
    return lax.dot_general(a, b, (((1,), (1,)), ((), ())), preferred_element_type=F32)


def _split3(x):
    hi = x.astype(BF16)
    r = x - hi.astype(F32)
    mid = r.astype(BF16)
    lo = (r - mid.astype(F32)).astype(BF16)
    return hi, mid, lo


def _dot01_left(m01, x):
    hi, mid, lo = _split3(x)
    return _dot(m01, hi) + _dot(m01, mid) + _dot(m01, lo)


def _dot01_right(x, m01):
    hi, mid, lo = _split3(x)
    return _dot(hi, m01) + _dot(mid, m01) + _dot(lo, m01)


def _silu(x):
    h = 0.5 * x
    return h + h * jnp.tanh(h)


def _softplus(x):
    return jnp.maximum(x, 0.0) + jnp.log1p(jnp.exp(-jnp.abs(x)))


def _gelu_exact(x):
    return 0.5 * x * (1.0 + lax.erf(x * np.float32(np.sqrt(0.5))))


def _inproj_kernel(x_ref, g_ref, w_ref, o_ref):
    x = x_ref[...]
    ms = jnp.mean(x * x, axis=-1, keepdims=True)
    xn = (x * lax.rsqrt(ms + EPS) * g_ref[...]).astype(BF16)
    o_ref[...] = _dot(xn, w_ref[...])


def _inproj(x2d, gain, w_p):
    m = x2d.shape[0]
    tm = min(m, 512)
    return pl.pallas_call(
        _inproj_kernel,
        grid=(m // tm,),
        in_specs=[
            pl.BlockSpec((tm, D_MODEL), lambda i: (i, 0)),
            pl.BlockSpec((1, D_MODEL), lambda i: (0, 0)),
            pl.BlockSpec((D_MODEL, D_PROJ), lambda i: (0, 0), pipeline_mode=pl.Buffered(1)),
        ],
        out_specs=pl.BlockSpec((tm, D_PROJ), lambda i: (i, 0)),
        out_shape=jax.ShapeDtypeStruct((m, D_PROJ), F32),
        compiler_params=_params("parallel"),
        name="inproj",
    )(x2d, gain, w_p)


def _t5_bucket(dist):
    n = np.maximum(dist, 0)
    max_exact = REL_BUCKETS // 2
    large = max_exact + (np.log(np.maximum(n, 1) / max_exact) / np.log(REL_MAX_DIST / max_exact)
                         * (REL_BUCKETS - max_exact)).astype(np.int32)
    large = np.minimum(large, REL_BUCKETS - 1)
    return np.where(n < max_exact, n, large).astype(np.int32)


N_BAND = BLK + N_META


def _band_tables(nb):
    r = np.arange(BLK)[:, None]
    q = np.arange(BLK)[None, :]
    m = np.arange(N_META)[:, None]
    buckets, valids = [], []
    for n in range(nb):
        upper = r > q
        d_band = np.where(upper, q - r + BLK, q - r)
        d_meta = N_META + n * BLK + q - m
        assert (d_band[upper] < WINDOW).all() and (d_band >= 0).all() and (d_meta >= 0).all()
        buckets.append(_t5_bucket(np.concatenate([d_band, d_meta], axis=0)))
        valids.append(np.concatenate([~upper | (n > 0), np.ones((N_META, BLK), bool)], axis=0))
    for n in range(2, nb):
        assert (buckets[n] == buckets[1]).all() and (valids[n] == valids[1]).all()
    last = min(1, nb - 1)
    return np.stack([buckets[0], buckets[last]]), np.stack([valids[0], valids[last]])


ATTN_QB = 8


def _attn_kernel(n_qb, sink_ref, q_ref, kp_ref, ko_ref, km_ref, vp_ref, vo_ref, vmt_ref,
                 bias0_ref, bias1_ref, nw_ref, o_ref, yt_ref):
    upper = (lax.broadcasted_iota(jnp.int32, (BLK, BLK), 0)
             > lax.broadcasted_iota(jnp.int32, (BLK, BLK), 1))
    vmt = vmt_ref[...].astype(BF16)
    km = km_ref[...].astype(BF16)
    v_t = [vp_ref[...].T.astype(BF16)]
    k_b = [kp_ref[...].astype(BF16)]
    for s in range(n_qb):
        v_t.append(vo_ref[s * BLK:(s + 1) * BLK, :].T.astype(BF16))
        k_b.append(ko_ref[s * BLK:(s + 1) * BLK, :].astype(BF16))
    for s in range(n_qb):
        bias_ref = bias0_ref if s == 0 else bias1_ref
        qcols = slice(s * BLK, (s + 1) * BLK)
        q = (q_ref[qcols, :] * np.float32(HEAD_DIM ** -0.5)).astype(BF16)
        for j in range(ATTN_KV_HEADS):
            ks = slice(j * HEAD_DIM, (j + 1) * HEAD_DIM)
            heads = [j * ATTN_GROUP + g for g in range(ATTN_GROUP)]
            q4 = jnp.concatenate([q[:, h * HEAD_DIM:(h + 1) * HEAD_DIM] for h in heads], axis=0)
            lp = _dot_nt(k_b[s][:, ks], q4)
            lo = _dot_nt(k_b[s + 1][:, ks], q4)
            lm = _dot_nt(km[:, ks], q4)
            e_prev, e_own, e_meta, inv = [], [], [], []
            for g, h in enumerate(heads):
                cols = slice(g * BLK, (g + 1) * BLK)
                band = jnp.where(upper, lp[:, cols], lo[:, cols]) + bias_ref[0, h, 0:BLK, :]
                meta = lm[:, cols] + bias_ref[0, h, BLK:N_BAND, :]
                sink = sink_ref[h]
                mx = jnp.maximum(jnp.maximum(jnp.max(band, axis=0, keepdims=True),
                                             jnp.max(meta, axis=0, keepdims=True)), sink)
                eb = jnp.exp(band - mx)
                em = jnp.exp(meta - mx)
                denom = (jnp.sum(eb, axis=0, keepdims=True) + jnp.sum(em, axis=0, keepdims=True)
                         + jnp.exp(sink - mx))
                inv.append(1.0 / denom)
                e_prev.append(jnp.where(upper, eb, 0.0).astype(BF16))
                e_own.append(jnp.where(upper, 0.0, eb).astype(BF16))
                e_meta.append(em.astype(BF16))
            cat = lambda parts: jnp.concatenate(parts, axis=1)
            ot = (_dot(v_t[s][ks, :], cat(e_prev)) + _dot(v_t[s + 1][ks, :], cat(e_own))
                  + _dot(vmt[ks, :], cat(e_meta))) * cat(inv)
            for g, h in enumerate(heads):
                yt_ref[h * HEAD_DIM:(h + 1) * HEAD_DIM, qcols] = ot[:, g * BLK:(g + 1) * BLK]
    yt = yt_ref[...]
    ms = jnp.mean(yt * yt, axis=0, keepdims=True)
    o_ref[...] = ((yt * lax.rsqrt(ms + EPS)).T * nw_ref[...]).astype(o_ref.dtype)


def _attention(proj, proj_meta, sinks, bias, norm_w, batch, nb, ymix_dtype):
    rows = batch * nb * BLK
    kcol, vcol = COL_K // D_KV, COL_V // D_KV
    meta_blk = (BLK - N_META) // N_META
    vm_t = proj_meta[BLK - N_META:, COL_V:COL_V + D_KV].T

    n_qb = max(d for d in range(1, ATTN_QB + 1) if nb % d == 0)
    ns = nb // n_qb
    wide = n_qb * BLK

    def cur(col):
        return lambda b, n: (b * ns + n, col)

    def prev(col):
        return lambda b, n: (jnp.maximum((b * ns + n) * n_qb - 1, 0), col)

    bias_spec = lambda pick: pl.BlockSpec((1, ATTN_HEADS, N_BAND, BLK), lambda b, n: (pick(n), 0, 0, 0))
    return pl.pallas_call(
        functools.partial(_attn_kernel, n_qb),
        grid=(batch, ns),
        in_specs=[
            pl.BlockSpec(memory_space=pltpu.SMEM),
            pl.BlockSpec((wide, D_ATTN), cur(0)),
            pl.BlockSpec((BLK, D_KV), prev(kcol)),
            pl.BlockSpec((wide, D_KV), cur(kcol)),
            pl.BlockSpec((N_META, D_KV), lambda b, n: (meta_blk, kcol)),
            pl.BlockSpec((BLK, D_KV), prev(vcol)),
            pl.BlockSpec((wide, D_KV), cur(vcol)),
            pl.BlockSpec((D_KV, N_META), lambda b, n: (0, 0)),
            bias_spec(lambda n: jnp.minimum(n, 1)),
            bias_spec(lambda n: 1),
            pl.BlockSpec((1, D_ATTN), lambda b, n: (0, 0)),
        ],
        out_specs=pl.BlockSpec((wide, D_ATTN), cur(0)),
        out_shape=jax.ShapeDtypeStruct((rows, D_ATTN), ymix_dtype),
        scratch_shapes=[pltpu.VMEM((D_ATTN, wide), F32)],
        compiler_params=_params("parallel", "arbitrary"),
        name="swa_attention",
    )(sinks, proj, proj, proj, proj_meta, proj, proj, vm_t, bias, bias, norm_w)


def _ssd_chunk(xs_raw, bc_raw, tail_xs, tail_bc, dt_raw, cw, cb, dtb, alog, row_mask, state):
    def conv(blk, tail, w, b):
        ext = jnp.concatenate([tail, blk], axis=0)
        acc = b + w[CONV_WIDTH - 1:CONV_WIDTH, :] * blk
        for back in range(1, CONV_WIDTH):
            shifted = pltpu.roll(ext, back, 0)[8:, :]
            acc = acc + w[CONV_WIDTH - 1 - back:CONV_WIDTH - back, :] * shifted
        return _silu(acc)

    xs = conv(xs_raw, tail_xs, cw[:, :SSM_D_INNER], cb[:, :SSM_D_INNER])
    bc = conv(bc_raw, tail_bc, cw[:, SSM_D_INNER:], cb[:, SSM_D_INNER:])
    dt = _softplus(dt_raw + dtb)
    if row_mask is not None:
        xs = jnp.where(row_mask, xs, 0.0)
        bc = jnp.where(row_mask, bc, 0.0)
        dt = jnp.where(row_mask, dt, 0.0)
    a_neg = -jnp.exp(alog)
    d_a = dt * a_neg

    r = lax.broadcasted_iota(jnp.int32, (BLK, BLK), 0)
    c = lax.broadcasted_iota(jnp.int32, (BLK, BLK), 1)
    tri = r >= c
    cs = _dot01_left(tri.astype(BF16), d_a)
    cs_t = cs.T
    hh = lax.broadcasted_iota(jnp.int32, (BLK, SSM_D_INNER), 0)
    cc = lax.broadcasted_iota(jnp.int32, (BLK, SSM_D_INNER), 1)
    expand = (cc // SSM_HEAD_DIM == hh).astype(BF16)
    dt_rep = _dot01_right(dt, expand)
    ecs_rep = _dot01_right(jnp.exp(cs), expand)
    dec_rep = _dot01_right(jnp.exp(cs[BLK - 1:BLK, :] - cs), expand)

    xdt = xs * dt_rep
    xdtd = (xdt * dec_rep).astype(BF16)
    xdt_b = xdt.astype(BF16)
    chunk_decay = ecs_rep[BLK - 1:BLK, :]

    y_parts, new_state = [], []
    for g in range(SSM_GROUPS):
        b_g = bc[:, g * SSM_STATE:(g + 1) * SSM_STATE]
        c_g = bc[:, (SSM_GROUPS + g) * SSM_STATE:(SSM_GROUPS + g + 1) * SSM_STATE]
        cols = slice(g * 512, (g + 1) * 512)
        cb_g = _dot_nt(c_g.astype(BF16), b_g.astype(BF16))
        y_off = _dot(c_g.astype(BF16), state[g].astype(BF16)) * ecs_rep[:, cols]
        new_state.append(chunk_decay[:, cols] * state[g] + _dot(b_g.T.astype(BF16), xdtd[:, cols]))
        y_diag = []
        for hp in range(SSM_HPG):
            h = g * SSM_HPG + hp
            seg = cs[:, h:h + 1] - cs_t[h:h + 1, :]
            lmat = jnp.exp(jnp.where(tri, seg, -jnp.inf))
            m = (cb_g * lmat).astype(BF16)
            y_diag.append(_dot(m, xdt_b[:, h * SSM_HEAD_DIM:(h + 1) * SSM_HEAD_DIM]))
        y_parts.append(jnp.concatenate(y_diag, axis=1) + y_off)
    return jnp.concatenate(y_parts, axis=1), xs, new_state


def _halves(lo_ref, hi_ref):
    return jnp.concatenate([lo_ref[...], hi_ref[...]], axis=1)


def _ssd_meta_kernel(xs0_ref, xs1_ref, bc_ref, dt_ref, cw_ref, cb_ref, dtb_ref, alog_ref, st_ref):
    rows = lax.broadcasted_iota(jnp.int32, (BLK, 1), 0)
    zero_state = [jnp.zeros((SSM_STATE, 512), F32) for _ in range(SSM_GROUPS)]
    _, _, st = _ssd_chunk(_halves(xs0_ref, xs1_ref), bc_ref[...], jnp.zeros((8, SSM_D_INNER), F32),
                          jnp.zeros((8, D_BC), F32), dt_ref[...], cw_ref[...], cb_ref[...],
                          dtb_ref[...], alog_ref[...], rows >= BLK - N_META, zero_state)
    for g in range(SSM_GROUPS):
        st_ref[g] = st[g]


def _ssd_kernel(xs0_ref, xs1_ref, bc_ref, z0_ref, z1_ref, dt_ref, txs0_ref, txs1_ref, tbc_ref,
                mxs0_ref, mxs1_ref, mbc_ref, st0_ref,
                cw_ref, cb_ref, dtb_ref, alog_ref, dsk_ref, nw_ref, o_ref, st_ref):
    first = pl.program_id(1) == 0

    @pl.when(first)
    def _():
        st_ref[...] = st0_ref[...]

    tail_xs = jnp.where(first, _halves(mxs0_ref, mxs1_ref), _halves(txs0_ref, txs1_ref))
    tail_bc = jnp.where(first, mbc_ref[...], tbc_ref[...])
    state = [st_ref[g] for g in range(SSM_GROUPS)]
    y, xs, new_state = _ssd_chunk(_halves(xs0_ref, xs1_ref), bc_ref[...], tail_xs, tail_bc, dt_ref[...],
                                  cw_ref[...], cb_ref[...], dtb_ref[...], alog_ref[...], None, state)
    for g in range(SSM_GROUPS):
        st_ref[g] = new_state[g]
    y = y + xs * dsk_ref[...]
    yg = y * _silu(_halves(z0_ref, z1_ref))
    outs = []
    for g in range(SSM_GROUPS):
        part = yg[:, g * 512:(g + 1) * 512]
        ms = jnp.mean(part * part, axis=-1, keepdims=True)
        outs.append(part * lax.rsqrt(ms + EPS))
    o_ref[...] = (jnp.concatenate(outs, axis=1) * nw_ref[...]).astype(o_ref.dtype)


def _pad_lanes(v, n=BLK):
    v = v.reshape(1, -1)
    return jnp.pad(v, ((0, 0), (0, n - v.shape[1])))


def _ssd(proj, proj_meta, conv_w, conv_b, dt_bias, a_log, d_skip, norm_w, batch, nc, ymix_dtype):
    rows = batch * nc * BLK
    cb = conv_b.reshape(1, D_XBC)
    dtb, alog = _pad_lanes(dt_bias), _pad_lanes(a_log)
    dsk = jnp.repeat(d_skip, SSM_HEAD_DIM).reshape(1, SSM_D_INNER)
    xs_c, z_c, bc_c, dt_c = COL_XS // HALF, COL_Z // HALF, COL_BC // D_BC, COL_DT // BLK
    full = lambda shape: pl.BlockSpec(shape, lambda *_: (0,) * len(shape))

    state0 = pl.pallas_call(
        _ssd_meta_kernel,
        grid=(1,),
        in_specs=[
            pl.BlockSpec((BLK, HALF), lambda i: (0, xs_c)),
            pl.BlockSpec((BLK, HALF), lambda i: (0, xs_c + 1)),
            pl.BlockSpec((BLK, D_BC), lambda i: (0, bc_c)),
            pl.BlockSpec((BLK, BLK), lambda i: (0, dt_c)),
            full((CONV_WIDTH, D_XBC)), full((1, D_XBC)), full((1, BLK)), full((1, BLK)),
        ],
        out_specs=full((SSM_GROUPS, SSM_STATE, 512)),
        out_shape=jax.ShapeDtypeStruct((SSM_GROUPS, SSM_STATE, 512), F32),
        compiler_params=_params("arbitrary"),
        name="ssd_meta_state",
    )(proj_meta, proj_meta, proj_meta, proj_meta, conv_w, cb, dtb, alog)

    def cur(col, width):
        return pl.BlockSpec((BLK, width), lambda b, c: (b * nc + c, col))

    def tail(col, width):
        return pl.BlockSpec((8, width), lambda b, c: (jnp.maximum((b * nc + c) * (BLK // 8) - 1, 0), col))

    def meta_tail(col, width):
        return pl.BlockSpec((8, width), lambda b, c: (BLK // 8 - 1, col))

    return pl.pallas_call(
        _ssd_kernel,
        grid=(batch, nc),
        in_specs=[
            cur(xs_c, HALF), cur(xs_c + 1, HALF), cur(bc_c, D_BC), cur(z_c, HALF), cur(z_c + 1, HALF),
            cur(dt_c, BLK),
            tail(xs_c, HALF), tail(xs_c + 1, HALF), tail(bc_c, D_BC),
            meta_tail(xs_c, HALF), meta_tail(xs_c + 1, HALF), meta_tail(bc_c, D_BC),
            full((SSM_GROUPS, SSM_STATE, 512)),
            full((CONV_WIDTH, D_XBC)), full((1, D_XBC)), full((1, BLK)), full((1, BLK)),
            full((1, SSM_D_INNER)), full((1, SSM_D_INNER)),
        ],
        out_specs=pl.BlockSpec((BLK, SSM_D_INNER), lambda b, c: (b * nc + c, 0)),
        out_shape=jax.ShapeDtypeStruct((rows, SSM_D_INNER), ymix_dtype),
        scratch_shapes=[pltpu.VMEM((SSM_GROUPS, SSM_STATE, 512), F32)],
        compiler_params=_params("parallel", "arbitrary"),
        name="ssd_mixer",
    )(proj, proj, proj, proj, proj, proj, proj, proj, proj, proj_meta, proj_meta, proj_meta, state0,
      conv_w, cb, dtb, alog, dsk, norm_w)


def _outproj_kernel(ya_ref, ys_ref, x_ref, w_ref, g_ref, h_ref, xn_ref):
    y = jnp.concatenate([ya_ref[...], ys_ref[...]], axis=1).astype(BF16)
    h = x_ref[...] + _dot(y, w_ref[...].astype(BF16))
    h_ref[...] = h
    ms = jnp.mean(h * h, axis=-1, keepdims=True)
    xn_ref[...] = (h * lax.rsqrt(ms + EPS) * g_ref[...]).astype(BF16)


def _outproj(ya, ys, x2d, w_out, gain):
    m = x2d.shape[0]
    tm = min(m, 512)
    return pl.pallas_call(
        _outproj_kernel,
        grid=(m // tm,),
        in_specs=[
            pl.BlockSpec((tm, D_ATTN), lambda i: (i, 0)),
            pl.BlockSpec((tm, SSM_D_INNER), lambda i: (i, 0)),
            pl.BlockSpec((tm, D_MODEL), lambda i: (i, 0)),
            pl.BlockSpec((D_MODEL, D_MODEL), lambda i: (0, 0), pipeline_mode=pl.Buffered(1)),
            pl.BlockSpec((1, D_MODEL), lambda i: (0, 0)),
        ],
        out_specs=[pl.BlockSpec((tm, D_MODEL), lambda i: (i, 0)),
                   pl.BlockSpec((tm, D_MODEL), lambda i: (i, 0))],
        out_shape=[jax.ShapeDtypeStruct((m, D_MODEL), F32),
                   jax.ShapeDtypeStruct((m, D_MODEL), BF16)],
        compiler_params=_params("parallel"),
        name="outproj",
    )(ya, ys, x2d, w_out, gain)


def _oddeven_sort_pairs(n):
    pairs = []
    p = 1
    while p < n:
        k = p
        while k >= 1:
            for j in range(k % p, n - k, 2 * k):
                for i in range(min(k, n - j - k)):
                    if (i + j) // (2 * p) == (i + j + k) // (2 * p):
                        pairs.append((i + j, i + j + k))
            k //= 2
        p *= 2
    return pairs


_SORT16 = _oddeven_sort_pairs(PEER_TOPK)


def _top16_tile(x):
    x = list(x)
    for i, j in _SORT16:
        x[i], x[j] = jnp.maximum(x[i], x[j]), jnp.minimum(x[i], x[j])
    for shift in (4, 2, 1):
        other = [pltpu.roll(v, shift, 0) for v in x]
        x = [jnp.maximum(x[r], other[PEER_TOPK - 1 - r]) for r in range(PEER_TOPK)]
        dist = PEER_TOPK // 2
        while dist >= 1:
            for i in range(PEER_TOPK):
                if i & dist == 0:
                    x[i], x[i + dist] = jnp.maximum(x[i], x[i + dist]), jnp.minimum(x[i], x[i + dist])
            dist //= 2
    return x


def _route_kernel(xn_ref, wq_ref, keys_ref, n_ref, e1_ref, r2_ref, e2_ref, v_ref):
    q = _dot(xn_ref[...], wq_ref[...].astype(BF16)).astype(BF16)
    n_tiles = xn_ref.shape[0] // BLK
    for h in range(PEER_HEADS):
        s, tops = [], []
        for c in range(2):
            qs = q[:, (2 * h + c) * PEER_HALF:(2 * h + c + 1) * PEER_HALF]
            sc = _dot_nt(keys_ref[h, c], qs)
            s.append(sc)
            for lt in range(n_tiles):
                lanes = slice(lt * BLK, (lt + 1) * BLK)
                keys8 = [sc[8 * r:8 * r + 8, lanes] for r in range(N_KEYS // 8)]
                best = _top16_tile(keys8)
                for r in range(PEER_TOPK):
                    v_ref[c, r:r + 1, lanes] = best[r][0:1, :]
                if c == 1:
                    ranks = []
                    for x in keys8:
                        rank = jnp.full_like(x, float(PEER_TOPK))
                        for a in range(PEER_TOPK):
                            rank = jnp.where(x == best[a], float(a), rank)
                        ranks.append(rank)
                    r2 = jnp.concatenate(ranks, axis=0).astype(BF16)
                    e2 = jnp.exp(sc[:, lanes] - best[0][0:1, :]).astype(BF16)
                    r2_ref[h, lt] = pltpu.bitcast(r2, jnp.uint32)
                    e2_ref[h, lt] = pltpu.bitcast(e2, jnp.uint32)
            tops.append(v_ref[c])
        v1, v2 = tops
        blocks = [v1[0:1] + v2, v1[1:2] + v2[0:8]]
        blocks += [v1[a:a + 1] + v2[0:8] for a in range(2, 8)]
        blocks.append(v1[8:16] + v2[0:1])
        cand = jnp.concatenate(blocks, axis=0)
        top = v1[0:1] + v2[0:1]
        rem = cand
        for _ in range(PEER_TOPK - 1):
            m = jnp.max(rem, axis=0, keepdims=True)
            rem = jnp.where(rem == m, -jnp.inf, rem)
        tau = jnp.max(rem, axis=0, keepdims=True)
        z = jnp.sum(jnp.where(cand >= tau, jnp.exp(cand - top), 0.0), axis=0, keepdims=True)
        cnt = [jnp.sum(jnp.where(blocks[a] >= tau, 1.0, 0.0), axis=0, keepdims=True) for a in range(8)]
        cnt_hi = jnp.where(blocks[8] >= tau, 1.0, 0.0)
        n_sel = jnp.zeros_like(s[0])
        for a in range(PEER_TOPK):
            n_a = cnt[a] if a < 8 else cnt_hi[a - 8:a - 7]
            n_sel = jnp.where(s[0] == v1[a:a + 1], n_a, n_sel)
        n_ref[h] = n_sel
        e1_ref[h] = jnp.exp(s[0] - v1[0:1]) / z


def _route(xn, wq, keys):
    t = xn.shape[0]
    tm = min(t, 256)
    big = pl.BlockSpec((PEER_HEADS, N_KEYS, tm), lambda i: (0, 0, i))
    big_shape = jax.ShapeDtypeStruct((PEER_HEADS, N_KEYS, t), F32)
    tiled = pl.BlockSpec((PEER_HEADS, tm // BLK, N_KEYS // 2, BLK), lambda i: (0, i, 0, 0))
    tiled_shape = jax.ShapeDtypeStruct((PEER_HEADS, t // BLK, N_KEYS // 2, BLK), jnp.uint32)
    return pl.pallas_call(
        _route_kernel,
        grid=(t // tm,),
        in_specs=[
            pl.BlockSpec((tm, D_MODEL), lambda i: (i, 0)),
            pl.BlockSpec((D_MODEL, D_MODEL), lambda i: (0, 0), pipeline_mode=pl.Buffered(1)),
            pl.BlockSpec((PEER_HEADS, 2, N_KEYS, PEER_HALF), lambda i: (0, 0, 0, 0)),
        ],
        out_specs=[big, big, tiled, tiled],
        out_shape=[big_shape, big_shape, tiled_shape, tiled_shape],
        scratch_shapes=[pltpu.VMEM((2, PEER_TOPK, tm), F32)],
        compiler_params=_params("parallel"),
        name="peer_route",
    )(xn, wq, keys)


PEER_TB = 512
PEER_EB = 1024
PEER_JCH = 128


def _peer_kernel(xn_ref, u_ref, vt_ref, n_ref, e1_ref, r2_ref, e2_ref, o_ref, a_ref, w_ref):
    @pl.when(pl.program_id(1) == 0)
    def _():
        o_ref[...] = jnp.zeros_like(o_ref)

    a_ref[...] = _gelu_exact(_dot_nt(u_ref[...].astype(BF16), xn_ref[...])).astype(BF16)
    tb = xn_ref.shape[0]

    def per_key(ii, carry):
        row0 = pl.multiple_of(ii * N_KEYS, N_KEYS)
        n_rows = [n_ref[h, pl.ds(ii, 1), :] for h in range(PEER_HEADS)]
        e1_rows = [e1_ref[h, pl.ds(ii, 1), :] for h in range(PEER_HEADS)]
        for lt in range(tb // BLK):
            lanes = slice(lt * BLK, (lt + 1) * BLK)
            wide = lambda row: jnp.broadcast_to(row[:, lanes], (PEER_JCH, BLK)).astype(BF16)
            n_b = [wide(r) for r in n_rows]
            e1_b = [wide(r) for r in e1_rows]
            for j0 in range(0, N_KEYS, PEER_JCH):
                js = slice(j0 // 2, (j0 + PEER_JCH) // 2)
                acc = None
                for h in range(PEER_HEADS):
                    keep = pltpu.bitcast(r2_ref[h, lt, js, :], BF16) < n_b[h]
                    gate = jnp.where(keep, pltpu.bitcast(e2_ref[h, lt, js, :], BF16), 0.0) * e1_b[h]
                    acc = gate if acc is None else acc + gate
                rows = pl.ds(row0 + j0, PEER_JCH)
                w_ref[rows, lanes] = acc * a_ref[rows, lanes]
        return carry

    lax.fori_loop(0, PEER_EB // N_KEYS, per_key, 0)
    o_ref[...] += _dot(vt_ref[...], w_ref[...])


def _transpose_bf16_kernel(x_ref, o_ref):
    o_ref[...] = x_ref[...].T.astype(BF16)


def _transpose_bf16(x, rows=512):
    r, c = x.shape
    return pl.pallas_call(
        _transpose_bf16_kernel,
        grid=(r // rows,),
        in_specs=[pl.BlockSpec((rows, c), lambda i: (i, 0))],
        out_specs=pl.BlockSpec((c, rows), lambda i: (0, i)),
        out_shape=jax.ShapeDtypeStruct((c, r), BF16),
        compiler_params=_params("parallel"),
        name="transpose_v",
    )(x)


def _peer(xn, u, vt_b, n_sel, e1, r2, e2):
    t = xn.shape[0]
    tb = min(t, PEER_TB)
    assert tb == PEER_TB
    ni = PEER_EB // N_KEYS
    small = pl.BlockSpec((PEER_HEADS, ni, tb), lambda i, e: (0, e, i))
    big = pl.BlockSpec((PEER_HEADS, tb // BLK, N_KEYS // 2, BLK), lambda i, e: (0, i, 0, 0))
    return pl.pallas_call(
        _peer_kernel,
        grid=(t // tb, N_EXPERTS // PEER_EB),
        in_specs=[
            pl.BlockSpec((tb, D_MODEL), lambda i, e: (i, 0)),
            pl.BlockSpec((PEER_EB, D_MODEL), lambda i, e: (e, 0)),
            pl.BlockSpec((D_MODEL, PEER_EB), lambda i, e: (0, e)),
            small, small, big, big,
        ],
        out_specs=pl.BlockSpec((D_MODEL, tb), lambda i, e: (0, i)),
        out_shape=jax.ShapeDtypeStruct((D_MODEL, t), F32),
        scratch_shapes=[pltpu.VMEM((PEER_EB, tb), BF16), pltpu.VMEM((PEER_EB, tb), BF16)],
        compiler_params=_params("parallel", "arbitrary"),
        name="peer_experts",
    )(xn, u, vt_b, n_sel, e1, r2, e2)


def _final_kernel(h_ref, pt_ref, g_ref, o_ref):
    h = h_ref[...] + pt_ref[...].T
    ms = jnp.mean(h * h, axis=-1, keepdims=True)
    o_ref[...] = h * lax.rsqrt(ms + EPS) * g_ref[...]


def _final(h1, peer_t, gain):
    t = h1.shape[0]
    tm = min(t, 512)
    rows = pl.BlockSpec((tm, D_MODEL), lambda i: (i, 0))
    return pl.pallas_call(
        _final_kernel,
        grid=(t // tm,),
        in_specs=[rows, pl.BlockSpec((D_MODEL, tm), lambda i: (0, i)),
                  pl.BlockSpec((1, D_MODEL), lambda i: (0, 0))],
        out_specs=rows,
        out_shape=jax.ShapeDtypeStruct((t, D_MODEL), F32),
        compiler_params=_params("parallel"),
        name="final_norm",
    )(h1, peer_t, gain)


def _mixer(x2d, batch, seq, meta_tokens, rel_bias, ln_mix, w_in, sinks, conv_w, conv_b, dt_bias,
           a_log, d_skip, attn_norm_w, ssm_norm_w):
    nb = seq // BLK
    w_p = jnp.pad(w_in.astype(BF16), ((0, 0), (0, D_PROJ - D_IN)))
    gain = ln_mix.reshape(1, D_MODEL)
    meta_pad = jnp.concatenate([jnp.zeros((BLK - N_META, D_MODEL), F32), meta_tokens.astype(F32)], axis=0)
    proj = _inproj(x2d, gain, w_p)
    proj_meta = _inproj(meta_pad, gain, w_p)

    bucket, valid = _band_tables(nb)
    tab = rel_bias.astype(F32)
    bias = jnp.full((2, ATTN_HEADS) + bucket.shape[1:], NEG, F32)
    for b in range(REL_BUCKETS):
        bias = jnp.where((valid & (bucket == b))[:, None], tab[b][None, :, None, None], bias)
    ya = _attention(proj, proj_meta, sinks.astype(F32), bias, attn_norm_w.reshape(1, D_ATTN),
                    batch, nb, F32)
    ys = _ssd(proj, proj_meta, conv_w, conv_b, dt_bias, a_log, d_skip,
                   ssm_norm_w.reshape(1, SSM_D_INNER), batch, nb, F32)
    return ya, ys


def kernel(x, meta_tokens, rel_bias, ln_mix, w_in, attn_sinks, conv_w, conv_b, dt_bias, a_log, d_skip,
           attn_norm_w, ssm_norm_w, w_out, ln_ffn, peer_wq, peer_keys, peer_u, peer_v, ln_final):
    batch, seq, _ = x.shape
    x2d = x.reshape(batch * seq, D_MODEL)
    ya, ys = _mixer(x2d, batch, seq, meta_tokens, rel_bias, ln_mix[0], w_in[0], attn_sinks[0],
                         conv_w[0], conv_b[0], dt_bias[0], a_log[0], d_skip[0], attn_norm_w[0],
                         ssm_norm_w[0])
    h1, xn = _outproj(ya, ys, x2d, w_out[0], ln_ffn[0].reshape(1, D_MODEL))
    n_sel, e1, r2, e2 = _route(xn, peer_wq[0], peer_keys[0].astype(BF16))
    peer_t = _peer(xn, peer_u[0], _transpose_bf16(peer_v[0]), n_sel, e1, r2, e2)
    out = _final(h1, peer_t, ln_final.reshape(1, D_MODEL))
    return out.reshape(batch, seq, D_MODEL)
```

```python
import functools

import jax
import jax.numpy as jnp
import numpy as np
from jax import lax
from jax.experimental import pallas as pl
from jax.experimental.pallas import tpu as pltpu

F32 = jnp.float32
BF16 = jnp.bfloat16

D_MODEL = 2048
N_META = 16
HEAD_DIM = 64
D_ATTN = 1024
ATTN_HEADS = 16
ATTN_KV_HEADS = 4
ATTN_GROUP = 4
D_KV = 256
WINDOW = 128
BLK = 128
REL_BUCKETS = 32
REL_MAX_DIST = 128
SSM_D_INNER = 1024
SSM_HEAD_DIM = 64
SSM_HEADS = 16
SSM_GROUPS = 2
SSM_HPG = 8
SSM_STATE = 128
CONV_WIDTH = 4
D_XBC = 1536
D_BC = 2 * SSM_GROUPS * SSM_STATE
PEER_HEADS = 8
PEER_TOPK = 16
N_KEYS = 128
N_EXPERTS = N_KEYS * N_KEYS
PEER_HALF = 128
EPS = 1e-6
NEG = -1e30

COL_Q = 0
COL_K = 1024
COL_V = 1280
COL_Z = 1536
COL_XS = 2560
COL_BC = 3584
COL_DT = 4096
D_IN = 4112
D_PROJ = 4224
HALF = 512

VMEM_LIMIT = 56 * 1024 * 1024


def _params(*sem, flags=None):
    return pltpu.CompilerParams(dimension_semantics=sem, vmem_limit_bytes=VMEM_LIMIT, flags=flags)


def _dot(a, b):
    return jnp.dot(a, b, preferred_element_type=F32)


def _dot_nt(a, b):
    return lax.dot_general(a, b, (((1,), (1,)), ((), ())), preferred_element_type=F32)


def _split3(x):
    hi = x.astype(BF16)
    r = x - hi.astype(F32)
    mid = r.astype(BF16)
    lo = (r - mid.astype(F32)).astype(BF16)
    return hi, mid, lo


def _dot01_left(m01, x):
    hi, mid, lo = _split3(x)
    return _dot(m01, hi) + _dot(m01, mid) + _dot(m01, lo)


def _dot01_right(x, m01):
    hi, mid, lo = _split3(x)
    return _dot(hi, m01) + _dot(mid, m01) + _dot(lo, m01)


def _silu(x):
    h = 0.5 * x
    return h + h * jnp.tanh(h)


def _softplus(x):
    return jnp.maximum(x, 0.0) + jnp.log1p(jnp.exp(-jnp.abs(x)))


def _gelu_exact(x):
    return 0.5 * x * (1.0 + lax.erf(x * np.float32(np.sqrt(0.5))))


def _inproj_kernel(x_ref, g_ref, w_ref, o_ref):
    x = x_ref[...]
    ms = jnp.mean(x * x, axis=-1, keepdims=True)
    xn = (x * lax.rsqrt(ms + EPS) * g_ref[...]).astype(BF16)
    o_ref[...] = _dot(xn, w_ref[...])


def _inproj(x2d, gain, w_p):
    m = x2d.shape[0]
    tm = min(m, 512)
    return pl.pallas_call(
        _inproj_kernel,
        grid=(m // tm,),
        in_specs=[
            pl.BlockSpec((tm, D_MODEL), lambda i: (i, 0)),
            pl.BlockSpec((1, D_MODEL), lambda i: (0, 0)),
            pl.BlockSpec((D_MODEL, D_PROJ), lambda i: (0, 0), pipeline_mode=pl.Buffered(1)),
        ],
        out_specs=pl.BlockSpec((tm, D_PROJ), lambda i: (i, 0)),
        out_shape=jax.ShapeDtypeStruct((m, D_PROJ), F32),
        compiler_params=_params("parallel"),
        name="inproj",
    )(x2d, gain, w_p)


def _t5_bucket(dist):
    n = np.maximum(dist, 0)
    max_exact = REL_BUCKETS // 2
    large = max_exact + (np.log(np.maximum(n, 1) / max_exact) / np.log(REL_MAX_DIST / max_exact)
                         * (REL_BUCKETS - max_exact)).astype(np.int32)
    large = np.minimum(large, REL_BUCKETS - 1)
    return np.where(n < max_exact, n, large).astype(np.int32)


N_BAND = BLK + N_META


def _band_tables(nb):
    r = np.arange(BLK)[:, None]
    q = np.arange(BLK)[None, :]
    m = np.arange(N_META)[:, None]
    buckets, valids = [], []
    for n in range(nb):
        upper = r > q
        d_band = np.where(upper, q - r + BLK, q - r)
        d_meta = N_META + n * BLK + q - m
        assert (d_band[upper] < WINDOW).all() and (d_band >= 0).all() and (d_meta >= 0).all()
        buckets.append(_t5_bucket(np.concatenate([d_band, d_meta], axis=0)))
        valids.append(np.concatenate([~upper | (n > 0), np.ones((N_META, BLK), bool)], axis=0))
    for n in range(2, nb):
        assert (buckets[n] == buckets[1]).all() and (valids[n] == valids[1]).all()
    last = min(1, nb - 1)
    return np.stack([buckets[0], buckets[last]]), np.stack([valids[0], valids[last]])


ATTN_QB = 8


def _attn_kernel(n_qb, sink_ref, q_ref, kp_ref, ko_ref, km_ref, vp_ref, vo_ref, vmt_ref,
                 bias0_ref, bias1_ref, nw_ref, o_ref, yt_ref):
    upper = (lax.broadcasted_iota(jnp.int32, (BLK, BLK), 0)
             > lax.broadcasted_iota(jnp.int32, (BLK, BLK), 1))
    vmt = vmt_ref[...].astype(BF16)
    km = km_ref[...].astype(BF16)
    v_t = [vp_ref[...].T.astype(BF16)]
    k_b = [kp_ref[...].astype(BF16)]
    for s in range(n_qb):
        v_t.append(vo_ref[s * BLK:(s + 1) * BLK, :].T.astype(BF16))
        k_b.append(ko_ref[s * BLK:(s + 1) * BLK, :].astype(BF16))
    for s in range(n_qb):
        bias_ref = bias0_ref if s == 0 else bias1_ref
        qcols = slice(s * BLK, (s + 1) * BLK)
        q = (q_ref[qcols, :] * np.float32(HEAD_DIM ** -0.5)).astype(BF16)
        for j in range(ATTN_KV_HEADS):
            ks = slice(j * HEAD_DIM, (j + 1) * HEAD_DIM)
            heads = [j * ATTN_GROUP + g for g in range(ATTN_GROUP)]
            q4 = jnp.concatenate([q[:, h * HEAD_DIM:(h + 1) * HEAD_DIM] for h in heads], axis=0)
            lp = _dot_nt(k_b[s][:, ks], q4)
            lo = _dot_nt(k_b[s + 1][:, ks], q4)
            lm = _dot_nt(km[:, ks], q4)
            e_prev, e_own, e_meta, inv = [], [], [], []
            for g, h in enumerate(heads):
                cols = slice(g * BLK, (g + 1) * BLK)
                band = jnp.where(upper, lp[:, cols], lo[:, cols]) + bias_ref[0, h, 0:BLK, :]
                meta = lm[:, cols] + bias_ref[0, h, BLK:N_BAND, :]
                sink = sink_ref[h]
                mx = jnp.maximum(jnp.maximum(jnp.max(band, axis=0, keepdims=True),
                                             jnp.max(meta, axis=0, keepdims=True)), sink)
                eb = jnp.exp(band - mx)
                em = jnp.exp(meta - mx)
                denom = (jnp.sum(eb, axis=0, keepdims=True) + jnp.sum(em, axis=0, keepdims=True)
                         + jnp.exp(sink - mx))
                inv.append(1.0 / denom)
                e_prev.append(jnp.where(upper, eb, 0.0).astype(BF16))
                e_own.append(jnp.where(upper, 0.0, eb).astype(BF16))
                e_meta.append(em.astype(BF16))
            cat = lambda parts: jnp.concatenate(parts, axis=1)
            ot = (_dot(v_t[s][ks, :], cat(e_prev)) + _dot(v_t[s + 1][ks, :], cat(e_own))
                  + _dot(vmt[ks, :], cat(e_meta))) * cat(inv)
            for g, h in enumerate(heads):
                yt_ref[h * HEAD_DIM:(h + 1) * HEAD_DIM, qcols] = ot[:, g * BLK:(g + 1) * BLK]
    yt = yt_ref[...]
    ms = jnp.mean(yt * yt, axis=0, keepdims=True)
    o_ref[...] = ((yt * lax.rsqrt(ms + EPS)).T * nw_ref[...]).astype(o_ref.dtype)


def _attention(proj, proj_meta, sinks, bias, norm_w, batch, nb, ymix_dtype):
    rows = batch * nb * BLK
    kcol, vcol = COL_K // D_KV, COL_V // D_KV
    meta_blk = (BLK - N_META) // N_META
    vm_t = proj_meta[BLK - N_META:, COL_V:COL_V + D_KV].T

    n_qb = max(d for d in range(1, ATTN_QB + 1) if nb % d == 0)
    ns = nb // n_qb
    wide = n_qb * BLK

    def cur(col):
        return lambda b, n: (b * ns + n, col)

    def prev(col):
        return lambda b, n: (jnp.maximum((b * ns + n) * n_qb - 1, 0), col)

    bias_spec = lambda pick: pl.BlockSpec((1, ATTN_HEADS, N_BAND, BLK), lambda b, n: (pick(n), 0, 0, 0))
    return pl.pallas_call(
        functools.partial(_attn_kernel, n_qb),
        grid=(batch, ns),
        in_specs=[
            pl.BlockSpec(memory_space=pltpu.SMEM),
            pl.BlockSpec((wide, D_ATTN), cur(0)),
            pl.BlockSpec((BLK, D_KV), prev(kcol)),
            pl.BlockSpec((wide, D_KV), cur(kcol)),
            pl.BlockSpec((N_META, D_KV), lambda b, n: (meta_blk, kcol)),
            pl.BlockSpec((BLK, D_KV), prev(vcol)),
            pl.BlockSpec((wide, D_KV), cur(vcol)),
            pl.BlockSpec((D_KV, N_META), lambda b, n: (0, 0)),
            bias_spec(lambda n: jnp.minimum(n, 1)),
            bias_spec(lambda n: 1),
            pl.BlockSpec((1, D_ATTN), lambda b, n: (0, 0)),
        ],
        out_specs=pl.BlockSpec((wide, D_ATTN), cur(0)),
        out_shape=jax.ShapeDtypeStruct((rows, D_ATTN), ymix_dtype),
        scratch_shapes=[pltpu.VMEM((D_ATTN, wide), F32)],
        compiler_params=_params("parallel", "arbitrary"),
        name="swa_attention",
    )(sinks, proj, proj, proj, proj_meta, proj, proj, vm_t, bias, bias, norm_w)


def _ssd_chunk(xs_raw, bc_raw, tail_xs, tail_bc, dt_raw, cw, cb, dtb, alog, row_mask, state):
    def conv(blk, tail, w, b):
        ext = jnp.concatenate([tail, blk], axis=0)
        acc = b + w[CONV_WIDTH - 1:CONV_WIDTH, :] * blk
        for back in range(1, CONV_WIDTH):
            shifted = pltpu.roll(ext, back, 0)[8:, :]
            acc = acc + w[CONV_WIDTH - 1 - back:CONV_WIDTH - back, :] * shifted
        return _silu(acc)

    xs = conv(xs_raw, tail_xs, cw[:, :SSM_D_INNER], cb[:, :SSM_D_INNER])
    bc = conv(bc_raw, tail_bc, cw[:, SSM_D_INNER:], cb[:, SSM_D_INNER:])
    dt = _softplus(dt_raw + dtb)
    if row_mask is not None:
        xs = jnp.where(row_mask, xs, 0.0)
        bc = jnp.where(row_mask, bc, 0.0)
        dt = jnp.where(row_mask, dt, 0.0)
    a_neg = -jnp.exp(alog)
    d_a = dt * a_neg

    r = lax.broadcasted_iota(jnp.int32, (BLK, BLK), 0)
    c = lax.broadcasted_iota(jnp.int32, (BLK, BLK), 1)
    tri = r >= c
    cs = _dot01_left(tri.astype(BF16), d_a)
    cs_t = cs.T
    hh = lax.broadcasted_iota(jnp.int32, (BLK, SSM_D_INNER), 0)
    cc = lax.broadcasted_iota(jnp.int32, (BLK, SSM_D_INNER), 1)
    expand = (cc // SSM_HEAD_DIM == hh).astype(BF16)
    dt_rep = _dot01_right(dt, expand)
    ecs_rep = _dot01_right(jnp.exp(cs), expand)
    dec_rep = _dot01_right(jnp.exp(cs[BLK - 1:BLK, :] - cs), expand)

    xdt = xs * dt_rep
    xdtd = (xdt * dec_rep).astype(BF16)
    xdt_b = xdt.astype(BF16)
    chunk_decay = ecs_rep[BLK - 1:BLK, :]

    y_parts, new_state = [], []
    for g in range(SSM_GROUPS):
        b_g = bc[:, g * SSM_STATE:(g + 1) * SSM_STATE]
        c_g = bc[:, (SSM_GROUPS + g) * SSM_STATE:(SSM_GROUPS + g + 1) * SSM_STATE]
        cols = slice(g * 512, (g + 1) * 512)
        cb_g = _dot_nt(c_g.astype(BF16), b_g.astype(BF16))
        y_off = _dot(c_g.astype(BF16), state[g].astype(BF16)) * ecs_rep[:, cols]
        new_state.append(chunk_decay[:, cols] * state[g] + _dot(b_g.T.astype(BF16), xdtd[:, cols]))
        y_diag = []
        for hp in range(SSM_HPG):
            h = g * SSM_HPG + hp
            seg = cs[:, h:h + 1] - cs_t[h:h + 1, :]
            lmat = jnp.exp(jnp.where(tri, seg, -jnp.inf))
            m = (cb_g * lmat).astype(BF16)
            y_diag.append(_dot(m, xdt_b[:, h * SSM_HEAD_DIM:(h + 1) * SSM_HEAD_DIM]))
        y_parts.append(jnp.concatenate(y_diag, axis=1) + y_off)
    return jnp.concatenate(y_parts, axis=1), xs, new_state


def _halves(lo_ref, hi_ref):
    return jnp.concatenate([lo_ref[...], hi_ref[...]], axis=1)


def _ssd_meta_kernel(xs0_ref, xs1_ref, bc_ref, dt_ref, cw_ref, cb_ref, dtb_ref, alog_ref, st_ref):
    rows = lax.broadcasted_iota(jnp.int32, (BLK, 1), 0)
    zero_state = [jnp.zeros((SSM_STATE, 512), F32) for _ in range(SSM_GROUPS)]
    _, _, st = _ssd_chunk(_halves(xs0_ref, xs1_ref), bc_ref[...], jnp.zeros((8, SSM_D_INNER), F32),
                          jnp.zeros((8, D_BC), F32), dt_ref[...], cw_ref[...], cb_ref[...],
                          dtb_ref[...], alog_ref[...], rows >= BLK - N_META, zero_state)
    for g in range(SSM_GROUPS):
        st_ref[g] = st[g]


SSD_ROWS = 4


def _ssd_kernel(xs0_ref, xs1_ref, bc_ref, z0_ref, z1_ref, dt_ref, txs0_ref, txs1_ref, tbc_ref,
                mxs0_ref, mxs1_ref, mbc_ref, st0_ref,
                cw_ref, cb_ref, dtb_ref, alog_ref, dsk_ref, nw_ref, o_ref, st_ref):
    first = pl.program_id(1) == 0
    n_rows = o_ref.shape[0]

    @pl.when(first)
    def _():
        for r in range(n_rows):
            st_ref[r] = st0_ref[...]

    halves = lambda lo, hi, r: jnp.concatenate([lo[r], hi[r]], axis=1)
    for r in range(n_rows):
        tail_xs = jnp.where(first, _halves(mxs0_ref, mxs1_ref), halves(txs0_ref, txs1_ref, r))
        tail_bc = jnp.where(first, mbc_ref[...], tbc_ref[r])
        state = [st_ref[r, g] for g in range(SSM_GROUPS)]
        y, xs, new_state = _ssd_chunk(halves(xs0_ref, xs1_ref, r), bc_ref[r], tail_xs, tail_bc, dt_ref[r],
                                      cw_ref[...], cb_ref[...], dtb_ref[...], alog_ref[...], None, state)
        for g in range(SSM_GROUPS):
            st_ref[r, g] = new_state[g]
        y = y + xs * dsk_ref[...]
        yg = y * _silu(halves(z0_ref, z1_ref, r))
        outs = []
        for g in range(SSM_GROUPS):
            part = yg[:, g * 512:(g + 1) * 512]
            ms = jnp.mean(part * part, axis=-1, keepdims=True)
            outs.append(part * lax.rsqrt(ms + EPS))
        o_ref[r] = (jnp.concatenate(outs, axis=1) * nw_ref[...]).astype(o_ref.dtype)


def _pad_lanes(v, n=BLK):
    v = v.reshape(1, -1)
    return jnp.pad(v, ((0, 0), (0, n - v.shape[1])))


def _ssd(proj, proj_meta, conv_w, conv_b, dt_bias, a_log, d_skip, norm_w, batch, nc, ymix_dtype):
    rows = batch * nc * BLK
    cb = conv_b.reshape(1, D_XBC)
    dtb, alog = _pad_lanes(dt_bias), _pad_lanes(a_log)
    dsk = jnp.repeat(d_skip, SSM_HEAD_DIM).reshape(1, SSM_D_INNER)
    xs_c, z_c, bc_c, dt_c = COL_XS // HALF, COL_Z // HALF, COL_BC // D_BC, COL_DT // BLK
    full = lambda shape: pl.BlockSpec(shape, lambda *_: (0,) * len(shape))

    state0 = pl.pallas_call(
        _ssd_meta_kernel,
        grid=(1,),
        in_specs=[
            pl.BlockSpec((BLK, HALF), lambda i: (0, xs_c)),
            pl.BlockSpec((BLK, HALF), lambda i: (0, xs_c + 1)),
            pl.BlockSpec((BLK, D_BC), lambda i: (0, bc_c)),
            pl.BlockSpec((BLK, BLK), lambda i: (0, dt_c)),
            full((CONV_WIDTH, D_XBC)), full((1, D_XBC)), full((1, BLK)), full((1, BLK)),
        ],
        out_specs=full((SSM_GROUPS, SSM_STATE, 512)),
        out_shape=jax.ShapeDtypeStruct((SSM_GROUPS, SSM_STATE, 512), F32),
        compiler_params=_params("arbitrary"),
        name="ssd_meta_state",
    )(proj_meta, proj_meta, proj_meta, proj_meta, conv_w, cb, dtb, alog)

    n_rows = max(d for d in range(1, SSD_ROWS + 1) if batch % d == 0)
    proj3 = proj.reshape(batch, nc * BLK, D_PROJ)

    def cur(col, width):
        return pl.BlockSpec((n_rows, BLK, width), lambda b, c: (b, c, col))

    def tail(col, width):
        return pl.BlockSpec((n_rows, 8, width), lambda b, c: (b, jnp.maximum(c * (BLK // 8) - 1, 0), col))

    def meta_tail(col, width):
        return pl.BlockSpec((8, width), lambda b, c: (BLK // 8 - 1, col))

    out = pl.pallas_call(
        _ssd_kernel,
        grid=(batch // n_rows, nc),
        in_specs=[
            cur(xs_c, HALF), cur(xs_c + 1, HALF), cur(bc_c, D_BC), cur(z_c, HALF), cur(z_c + 1, HALF),
            cur(dt_c, BLK),
            tail(xs_c, HALF), tail(xs_c + 1, HALF), tail(bc_c, D_BC),
            meta_tail(xs_c, HALF), meta_tail(xs_c + 1, HALF), meta_tail(bc_c, D_BC),
            full((SSM_GROUPS, SSM_STATE, 512)),
            full((CONV_WIDTH, D_XBC)), full((1, D_XBC)), full((1, BLK)), full((1, BLK)),
            full((1, SSM_D_INNER)), full((1, SSM_D_INNER)),
        ],
        out_specs=pl.BlockSpec((n_rows, BLK, SSM_D_INNER), lambda b, c: (b, c, 0)),
        out_shape=jax.ShapeDtypeStruct((batch, nc * BLK, SSM_D_INNER), ymix_dtype),
        scratch_shapes=[pltpu.VMEM((n_rows, SSM_GROUPS, SSM_STATE, 512), F32)],
        compiler_params=_params("parallel", "arbitrary"),
        name="ssd_mixer",
    )(proj3, proj3, proj3, proj3, proj3, proj3, proj3, proj3, proj3, proj_meta, proj_meta, proj_meta,
      state0, conv_w, cb, dtb, alog, dsk, norm_w)
    return out.reshape(rows, SSM_D_INNER)


def _outproj_kernel(ya_ref, ys_ref, x_ref, w_ref, g_ref, h_ref, xn_ref):
    y = jnp.concatenate([ya_ref[...], ys_ref[...]], axis=1).astype(BF16)
    h = x_ref[...] + _dot(y, w_ref[...].astype(BF16))
    h_ref[...] = h
    ms = jnp.mean(h * h, axis=-1, keepdims=True)
    xn_ref[...] = (h * lax.rsqrt(ms + EPS) * g_ref[...]).astype(BF16)


def _outproj(ya, ys, x2d, w_out, gain):
    m = x2d.shape[0]
    tm = min(m, 512)
    return pl.pallas_call(
        _outproj_kernel,
        grid=(m // tm,),
        in_specs=[
            pl.BlockSpec((tm, D_ATTN), lambda i: (i, 0)),
            pl.BlockSpec((tm, SSM_D_INNER), lambda i: (i, 0)),
            pl.BlockSpec((tm, D_MODEL), lambda i: (i, 0)),
            pl.BlockSpec((D_MODEL, D_MODEL), lambda i: (0, 0), pipeline_mode=pl.Buffered(1)),
            pl.BlockSpec((1, D_MODEL), lambda i: (0, 0)),
        ],
        out_specs=[pl.BlockSpec((tm, D_MODEL), lambda i: (i, 0)),
                   pl.BlockSpec((tm, D_MODEL), lambda i: (i, 0))],
        out_shape=[jax.ShapeDtypeStruct((m, D_MODEL), F32),
                   jax.ShapeDtypeStruct((m, D_MODEL), BF16)],
        compiler_params=_params("parallel"),
        name="outproj",
    )(ya, ys, x2d, w_out, gain)


def _oddeven_sort_pairs(n):
    pairs = []
    p = 1
    while p < n:
        k = p
        while k >= 1:
            for j in range(k % p, n - k, 2 * k):
                for i in range(min(k, n - j - k)):
                    if (i + j) // (2 * p) == (i + j + k) // (2 * p):
                        pairs.append((i + j, i + j + k))
            k //= 2
        p *= 2
    return pairs


_SORT16 = _oddeven_sort_pairs(PEER_TOPK)


def _top16_tile(x):
    x = list(x)
    for i, j in _SORT16:
        x[i], x[j] = jnp.maximum(x[i], x[j]), jnp.minimum(x[i], x[j])
    for shift in (4, 2, 1):
        other = [pltpu.roll(v, shift, 0) for v in x]
        x = [jnp.maximum(x[r], other[PEER_TOPK - 1 - r]) for r in range(PEER_TOPK)]
        dist = PEER_TOPK // 2
        while dist >= 1:
            for i in range(PEER_TOPK):
                if i & dist == 0:
                    x[i], x[i + dist] = jnp.maximum(x[i], x[i + dist]), jnp.minimum(x[i], x[i + dist])
            dist //= 2
    return x


def _route_kernel(xn_ref, wq_ref, keys_ref, n_ref, e1_ref, r2_ref, e2_ref, v_ref):
    q = _dot(xn_ref[...], wq_ref[...].astype(BF16)).astype(BF16)
    n_tiles = xn_ref.shape[0] // BLK
    for h in range(PEER_HEADS):
        s, tops = [], []
        for c in range(2):
            qs = q[:, (2 * h + c) * PEER_HALF:(2 * h + c + 1) * PEER_HALF]
            sc = _dot_nt(keys_ref[h, c], qs)
            s.append(sc)
            for lt in range(n_tiles):
                lanes = slice(lt * BLK, (lt + 1) * BLK)
                keys8 = [sc[8 * r:8 * r + 8, lanes] for r in range(N_KEYS // 8)]
                best = _top16_tile(keys8)
                for r in range(PEER_TOPK):
                    v_ref[c, r:r + 1, lanes] = best[r][0:1, :]
                if c == 1:
                    ranks = []
                    for x in keys8:
                        rank = jnp.full_like(x, float(PEER_TOPK))
                        for a in range(PEER_TOPK):
                            rank = jnp.where(x == best[a], float(a), rank)
                        ranks.append(rank)
                    r2 = jnp.concatenate(ranks, axis=0).astype(BF16)
                    e2 = jnp.exp(sc[:, lanes] - best[0][0:1, :]).astype(BF16)
                    r2_ref[h, lt] = pltpu.bitcast(r2, jnp.uint32)
                    e2_ref[h, lt] = pltpu.bitcast(e2, jnp.uint32)
            tops.append(v_ref[c])
        v1, v2 = tops
        blocks = [v1[0:1] + v2, v1[1:2] + v2[0:8]]
        blocks += [v1[a:a + 1] + v2[0:8] for a in range(2, 8)]
        blocks.append(v1[8:16] + v2[0:1])
        cand = jnp.concatenate(blocks, axis=0)
        top = v1[0:1] + v2[0:1]
        rem = cand
        for _ in range(PEER_TOPK - 1):
            m = jnp.max(rem, axis=0, keepdims=True)
            rem = jnp.where(rem == m, -jnp.inf, rem)
        tau = jnp.max(rem, axis=0, keepdims=True)
        z = jnp.sum(jnp.where(cand >= tau, jnp.exp(cand - top), 0.0), axis=0, keepdims=True)
        cnt = [jnp.sum(jnp.where(blocks[a] >= tau, 1.0, 0.0), axis=0, keepdims=True) for a in range(8)]
        cnt_hi = jnp.where(blocks[8] >= tau, 1.0, 0.0)
        n_sel = jnp.zeros_like(s[0])
        for a in range(PEER_TOPK):
            n_a = cnt[a] if a < 8 else cnt_hi[a - 8:a - 7]
            n_sel = jnp.where(s[0] == v1[a:a + 1], n_a, n_sel)
        n_ref[h] = n_sel
        e1_ref[h] = jnp.exp(s[0] - v1[0:1]) / z


def _route(xn, wq, keys):
    t = xn.shape[0]
    tm = min(t, 256)
    big = pl.BlockSpec((PEER_HEADS, N_KEYS, tm), lambda i: (0, 0, i))
    big_shape = jax.ShapeDtypeStruct((PEER_HEADS, N_KEYS, t), F32)
    tiled = pl.BlockSpec((PEER_HEADS, tm // BLK, N_KEYS // 2, BLK), lambda i: (0, i, 0, 0))
    tiled_shape = jax.ShapeDtypeStruct((PEER_HEADS, t // BLK, N_KEYS // 2, BLK), jnp.uint32)
    return pl.pallas_call(
        _route_kernel,
        grid=(t // tm,),
        in_specs=[
            pl.BlockSpec((tm, D_MODEL), lambda i: (i, 0)),
            pl.BlockSpec((D_MODEL, D_MODEL), lambda i: (0, 0), pipeline_mode=pl.Buffered(1)),
            pl.BlockSpec((PEER_HEADS, 2, N_KEYS, PEER_HALF), lambda i: (0, 0, 0, 0)),
        ],
        out_specs=[big, big, tiled, tiled],
        out_shape=[big_shape, big_shape, tiled_shape, tiled_shape],
        scratch_shapes=[pltpu.VMEM((2, PEER_TOPK, tm), F32)],
        compiler_params=_params("parallel"),
        name="peer_route",
    )(xn, wq, keys)


PEER_TB = 512
PEER_EB = 1024
PEER_JCH = 128


def _peer_kernel(xn_ref, u_ref, vt_ref, n_ref, e1_ref, r2_ref, e2_ref, o_ref, a_ref, w_ref):
    @pl.when(pl.program_id(1) == 0)
    def _():
        o_ref[...] = jnp.zeros_like(o_ref)

    a_ref[...] = _gelu_exact(_dot_nt(u_ref[...].astype(BF16), xn_ref[...])).astype(BF16)
    tb = xn_ref.shape[0]

    def per_key(ii, carry):
        row0 = pl.multiple_of(ii * N_KEYS, N_KEYS)
        n_rows = [n_ref[h, pl.ds(ii, 1), :] for h in range(PEER_HEADS)]
        e1_rows = [e1_ref[h, pl.ds(ii, 1), :] for h in range(PEER_HEADS)]
        for lt in range(tb // BLK):
            lanes = slice(lt * BLK, (lt + 1) * BLK)
            wide = lambda row: jnp.broadcast_to(row[:, lanes], (PEER_JCH, BLK)).astype(BF16)
            n_b = [wide(r) for r in n_rows]
            e1_b = [wide(r) for r in e1_rows]
            for j0 in range(0, N_KEYS, PEER_JCH):
                js = slice(j0 // 2, (j0 + PEER_JCH) // 2)
                acc = None
                for h in range(PEER_HEADS):
                    keep = pltpu.bitcast(r2_ref[h, lt, js, :], BF16) < n_b[h]
                    gate = jnp.where(keep, pltpu.bitcast(e2_ref[h, lt, js, :], BF16), 0.0) * e1_b[h]
                    acc = gate if acc is None else acc + gate
                rows = pl.ds(row0 + j0, PEER_JCH)
                w_ref[rows, lanes] = acc * a_ref[rows, lanes]
        return carry

    lax.fori_loop(0, PEER_EB // N_KEYS, per_key, 0)
    o_ref[...] += _dot(vt_ref[...], w_ref[...])


def _transpose_bf16_kernel(x_ref, o_ref):
    o_ref[...] = x_ref[...].T.astype(BF16)


def _transpose_bf16(x, rows=512):
    r, c = x.shape
    return pl.pallas_call(
        _transpose_bf16_kernel,
        grid=(r // rows,),
        in_specs=[pl.BlockSpec((rows, c), lambda i: (i, 0))],
        out_specs=pl.BlockSpec((c, rows), lambda i: (0, i)),
        out_shape=jax.ShapeDtypeStruct((c, r), BF16),
        compiler_params=_params("parallel"),
        name="transpose_v",
    )(x)


def _peer(xn, u, vt_b, n_sel, e1, r2, e2):
    t = xn.shape[0]
    tb = min(t, PEER_TB)
    assert tb == PEER_TB
    ni = PEER_EB // N_KEYS
    small = pl.BlockSpec((PEER_HEADS, ni, tb), lambda i, e: (0, e, i))
    big = pl.BlockSpec((PEER_HEADS, tb // BLK, N_KEYS // 2, BLK), lambda i, e: (0, i, 0, 0))
    return pl.pallas_call(
        _peer_kernel,
        grid=(t // tb, N_EXPERTS // PEER_EB),
        in_specs=[
            pl.BlockSpec((tb, D_MODEL), lambda i, e: (i, 0)),
            pl.BlockSpec((PEER_EB, D_MODEL), lambda i, e: (e, 0)),
            pl.BlockSpec((D_MODEL, PEER_EB), lambda i, e: (0, e)),
            small, small, big, big,
        ],
        out_specs=pl.BlockSpec((D_MODEL, tb), lambda i, e: (0, i)),
        out_shape=jax.ShapeDtypeStruct((D_MODEL, t), F32),
        scratch_shapes=[pltpu.VMEM((PEER_EB, tb), BF16), pltpu.VMEM((PEER_EB, tb), BF16)],
        compiler_params=_params("parallel", "arbitrary"),
        name="peer_experts",
    )(xn, u, vt_b, n_sel, e1, r2, e2)


def _final_kernel(h_ref, pt_ref, g_ref, o_ref):
    h = h_ref[...] + pt_ref[...].T
    ms = jnp.mean(h * h, axis=-1, keepdims=True)
    o_ref[...] = h * lax.rsqrt(ms + EPS) * g_ref[...]


def _final(h1, peer_t, gain):
    t = h1.shape[0]
    tm = min(t, 512)
    rows = pl.BlockSpec((tm, D_MODEL), lambda i: (i, 0))
    return pl.pallas_call(
        _final_kernel,
        grid=(t // tm,),
        in_specs=[rows, pl.BlockSpec((D_MODEL, tm), lambda i: (0, i)),
                  pl.BlockSpec((1, D_MODEL), lambda i: (0, 0))],
        out_specs=rows,
        out_shape=jax.ShapeDtypeStruct((t, D_MODEL), F32),
        compiler_params=_params("parallel"),
        name="final_norm",
    )(h1, peer_t, gain)


def _mixer(x2d, batch, seq, meta_tokens, rel_bias, ln_mix, w_in, sinks, conv_w, conv_b, dt_bias,
           a_log, d_skip, attn_norm_w, ssm_norm_w):
    nb = seq // BLK
    w_p = jnp.pad(w_in.astype(BF16), ((0, 0), (0, D_PROJ - D_IN)))
    gain = ln_mix.reshape(1, D_MODEL)
    meta_pad = jnp.concatenate([jnp.zeros((BLK - N_META, D_MODEL), F32), meta_tokens.astype(F32)], axis=0)
    proj = _inproj(x2d, gain, w_p)
    proj_meta = _inproj(meta_pad, gain, w_p)

    bucket, valid = _band_tables(nb)
    tab = rel_bias.astype(F32)
    bias = jnp.full((2, ATTN_HEADS) + bucket.shape[1:], NEG, F32)
    for b in range(REL_BUCKETS):
        bias = jnp.where((valid & (bucket == b))[:, None], tab[b][None, :, None, None], bias)
    ya = _attention(proj, proj_meta, sinks.astype(F32), bias, attn_norm_w.reshape(1, D_ATTN),
                    batch, nb, F32)
    ys = _ssd(proj, proj_meta, conv_w, conv_b, dt_bias, a_log, d_skip,
                   ssm_norm_w.reshape(1, SSM_D_INNER), batch, nb, F32)
    return ya, ys


def kernel(x, meta_tokens, rel_bias, ln_mix, w_in, attn_sinks, conv_w, conv_b, dt_bias, a_log, d_skip,
           attn_norm_w, ssm_norm_w, w_out, ln_ffn, peer_wq, peer_keys, peer_u, peer_v, ln_final):
    batch, seq, _ = x.shape
    x2d = x.reshape(batch * seq, D_MODEL)
    ya, ys = _mixer(x2d, batch, seq, meta_tokens, rel_bias, ln_mix[0], w_in[0], attn_sinks[0],
                         conv_w[0], conv_b[0], dt_bias[0], a_log[0], d_skip[0], attn_norm_w[0],
                         ssm_norm_w[0])
    h1, xn = _outproj(ya, ys, x2d, w_out[0], ln_ffn[0].reshape(1, D_MODEL))
    n_sel, e1, r2, e2 = _route(xn, peer_wq[0], peer_keys[0].astype(BF16))
    peer_t = _peer(xn, peer_u[0], _transpose_bf16(peer_v[0]), n_sel, e1, r2, e2)
    out = _final(h1, peer_t, ln_final.reshape(1, D_MODEL))
    return out.reshape(batch, seq, D_MODEL)
```

```python
import functools

import jax
import jax.numpy as jnp
import numpy as np
from jax import lax
from jax.experimental import pallas as pl
from jax.experimental.pallas import tpu as pltpu

F32 = jnp.float32
BF16 = jnp.bfloat16

D_MODEL = 2048
N_META = 16
HEAD_DIM = 64
D_ATTN = 1024
ATTN_HEADS = 16
ATTN_KV_HEADS = 4
ATTN_GROUP = 4
D_KV = 256
WINDOW = 128
BLK = 128
REL_BUCKETS = 32
REL_MAX_DIST = 128
SSM_D_INNER = 1024
SSM_HEAD_DIM = 64
SSM_HEADS = 16
SSM_GROUPS = 2
SSM_HPG = 8
SSM_STATE = 128
CONV_WIDTH = 4
D_XBC = 1536
D_BC = 2 * SSM_GROUPS * SSM_STATE
PEER_HEADS = 8
PEER_TOPK = 16
N_KEYS = 128
N_EXPERTS = N_KEYS * N_KEYS
PEER_HALF = 128
EPS = 1e-6
NEG = -1e30

COL_Q = 0
COL_K = 1024
COL_V = 1280
COL_Z = 1536
COL_XS = 2560
COL_BC = 3584
COL_DT = 4096
D_IN = 4112
D_PROJ = 4224
HALF = 512

VMEM_LIMIT = 56 * 1024 * 1024


def _params(*sem, flags=None):
    return pltpu.CompilerParams(dimension_semantics=sem, vmem_limit_bytes=VMEM_LIMIT, flags=flags)


def _dot(a, b):
    return jnp.dot(a, b, preferred_element_type=F32)


def _dot_nt(a, b):
    return lax.dot_general(a, b, (((1,), (1,)), ((), ())), preferred_element_type=F32)


def _split3(x):
    hi = x.astype(BF16)
    r = x - hi.astype(F32)
    mid = r.astype(BF16)
    lo = (r - mid.astype(F32)).astype(BF16)
    return hi, mid, lo


def _dot01_left(m01, x):
    hi, mid, lo = _split3(x)
    return _dot(m01, hi) + _dot(m01, mid) + _dot(m01, lo)


def _dot01_right(x, m01):
    hi, mid, lo = _split3(x)
    return _dot(hi, m01) + _dot(mid, m01) + _dot(lo, m01)


def _silu(x):
    h = 0.5 * x
    return h + h * jnp.tanh(h)


def _softplus(x):
    return jnp.maximum(x, 0.0) + jnp.log1p(jnp.exp(-jnp.abs(x)))


def _gelu_exact(x):
    return 0.5 * x * (1.0 + lax.erf(x * np.float32(np.sqrt(0.5))))


def _inproj_kernel(x_ref, g_ref, w_ref, o_ref):
    x = x_ref[...]
    ms = jnp.mean(x * x, axis=-1, keepdims=True)
    xn = (x * lax.rsqrt(ms + EPS) * g_ref[...]).astype(BF16)
    o_ref[:, :D_IN] = _dot(xn, w_ref[...].astype(BF16))
    o_ref[:, D_IN:] = jnp.zeros((x.shape[0], D_PROJ - D_IN), F32)


def _inproj(x2d, gain, w_in):
    m = x2d.shape[0]
    tm = min(m, 256)
    return pl.pallas_call(
        _inproj_kernel,
        grid=(m // tm,),
        in_specs=[
            pl.BlockSpec((tm, D_MODEL), lambda i: (i, 0)),
            pl.BlockSpec((1, D_MODEL), lambda i: (0, 0)),
            pl.BlockSpec((D_MODEL, D_IN), lambda i: (0, 0), pipeline_mode=pl.Buffered(1)),
        ],
        out_specs=pl.BlockSpec((tm, D_PROJ), lambda i: (i, 0)),
        out_shape=jax.ShapeDtypeStruct((m, D_PROJ), F32),
        compiler_params=_params("parallel"),
        name="inproj",
    )(x2d, gain, w_in)


def _t5_bucket(dist):
    n = np.maximum(dist, 0)
    max_exact = REL_BUCKETS // 2
    large = max_exact + (np.log(np.maximum(n, 1) / max_exact) / np.log(REL_MAX_DIST / max_exact)
                         * (REL_BUCKETS - max_exact)).astype(np.int32)
    large = np.minimum(large, REL_BUCKETS - 1)
    return np.where(n < max_exact, n, large).astype(np.int32)


N_BAND = BLK + N_META


def _band_tables(nb):
    r = np.arange(BLK)[:, None]
    q = np.arange(BLK)[None, :]
    m = np.arange(N_META)[:, None]
    buckets, valids = [], []
    for n in range(nb):
        upper = r > q
        d_band = np.where(upper, q - r + BLK, q - r)
        d_meta = N_META + n * BLK + q - m
        assert (d_band[upper] < WINDOW).all() and (d_band >= 0).all() and (d_meta >= 0).all()
        buckets.append(_t5_bucket(np.concatenate([d_band, d_meta], axis=0)))
        valids.append(np.concatenate([~upper | (n > 0), np.ones((N_META, BLK), bool)], axis=0))
    for n in range(2, nb):
        assert (buckets[n] == buckets[1]).all() and (valids[n] == valids[1]).all()
    last = min(1, nb - 1)
    return np.stack([buckets[0], buckets[last]]), np.stack([valids[0], valids[last]])


ATTN_QB = 8


def _attn_kernel(n_qb, sink_ref, q_ref, kp_ref, ko_ref, km_ref, vp_ref, vo_ref, vmt_ref,
                 bias0_ref, bias1_ref, nw_ref, o_ref, yt_ref):
    upper = (lax.broadcasted_iota(jnp.int32, (BLK, BLK), 0)
             > lax.broadcasted_iota(jnp.int32, (BLK, BLK), 1))
    vmt = vmt_ref[...].astype(BF16)
    km = km_ref[...].astype(BF16)
    v_t = [vp_ref[...].T.astype(BF16)]
    k_b = [kp_ref[...].astype(BF16)]
    for s in range(n_qb):
        v_t.append(vo_ref[s * BLK:(s + 1) * BLK, :].T.astype(BF16))
        k_b.append(ko_ref[s * BLK:(s + 1) * BLK, :].astype(BF16))
    for s in range(n_qb):
        bias_ref = bias0_ref if s == 0 else bias1_ref
        qcols = slice(s * BLK, (s + 1) * BLK)
        q = (q_ref[qcols, :] * np.float32(HEAD_DIM ** -0.5)).astype(BF16)
        for j in range(ATTN_KV_HEADS):
            ks = slice(j * HEAD_DIM, (j + 1) * HEAD_DIM)
            heads = [j * ATTN_GROUP + g for g in range(ATTN_GROUP)]
            q4 = jnp.concatenate([q[:, h * HEAD_DIM:(h + 1) * HEAD_DIM] for h in heads], axis=0)
            lp = _dot_nt(k_b[s][:, ks], q4)
            lo = _dot_nt(k_b[s + 1][:, ks], q4)
            lm = _dot_nt(km[:, ks], q4)
            e_prev, e_own, e_meta, inv = [], [], [], []
            for g, h in enumerate(heads):
                cols = slice(g * BLK, (g + 1) * BLK)
                band = jnp.where(upper, lp[:, cols], lo[:, cols]) + bias_ref[0, h, 0:BLK, :]
                meta = lm[:, cols] + bias_ref[0, h, BLK:N_BAND, :]
                sink = sink_ref[h]
                mx = jnp.maximum(jnp.maximum(jnp.max(band, axis=0, keepdims=True),
                                             jnp.max(meta, axis=0, keepdims=True)), sink)
                eb = jnp.exp(band - mx)
                em = jnp.exp(meta - mx)
                denom = (jnp.sum(eb, axis=0, keepdims=True) + jnp.sum(em, axis=0, keepdims=True)
                         + jnp.exp(sink - mx))
                inv.append(1.0 / denom)
                e_prev.append(jnp.where(upper, eb, 0.0).astype(BF16))
                e_own.append(jnp.where(upper, 0.0, eb).astype(BF16))
                e_meta.append(em.astype(BF16))
            cat = lambda parts: jnp.concatenate(parts, axis=1)
            ot = (_dot(v_t[s][ks, :], cat(e_prev)) + _dot(v_t[s + 1][ks, :], cat(e_own))
                  + _dot(vmt[ks, :], cat(e_meta))) * cat(inv)
            for g, h in enumerate(heads):
                yt_ref[h * HEAD_DIM:(h + 1) * HEAD_DIM, qcols] = ot[:, g * BLK:(g + 1) * BLK]
    yt = yt_ref[...]
    ms = jnp.mean(yt * yt, axis=0, keepdims=True)
    o_ref[...] = ((yt * lax.rsqrt(ms + EPS)).T * nw_ref[...]).astype(o_ref.dtype)


def _attention(proj, proj_meta, sinks, bias, norm_w, batch, nb, ymix_dtype):
    rows = batch * nb * BLK
    kcol, vcol = COL_K // D_KV, COL_V // D_KV
    meta_blk = (BLK - N_META) // N_META
    vm_t = proj_meta[BLK - N_META:, COL_V:COL_V + D_KV].T

    n_qb = max(d for d in range(1, ATTN_QB + 1) if nb % d == 0)
    ns = nb // n_qb
    wide = n_qb * BLK

    def cur(col):
        return lambda b, n: (b * ns + n, col)

    def prev(col):
        return lambda b, n: (jnp.maximum((b * ns + n) * n_qb - 1, 0), col)

    bias_spec = lambda pick: pl.BlockSpec((1, ATTN_HEADS, N_BAND, BLK), lambda b, n: (pick(n), 0, 0, 0))
    return pl.pallas_call(
        functools.partial(_attn_kernel, n_qb),
        grid=(batch, ns),
        in_specs=[
            pl.BlockSpec(memory_space=pltpu.SMEM),
            pl.BlockSpec((wide, D_ATTN), cur(0)),
            pl.BlockSpec((BLK, D_KV), prev(kcol)),
            pl.BlockSpec((wide, D_KV), cur(kcol)),
            pl.BlockSpec((N_META, D_KV), lambda b, n: (meta_blk, kcol)),
            pl.BlockSpec((BLK, D_KV), prev(vcol)),
            pl.BlockSpec((wide, D_KV), cur(vcol)),
            pl.BlockSpec((D_KV, N_META), lambda b, n: (0, 0)),
            bias_spec(lambda n: jnp.minimum(n, 1)),
            bias_spec(lambda n: 1),
            pl.BlockSpec((1, D_ATTN), lambda b, n: (0, 0)),
        ],
        out_specs=pl.BlockSpec((wide, D_ATTN), cur(0)),
        out_shape=jax.ShapeDtypeStruct((rows, D_ATTN), ymix_dtype),
        scratch_shapes=[pltpu.VMEM((D_ATTN, wide), F32)],
        compiler_params=_params("parallel", "arbitrary"),
        name="swa_attention",
    )(sinks, proj, proj, proj, proj_meta, proj, proj, vm_t, bias, bias, norm_w)


def _ssd_chunk(xs_raw, bc_raw, tail_xs, tail_bc, dt_raw, cw, cb, dtb, alog, row_mask, state):
    def conv(blk, tail, w, b):
        ext = jnp.concatenate([tail, blk], axis=0)
        acc = b + w[CONV_WIDTH - 1:CONV_WIDTH, :] * blk
        for back in range(1, CONV_WIDTH):
            shifted = pltpu.roll(ext, back, 0)[8:, :]
            acc = acc + w[CONV_WIDTH - 1 - back:CONV_WIDTH - back, :] * shifted
        return _silu(acc)

    xs = conv(xs_raw, tail_xs, cw[:, :SSM_D_INNER], cb[:, :SSM_D_INNER])
    bc = conv(bc_raw, tail_bc, cw[:, SSM_D_INNER:], cb[:, SSM_D_INNER:])
    dt = _softplus(dt_raw + dtb)
    if row_mask is not None:
        xs = jnp.where(row_mask, xs, 0.0)
        bc = jnp.where(row_mask, bc, 0.0)
        dt = jnp.where(row_mask, dt, 0.0)
    a_neg = -jnp.exp(alog)
    d_a = dt * a_neg

    r = lax.broadcasted_iota(jnp.int32, (BLK, BLK), 0)
    c = lax.broadcasted_iota(jnp.int32, (BLK, BLK), 1)
    tri = r >= c
    cs = _dot01_left(tri.astype(BF16), d_a)
    cs_t = cs.T
    hh = lax.broadcasted_iota(jnp.int32, (BLK, SSM_D_INNER), 0)
    cc = lax.broadcasted_iota(jnp.int32, (BLK, SSM_D_INNER), 1)
    expand = (cc // SSM_HEAD_DIM == hh).astype(BF16)
    dt_rep = _dot01_right(dt, expand)
    ecs_rep = _dot01_right(jnp.exp(cs), expand)
    dec_rep = _dot01_right(jnp.exp(cs[BLK - 1:BLK, :] - cs), expand)

    xdt = xs * dt_rep
    xdtd = (xdt * dec_rep).astype(BF16)
    xdt_b = xdt.astype(BF16)
    chunk_decay = ecs_rep[BLK - 1:BLK, :]

    y_parts, new_state = [], []
    for g in range(SSM_GROUPS):
        b_g = bc[:, g * SSM_STATE:(g + 1) * SSM_STATE]
        c_g = bc[:, (SSM_GROUPS + g) * SSM_STATE:(SSM_GROUPS + g + 1) * SSM_STATE]
        cols = slice(g * 512, (g + 1) * 512)
        cb_g = _dot_nt(c_g.astype(BF16), b_g.astype(BF16))
        y_off = _dot(c_g.astype(BF16), state[g].astype(BF16)) * ecs_rep[:, cols]
        new_state.append(chunk_decay[:, cols] * state[g] + _dot(b_g.T.astype(BF16), xdtd[:, cols]))
        y_diag = []
        for hp in range(SSM_HPG):
            h = g * SSM_HPG + hp
            seg = cs[:, h:h + 1] - cs_t[h:h + 1, :]
            lmat = jnp.exp(jnp.where(tri, seg, -jnp.inf))
            m = (cb_g * lmat).astype(BF16)
            y_diag.append(_dot(m, xdt_b[:, h * SSM_HEAD_DIM:(h + 1) * SSM_HEAD_DIM]))
        y_parts.append(jnp.concatenate(y_diag, axis=1) + y_off)
    return jnp.concatenate(y_parts, axis=1), xs, new_state


def _halves(lo_ref, hi_ref):
    return jnp.concatenate([lo_ref[...], hi_ref[...]], axis=1)


def _ssd_meta_kernel(xs0_ref, xs1_ref, bc_ref, dt_ref, cw_ref, cb_ref, dtb_ref, alog_ref, st_ref):
    rows = lax.broadcasted_iota(jnp.int32, (BLK, 1), 0)
    zero_state = [jnp.zeros((SSM_STATE, 512), F32) for _ in range(SSM_GROUPS)]
    _, _, st = _ssd_chunk(_halves(xs0_ref, xs1_ref), bc_ref[...], jnp.zeros((8, SSM_D_INNER), F32),
                          jnp.zeros((8, D_BC), F32), dt_ref[...], cw_ref[...], cb_ref[...],
                          dtb_ref[...], alog_ref[...], rows >= BLK - N_META, zero_state)
    for g in range(SSM_GROUPS):
        st_ref[g] = st[g]


SSD_ROWS = 4


def _ssd_kernel(xs0_ref, xs1_ref, bc_ref, z0_ref, z1_ref, dt_ref, txs0_ref, txs1_ref, tbc_ref,
                mxs0_ref, mxs1_ref, mbc_ref, st0_ref,
                cw_ref, cb_ref, dtb_ref, alog_ref, dsk_ref, nw_ref, o_ref, st_ref):
    first = pl.program_id(1) == 0
    n_rows = o_ref.shape[0]

    @pl.when(first)
    def _():
        for r in range(n_rows):
            st_ref[r] = st0_ref[...]

    halves = lambda lo, hi, r: jnp.concatenate([lo[r], hi[r]], axis=1)
    for r in range(n_rows):
        tail_xs = jnp.where(first, _halves(mxs0_ref, mxs1_ref), halves(txs0_ref, txs1_ref, r))
        tail_bc = jnp.where(first, mbc_ref[...], tbc_ref[r])
        state = [st_ref[r, g] for g in range(SSM_GROUPS)]
        y, xs, new_state = _ssd_chunk(halves(xs0_ref, xs1_ref, r), bc_ref[r], tail_xs, tail_bc, dt_ref[r],
                                      cw_ref[...], cb_ref[...], dtb_ref[...], alog_ref[...], None, state)
        for g in range(SSM_GROUPS):
            st_ref[r, g] = new_state[g]
        y = y + xs * dsk_ref[...]
        yg = y * _silu(halves(z0_ref, z1_ref, r))
        outs = []
        for g in range(SSM_GROUPS):
            part = yg[:, g * 512:(g + 1) * 512]
            ms = jnp.mean(part * part, axis=-1, keepdims=True)
            outs.append(part * lax.rsqrt(ms + EPS))
        o_ref[r] = (jnp.concatenate(outs, axis=1) * nw_ref[...]).astype(o_ref.dtype)


def _pad_lanes(v, n=BLK):
    v = v.reshape(1, -1)
    return jnp.pad(v, ((0, 0), (0, n - v.shape[1])))


def _ssd(proj, proj_meta, conv_w, conv_b, dt_bias, a_log, d_skip, norm_w, batch, nc, ymix_dtype):
    rows = batch * nc * BLK
    cb = conv_b.reshape(1, D_XBC)
    dtb, alog = _pad_lanes(dt_bias), _pad_lanes(a_log)
    dsk = jnp.repeat(d_skip, SSM_HEAD_DIM).reshape(1, SSM_D_INNER)
    xs_c, z_c, bc_c, dt_c = COL_XS // HALF, COL_Z // HALF, COL_BC // D_BC, COL_DT // BLK
    full = lambda shape: pl.BlockSpec(shape, lambda *_: (0,) * len(shape))

    state0 = pl.pallas_call(
        _ssd_meta_kernel,
        grid=(1,),
        in_specs=[
            pl.BlockSpec((BLK, HALF), lambda i: (0, xs_c)),
            pl.BlockSpec((BLK, HALF), lambda i: (0, xs_c + 1)),
            pl.BlockSpec((BLK, D_BC), lambda i: (0, bc_c)),
            pl.BlockSpec((BLK, BLK), lambda i: (0, dt_c)),
            full((CONV_WIDTH, D_XBC)), full((1, D_XBC)), full((1, BLK)), full((1, BLK)),
        ],
        out_specs=full((SSM_GROUPS, SSM_STATE, 512)),
        out_shape=jax.ShapeDtypeStruct((SSM_GROUPS, SSM_STATE, 512), F32),
        compiler_params=_params("arbitrary"),
        name="ssd_meta_state",
    )(proj_meta, proj_meta, proj_meta, proj_meta, conv_w, cb, dtb, alog)

    n_rows = max(d for d in range(1, SSD_ROWS + 1) if batch % d == 0)
    proj3 = proj.reshape(batch, nc * BLK, D_PROJ)

    def cur(col, width):
        return pl.BlockSpec((n_rows, BLK, width), lambda b, c: (b, c, col))

    def tail(col, width):
        return pl.BlockSpec((n_rows, 8, width), lambda b, c: (b, jnp.maximum(c * (BLK // 8) - 1, 0), col))

    def meta_tail(col, width):
        return pl.BlockSpec((8, width), lambda b, c: (BLK // 8 - 1, col))

    out = pl.pallas_call(
        _ssd_kernel,
        grid=(batch // n_rows, nc),
        in_specs=[
            cur(xs_c, HALF), cur(xs_c + 1, HALF), cur(bc_c, D_BC), cur(z_c, HALF), cur(z_c + 1, HALF),
            cur(dt_c, BLK),
            tail(xs_c, HALF), tail(xs_c + 1, HALF), tail(bc_c, D_BC),
            meta_tail(xs_c, HALF), meta_tail(xs_c + 1, HALF), meta_tail(bc_c, D_BC),
            full((SSM_GROUPS, SSM_STATE, 512)),
            full((CONV_WIDTH, D_XBC)), full((1, D_XBC)), full((1, BLK)), full((1, BLK)),
            full((1, SSM_D_INNER)), full((1, SSM_D_INNER)),
        ],
        out_specs=pl.BlockSpec((n_rows, BLK, SSM_D_INNER), lambda b, c: (b, c, 0)),
        out_shape=jax.ShapeDtypeStruct((batch, nc * BLK, SSM_D_INNER), ymix_dtype),
        scratch_shapes=[pltpu.VMEM((n_rows, SSM_GROUPS, SSM_STATE, 512), F32)],
        compiler_params=_params("parallel", "arbitrary"),
        name="ssd_mixer",
    )(proj3, proj3, proj3, proj3, proj3, proj3, proj3, proj3, proj3, proj_meta, proj_meta, proj_meta,
      state0, conv_w, cb, dtb, alog, dsk, norm_w)
    return out.reshape(rows, SSM_D_INNER)


def _outproj_kernel(ya_ref, ys_ref, x_ref, w_ref, g_ref, h_ref, xn_ref):
    y = jnp.concatenate([ya_ref[...], ys_ref[...]], axis=1).astype(BF16)
    h = x_ref[...] + _dot(y, w_ref[...].astype(BF16))
    h_ref[...] = h
    ms = jnp.mean(h * h, axis=-1, keepdims=True)
    xn_ref[...] = (h * lax.rsqrt(ms + EPS) * g_ref[...]).astype(BF16)


def _outproj(ya, ys, x2d, w_out, gain):
    m = x2d.shape[0]
    tm = min(m, 512)
    return pl.pallas_call(
        _outproj_kernel,
        grid=(m // tm,),
        in_specs=[
            pl.BlockSpec((tm, D_ATTN), lambda i: (i, 0)),
            pl.BlockSpec((tm, SSM_D_INNER), lambda i: (i, 0)),
            pl.BlockSpec((tm, D_MODEL), lambda i: (i, 0)),
            pl.BlockSpec((D_MODEL, D_MODEL), lambda i: (0, 0), pipeline_mode=pl.Buffered(1)),
            pl.BlockSpec((1, D_MODEL), lambda i: (0, 0)),
        ],
        out_specs=[pl.BlockSpec((tm, D_MODEL), lambda i: (i, 0)),
                   pl.BlockSpec((tm, D_MODEL), lambda i: (i, 0))],
        out_shape=[jax.ShapeDtypeStruct((m, D_MODEL), F32),
                   jax.ShapeDtypeStruct((m, D_MODEL), BF16)],
        compiler_params=_params("parallel"),
        name="outproj",
    )(ya, ys, x2d, w_out, gain)


def _oddeven_sort_pairs(n):
    pairs = []
    p = 1
    while p < n:
        k = p
        while k >= 1:
            for j in range(k % p, n - k, 2 * k):
                for i in range(min(k, n - j - k)):
                    if (i + j) // (2 * p) == (i + j + k) // (2 * p):
                        pairs.append((i + j, i + j + k))
            k //= 2
        p *= 2
    return pairs


_SORT16 = _oddeven_sort_pairs(PEER_TOPK)


def _top16_tile(x):
    x = list(x)
    for i, j in _SORT16:
        x[i], x[j] = jnp.maximum(x[i], x[j]), jnp.minimum(x[i], x[j])
    for shift in (4, 2, 1):
        other = [pltpu.roll(v, shift, 0) for v in x]
        x = [jnp.maximum(x[r], other[PEER_TOPK - 1 - r]) for r in range(PEER_TOPK)]
        dist = PEER_TOPK // 2
        while dist >= 1:
            for i in range(PEER_TOPK):
                if i & dist == 0:
                    x[i], x[i + dist] = jnp.maximum(x[i], x[i + dist]), jnp.minimum(x[i], x[i + dist])
            dist //= 2
    return x


def _route_kernel(xn_ref, wq_ref, keys_ref, n_ref, e1_ref, r2_ref, e2_ref, v_ref):
    q = _dot(xn_ref[...], wq_ref[...].astype(BF16)).astype(BF16)
    n_tiles = xn_ref.shape[0] // BLK
    for h in range(PEER_HEADS):
        s, tops = [], []
        for c in range(2):
            qs = q[:, (2 * h + c) * PEER_HALF:(2 * h + c + 1) * PEER_HALF]
            sc = _dot_nt(keys_ref[h, c], qs)
            s.append(sc)
            for lt in range(n_tiles):
                lanes = slice(lt * BLK, (lt + 1) * BLK)
                keys8 = [sc[8 * r:8 * r + 8, lanes] for r in range(N_KEYS // 8)]
                best = _top16_tile(keys8)
                for r in range(PEER_TOPK):
                    v_ref[c, r:r + 1, lanes] = best[r][0:1, :]
                if c == 1:
                    ranks = []
                    for x in keys8:
                        rank = jnp.full_like(x, float(PEER_TOPK))
                        for a in range(PEER_TOPK):
                            rank = jnp.where(x == best[a], float(a), rank)
                        ranks.append(rank)
                    r2 = jnp.concatenate(ranks, axis=0).astype(BF16)
                    e2 = jnp.exp(sc[:, lanes] - best[0][0:1, :]).astype(BF16)
                    r2_ref[h, lt] = pltpu.bitcast(r2, jnp.uint32)
                    e2_ref[h, lt] = pltpu.bitcast(e2, jnp.uint32)
            tops.append(v_ref[c])
        v1, v2 = tops
        blocks = [v1[0:1] + v2, v1[1:2] + v2[0:8]]
        blocks += [v1[a:a + 1] + v2[0:8] for a in range(2, 8)]
        blocks.append(v1[8:16] + v2[0:1])
        cand = jnp.concatenate(blocks, axis=0)
        top = v1[0:1] + v2[0:1]
        rem = cand
        for _ in range(PEER_TOPK - 1):
            m = jnp.max(rem, axis=0, keepdims=True)
            rem = jnp.where(rem == m, -jnp.inf, rem)
        tau = jnp.max(rem, axis=0, keepdims=True)
        z = jnp.sum(jnp.where(cand >= tau, jnp.exp(cand - top), 0.0), axis=0, keepdims=True)
        cnt = [jnp.sum(jnp.where(blocks[a] >= tau, 1.0, 0.0), axis=0, keepdims=True) for a in range(8)]
        cnt_hi = jnp.where(blocks[8] >= tau, 1.0, 0.0)
        n_sel = jnp.zeros_like(s[0])
        for a in range(PEER_TOPK):
            n_a = cnt[a] if a < 8 else cnt_hi[a - 8:a - 7]
            n_sel = jnp.where(s[0] == v1[a:a + 1], n_a, n_sel)
        n_ref[h] = n_sel
        e1_ref[h] = jnp.exp(s[0] - v1[0:1]) / z


def _route(xn, wq, keys):
    t = xn.shape[0]
    tm = min(t, 256)
    big = pl.BlockSpec((PEER_HEADS, N_KEYS, tm), lambda i: (0, 0, i))
    big_shape = jax.ShapeDtypeStruct((PEER_HEADS, N_KEYS, t), F32)
    tiled = pl.BlockSpec((PEER_HEADS, tm // BLK, N_KEYS // 2, BLK), lambda i: (0, i, 0, 0))
    tiled_shape = jax.ShapeDtypeStruct((PEER_HEADS, t // BLK, N_KEYS // 2, BLK), jnp.uint32)
    return pl.pallas_call(
        _route_kernel,
        grid=(t // tm,),
        in_specs=[
            pl.BlockSpec((tm, D_MODEL), lambda i: (i, 0)),
            pl.BlockSpec((D_MODEL, D_MODEL), lambda i: (0, 0), pipeline_mode=pl.Buffered(1)),
            pl.BlockSpec((PEER_HEADS, 2, N_KEYS, PEER_HALF), lambda i: (0, 0, 0, 0)),
        ],
        out_specs=[big, big, tiled, tiled],
        out_shape=[big_shape, big_shape, tiled_shape, tiled_shape],
        scratch_shapes=[pltpu.VMEM((2, PEER_TOPK, tm), F32)],
        compiler_params=_params("parallel"),
        name="peer_route",
    )(xn, wq, keys)


PEER_TB = 512
PEER_EB = 1024
PEER_JCH = 128


def _peer_kernel(xn_ref, u_ref, vt_ref, n_ref, e1_ref, r2_ref, e2_ref, o_ref, a_ref, w_ref):
    @pl.when(pl.program_id(1) == 0)
    def _():
        o_ref[...] = jnp.zeros_like(o_ref)

    a_ref[...] = _gelu_exact(_dot_nt(u_ref[...].astype(BF16), xn_ref[...])).astype(BF16)
    tb = xn_ref.shape[0]

    def per_key(ii, carry):
        row0 = pl.multiple_of(ii * N_KEYS, N_KEYS)
        n_rows = [n_ref[h, pl.ds(ii, 1), :] for h in range(PEER_HEADS)]
        e1_rows = [e1_ref[h, pl.ds(ii, 1), :] for h in range(PEER_HEADS)]
        for lt in range(tb // BLK):
            lanes = slice(lt * BLK, (lt + 1) * BLK)
            wide = lambda row: jnp.broadcast_to(row[:, lanes], (PEER_JCH, BLK)).astype(BF16)
            n_b = [wide(r) for r in n_rows]
            e1_b = [wide(r) for r in e1_rows]
            for j0 in range(0, N_KEYS, PEER_JCH):
                js = slice(j0 // 2, (j0 + PEER_JCH) // 2)
                acc = None
                for h in range(PEER_HEADS):
                    keep = pltpu.bitcast(r2_ref[h, lt, js, :], BF16) < n_b[h]
                    gate = jnp.where(keep, pltpu.bitcast(e2_ref[h, lt, js, :], BF16), 0.0) * e1_b[h]
                    acc = gate if acc is None else acc + gate
                rows = pl.ds(row0 + j0, PEER_JCH)
                w_ref[rows, lanes] = acc * a_ref[rows, lanes]
        return carry

    lax.fori_loop(0, PEER_EB // N_KEYS, per_key, 0)
    o_ref[...] += _dot(vt_ref[...], w_ref[...])


def _transpose_bf16_kernel(x_ref, o_ref):
    o_ref[...] = x_ref[...].T.astype(BF16)


def _transpose_bf16(x, rows=512):
    r, c = x.shape
    return pl.pallas_call(
        _transpose_bf16_kernel,
        grid=(r // rows,),
        in_specs=[pl.BlockSpec((rows, c), lambda i: (i, 0))],
        out_specs=pl.BlockSpec((c, rows), lambda i: (0, i)),
        out_shape=jax.ShapeDtypeStruct((c, r), BF16),
        compiler_params=_params("parallel"),
        name="transpose_v",
    )(x)


def _peer(xn, u, vt_b, n_sel, e1, r2, e2):
    t = xn.shape[0]
    tb = min(t, PEER_TB)
    assert tb == PEER_TB
    ni = PEER_EB // N_KEYS
    small = pl.BlockSpec((PEER_HEADS, ni, tb), lambda i, e: (0, e, i))
    big = pl.BlockSpec((PEER_HEADS, tb // BLK, N_KEYS // 2, BLK), lambda i, e: (0, i, 0, 0))
    return pl.pallas_call(
        _peer_kernel,
        grid=(t // tb, N_EXPERTS // PEER_EB),
        in_specs=[
            pl.BlockSpec((tb, D_MODEL), lambda i, e: (i, 0)),
            pl.BlockSpec((PEER_EB, D_MODEL), lambda i, e: (e, 0)),
            pl.BlockSpec((D_MODEL, PEER_EB), lambda i, e: (0, e)),
            small, small, big, big,
        ],
        out_specs=pl.BlockSpec((D_MODEL, tb), lambda i, e: (0, i)),
        out_shape=jax.ShapeDtypeStruct((D_MODEL, t), F32),
        scratch_shapes=[pltpu.VMEM((PEER_EB, tb), BF16), pltpu.VMEM((PEER_EB, tb), BF16)],
        compiler_params=_params("parallel", "arbitrary"),
        name="peer_experts",
    )(xn, u, vt_b, n_sel, e1, r2, e2)


def _final_kernel(h_ref, pt_ref, g_ref, o_ref):
    h = h_ref[...] + pt_ref[...].T
    ms = jnp.mean(h * h, axis=-1, keepdims=True)
    o_ref[...] = h * lax.rsqrt(ms + EPS) * g_ref[...]


def _final(h1, peer_t, gain):
    t = h1.shape[0]
    tm = min(t, 512)
    rows = pl.BlockSpec((tm, D_MODEL), lambda i: (i, 0))
    return pl.pallas_call(
        _final_kernel,
        grid=(t // tm,),
        in_specs=[rows, pl.BlockSpec((D_MODEL, tm), lambda i: (0, i)),
                  pl.BlockSpec((1, D_MODEL), lambda i: (0, 0))],
        out_specs=rows,
        out_shape=jax.ShapeDtypeStruct((t, D_MODEL), F32),
        compiler_params=_params("parallel"),
        name="final_norm",
    )(h1, peer_t, gain)


def _mixer(x2d, batch, seq, meta_tokens, rel_bias, ln_mix, w_in, sinks, conv_w, conv_b, dt_bias,
           a_log, d_skip, attn_norm_w, ssm_norm_w):
    nb = seq // BLK
    gain = ln_mix.reshape(1, D_MODEL)
    meta_pad = jnp.concatenate([jnp.zeros((BLK - N_META, D_MODEL), F32), meta_tokens.astype(F32)], axis=0)
    proj = _inproj(x2d, gain, w_in)
    proj_meta = _inproj(meta_pad, gain, w_in)

    bucket, valid = _band_tables(nb)
    tab = rel_bias.astype(F32)
    bias = jnp.full((2, ATTN_HEADS) + bucket.shape[1:], NEG, F32)
    for b in range(REL_BUCKETS):
        bias = jnp.where((valid & (bucket == b))[:, None], tab[b][None, :, None, None], bias)
    ya = _attention(proj, proj_meta, sinks.astype(F32), bias, attn_norm_w.reshape(1, D_ATTN),
                    batch, nb, F32)
    ys = _ssd(proj, proj_meta, conv_w, conv_b, dt_bias, a_log, d_skip,
                   ssm_norm_w.reshape(1, SSM_D_INNER), batch, nb, F32)
    return ya, ys


def kernel(x, meta_tokens, rel_bias, ln_mix, w_in, attn_sinks, conv_w, conv_b, dt_bias, a_log, d_skip,
           attn_norm_w, ssm_norm_w, w_out, ln_ffn, peer_wq, peer_keys, peer_u, peer_v, ln_final):
    batch, seq, _ = x.shape
    x2d = x.reshape(batch * seq, D_MODEL)
    ya, ys = _mixer(x2d, batch, seq, meta_tokens, rel_bias, ln_mix[0], w_in[0], attn_sinks[0],
                         conv_w[0], conv_b[0], dt_bias[0], a_log[0], d_skip[0], attn_norm_w[0],
                         ssm_norm_w[0])
    h1, xn = _outproj(ya, ys, x2d, w_out[0], ln_ffn[0].reshape(1, D_MODEL))
    n_sel, e1, r2, e2 = _route(xn, peer_wq[0], peer_keys[0].astype(BF16))
    peer_t = _peer(xn, peer_u[0], _transpose_bf16(peer_v[0]), n_sel, e1, r2, e2)
    out = _final(h1, peer_t, ln_final.reshape(1, D_MODEL))
    return out.reshape(batch, seq, D_MODEL)
```

```python
import functools

import jax
import jax.numpy as jnp
import numpy as np
from jax import lax
from jax.experimental import pallas as pl
from jax.experimental.pallas import tpu as pltpu

F32 = jnp.float32
BF16 = jnp.bfloat16

D_MODEL = 2048
N_META = 16
HEAD_DIM = 64
D_ATTN = 1024
ATTN_HEADS = 16
ATTN_KV_HEADS = 4
ATTN_GROUP = 4
D_KV = 256
WINDOW = 128
BLK = 128
REL_BUCKETS = 32
REL_MAX_DIST = 128
SSM_D_INNER = 1024
SSM_HEAD_DIM = 64
SSM_HEADS = 16
SSM_GROUPS = 2
SSM_HPG = 8
SSM_STATE = 128
CONV_WIDTH = 4
D_XBC = 1536
D_BC = 2 * SSM_GROUPS * SSM_STATE
PEER_HEADS = 8
PEER_TOPK = 16
N_KEYS = 128
N_EXPERTS = N_KEYS * N_KEYS
PEER_HALF = 128
EPS = 1e-6
NEG = -1e30

COL_Q = 0
COL_K = 1024
COL_V = 1280
COL_Z = 1536
COL_XS = 2560
COL_BC = 3584
COL_DT = 4096
D_IN = 4112
D_PROJ = 4224
HALF = 512

VMEM_LIMIT = 56 * 1024 * 1024


def _params(*sem, flags=None):
    return pltpu.CompilerParams(dimension_semantics=sem, vmem_limit_bytes=VMEM_LIMIT, flags=flags)


def _dot(a, b):
    return jnp.dot(a, b, preferred_element_type=F32)


def _dot_nt(a, b):
    return lax.dot_general(a, b, (((1,), (1,)), ((), ())), preferred_element_type=F32)


def _split3(x):
    hi = x.astype(BF16)
    r = x - hi.astype(F32)
    mid = r.astype(BF16)
    lo = (r - mid.astype(F32)).astype(BF16)
    return hi, mid, lo


def _dot01_left(m01, x):
    hi, mid, lo = _split3(x)
    return _dot(m01, hi) + _dot(m01, mid) + _dot(m01, lo)


def _dot01_right(x, m01):
    hi, mid, lo = _split3(x)
    return _dot(hi, m01) + _dot(mid, m01) + _dot(lo, m01)


def _silu(x):
    h = 0.5 * x
    return h + h * jnp.tanh(h)


def _softplus(x):
    return jnp.maximum(x, 0.0) + jnp.log1p(jnp.exp(-jnp.abs(x)))


def _gelu_exact(x):
    return 0.5 * x * (1.0 + lax.erf(x * np.float32(np.sqrt(0.5))))


def _inproj_kernel(x_ref, g_ref, w_ref, o_ref):
    x = x_ref[...]
    ms = jnp.mean(x * x, axis=-1, keepdims=True)
    xn = (x * lax.rsqrt(ms + EPS) * g_ref[...]).astype(BF16)
    o_ref[:, :D_IN] = _dot(xn, w_ref[...])
    o_ref[:, D_IN:] = jnp.zeros((x.shape[0], D_PROJ - D_IN), F32)


def _inproj(x2d, gain, w_in):
    m = x2d.shape[0]
    tm = min(m, 512)
    return pl.pallas_call(
        _inproj_kernel,
        grid=(m // tm,),
        in_specs=[
            pl.BlockSpec((tm, D_MODEL), lambda i: (i, 0)),
            pl.BlockSpec((1, D_MODEL), lambda i: (0, 0)),
            pl.BlockSpec((D_MODEL, D_IN), lambda i: (0, 0), pipeline_mode=pl.Buffered(1)),
        ],
        out_specs=pl.BlockSpec((tm, D_PROJ), lambda i: (i, 0)),
        out_shape=jax.ShapeDtypeStruct((m, D_PROJ), F32),
        compiler_params=_params("parallel"),
        name="inproj",
    )(x2d, gain, w_in)


def _t5_bucket(dist):
    n = np.maximum(dist, 0)
    max_exact = REL_BUCKETS // 2
    large = max_exact + (np.log(np.maximum(n, 1) / max_exact) / np.log(REL_MAX_DIST / max_exact)
                         * (REL_BUCKETS - max_exact)).astype(np.int32)
    large = np.minimum(large, REL_BUCKETS - 1)
    return np.where(n < max_exact, n, large).astype(np.int32)


N_BAND = BLK + N_META


def _band_tables(nb):
    r = np.arange(BLK)[:, None]
    q = np.arange(BLK)[None, :]
    m = np.arange(N_META)[:, None]
    buckets, valids = [], []
    for n in range(nb):
        upper = r > q
        d_band = np.where(upper, q - r + BLK, q - r)
        d_meta = N_META + n * BLK + q - m
        assert (d_band[upper] < WINDOW).all() and (d_band >= 0).all() and (d_meta >= 0).all()
        buckets.append(_t5_bucket(np.concatenate([d_band, d_meta], axis=0)))
        valids.append(np.concatenate([~upper | (n > 0), np.ones((N_META, BLK), bool)], axis=0))
    for n in range(2, nb):
        assert (buckets[n] == buckets[1]).all() and (valids[n] == valids[1]).all()
    last = min(1, nb - 1)
    return np.stack([buckets[0], buckets[last]]), np.stack([valids[0], valids[last]])


ATTN_QB = 8


def _attn_kernel(n_qb, sink_ref, q_ref, kp_ref, ko_ref, km_ref, vp_ref, vo_ref, vmt_ref,
                 bias0_ref, bias1_ref, nw_ref, o_ref, yt_ref):
    upper = (lax.broadcasted_iota(jnp.int32, (BLK, BLK), 0)
             > lax.broadcasted_iota(jnp.int32, (BLK, BLK), 1))
    vmt = vmt_ref[...].astype(BF16)
    km = km_ref[...].astype(BF16)
    v_t = [vp_ref[...].T.astype(BF16)]
    k_b = [kp_ref[...].astype(BF16)]
    for s in range(n_qb):
        v_t.append(vo_ref[s * BLK:(s + 1) * BLK, :].T.astype(BF16))
        k_b.append(ko_ref[s * BLK:(s + 1) * BLK, :].astype(BF16))
    for s in range(n_qb):
        bias_ref = bias0_ref if s == 0 else bias1_ref
        qcols = slice(s * BLK, (s + 1) * BLK)
        q = (q_ref[qcols, :] * np.float32(HEAD_DIM ** -0.5)).astype(BF16)
        for j in range(ATTN_KV_HEADS):
            ks = slice(j * HEAD_DIM, (j + 1) * HEAD_DIM)
            heads = [j * ATTN_GROUP + g for g in range(ATTN_GROUP)]
            q4 = jnp.concatenate([q[:, h * HEAD_DIM:(h + 1) * HEAD_DIM] for h in heads], axis=0)
            lp = _dot_nt(k_b[s][:, ks], q4)
            lo = _dot_nt(k_b[s + 1][:, ks], q4)
            lm = _dot_nt(km[:, ks], q4)
            e_prev, e_own, e_meta, inv = [], [], [], []
            for g, h in enumerate(heads):
                cols = slice(g * BLK, (g + 1) * BLK)
                band = jnp.where(upper, lp[:, cols], lo[:, cols]) + bias_ref[0, h, 0:BLK, :]
                meta = lm[:, cols] + bias_ref[0, h, BLK:N_BAND, :]
                sink = sink_ref[h]
                mx = jnp.maximum(jnp.maximum(jnp.max(band, axis=0, keepdims=True),
                                             jnp.max(meta, axis=0, keepdims=True)), sink)
                eb = jnp.exp(band - mx)
                em = jnp.exp(meta - mx)
                denom = (jnp.sum(eb, axis=0, keepdims=True) + jnp.sum(em, axis=0, keepdims=True)
                         + jnp.exp(sink - mx))
                inv.append(1.0 / denom)
                e_prev.append(jnp.where(upper, eb, 0.0).astype(BF16))
                e_own.append(jnp.where(upper, 0.0, eb).astype(BF16))
                e_meta.append(em.astype(BF16))
            cat = lambda parts: jnp.concatenate(parts, axis=1)
            ot = (_dot(v_t[s][ks, :], cat(e_prev)) + _dot(v_t[s + 1][ks, :], cat(e_own))
                  + _dot(vmt[ks, :], cat(e_meta))) * cat(inv)
            for g, h in enumerate(heads):
                yt_ref[h * HEAD_DIM:(h + 1) * HEAD_DIM, qcols] = ot[:, g * BLK:(g + 1) * BLK]
    yt = yt_ref[...]
    ms = jnp.mean(yt * yt, axis=0, keepdims=True)
    o_ref[...] = ((yt * lax.rsqrt(ms + EPS)).T * nw_ref[...]).astype(o_ref.dtype)


def _attention(proj, proj_meta, sinks, bias, norm_w, batch, nb, ymix_dtype):
    rows = batch * nb * BLK
    kcol, vcol = COL_K // D_KV, COL_V // D_KV
    meta_blk = (BLK - N_META) // N_META
    vm_t = proj_meta[BLK - N_META:, COL_V:COL_V + D_KV].T

    n_qb = max(d for d in range(1, ATTN_QB + 1) if nb % d == 0)
    ns = nb // n_qb
    wide = n_qb * BLK

    def cur(col):
        return lambda b, n: (b * ns + n, col)

    def prev(col):
        return lambda b, n: (jnp.maximum((b * ns + n) * n_qb - 1, 0), col)

    bias_spec = lambda pick: pl.BlockSpec((1, ATTN_HEADS, N_BAND, BLK), lambda b, n: (pick(n), 0, 0, 0))
    return pl.pallas_call(
        functools.partial(_attn_kernel, n_qb),
        grid=(batch, ns),
        in_specs=[
            pl.BlockSpec(memory_space=pltpu.SMEM),
            pl.BlockSpec((wide, D_ATTN), cur(0)),
            pl.BlockSpec((BLK, D_KV), prev(kcol)),
            pl.BlockSpec((wide, D_KV), cur(kcol)),
            pl.BlockSpec((N_META, D_KV), lambda b, n: (meta_blk, kcol)),
            pl.BlockSpec((BLK, D_KV), prev(vcol)),
            pl.BlockSpec((wide, D_KV), cur(vcol)),
            pl.BlockSpec((D_KV, N_META), lambda b, n: (0, 0)),
            bias_spec(lambda n: jnp.minimum(n, 1)),
            bias_spec(lambda n: 1),
            pl.BlockSpec((1, D_ATTN), lambda b, n: (0, 0)),
        ],
        out_specs=pl.BlockSpec((wide, D_ATTN), cur(0)),
        out_shape=jax.ShapeDtypeStruct((rows, D_ATTN), ymix_dtype),
        scratch_shapes=[pltpu.VMEM((D_ATTN, wide), F32)],
        compiler_params=_params("parallel", "arbitrary"),
        name="swa_attention",
    )(sinks, proj, proj, proj, proj_meta, proj, proj, vm_t, bias, bias, norm_w)


def _ssd_chunk(xs_raw, bc_raw, tail_xs, tail_bc, dt_raw, cw, cb, dtb, alog, row_mask, state):
    def conv(blk, tail, w, b):
        ext = jnp.concatenate([tail, blk], axis=0)
        acc = b + w[CONV_WIDTH - 1:CONV_WIDTH, :] * blk
        for back in range(1, CONV_WIDTH):
            shifted = pltpu.roll(ext, back, 0)[8:, :]
            acc = acc + w[CONV_WIDTH - 1 - back:CONV_WIDTH - back, :] * shifted
        return _silu(acc)

    xs = conv(xs_raw, tail_xs, cw[:, :SSM_D_INNER], cb[:, :SSM_D_INNER])
    bc = conv(bc_raw, tail_bc, cw[:, SSM_D_INNER:], cb[:, SSM_D_INNER:])
    dt = _softplus(dt_raw + dtb)
    if row_mask is not None:
        xs = jnp.where(row_mask, xs, 0.0)
        bc = jnp.where(row_mask, bc, 0.0)
        dt = jnp.where(row_mask, dt, 0.0)
    a_neg = -jnp.exp(alog)
    d_a = dt * a_neg

    r = lax.broadcasted_iota(jnp.int32, (BLK, BLK), 0)
    c = lax.broadcasted_iota(jnp.int32, (BLK, BLK), 1)
    tri = r >= c
    cs = _dot01_left(tri.astype(BF16), d_a)
    cs_t = cs.T
    hh = lax.broadcasted_iota(jnp.int32, (BLK, SSM_D_INNER), 0)
    cc = lax.broadcasted_iota(jnp.int32, (BLK, SSM_D_INNER), 1)
    expand = (cc // SSM_HEAD_DIM == hh).astype(BF16)
    dt_rep = _dot01_right(dt, expand)
    ecs_rep = _dot01_right(jnp.exp(cs), expand)
    dec_rep = _dot01_right(jnp.exp(cs[BLK - 1:BLK, :] - cs), expand)

    xdt = xs * dt_rep
    xdtd = (xdt * dec_rep).astype(BF16)
    xdt_b = xdt.astype(BF16)
    chunk_decay = ecs_rep[BLK - 1:BLK, :]

    y_parts, new_state = [], []
    for g in range(SSM_GROUPS):
        b_g = bc[:, g * SSM_STATE:(g + 1) * SSM_STATE]
        c_g = bc[:, (SSM_GROUPS + g) * SSM_STATE:(SSM_GROUPS + g + 1) * SSM_STATE]
        cols = slice(g * 512, (g + 1) * 512)
        cb_g = _dot_nt(c_g.astype(BF16), b_g.astype(BF16))
        y_off = _dot(c_g.astype(BF16), state[g].astype(BF16)) * ecs_rep[:, cols]
        new_state.append(chunk_decay[:, cols] * state[g] + _dot(b_g.T.astype(BF16), xdtd[:, cols]))
        y_diag = []
        for hp in range(SSM_HPG):
            h = g * SSM_HPG + hp
            seg = cs[:, h:h + 1] - cs_t[h:h + 1, :]
            lmat = jnp.exp(jnp.where(tri, seg, -jnp.inf))
            m = (cb_g * lmat).astype(BF16)
            y_diag.append(_dot(m, xdt_b[:, h * SSM_HEAD_DIM:(h + 1) * SSM_HEAD_DIM]))
        y_parts.append(jnp.concatenate(y_diag, axis=1) + y_off)
    return jnp.concatenate(y_parts, axis=1), xs, new_state


def _halves(lo_ref, hi_ref):
    return jnp.concatenate([lo_ref[...], hi_ref[...]], axis=1)


def _ssd_meta_kernel(xs0_ref, xs1_ref, bc_ref, dt_ref, cw_ref, cb_ref, dtb_ref, alog_ref, st_ref):
    rows = lax.broadcasted_iota(jnp.int32, (BLK, 1), 0)
    zero_state = [jnp.zeros((SSM_STATE, 512), F32) for _ in range(SSM_GROUPS)]
    _, _, st = _ssd_chunk(_halves(xs0_ref, xs1_ref), bc_ref[...], jnp.zeros((8, SSM_D_INNER), F32),
                          jnp.zeros((8, D_BC), F32), dt_ref[...], cw_ref[...], cb_ref[...],
                          dtb_ref[...], alog_ref[...], rows >= BLK - N_META, zero_state)
    for g in range(SSM_GROUPS):
        st_ref[g] = st[g]


SSD_ROWS = 4


def _ssd_kernel(xs0_ref, xs1_ref, bc_ref, z0_ref, z1_ref, dt_ref, txs0_ref, txs1_ref, tbc_ref,
                mxs0_ref, mxs1_ref, mbc_ref, st0_ref,
                cw_ref, cb_ref, dtb_ref, alog_ref, dsk_ref, nw_ref, o_ref, st_ref):
    first = pl.program_id(1) == 0
    n_rows = o_ref.shape[0]

    @pl.when(first)
    def _():
        for r in range(n_rows):
            st_ref[r] = st0_ref[...]

    halves = lambda lo, hi, r: jnp.concatenate([lo[r], hi[r]], axis=1)
    for r in range(n_rows):
        tail_xs = jnp.where(first, _halves(mxs0_ref, mxs1_ref), halves(txs0_ref, txs1_ref, r))
        tail_bc = jnp.where(first, mbc_ref[...], tbc_ref[r])
        state = [st_ref[r, g] for g in range(SSM_GROUPS)]
        y, xs, new_state = _ssd_chunk(halves(xs0_ref, xs1_ref, r), bc_ref[r], tail_xs, tail_bc, dt_ref[r],
                                      cw_ref[...], cb_ref[...], dtb_ref[...], alog_ref[...], None, state)
        for g in range(SSM_GROUPS):
            st_ref[r, g] = new_state[g]
        y = y + xs * dsk_ref[...]
        yg = y * _silu(halves(z0_ref, z1_ref, r))
        outs = []
        for g in range(SSM_GROUPS):
            part = yg[:, g * 512:(g + 1) * 512]
            ms = jnp.mean(part * part, axis=-1, keepdims=True)
            outs.append(part * lax.rsqrt(ms + EPS))
        o_ref[r] = (jnp.concatenate(outs, axis=1) * nw_ref[...]).astype(o_ref.dtype)


def _pad_lanes(v, n=BLK):
    v = v.reshape(1, -1)
    return jnp.pad(v, ((0, 0), (0, n - v.shape[1])))


def _ssd(proj, proj_meta, conv_w, conv_b, dt_bias, a_log, d_skip, norm_w, batch, nc, ymix_dtype):
    rows = batch * nc * BLK
    cb = conv_b.reshape(1, D_XBC)
    dtb, alog = _pad_lanes(dt_bias), _pad_lanes(a_log)
    dsk = jnp.repeat(d_skip, SSM_HEAD_DIM).reshape(1, SSM_D_INNER)
    xs_c, z_c, bc_c, dt_c = COL_XS // HALF, COL_Z // HALF, COL_BC // D_BC, COL_DT // BLK
    full = lambda shape: pl.BlockSpec(shape, lambda *_: (0,) * len(shape))

    state0 = pl.pallas_call(
        _ssd_meta_kernel,
        grid=(1,),
        in_specs=[
            pl.BlockSpec((BLK, HALF), lambda i: (0, xs_c)),
            pl.BlockSpec((BLK, HALF), lambda i: (0, xs_c + 1)),
            pl.BlockSpec((BLK, D_BC), lambda i: (0, bc_c)),
            pl.BlockSpec((BLK, BLK), lambda i: (0, dt_c)),
            full((CONV_WIDTH, D_XBC)), full((1, D_XBC)), full((1, BLK)), full((1, BLK)),
        ],
        out_specs=full((SSM_GROUPS, SSM_STATE, 512)),
        out_shape=jax.ShapeDtypeStruct((SSM_GROUPS, SSM_STATE, 512), F32),
        compiler_params=_params("arbitrary"),
        name="ssd_meta_state",
    )(proj_meta, proj_meta, proj_meta, proj_meta, conv_w, cb, dtb, alog)

    n_rows = max(d for d in range(1, SSD_ROWS + 1) if batch % d == 0)
    proj3 = proj.reshape(batch, nc * BLK, D_PROJ)

    def cur(col, width):
        return pl.BlockSpec((n_rows, BLK, width), lambda b, c: (b, c, col))

    def tail(col, width):
        return pl.BlockSpec((n_rows, 8, width), lambda b, c: (b, jnp.maximum(c * (BLK // 8) - 1, 0), col))

    def meta_tail(col, width):
        return pl.BlockSpec((8, width), lambda b, c: (BLK // 8 - 1, col))

    out = pl.pallas_call(
        _ssd_kernel,
        grid=(batch // n_rows, nc),
        in_specs=[
            cur(xs_c, HALF), cur(xs_c + 1, HALF), cur(bc_c, D_BC), cur(z_c, HALF), cur(z_c + 1, HALF),
            cur(dt_c, BLK),
            tail(xs_c, HALF), tail(xs_c + 1, HALF), tail(bc_c, D_BC),
            meta_tail(xs_c, HALF), meta_tail(xs_c + 1, HALF), meta_tail(bc_c, D_BC),
            full((SSM_GROUPS, SSM_STATE, 512)),
            full((CONV_WIDTH, D_XBC)), full((1, D_XBC)), full((1, BLK)), full((1, BLK)),
            full((1, SSM_D_INNER)), full((1, SSM_D_INNER)),
        ],
        out_specs=pl.BlockSpec((n_rows, BLK, SSM_D_INNER), lambda b, c: (b, c, 0)),
        out_shape=jax.ShapeDtypeStruct((batch, nc * BLK, SSM_D_INNER), ymix_dtype),
        scratch_shapes=[pltpu.VMEM((n_rows, SSM_GROUPS, SSM_STATE, 512), F32)],
        compiler_params=_params("parallel", "arbitrary"),
        name="ssd_mixer",
    )(proj3, proj3, proj3, proj3, proj3, proj3, proj3, proj3, proj3, proj_meta, proj_meta, proj_meta,
      state0, conv_w, cb, dtb, alog, dsk, norm_w)
    return out.reshape(rows, SSM_D_INNER)


def _outproj_kernel(ya_ref, ys_ref, x_ref, w_ref, g_ref, h_ref, xn_ref):
    y = jnp.concatenate([ya_ref[...], ys_ref[...]], axis=1).astype(BF16)
    h = x_ref[...] + _dot(y, w_ref[...].astype(BF16))
    h_ref[...] = h
    ms = jnp.mean(h * h, axis=-1, keepdims=True)
    xn_ref[...] = (h * lax.rsqrt(ms + EPS) * g_ref[...]).astype(BF16)


def _outproj(ya, ys, x2d, w_out, gain):
    m = x2d.shape[0]
    tm = min(m, 512)
    return pl.pallas_call(
        _outproj_kernel,
        grid=(m // tm,),
        in_specs=[
            pl.BlockSpec((tm, D_ATTN), lambda i: (i, 0)),
            pl.BlockSpec((tm, SSM_D_INNER), lambda i: (i, 0)),
            pl.BlockSpec((tm, D_MODEL), lambda i: (i, 0)),
            pl.BlockSpec((D_MODEL, D_MODEL), lambda i: (0, 0), pipeline_mode=pl.Buffered(1)),
            pl.BlockSpec((1, D_MODEL), lambda i: (0, 0)),
        ],
        out_specs=[pl.BlockSpec((tm, D_MODEL), lambda i: (i, 0)),
                   pl.BlockSpec((tm, D_MODEL), lambda i: (i, 0))],
        out_shape=[jax.ShapeDtypeStruct((m, D_MODEL), F32),
                   jax.ShapeDtypeStruct((m, D_MODEL), BF16)],
        compiler_params=_params("parallel"),
        name="outproj",
    )(ya, ys, x2d, w_out, gain)


def _oddeven_sort_pairs(n):
    pairs = []
    p = 1
    while p < n:
        k = p
        while k >= 1:
            for j in range(k % p, n - k, 2 * k):
                for i in range(min(k, n - j - k)):
                    if (i + j) // (2 * p) == (i + j + k) // (2 * p):
                        pairs.append((i + j, i + j + k))
            k //= 2
        p *= 2
    return pairs


_SORT16 = _oddeven_sort_pairs(PEER_TOPK)


def _top16_tile(x):
    x = list(x)
    for i, j in _SORT16:
        x[i], x[j] = jnp.maximum(x[i], x[j]), jnp.minimum(x[i], x[j])
    for shift in (4, 2, 1):
        other = [pltpu.roll(v, shift, 0) for v in x]
        x = [jnp.maximum(x[r], other[PEER_TOPK - 1 - r]) for r in range(PEER_TOPK)]
        dist = PEER_TOPK // 2
        while dist >= 1:
            for i in range(PEER_TOPK):
                if i & dist == 0:
                    x[i], x[i + dist] = jnp.maximum(x[i], x[i + dist]), jnp.minimum(x[i], x[i + dist])
            dist //= 2
    return x


def _route_kernel(xn_ref, wq_ref, keys_ref, n_ref, e1_ref, r2_ref, e2_ref, v_ref):
    q = _dot(xn_ref[...], wq_ref[...].astype(BF16)).astype(BF16)
    n_tiles = xn_ref.shape[0] // BLK
    for h in range(PEER_HEADS):
        s, tops = [], []
        for c in range(2):
            qs = q[:, (2 * h + c) * PEER_HALF:(2 * h + c + 1) * PEER_HALF]
            sc = _dot_nt(keys_ref[h, c], qs)
            s.append(sc)
            for lt in range(n_tiles):
                lanes = slice(lt * BLK, (lt + 1) * BLK)
                keys8 = [sc[8 * r:8 * r + 8, lanes] for r in range(N_KEYS // 8)]
                best = _top16_tile(keys8)
                for r in range(PEER_TOPK):
                    v_ref[c, r:r + 1, lanes] = best[r][0:1, :]
                if c == 1:
                    ranks = []
                    for x in keys8:
                        rank = jnp.full_like(x, float(PEER_TOPK))
                        for a in range(PEER_TOPK):
                            rank = jnp.where(x == best[a], float(a), rank)
                        ranks.append(rank)
                    r2 = jnp.concatenate(ranks, axis=0).astype(BF16)
                    e2 = jnp.exp(sc[:, lanes] - best[0][0:1, :]).astype(BF16)
                    r2_ref[h, lt] = pltpu.bitcast(r2, jnp.uint32)
                    e2_ref[h, lt] = pltpu.bitcast(e2, jnp.uint32)
            tops.append(v_ref[c])
        v1, v2 = tops
        blocks = [v1[0:1] + v2, v1[1:2] + v2[0:8]]
        blocks += [v1[a:a + 1] + v2[0:8] for a in range(2, 8)]
        blocks.append(v1[8:16] + v2[0:1])
        cand = jnp.concatenate(blocks, axis=0)
        top = v1[0:1] + v2[0:1]
        rem = cand
        for _ in range(PEER_TOPK - 1):
            m = jnp.max(rem, axis=0, keepdims=True)
            rem = jnp.where(rem == m, -jnp.inf, rem)
        tau = jnp.max(rem, axis=0, keepdims=True)
        z = jnp.sum(jnp.where(cand >= tau, jnp.exp(cand - top), 0.0), axis=0, keepdims=True)
        cnt = [jnp.sum(jnp.where(blocks[a] >= tau, 1.0, 0.0), axis=0, keepdims=True) for a in range(8)]
        cnt_hi = jnp.where(blocks[8] >= tau, 1.0, 0.0)
        n_sel = jnp.zeros_like(s[0])
        for a in range(PEER_TOPK):
            n_a = cnt[a] if a < 8 else cnt_hi[a - 8:a - 7]
            n_sel = jnp.where(s[0] == v1[a:a + 1], n_a, n_sel)
        n_ref[h] = n_sel
        e1_ref[h] = jnp.exp(s[0] - v1[0:1]) / z


def _route(xn, wq, keys):
    t = xn.shape[0]
    tm = min(t, 256)
    big = pl.BlockSpec((PEER_HEADS, N_KEYS, tm), lambda i: (0, 0, i))
    big_shape = jax.ShapeDtypeStruct((PEER_HEADS, N_KEYS, t), F32)
    tiled = pl.BlockSpec((PEER_HEADS, tm // BLK, N_KEYS // 2, BLK), lambda i: (0, i, 0, 0))
    tiled_shape = jax.ShapeDtypeStruct((PEER_HEADS, t // BLK, N_KEYS // 2, BLK), jnp.uint32)
    return pl.pallas_call(
        _route_kernel,
        grid=(t // tm,),
        in_specs=[
            pl.BlockSpec((tm, D_MODEL), lambda i: (i, 0)),
            pl.BlockSpec((D_MODEL, D_MODEL), lambda i: (0, 0), pipeline_mode=pl.Buffered(1)),
            pl.BlockSpec((PEER_HEADS, 2, N_KEYS, PEER_HALF), lambda i: (0, 0, 0, 0)),
        ],
        out_specs=[big, big, tiled, tiled],
        out_shape=[big_shape, big_shape, tiled_shape, tiled_shape],
        scratch_shapes=[pltpu.VMEM((2, PEER_TOPK, tm), F32)],
        compiler_params=_params("parallel"),
        name="peer_route",
    )(xn, wq, keys)


PEER_TB = 512
PEER_EB = 1024
PEER_JCH = 128


def _peer_kernel(xn_ref, u_ref, vt_ref, n_ref, e1_ref, r2_ref, e2_ref, o_ref, a_ref, w_ref):
    @pl.when(pl.program_id(1) == 0)
    def _():
        o_ref[...] = jnp.zeros_like(o_ref)

    a_ref[...] = _gelu_exact(_dot_nt(u_ref[...].astype(BF16), xn_ref[...])).astype(BF16)
    tb = xn_ref.shape[0]

    def per_key(ii, carry):
        row0 = pl.multiple_of(ii * N_KEYS, N_KEYS)
        n_rows = [n_ref[h, pl.ds(ii, 1), :] for h in range(PEER_HEADS)]
        e1_rows = [e1_ref[h, pl.ds(ii, 1), :] for h in range(PEER_HEADS)]
        for lt in range(tb // BLK):
            lanes = slice(lt * BLK, (lt + 1) * BLK)
            wide = lambda row: jnp.broadcast_to(row[:, lanes], (PEER_JCH, BLK)).astype(BF16)
            n_b = [wide(r) for r in n_rows]
            e1_b = [wide(r) for r in e1_rows]
            for j0 in range(0, N_KEYS, PEER_JCH):
                js = slice(j0 // 2, (j0 + PEER_JCH) // 2)
                acc = None
                for h in range(PEER_HEADS):
                    keep = pltpu.bitcast(r2_ref[h, lt, js, :], BF16) < n_b[h]
                    gate = jnp.where(keep, pltpu.bitcast(e2_ref[h, lt, js, :], BF16), 0.0) * e1_b[h]
                    acc = gate if acc is None else acc + gate
                rows = pl.ds(row0 + j0, PEER_JCH)
                w_ref[rows, lanes] = acc * a_ref[rows, lanes]
        return carry

    lax.fori_loop(0, PEER_EB // N_KEYS, per_key, 0)
    o_ref[...] += _dot(vt_ref[...], w_ref[...])


def _transpose_bf16_kernel(x_ref, o_ref):
    o_ref[...] = x_ref[...].T.astype(BF16)


def _transpose_bf16(x, rows=512):
    r, c = x.shape
    return pl.pallas_call(
        _transpose_bf16_kernel,
        grid=(r // rows,),
        in_specs=[pl.BlockSpec((rows, c), lambda i: (i, 0))],
        out_specs=pl.BlockSpec((c, rows), lambda i: (0, i)),
        out_shape=jax.ShapeDtypeStruct((c, r), BF16),
        compiler_params=_params("parallel"),
        name="transpose_v",
    )(x)


def _peer(xn, u, vt_b, n_sel, e1, r2, e2):
    t = xn.shape[0]
    tb = min(t, PEER_TB)
    assert tb == PEER_TB
    ni = PEER_EB // N_KEYS
    small = pl.BlockSpec((PEER_HEADS, ni, tb), lambda i, e: (0, e, i))
    big = pl.BlockSpec((PEER_HEADS, tb // BLK, N_KEYS // 2, BLK), lambda i, e: (0, i, 0, 0))
    return pl.pallas_call(
        _peer_kernel,
        grid=(t // tb, N_EXPERTS // PEER_EB),
        in_specs=[
            pl.BlockSpec((tb, D_MODEL), lambda i, e: (i, 0)),
            pl.BlockSpec((PEER_EB, D_MODEL), lambda i, e: (e, 0)),
            pl.BlockSpec((D_MODEL, PEER_EB), lambda i, e: (0, e)),
            small, small, big, big,
        ],
        out_specs=pl.BlockSpec((D_MODEL, tb), lambda i, e: (0, i)),
        out_shape=jax.ShapeDtypeStruct((D_MODEL, t), F32),
        scratch_shapes=[pltpu.VMEM((PEER_EB, tb), BF16), pltpu.VMEM((PEER_EB, tb), BF16)],
        compiler_params=_params("parallel", "arbitrary"),
        name="peer_experts",
    )(xn, u, vt_b, n_sel, e1, r2, e2)


def _final_kernel(h_ref, pt_ref, g_ref, o_ref):
    h = h_ref[...] + pt_ref[...].T
    ms = jnp.mean(h * h, axis=-1, keepdims=True)
    o_ref[...] = h * lax.rsqrt(ms + EPS) * g_ref[...]


def _final(h1, peer_t, gain):
    t = h1.shape[0]
    tm = min(t, 512)
    rows = pl.BlockSpec((tm, D_MODEL), lambda i: (i, 0))
    return pl.pallas_call(
        _final_kernel,
        grid=(t // tm,),
        in_specs=[rows, pl.BlockSpec((D_MODEL, tm), lambda i: (0, i)),
                  pl.BlockSpec((1, D_MODEL), lambda i: (0, 0))],
        out_specs=rows,
        out_shape=jax.ShapeDtypeStruct((t, D_MODEL), F32),
        compiler_params=_params("parallel"),
        name="final_norm",
    )(h1, peer_t, gain)


def _mixer(x2d, batch, seq, meta_tokens, rel_bias, ln_mix, w_in, sinks, conv_w, conv_b, dt_bias,
           a_log, d_skip, attn_norm_w, ssm_norm_w):
    nb = seq // BLK
    gain = ln_mix.reshape(1, D_MODEL)
    meta_pad = jnp.concatenate([jnp.zeros((BLK - N_META, D_MODEL), F32), meta_tokens.astype(F32)], axis=0)
    w_b = w_in.astype(BF16)
    proj = _inproj(x2d, gain, w_b)
    proj_meta = _inproj(meta_pad, gain, w_b)

    bucket, valid = _band_tables(nb)
    tab = rel_bias.astype(F32)
    bias = jnp.full((2, ATTN_HEADS) + bucket.shape[1:], NEG, F32)
    for b in range(REL_BUCKETS):
        bias = jnp.where((valid & (bucket == b))[:, None], tab[b][None, :, None, None], bias)
    ya = _attention(proj, proj_meta, sinks.astype(F32), bias, attn_norm_w.reshape(1, D_ATTN),
                    batch, nb, F32)
    ys = _ssd(proj, proj_meta, conv_w, conv_b, dt_bias, a_log, d_skip,
                   ssm_norm_w.reshape(1, SSM_D_INNER), batch, nb, F32)
    return ya, ys


def kernel(x, meta_tokens, rel_bias, ln_mix, w_in, attn_sinks, conv_w, conv_b, dt_bias, a_log, d_skip,
           attn_norm_w, ssm_norm_w, w_out, ln_ffn, peer_wq, peer_keys, peer_u, peer_v, ln_final):
    batch, seq, _ = x.shape
    x2d = x.reshape(batch * seq, D_MODEL)
    ya, ys = _mixer(x2d, batch, seq, meta_tokens, rel_bias, ln_mix[0], w_in[0], attn_sinks[0],
                         conv_w[0], conv_b[0], dt_bias[0], a_log[0], d_skip[0], attn_norm_w[0],
                         ssm_norm_w[0])
    h1, xn = _outproj(ya, ys, x2d, w_out[0], ln_ffn[0].reshape(1, D_MODEL))
    n_sel, e1, r2, e2 = _route(xn, peer_wq[0], peer_keys[0].astype(BF16))
    peer_t = _peer(xn, peer_u[0], _transpose_bf16(peer_v[0]), n_sel, e1, r2, e2)
    out = _final(h1, peer_t, ln_final.reshape(1, D_MODEL))
    return out.reshape(batch, seq, D_MODEL)
```

```python
import functools

import jax
import jax.numpy as jnp
import numpy as np
from jax import lax
from jax.experimental import pallas as pl
from jax.experimental.pallas import tpu as pltpu

F32 = jnp.float32
BF16 = jnp.bfloat16

D_MODEL = 2048
N_META = 16
HEAD_DIM = 64
D_ATTN = 1024
ATTN_HEADS = 16
ATTN_KV_HEADS = 4
ATTN_GROUP = 4
D_KV = 256
WINDOW = 128
BLK = 128
REL_BUCKETS = 32
REL_MAX_DIST = 128
SSM_D_INNER = 1024
SSM_HEAD_DIM = 64
SSM_HEADS = 16
SSM_GROUPS = 2
SSM_HPG = 8
SSM_STATE = 128
CONV_WIDTH = 4
D_XBC = 1536
D_BC = 2 * SSM_GROUPS * SSM_STATE
PEER_HEADS = 8
PEER_TOPK = 16
N_KEYS = 128
N_EXPERTS = N_KEYS * N_KEYS
PEER_HALF = 128
EPS = 1e-6
NEG = -1e30

COL_Q = 0
COL_K = 1024
COL_V = 1280
COL_Z = 1536
COL_XS = 2560
COL_BC = 3584
COL_DT = 4096
D_IN = 4112
D_PROJ = 4224
HALF = 512

VMEM_LIMIT = 56 * 1024 * 1024


def _params(*sem, flags=None):
    return pltpu.CompilerParams(dimension_semantics=sem, vmem_limit_bytes=VMEM_LIMIT, flags=flags)


def _dot(a, b):
    return jnp.dot(a, b, preferred_element_type=F32)


def _dot_nt(a, b):
    return lax.dot_general(a, b, (((1,), (1,)), ((), ())), preferred_element_type=F32)


def _split3(x):
    hi = x.astype(BF16)
    r = x - hi.astype(F32)
    mid = r.astype(BF16)
    lo = (r - mid.astype(F32)).astype(BF16)
    return hi, mid, lo


def _dot01_left(m01, x):
    hi, mid, lo = _split3(x)
    return _dot(m01, hi) + _dot(m01, mid) + _dot(m01, lo)


def _dot01_right(x, m01):
    hi, mid, lo = _split3(x)
    return _dot(hi, m01) + _dot(mid, m01) + _dot(lo, m01)


def _silu(x):
    h = 0.5 * x
    return h + h * jnp.tanh(h)


def _softplus(x):
    return jnp.maximum(x, 0.0) + jnp.log1p(jnp.exp(-jnp.abs(x)))


def _gelu_exact(x):
    return 0.5 * x * (1.0 + lax.erf(x * np.float32(np.sqrt(0.5))))


def _inproj_kernel(x_ref, g_ref, w_ref, o_ref):
    x = x_ref[...]
    ms = jnp.mean(x * x, axis=-1, keepdims=True)
    xn = (x * lax.rsqrt(ms + EPS) * g_ref[...]).astype(BF16)
    o_ref[:, :D_IN] = _dot_nt(xn, w_ref[...])
    o_ref[:, D_IN:] = jnp.zeros((x.shape[0], D_PROJ - D_IN), F32)


def _inproj(x2d, gain, w_t):
    m = x2d.shape[0]
    tm = min(m, 512)
    return pl.pallas_call(
        _inproj_kernel,
        grid=(m // tm,),
        in_specs=[
            pl.BlockSpec((tm, D_MODEL), lambda i: (i, 0)),
            pl.BlockSpec((1, D_MODEL), lambda i: (0, 0)),
            pl.BlockSpec((D_IN, D_MODEL), lambda i: (0, 0), pipeline_mode=pl.Buffered(1)),
        ],
        out_specs=pl.BlockSpec((tm, D_PROJ), lambda i: (i, 0)),
        out_shape=jax.ShapeDtypeStruct((m, D_PROJ), F32),
        compiler_params=_params("parallel"),
        name="inproj",
    )(x2d, gain, w_t)


def _t5_bucket(dist):
    n = np.maximum(dist, 0)
    max_exact = REL_BUCKETS // 2
    large = max_exact + (np.log(np.maximum(n, 1) / max_exact) / np.log(REL_MAX_DIST / max_exact)
                         * (REL_BUCKETS - max_exact)).astype(np.int32)
    large = np.minimum(large, REL_BUCKETS - 1)
    return np.where(n < max_exact, n, large).astype(np.int32)


N_BAND = BLK + N_META


def _band_tables(nb):
    r = np.arange(BLK)[:, None]
    q = np.arange(BLK)[None, :]
    m = np.arange(N_META)[:, None]
    buckets, valids = [], []
    for n in range(nb):
        upper = r > q
        d_band = np.where(upper, q - r + BLK, q - r)
        d_meta = N_META + n * BLK + q - m
        assert (d_band[upper] < WINDOW).all() and (d_band >= 0).all() and (d_meta >= 0).all()
        buckets.append(_t5_bucket(np.concatenate([d_band, d_meta], axis=0)))
        valids.append(np.concatenate([~upper | (n > 0), np.ones((N_META, BLK), bool)], axis=0))
    for n in range(2, nb):
        assert (buckets[n] == buckets[1]).all() and (valids[n] == valids[1]).all()
    last = min(1, nb - 1)
    return np.stack([buckets[0], buckets[last]]), np.stack([valids[0], valids[last]])


ATTN_QB = 8


def _attn_kernel(n_qb, sink_ref, q_ref, kp_ref, ko_ref, km_ref, vp_ref, vo_ref, vmt_ref,
                 bias0_ref, bias1_ref, nw_ref, o_ref, yt_ref):
    upper = (lax.broadcasted_iota(jnp.int32, (BLK, BLK), 0)
             > lax.broadcasted_iota(jnp.int32, (BLK, BLK), 1))
    vmt = vmt_ref[...].astype(BF16)
    km = km_ref[...].astype(BF16)
    v_t = [vp_ref[...].T.astype(BF16)]
    k_b = [kp_ref[...].astype(BF16)]
    for s in range(n_qb):
        v_t.append(vo_ref[s * BLK:(s + 1) * BLK, :].T.astype(BF16))
        k_b.append(ko_ref[s * BLK:(s + 1) * BLK, :].astype(BF16))
    for s in range(n_qb):
        bias_ref = bias0_ref if s == 0 else bias1_ref
        qcols = slice(s * BLK, (s + 1) * BLK)
        q = (q_ref[qcols, :] * np.float32(HEAD_DIM ** -0.5)).astype(BF16)
        for j in range(ATTN_KV_HEADS):
            ks = slice(j * HEAD_DIM, (j + 1) * HEAD_DIM)
            heads = [j * ATTN_GROUP + g for g in range(ATTN_GROUP)]
            q4 = jnp.concatenate([q[:, h * HEAD_DIM:(h + 1) * HEAD_DIM] for h in heads], axis=0)
            lp = _dot_nt(k_b[s][:, ks], q4)
            lo = _dot_nt(k_b[s + 1][:, ks], q4)
            lm = _dot_nt(km[:, ks], q4)
            e_prev, e_own, e_meta, inv = [], [], [], []
            for g, h in enumerate(heads):
                cols = slice(g * BLK, (g + 1) * BLK)
                band = jnp.where(upper, lp[:, cols], lo[:, cols]) + bias_ref[0, h, 0:BLK, :]
                meta = lm[:, cols] + bias_ref[0, h, BLK:N_BAND, :]
                sink = sink_ref[h]
                mx = jnp.maximum(jnp.maximum(jnp.max(band, axis=0, keepdims=True),
                                             jnp.max(meta, axis=0, keepdims=True)), sink)
                eb = jnp.exp(band - mx)
                em = jnp.exp(meta - mx)
                denom = (jnp.sum(eb, axis=0, keepdims=True) + jnp.sum(em, axis=0, keepdims=True)
                         + jnp.exp(sink - mx))
                inv.append(1.0 / denom)
                e_prev.append(jnp.where(upper, eb, 0.0).astype(BF16))
                e_own.append(jnp.where(upper, 0.0, eb).astype(BF16))
                e_meta.append(em.astype(BF16))
            cat = lambda parts: jnp.concatenate(parts, axis=1)
            ot = (_dot(v_t[s][ks, :], cat(e_prev)) + _dot(v_t[s + 1][ks, :], cat(e_own))
                  + _dot(vmt[ks, :], cat(e_meta))) * cat(inv)
            for g, h in enumerate(heads):
                yt_ref[h * HEAD_DIM:(h + 1) * HEAD_DIM, qcols] = ot[:, g * BLK:(g + 1) * BLK]
    yt = yt_ref[...]
    ms = jnp.mean(yt * yt, axis=0, keepdims=True)
    o_ref[...] = ((yt * lax.rsqrt(ms + EPS)).T * nw_ref[...]).astype(o_ref.dtype)


def _attention(proj, proj_meta, sinks, bias, norm_w, batch, nb, ymix_dtype):
    rows = batch * nb * BLK
    kcol, vcol = COL_K // D_KV, COL_V // D_KV
    meta_blk = (BLK - N_META) // N_META
    vm_t = proj_meta[BLK - N_META:, COL_V:COL_V + D_KV].T

    n_qb = max(d for d in range(1, ATTN_QB + 1) if nb % d == 0)
    ns = nb // n_qb
    wide = n_qb * BLK

    def cur(col):
        return lambda b, n: (b * ns + n, col)

    def prev(col):
        return lambda b, n: (jnp.maximum((b * ns + n) * n_qb - 1, 0), col)

    bias_spec = lambda pick: pl.BlockSpec((1, ATTN_HEADS, N_BAND, BLK), lambda b, n: (pick(n), 0, 0, 0))
    return pl.pallas_call(
        functools.partial(_attn_kernel, n_qb),
        grid=(batch, ns),
        in_specs=[
            pl.BlockSpec(memory_space=pltpu.SMEM),
            pl.BlockSpec((wide, D_ATTN), cur(0)),
            pl.BlockSpec((BLK, D_KV), prev(kcol)),
            pl.BlockSpec((wide, D_KV), cur(kcol)),
            pl.BlockSpec((N_META, D_KV), lambda b, n: (meta_blk, kcol)),
            pl.BlockSpec((BLK, D_KV), prev(vcol)),
            pl.BlockSpec((wide, D_KV), cur(vcol)),
            pl.BlockSpec((D_KV, N_META), lambda b, n: (0, 0)),
            bias_spec(lambda n: jnp.minimum(n, 1)),
            bias_spec(lambda n: 1),
            pl.BlockSpec((1, D_ATTN), lambda b, n: (0, 0)),
        ],
        out_specs=pl.BlockSpec((wide, D_ATTN), cur(0)),
        out_shape=jax.ShapeDtypeStruct((rows, D_ATTN), ymix_dtype),
        scratch_shapes=[pltpu.VMEM((D_ATTN, wide), F32)],
        compiler_params=_params("parallel", "arbitrary"),
        name="swa_attention",
    )(sinks, proj, proj, proj, proj_meta, proj, proj, vm_t, bias, bias, norm_w)


def _ssd_chunk(xs_raw, bc_raw, tail_xs, tail_bc, dt_raw, cw, cb, dtb, alog, row_mask, state):
    def conv(blk, tail, w, b):
        ext = jnp.concatenate([tail, blk], axis=0)
        acc = b + w[CONV_WIDTH - 1:CONV_WIDTH, :] * blk
        for back in range(1, CONV_WIDTH):
            shifted = pltpu.roll(ext, back, 0)[8:, :]
            acc = acc + w[CONV_WIDTH - 1 - back:CONV_WIDTH - back, :] * shifted
        return _silu(acc)

    xs = conv(xs_raw, tail_xs, cw[:, :SSM_D_INNER], cb[:, :SSM_D_INNER])
    bc = conv(bc_raw, tail_bc, cw[:, SSM_D_INNER:], cb[:, SSM_D_INNER:])
    dt = _softplus(dt_raw + dtb)
    if row_mask is not None:
        xs = jnp.where(row_mask, xs, 0.0)
        bc = jnp.where(row_mask, bc, 0.0)
        dt = jnp.where(row_mask, dt, 0.0)
    a_neg = -jnp.exp(alog)
    d_a = dt * a_neg

    r = lax.broadcasted_iota(jnp.int32, (BLK, BLK), 0)
    c = lax.broadcasted_iota(jnp.int32, (BLK, BLK), 1)
    tri = r >= c
    cs = _dot01_left(tri.astype(BF16), d_a)
    cs_t = cs.T
    hh = lax.broadcasted_iota(jnp.int32, (BLK, SSM_D_INNER), 0)
    cc = lax.broadcasted_iota(jnp.int32, (BLK, SSM_D_INNER), 1)
    expand = (cc // SSM_HEAD_DIM == hh).astype(BF16)
    dt_rep = _dot01_right(dt, expand)
    ecs_rep = _dot01_right(jnp.exp(cs), expand)
    dec_rep = _dot01_right(jnp.exp(cs[BLK - 1:BLK, :] - cs), expand)

    xdt = xs * dt_rep
    xdtd = (xdt * dec_rep).astype(BF16)
    xdt_b = xdt.astype(BF16)
    chunk_decay = ecs_rep[BLK - 1:BLK, :]

    y_parts, new_state = [], []
    for g in range(SSM_GROUPS):
        b_g = bc[:, g * SSM_STATE:(g + 1) * SSM_STATE]
        c_g = bc[:, (SSM_GROUPS + g) * SSM_STATE:(SSM_GROUPS + g + 1) * SSM_STATE]
        cols = slice(g * 512, (g + 1) * 512)
        cb_g = _dot_nt(c_g.astype(BF16), b_g.astype(BF16))
        y_off = _dot(c_g.astype(BF16), state[g].astype(BF16)) * ecs_rep[:, cols]
        new_state.append(chunk_decay[:, cols] * state[g] + _dot(b_g.T.astype(BF16), xdtd[:, cols]))
        y_diag = []
        for hp in range(SSM_HPG):
            h = g * SSM_HPG + hp
            seg = cs[:, h:h + 1] - cs_t[h:h + 1, :]
            lmat = jnp.exp(jnp.where(tri, seg, -jnp.inf))
            m = (cb_g * lmat).astype(BF16)
            y_diag.append(_dot(m, xdt_b[:, h * SSM_HEAD_DIM:(h + 1) * SSM_HEAD_DIM]))
        y_parts.append(jnp.concatenate(y_diag, axis=1) + y_off)
    return jnp.concatenate(y_parts, axis=1), xs, new_state


def _halves(lo_ref, hi_ref):
    return jnp.concatenate([lo_ref[...], hi_ref[...]], axis=1)


def _ssd_meta_kernel(xs0_ref, xs1_ref, bc_ref, dt_ref, cw_ref, cb_ref, dtb_ref, alog_ref, st_ref):
    rows = lax.broadcasted_iota(jnp.int32, (BLK, 1), 0)
    zero_state = [jnp.zeros((SSM_STATE, 512), F32) for _ in range(SSM_GROUPS)]
    _, _, st = _ssd_chunk(_halves(xs0_ref, xs1_ref), bc_ref[...], jnp.zeros((8, SSM_D_INNER), F32),
                          jnp.zeros((8, D_BC), F32), dt_ref[...], cw_ref[...], cb_ref[...],
                          dtb_ref[...], alog_ref[...], rows >= BLK - N_META, zero_state)
    for g in range(SSM_GROUPS):
        st_ref[g] = st[g]


SSD_ROWS = 4


def _ssd_kernel(xs0_ref, xs1_ref, bc_ref, z0_ref, z1_ref, dt_ref, txs0_ref, txs1_ref, tbc_ref,
                mxs0_ref, mxs1_ref, mbc_ref, st0_ref,
                cw_ref, cb_ref, dtb_ref, alog_ref, dsk_ref, nw_ref, o_ref, st_ref):
    first = pl.program_id(1) == 0
    n_rows = o_ref.shape[0]

    @pl.when(first)
    def _():
        for r in range(n_rows):
            st_ref[r] = st0_ref[...]

    halves = lambda lo, hi, r: jnp.concatenate([lo[r], hi[r]], axis=1)
    for r in range(n_rows):
        tail_xs = jnp.where(first, _halves(mxs0_ref, mxs1_ref), halves(txs0_ref, txs1_ref, r))
        tail_bc = jnp.where(first, mbc_ref[...], tbc_ref[r])
        state = [st_ref[r, g] for g in range(SSM_GROUPS)]
        y, xs, new_state = _ssd_chunk(halves(xs0_ref, xs1_ref, r), bc_ref[r], tail_xs, tail_bc, dt_ref[r],
                                      cw_ref[...], cb_ref[...], dtb_ref[...], alog_ref[...], None, state)
        for g in range(SSM_GROUPS):
            st_ref[r, g] = new_state[g]
        y = y + xs * dsk_ref[...]
        yg = y * _silu(halves(z0_ref, z1_ref, r))
        outs = []
        for g in range(SSM_GROUPS):
            part = yg[:, g * 512:(g + 1) * 512]
            ms = jnp.mean(part * part, axis=-1, keepdims=True)
            outs.append(part * lax.rsqrt(ms + EPS))
        o_ref[r] = (jnp.concatenate(outs, axis=1) * nw_ref[...]).astype(o_ref.dtype)


def _pad_lanes(v, n=BLK):
    v = v.reshape(1, -1)
    return jnp.pad(v, ((0, 0), (0, n - v.shape[1])))


def _ssd(proj, proj_meta, conv_w, conv_b, dt_bias, a_log, d_skip, norm_w, batch, nc, ymix_dtype):
    rows = batch * nc * BLK
    cb = conv_b.reshape(1, D_XBC)
    dtb, alog = _pad_lanes(dt_bias), _pad_lanes(a_log)
    dsk = jnp.repeat(d_skip, SSM_HEAD_DIM).reshape(1, SSM_D_INNER)
    xs_c, z_c, bc_c, dt_c = COL_XS // HALF, COL_Z // HALF, COL_BC // D_BC, COL_DT // BLK
    full = lambda shape: pl.BlockSpec(shape, lambda *_: (0,) * len(shape))

    state0 = pl.pallas_call(
        _ssd_meta_kernel,
        grid=(1,),
        in_specs=[
            pl.BlockSpec((BLK, HALF), lambda i: (0, xs_c)),
            pl.BlockSpec((BLK, HALF), lambda i: (0, xs_c + 1)),
            pl.BlockSpec((BLK, D_BC), lambda i: (0, bc_c)),
            pl.BlockSpec((BLK, BLK), lambda i: (0, dt_c)),
            full((CONV_WIDTH, D_XBC)), full((1, D_XBC)), full((1, BLK)), full((1, BLK)),
        ],
        out_specs=full((SSM_GROUPS, SSM_STATE, 512)),
        out_shape=jax.ShapeDtypeStruct((SSM_GROUPS, SSM_STATE, 512), F32),
        compiler_params=_params("arbitrary"),
        name="ssd_meta_state",
    )(proj_meta, proj_meta, proj_meta, proj_meta, conv_w, cb, dtb, alog)

    n_rows = max(d for d in range(1, SSD_ROWS + 1) if batch % d == 0)
    proj3 = proj.reshape(batch, nc * BLK, D_PROJ)

    def cur(col, width):
        return pl.BlockSpec((n_rows, BLK, width), lambda b, c: (b, c, col))

    def tail(col, width):
        return pl.BlockSpec((n_rows, 8, width), lambda b, c: (b, jnp.maximum(c * (BLK // 8) - 1, 0), col))

    def meta_tail(col, width):
        return pl.BlockSpec((8, width), lambda b, c: (BLK // 8 - 1, col))

    out = pl.pallas_call(
        _ssd_kernel,
        grid=(batch // n_rows, nc),
        in_specs=[
            cur(xs_c, HALF), cur(xs_c + 1, HALF), cur(bc_c, D_BC), cur(z_c, HALF), cur(z_c + 1, HALF),
            cur(dt_c, BLK),
            tail(xs_c, HALF), tail(xs_c + 1, HALF), tail(bc_c, D_BC),
            meta_tail(xs_c, HALF), meta_tail(xs_c + 1, HALF), meta_tail(bc_c, D_BC),
            full((SSM_GROUPS, SSM_STATE, 512)),
            full((CONV_WIDTH, D_XBC)), full((1, D_XBC)), full((1, BLK)), full((1, BLK)),
            full((1, SSM_D_INNER)), full((1, SSM_D_INNER)),
        ],
        out_specs=pl.BlockSpec((n_rows, BLK, SSM_D_INNER), lambda b, c: (b, c, 0)),
        out_shape=jax.ShapeDtypeStruct((batch, nc * BLK, SSM_D_INNER), ymix_dtype),
        scratch_shapes=[pltpu.VMEM((n_rows, SSM_GROUPS, SSM_STATE, 512), F32)],
        compiler_params=_params("parallel", "arbitrary"),
        name="ssd_mixer",
    )(proj3, proj3, proj3, proj3, proj3, proj3, proj3, proj3, proj3, proj_meta, proj_meta, proj_meta,
      state0, conv_w, cb, dtb, alog, dsk, norm_w)
    return out.reshape(rows, SSM_D_INNER)


def _outproj_kernel(ya_ref, ys_ref, x_ref, w_ref, g_ref, h_ref, xn_ref):
    y = jnp.concatenate([ya_ref[...], ys_ref[...]], axis=1).astype(BF16)
    h = x_ref[...] + _dot(y, w_ref[...].astype(BF16))
    h_ref[...] = h
    ms = jnp.mean(h * h, axis=-1, keepdims=True)
    xn_ref[...] = (h * lax.rsqrt(ms + EPS) * g_ref[...]).astype(BF16)


def _outproj(ya, ys, x2d, w_out, gain):
    m = x2d.shape[0]
    tm = min(m, 512)
    return pl.pallas_call(
        _outproj_kernel,
        grid=(m // tm,),
        in_specs=[
            pl.BlockSpec((tm, D_ATTN), lambda i: (i, 0)),
            pl.BlockSpec((tm, SSM_D_INNER), lambda i: (i, 0)),
            pl.BlockSpec((tm, D_MODEL), lambda i: (i, 0)),
            pl.BlockSpec((D_MODEL, D_MODEL), lambda i: (0, 0), pipeline_mode=pl.Buffered(1)),
            pl.BlockSpec((1, D_MODEL), lambda i: (0, 0)),
        ],
        out_specs=[pl.BlockSpec((tm, D_MODEL), lambda i: (i, 0)),
                   pl.BlockSpec((tm, D_MODEL), lambda i: (i, 0))],
        out_shape=[jax.ShapeDtypeStruct((m, D_MODEL), F32),
                   jax.ShapeDtypeStruct((m, D_MODEL), BF16)],
        compiler_params=_params("parallel"),
        name="outproj",
    )(ya, ys, x2d, w_out, gain)


def _oddeven_sort_pairs(n):
    pairs = []
    p = 1
    while p < n:
        k = p
        while k >= 1:
            for j in range(k % p, n - k, 2 * k):
                for i in range(min(k, n - j - k)):
                    if (i + j) // (2 * p) == (i + j + k) // (2 * p):
                        pairs.append((i + j, i + j + k))
            k //= 2
        p *= 2
    return pairs


_SORT16 = _oddeven_sort_pairs(PEER_TOPK)


def _top16_tile(x):
    x = list(x)
    for i, j in _SORT16:
        x[i], x[j] = jnp.maximum(x[i], x[j]), jnp.minimum(x[i], x[j])
    for shift in (4, 2, 1):
        other = [pltpu.roll(v, shift, 0) for v in x]
        x = [jnp.maximum(x[r], other[PEER_TOPK - 1 - r]) for r in range(PEER_TOPK)]
        dist = PEER_TOPK // 2
        while dist >= 1:
            for i in range(PEER_TOPK):
                if i & dist == 0:
                    x[i], x[i + dist] = jnp.maximum(x[i], x[i + dist]), jnp.minimum(x[i], x[i + dist])
            dist //= 2
    return x


def _route_kernel(xn_ref, wq_ref, keys_ref, n_ref, e1_ref, r2_ref, e2_ref, v_ref):
    q = _dot(xn_ref[...], wq_ref[...].astype(BF16)).astype(BF16)
    n_tiles = xn_ref.shape[0] // BLK
    for h in range(PEER_HEADS):
        s, tops = [], []
        for c in range(2):
            qs = q[:, (2 * h + c) * PEER_HALF:(2 * h + c + 1) * PEER_HALF]
            sc = _dot_nt(keys_ref[h, c], qs)
            s.append(sc)
            for lt in range(n_tiles):
                lanes = slice(lt * BLK, (lt + 1) * BLK)
                keys8 = [sc[8 * r:8 * r + 8, lanes] for r in range(N_KEYS // 8)]
                best = _top16_tile(keys8)
                for r in range(PEER_TOPK):
                    v_ref[c, r:r + 1, lanes] = best[r][0:1, :]
                if c == 1:
                    ranks = []
                    for x in keys8:
                        rank = jnp.full_like(x, float(PEER_TOPK))
                        for a in range(PEER_TOPK):
                            rank = jnp.where(x == best[a], float(a), rank)
                        ranks.append(rank)
                    r2 = jnp.concatenate(ranks, axis=0).astype(BF16)
                    e2 = jnp.exp(sc[:, lanes] - best[0][0:1, :]).astype(BF16)
                    r2_ref[h, lt] = pltpu.bitcast(r2, jnp.uint32)
                    e2_ref[h, lt] = pltpu.bitcast(e2, jnp.uint32)
            tops.append(v_ref[c])
        v1, v2 = tops
        blocks = [v1[0:1] + v2, v1[1:2] + v2[0:8]]
        blocks += [v1[a:a + 1] + v2[0:8] for a in range(2, 8)]
        blocks.append(v1[8:16] + v2[0:1])
        cand = jnp.concatenate(blocks, axis=0)
        top = v1[0:1] + v2[0:1]
        rem = cand
        for _ in range(PEER_TOPK - 1):
            m = jnp.max(rem, axis=0, keepdims=True)
            rem = jnp.where(rem == m, -jnp.inf, rem)
        tau = jnp.max(rem, axis=0, keepdims=True)
        z = jnp.sum(jnp.where(cand >= tau, jnp.exp(cand - top), 0.0), axis=0, keepdims=True)
        cnt = [jnp.sum(jnp.where(blocks[a] >= tau, 1.0, 0.0), axis=0, keepdims=True) for a in range(8)]
        cnt_hi = jnp.where(blocks[8] >= tau, 1.0, 0.0)
        n_sel = jnp.zeros_like(s[0])
        for a in range(PEER_TOPK):
            n_a = cnt[a] if a < 8 else cnt_hi[a - 8:a - 7]
            n_sel = jnp.where(s[0] == v1[a:a + 1], n_a, n_sel)
        n_ref[h] = n_sel
        e1_ref[h] = jnp.exp(s[0] - v1[0:1]) / z


def _route(xn, wq, keys):
    t = xn.shape[0]
    tm = min(t, 256)
    big = pl.BlockSpec((PEER_HEADS, N_KEYS, tm), lambda i: (0, 0, i))
    big_shape = jax.ShapeDtypeStruct((PEER_HEADS, N_KEYS, t), F32)
    tiled = pl.BlockSpec((PEER_HEADS, tm // BLK, N_KEYS // 2, BLK), lambda i: (0, i, 0, 0))
    tiled_shape = jax.ShapeDtypeStruct((PEER_HEADS, t // BLK, N_KEYS // 2, BLK), jnp.uint32)
    return pl.pallas_call(
        _route_kernel,
        grid=(t // tm,),
        in_specs=[
            pl.BlockSpec((tm, D_MODEL), lambda i: (i, 0)),
            pl.BlockSpec((D_MODEL, D_MODEL), lambda i: (0, 0), pipeline_mode=pl.Buffered(1)),
            pl.BlockSpec((PEER_HEADS, 2, N_KEYS, PEER_HALF), lambda i: (0, 0, 0, 0)),
        ],
        out_specs=[big, big, tiled, tiled],
        out_shape=[big_shape, big_shape, tiled_shape, tiled_shape],
        scratch_shapes=[pltpu.VMEM((2, PEER_TOPK, tm), F32)],
        compiler_params=_params("parallel"),
        name="peer_route",
    )(xn, wq, keys)


PEER_TB = 512
PEER_EB = 1024
PEER_JCH = 128


def _peer_kernel(xn_ref, u_ref, vt_ref, n_ref, e1_ref, r2_ref, e2_ref, o_ref, a_ref, w_ref):
    @pl.when(pl.program_id(1) == 0)
    def _():
        o_ref[...] = jnp.zeros_like(o_ref)

    a_ref[...] = _gelu_exact(_dot_nt(u_ref[...].astype(BF16), xn_ref[...])).astype(BF16)
    tb = xn_ref.shape[0]

    def per_key(ii, carry):
        row0 = pl.multiple_of(ii * N_KEYS, N_KEYS)
        n_rows = [n_ref[h, pl.ds(ii, 1), :] for h in range(PEER_HEADS)]
        e1_rows = [e1_ref[h, pl.ds(ii, 1), :] for h in range(PEER_HEADS)]
        for lt in range(tb // BLK):
            lanes = slice(lt * BLK, (lt + 1) * BLK)
            wide = lambda row: jnp.broadcast_to(row[:, lanes], (PEER_JCH, BLK)).astype(BF16)
            n_b = [wide(r) for r in n_rows]
            e1_b = [wide(r) for r in e1_rows]
            for j0 in range(0, N_KEYS, PEER_JCH):
                js = slice(j0 // 2, (j0 + PEER_JCH) // 2)
                acc = None
                for h in range(PEER_HEADS):
                    keep = pltpu.bitcast(r2_ref[h, lt, js, :], BF16) < n_b[h]
                    gate = jnp.where(keep, pltpu.bitcast(e2_ref[h, lt, js, :], BF16), 0.0) * e1_b[h]
                    acc = gate if acc is None else acc + gate
                rows = pl.ds(row0 + j0, PEER_JCH)
                w_ref[rows, lanes] = acc * a_ref[rows, lanes]
        return carry

    lax.fori_loop(0, PEER_EB // N_KEYS, per_key, 0)
    o_ref[...] += _dot(vt_ref[...], w_ref[...])


def _transpose_bf16_kernel(x_ref, o_ref):
    o_ref[...] = x_ref[...].T.astype(BF16)


def _transpose_bf16(x, rows=512):
    r, c = x.shape
    return pl.pallas_call(
        _transpose_bf16_kernel,
        grid=(r // rows,),
        in_specs=[pl.BlockSpec((rows, c), lambda i: (i, 0))],
        out_specs=pl.BlockSpec((c, rows), lambda i: (0, i)),
        out_shape=jax.ShapeDtypeStruct((c, r), BF16),
        compiler_params=_params("parallel"),
        name="transpose_v",
    )(x)


def _peer(xn, u, vt_b, n_sel, e1, r2, e2):
    t = xn.shape[0]
    tb = min(t, PEER_TB)
    assert tb == PEER_TB
    ni = PEER_EB // N_KEYS
    small = pl.BlockSpec((PEER_HEADS, ni, tb), lambda i, e: (0, e, i))
    big = pl.BlockSpec((PEER_HEADS, tb // BLK, N_KEYS // 2, BLK), lambda i, e: (0, i, 0, 0))
    return pl.pallas_call(
        _peer_kernel,
        grid=(t // tb, N_EXPERTS // PEER_EB),
        in_specs=[
            pl.BlockSpec((tb, D_MODEL), lambda i, e: (i, 0)),
            pl.BlockSpec((PEER_EB, D_MODEL), lambda i, e: (e, 0)),
            pl.BlockSpec((D_MODEL, PEER_EB), lambda i, e: (0, e)),
            small, small, big, big,
        ],
        out_specs=pl.BlockSpec((D_MODEL, tb), lambda i, e: (0, i)),
        out_shape=jax.ShapeDtypeStruct((D_MODEL, t), F32),
        scratch_shapes=[pltpu.VMEM((PEER_EB, tb), BF16), pltpu.VMEM((PEER_EB, tb), BF16)],
        compiler_params=_params("parallel", "arbitrary"),
        name="peer_experts",
    )(xn, u, vt_b, n_sel, e1, r2, e2)


def _final_kernel(h_ref, pt_ref, g_ref, o_ref):
    h = h_ref[...] + pt_ref[...].T
    ms = jnp.mean(h * h, axis=-1, keepdims=True)
    o_ref[...] = h * lax.rsqrt(ms + EPS) * g_ref[...]


def _final(h1, peer_t, gain):
    t = h1.shape[0]
    tm = min(t, 512)
    rows = pl.BlockSpec((tm, D_MODEL), lambda i: (i, 0))
    return pl.pallas_call(
        _final_kernel,
        grid=(t // tm,),
        in_specs=[rows, pl.BlockSpec((D_MODEL, tm), lambda i: (0, i)),
                  pl.BlockSpec((1, D_MODEL), lambda i: (0, 0))],
        out_specs=rows,
        out_shape=jax.ShapeDtypeStruct((t, D_MODEL), F32),
        compiler_params=_params("parallel"),
        name="final_norm",
    )(h1, peer_t, gain)


def _mixer(x2d, batch, seq, meta_tokens, rel_bias, ln_mix, w_in, sinks, conv_w, conv_b, dt_bias,
           a_log, d_skip, attn_norm_w, ssm_norm_w):
    nb = seq // BLK
    gain = ln_mix.reshape(1, D_MODEL)
    meta_pad = jnp.concatenate([jnp.zeros((BLK - N_META, D_MODEL), F32), meta_tokens.astype(F32)], axis=0)
    w_t = w_in.T.astype(BF16)
    proj = _inproj(x2d, gain, w_t)
    proj_meta = _inproj(meta_pad, gain, w_t)

    bucket, valid = _band_tables(nb)
    tab = rel_bias.astype(F32)
    bias = jnp.full((2, ATTN_HEADS) + bucket.shape[1:], NEG, F32)
    for b in range(REL_BUCKETS):
        bias = jnp.where((valid & (bucket == b))[:, None], tab[b][None, :, None, None], bias)
    ya = _attention(proj, proj_meta, sinks.astype(F32), bias, attn_norm_w.reshape(1, D_ATTN),
                    batch, nb, F32)
    ys = _ssd(proj, proj_meta, conv_w, conv_b, dt_bias, a_log, d_skip,
                   ssm_norm_w.reshape(1, SSM_D_INNER), batch, nb, F32)
    return ya, ys


def kernel(x, meta_tokens, rel_bias, ln_mix, w_in, attn_sinks, conv_w, conv_b, dt_bias, a_log, d_skip,
           attn_norm_w, ssm_norm_w, w_out, ln_ffn, peer_wq, peer_keys, peer_u, peer_v, ln_final):
    batch, seq, _ = x.shape
    x2d = x.reshape(batch * seq, D_MODEL)
    ya, ys = _mixer(x2d, batch, seq, meta_tokens, rel_bias, ln_mix[0], w_in[0], attn_sinks[0],
                         conv_w[0], conv_b[0], dt_bias[0], a_log[0], d_skip[0], attn_norm_w[0],
                         ssm_norm_w[0])
    h1, xn = _outproj(ya, ys, x2d, w_out[0], ln_ffn[0].reshape(1, D_MODEL))
    n_sel, e1, r2, e2 = _route(xn, peer_wq[0], peer_keys[0].astype(BF16))
    peer_t = _peer(xn, peer_u[0], _transpose_bf16(peer_v[0]), n_sel, e1, r2, e2)
    out = _final(h1, peer_t, ln_final.reshape(1, D_MODEL))
    return out.reshape(batch, seq, D_MODEL)
```

```python
import functools

import jax
import jax.numpy as jnp
import numpy as np
from jax import lax
from jax.experimental import pallas as pl
from jax.experimental.pallas import tpu as pltpu

F32 = jnp.float32
BF16 = jnp.bfloat16

D_MODEL = 2048
N_META = 16
HEAD_DIM = 64
D_ATTN = 1024
ATTN_HEADS = 16
ATTN_KV_HEADS = 4
ATTN_GROUP = 4
D_KV = 256
WINDOW = 128
BLK = 128
REL_BUCKETS = 32
REL_MAX_DIST = 128
SSM_D_INNER = 1024
SSM_HEAD_DIM = 64
SSM_HEADS = 16
SSM_GROUPS = 2
SSM_HPG = 8
SSM_STATE = 128
CONV_WIDTH = 4
D_XBC = 1536
D_BC = 2 * SSM_GROUPS * SSM_STATE
PEER_HEADS = 8
PEER_TOPK = 16
N_KEYS = 128
N_EXPERTS = N_KEYS * N_KEYS
PEER_HALF = 128
EPS = 1e-6
NEG = -1e30

COL_Q = 0
COL_K = 1024
COL_V = 1280
COL_Z = 1536
COL_XS = 2560
COL_BC = 3584
COL_DT = 4096
D_IN = 4112
D_PROJ = 4224
HALF = 512

VMEM_LIMIT = 56 * 1024 * 1024


def _params(*sem, flags=None):
    return pltpu.CompilerParams(dimension_semantics=sem, vmem_limit_bytes=VMEM_LIMIT, flags=flags)


def _dot(a, b):
    return jnp.dot(a, b, preferred_element_type=F32)


def _dot_nt(a, b):
    return lax.dot_general(a, b, (((1,), (1,)), ((), ())), preferred_element_type=F32)


def _split3(x):
    hi = x.astype(BF16)
    r = x - hi.astype(F32)
    mid = r.astype(BF16)
    lo = (r - mid.astype(F32)).astype(BF16)
    return hi, mid, lo


def _dot01_left(m01, x):
    hi, mid, lo = _split3(x)
    return _dot(m01, hi) + _dot(m01, mid) + _dot(m01, lo)


def _dot01_right(x, m01):
    hi, mid, lo = _split3(x)
    return _dot(hi, m01) + _dot(mid, m01) + _dot(lo, m01)


def _silu(x):
    h = 0.5 * x
    return h + h * jnp.tanh(h)


def _softplus(x):
    return jnp.maximum(x, 0.0) + jnp.log1p(jnp.exp(-jnp.abs(x)))


def _gelu_exact(x):
    return 0.5 * x * (1.0 + lax.erf(x * np.float32(np.sqrt(0.5))))


def _inproj_kernel(x_ref, g_ref, w_ref, o_ref):
    x = x_ref[...]
    ms = jnp.mean(x * x, axis=-1, keepdims=True)
    xn = (x * lax.rsqrt(ms + EPS) * g_ref[...]).astype(BF16)
    o_ref[:, :D_IN] = _dot_nt(xn, w_ref[...])
    o_ref[:, D_IN:] = jnp.zeros((x.shape[0], D_PROJ - D_IN), F32)


def _inproj(x2d, gain, w_t):
    m = x2d.shape[0]
    tm = min(m, 512)
    return pl.pallas_call(
        _inproj_kernel,
        grid=(m // tm,),
        in_specs=[
            pl.BlockSpec((tm, D_MODEL), lambda i: (i, 0)),
            pl.BlockSpec((1, D_MODEL), lambda i: (0, 0)),
            pl.BlockSpec((D_IN, D_MODEL), lambda i: (0, 0), pipeline_mode=pl.Buffered(1)),
        ],
        out_specs=pl.BlockSpec((tm, D_PROJ), lambda i: (i, 0)),
        out_shape=jax.ShapeDtypeStruct((m, D_PROJ), F32),
        compiler_params=_params("parallel"),
        name="inproj",
    )(x2d, gain, w_t)


def _t5_bucket(dist):
    n = np.maximum(dist, 0)
    max_exact = REL_BUCKETS // 2
    large = max_exact + (np.log(np.maximum(n, 1) / max_exact) / np.log(REL_MAX_DIST / max_exact)
                         * (REL_BUCKETS - max_exact)).astype(np.int32)
    large = np.minimum(large, REL_BUCKETS - 1)
    return np.where(n < max_exact, n, large).astype(np.int32)


N_BAND = BLK + N_META


def _band_tables(nb):
    r = np.arange(BLK)[:, None]
    q = np.arange(BLK)[None, :]
    m = np.arange(N_META)[:, None]
    buckets, valids = [], []
    for n in range(nb):
        upper = r > q
        d_band = np.where(upper, q - r + BLK, q - r)
        d_meta = N_META + n * BLK + q - m
        assert (d_band[upper] < WINDOW).all() and (d_band >= 0).all() and (d_meta >= 0).all()
        buckets.append(_t5_bucket(np.concatenate([d_band, d_meta], axis=0)))
        valids.append(np.concatenate([~upper | (n > 0), np.ones((N_META, BLK), bool)], axis=0))
    for n in range(2, nb):
        assert (buckets[n] == buckets[1]).all() and (valids[n] == valids[1]).all()
    last = min(1, nb - 1)
    return np.stack([buckets[0], buckets[last]]), np.stack([valids[0], valids[last]])


ATTN_QB = 8


def _attn_kernel(n_qb, sink_ref, q_ref, kp_ref, ko_ref, km_ref, vp_ref, vo_ref, vmt_ref,
                 bias0_ref, bias1_ref, nw_ref, o_ref, yt_ref):
    upper = (lax.broadcasted_iota(jnp.int32, (BLK, BLK), 0)
             > lax.broadcasted_iota(jnp.int32, (BLK, BLK), 1))
    vmt = vmt_ref[...].astype(BF16)
    km = km_ref[...].astype(BF16)
    v_t = [vp_ref[...].T.astype(BF16)]
    k_b = [kp_ref[...].astype(BF16)]
    for s in range(n_qb):
        v_t.append(vo_ref[s * BLK:(s + 1) * BLK, :].T.astype(BF16))
        k_b.append(ko_ref[s * BLK:(s + 1) * BLK, :].astype(BF16))
    for s in range(n_qb):
        bias_ref = bias0_ref if s == 0 else bias1_ref
        qcols = slice(s * BLK, (s + 1) * BLK)
        q = (q_ref[qcols, :] * np.float32(HEAD_DIM ** -0.5)).astype(BF16)
        for j in range(ATTN_KV_HEADS):
            ks = slice(j * HEAD_DIM, (j + 1) * HEAD_DIM)
            heads = [j * ATTN_GROUP + g for g in range(ATTN_GROUP)]
            q4 = jnp.concatenate([q[:, h * HEAD_DIM:(h + 1) * HEAD_DIM] for h in heads], axis=0)
            lp = _dot_nt(k_b[s][:, ks], q4)
            lo = _dot_nt(k_b[s + 1][:, ks], q4)
            lm = _dot_nt(km[:, ks], q4)
            e_prev, e_own, e_meta, inv = [], [], [], []
            for g, h in enumerate(heads):
                cols = slice(g * BLK, (g + 1) * BLK)
                band = jnp.where(upper, lp[:, cols], lo[:, cols]) + bias_ref[0, h, 0:BLK, :]
                meta = lm[:, cols] + bias_ref[0, h, BLK:N_BAND, :]
                sink = sink_ref[h]
                mx = jnp.maximum(jnp.maximum(jnp.max(band, axis=0, keepdims=True),
                                             jnp.max(meta, axis=0, keepdims=True)), sink)
                eb = jnp.exp(band - mx)
                em = jnp.exp(meta - mx)
                denom = (jnp.sum(eb, axis=0, keepdims=True) + jnp.sum(em, axis=0, keepdims=True)
                         + jnp.exp(sink - mx))
                inv.append(1.0 / denom)
                e_prev.append(jnp.where(upper, eb, 0.0).astype(BF16))
                e_own.append(jnp.where(upper, 0.0, eb).astype(BF16))
                e_meta.append(em.astype(BF16))
            cat = lambda parts: jnp.concatenate(parts, axis=1)
            ot = (_dot(v_t[s][ks, :], cat(e_prev)) + _dot(v_t[s + 1][ks, :], cat(e_own))
                  + _dot(vmt[ks, :], cat(e_meta))) * cat(inv)
            for g, h in enumerate(heads):
                yt_ref[h * HEAD_DIM:(h + 1) * HEAD_DIM, qcols] = ot[:, g * BLK:(g + 1) * BLK]
    yt = yt_ref[...]
    ms = jnp.mean(yt * yt, axis=0, keepdims=True)
    o_ref[...] = ((yt * lax.rsqrt(ms + EPS)).T * nw_ref[...]).astype(o_ref.dtype)


def _attention(proj, proj_meta, sinks, bias, norm_w, batch, nb, ymix_dtype):
    rows = batch * nb * BLK
    kcol, vcol = COL_K // D_KV, COL_V // D_KV
    meta_blk = (BLK - N_META) // N_META
    vm_t = proj_meta[BLK - N_META:, COL_V:COL_V + D_KV].T

    n_qb = max(d for d in range(1, ATTN_QB + 1) if nb % d == 0)
    ns = nb // n_qb
    wide = n_qb * BLK

    def cur(col):
        return lambda b, n: (b * ns + n, col)

    def prev(col):
        return lambda b, n: (jnp.maximum((b * ns + n) * n_qb - 1, 0), col)

    bias_spec = lambda pick: pl.BlockSpec((1, ATTN_HEADS, N_BAND, BLK), lambda b, n: (pick(n), 0, 0, 0))
    return pl.pallas_call(
        functools.partial(_attn_kernel, n_qb),
        grid=(batch, ns),
        in_specs=[
            pl.BlockSpec(memory_space=pltpu.SMEM),
            pl.BlockSpec((wide, D_ATTN), cur(0)),
            pl.BlockSpec((BLK, D_KV), prev(kcol)),
            pl.BlockSpec((wide, D_KV), cur(kcol)),
            pl.BlockSpec((N_META, D_KV), lambda b, n: (meta_blk, kcol)),
            pl.BlockSpec((BLK, D_KV), prev(vcol)),
            pl.BlockSpec((wide, D_KV), cur(vcol)),
            pl.BlockSpec((D_KV, N_META), lambda b, n: (0, 0)),
            bias_spec(lambda n: jnp.minimum(n, 1)),
            bias_spec(lambda n: 1),
            pl.BlockSpec((1, D_ATTN), lambda b, n: (0, 0)),
        ],
        out_specs=pl.BlockSpec((wide, D_ATTN), cur(0)),
        out_shape=jax.ShapeDtypeStruct((rows, D_ATTN), ymix_dtype),
        scratch_shapes=[pltpu.VMEM((D_ATTN, wide), F32)],
        compiler_params=_params("parallel", "arbitrary"),
        name="swa_attention",
    )(sinks, proj, proj, proj, proj_meta, proj, proj, vm_t, bias, bias, norm_w)


def _ssd_chunk(xs_raw, bc_raw, tail_xs, tail_bc, dt_raw, cw, cb, dtb, alog, row_mask, state):
    def conv(blk, tail, w, b):
        ext = jnp.concatenate([tail, blk], axis=0)
        acc = b + w[CONV_WIDTH - 1:CONV_WIDTH, :] * blk
        for back in range(1, CONV_WIDTH):
            shifted = pltpu.roll(ext, back, 0)[8:, :]
            acc = acc + w[CONV_WIDTH - 1 - back:CONV_WIDTH - back, :] * shifted
        return _silu(acc)

    xs = conv(xs_raw, tail_xs, cw[:, :SSM_D_INNER], cb[:, :SSM_D_INNER])
    bc = conv(bc_raw, tail_bc, cw[:, SSM_D_INNER:], cb[:, SSM_D_INNER:])
    dt = _softplus(dt_raw + dtb)
    if row_mask is not None:
        xs = jnp.where(row_mask, xs, 0.0)
        bc = jnp.where(row_mask, bc, 0.0)
        dt = jnp.where(row_mask, dt, 0.0)
    a_neg = -jnp.exp(alog)
    d_a = dt * a_neg

    r = lax.broadcasted_iota(jnp.int32, (BLK, BLK), 0)
    c = lax.broadcasted_iota(jnp.int32, (BLK, BLK), 1)
    tri = r >= c
    cs = _dot01_left(tri.astype(BF16), d_a)
    cs_t = cs.T
    hh = lax.broadcasted_iota(jnp.int32, (BLK, SSM_D_INNER), 0)
    cc = lax.broadcasted_iota(jnp.int32, (BLK, SSM_D_INNER), 1)
    expand = (cc // SSM_HEAD_DIM == hh).astype(BF16)
    dt_rep = _dot01_right(dt, expand)
    ecs_rep = _dot01_right(jnp.exp(cs), expand)
    dec_rep = _dot01_right(jnp.exp(cs[BLK - 1:BLK, :] - cs), expand)

    xdt = xs * dt_rep
    xdtd = (xdt * dec_rep).astype(BF16)
    xdt_b = xdt.astype(BF16)
    chunk_decay = ecs_rep[BLK - 1:BLK, :]

    y_parts, new_state = [], []
    for g in range(SSM_GROUPS):
        b_g = bc[:, g * SSM_STATE:(g + 1) * SSM_STATE]
        c_g = bc[:, (SSM_GROUPS + g) * SSM_STATE:(SSM_GROUPS + g + 1) * SSM_STATE]
        cols = slice(g * 512, (g + 1) * 512)
        cb_g = _dot_nt(c_g.astype(BF16), b_g.astype(BF16))
        y_off = _dot(c_g.astype(BF16), state[g].astype(BF16)) * ecs_rep[:, cols]
        new_state.append(chunk_decay[:, cols] * state[g] + _dot(b_g.T.astype(BF16), xdtd[:, cols]))
        y_diag = []
        for hp in range(SSM_HPG):
            h = g * SSM_HPG + hp
            seg = cs[:, h:h + 1] - cs_t[h:h + 1, :]
            lmat = jnp.exp(jnp.where(tri, seg, -jnp.inf))
            m = (cb_g * lmat).astype(BF16)
            y_diag.append(_dot(m, xdt_b[:, h * SSM_HEAD_DIM:(h + 1) * SSM_HEAD_DIM]))
        y_parts.append(jnp.concatenate(y_diag, axis=1) + y_off)
    return jnp.concatenate(y_parts, axis=1), xs, new_state


def _halves(lo_ref, hi_ref):
    return jnp.concatenate([lo_ref[...], hi_ref[...]], axis=1)


def _ssd_meta_kernel(xs0_ref, xs1_ref, bc_ref, dt_ref, cw_ref, cb_ref, dtb_ref, alog_ref, st_ref):
    rows = lax.broadcasted_iota(jnp.int32, (BLK, 1), 0)
    zero_state = [jnp.zeros((SSM_STATE, 512), F32) for _ in range(SSM_GROUPS)]
    _, _, st = _ssd_chunk(_halves(xs0_ref, xs1_ref), bc_ref[...], jnp.zeros((8, SSM_D_INNER), F32),
                          jnp.zeros((8, D_BC), F32), dt_ref[...], cw_ref[...], cb_ref[...],
                          dtb_ref[...], alog_ref[...], rows >= BLK - N_META, zero_state)
    for g in range(SSM_GROUPS):
        st_ref[g] = st[g]


SSD_ROWS = 4


def _ssd_kernel(xs0_ref, xs1_ref, bc_ref, z0_ref, z1_ref, dt_ref, txs0_ref, txs1_ref, tbc_ref,
                mxs0_ref, mxs1_ref, mbc_ref, st0_ref,
                cw_ref, cb_ref, dtb_ref, alog_ref, dsk_ref, nw_ref, o_ref, st_ref):
    first = pl.program_id(1) == 0
    n_rows = o_ref.shape[0]

    @pl.when(first)
    def _():
        for r in range(n_rows):
            st_ref[r] = st0_ref[...]

    halves = lambda lo, hi, r: jnp.concatenate([lo[r], hi[r]], axis=1)
    for r in range(n_rows):
        tail_xs = jnp.where(first, _halves(mxs0_ref, mxs1_ref), halves(txs0_ref, txs1_ref, r))
        tail_bc = jnp.where(first, mbc_ref[...], tbc_ref[r])
        state = [st_ref[r, g] for g in range(SSM_GROUPS)]
        y, xs, new_state = _ssd_chunk(halves(xs0_ref, xs1_ref, r), bc_ref[r], tail_xs, tail_bc, dt_ref[r],
                                      cw_ref[...], cb_ref[...], dtb_ref[...], alog_ref[...], None, state)
        for g in range(SSM_GROUPS):
            st_ref[r, g] = new_state[g]
        y = y + xs * dsk_ref[...]
        yg = y * _silu(halves(z0_ref, z1_ref, r))
        outs = []
        for g in range(SSM_GROUPS):
            part = yg[:, g * 512:(g + 1) * 512]
            ms = jnp.mean(part * part, axis=-1, keepdims=True)
            outs.append(part * lax.rsqrt(ms + EPS))
        o_ref[r] = (jnp.concatenate(outs, axis=1) * nw_ref[...]).astype(o_ref.dtype)


def _pad_lanes(v, n=BLK):
    v = v.reshape(1, -1)
    return jnp.pad(v, ((0, 0), (0, n - v.shape[1])))


def _ssd(proj, proj_meta, conv_w, conv_b, dt_bias, a_log, d_skip, norm_w, batch, nc, ymix_dtype):
    rows = batch * nc * BLK
    cb = conv_b.reshape(1, D_XBC)
    dtb, alog = _pad_lanes(dt_bias), _pad_lanes(a_log)
    dsk = jnp.repeat(d_skip, SSM_HEAD_DIM).reshape(1, SSM_D_INNER)
    xs_c, z_c, bc_c, dt_c = COL_XS // HALF, COL_Z // HALF, COL_BC // D_BC, COL_DT // BLK
    full = lambda shape: pl.BlockSpec(shape, lambda *_: (0,) * len(shape))

    state0 = pl.pallas_call(
        _ssd_meta_kernel,
        grid=(1,),
        in_specs=[
            pl.BlockSpec((BLK, HALF), lambda i: (0, xs_c)),
            pl.BlockSpec((BLK, HALF), lambda i: (0, xs_c + 1)),
            pl.BlockSpec((BLK, D_BC), lambda i: (0, bc_c)),
            pl.BlockSpec((BLK, BLK), lambda i: (0, dt_c)),
            full((CONV_WIDTH, D_XBC)), full((1, D_XBC)), full((1, BLK)), full((1, BLK)),
        ],
        out_specs=full((SSM_GROUPS, SSM_STATE, 512)),
        out_shape=jax.ShapeDtypeStruct((SSM_GROUPS, SSM_STATE, 512), F32),
        compiler_params=_params("arbitrary"),
        name="ssd_meta_state",
    )(proj_meta, proj_meta, proj_meta, proj_meta, conv_w, cb, dtb, alog)

    n_rows = max(d for d in range(1, SSD_ROWS + 1) if batch % d == 0)
    proj3 = proj.reshape(batch, nc * BLK, D_PROJ)

    def cur(col, width):
        return pl.BlockSpec((n_rows, BLK, width), lambda b, c: (b, c, col))

    def tail(col, width):
        return pl.BlockSpec((n_rows, 8, width), lambda b, c: (b, jnp.maximum(c * (BLK // 8) - 1, 0), col))

    def meta_tail(col, width):
        return pl.BlockSpec((8, width), lambda b, c: (BLK // 8 - 1, col))

    out = pl.pallas_call(
        _ssd_kernel,
        grid=(batch // n_rows, nc),
        in_specs=[
            cur(xs_c, HALF), cur(xs_c + 1, HALF), cur(bc_c, D_BC), cur(z_c, HALF), cur(z_c + 1, HALF),
            cur(dt_c, BLK),
            tail(xs_c, HALF), tail(xs_c + 1, HALF), tail(bc_c, D_BC),
            meta_tail(xs_c, HALF), meta_tail(xs_c + 1, HALF), meta_tail(bc_c, D_BC),
            full((SSM_GROUPS, SSM_STATE, 512)),
            full((CONV_WIDTH, D_XBC)), full((1, D_XBC)), full((1, BLK)), full((1, BLK)),
            full((1, SSM_D_INNER)), full((1, SSM_D_INNER)),
        ],
        out_specs=pl.BlockSpec((n_rows, BLK, SSM_D_INNER), lambda b, c: (b, c, 0)),
        out_shape=jax.ShapeDtypeStruct((batch, nc * BLK, SSM_D_INNER), ymix_dtype),
        scratch_shapes=[pltpu.VMEM((n_rows, SSM_GROUPS, SSM_STATE, 512), F32)],
        compiler_params=_params("parallel", "arbitrary"),
        name="ssd_mixer",
    )(proj3, proj3, proj3, proj3, proj3, proj3, proj3, proj3, proj3, proj_meta, proj_meta, proj_meta,
      state0, conv_w, cb, dtb, alog, dsk, norm_w)
    return out.reshape(rows, SSM_D_INNER)


def _outproj_kernel(ya_ref, ys_ref, x_ref, w_ref, g_ref, h_ref, xn_ref):
    y = jnp.concatenate([ya_ref[...], ys_ref[...]], axis=1).astype(BF16)
    h = x_ref[...] + _dot(y, w_ref[...].astype(BF16))
    h_ref[...] = h
    ms = jnp.mean(h * h, axis=-1, keepdims=True)
    xn_ref[...] = (h * lax.rsqrt(ms + EPS) * g_ref[...]).astype(BF16)


def _outproj(ya, ys, x2d, w_out, gain):
    m = x2d.shape[0]
    tm = min(m, 512)
    return pl.pallas_call(
        _outproj_kernel,
        grid=(m // tm,),
        in_specs=[
            pl.BlockSpec((tm, D_ATTN), lambda i: (i, 0)),
            pl.BlockSpec((tm, SSM_D_INNER), lambda i: (i, 0)),
            pl.BlockSpec((tm, D_MODEL), lambda i: (i, 0)),
            pl.BlockSpec((D_MODEL, D_MODEL), lambda i: (0, 0), pipeline_mode=pl.Buffered(1)),
            pl.BlockSpec((1, D_MODEL), lambda i: (0, 0)),
        ],
        out_specs=[pl.BlockSpec((tm, D_MODEL), lambda i: (i, 0)),
                   pl.BlockSpec((tm, D_MODEL), lambda i: (i, 0))],
        out_shape=[jax.ShapeDtypeStruct((m, D_MODEL), F32),
                   jax.ShapeDtypeStruct((m, D_MODEL), BF16)],
        compiler_params=_params("parallel"),
        name="outproj",
    )(ya, ys, x2d, w_out, gain)


def _oddeven_sort_pairs(n):
    pairs = []
    p = 1
    while p < n:
        k = p
        while k >= 1:
            for j in range(k % p, n - k, 2 * k):
                for i in range(min(k, n - j - k)):
                    if (i + j) // (2 * p) == (i + j + k) // (2 * p):
                        pairs.append((i + j, i + j + k))
            k //= 2
        p *= 2
    return pairs


_SORT16 = _oddeven_sort_pairs(PEER_TOPK)


def _top16_tile(x):
    x = list(x)
    for i, j in _SORT16:
        x[i], x[j] = jnp.maximum(x[i], x[j]), jnp.minimum(x[i], x[j])
    for shift in (4, 2, 1):
        other = [pltpu.roll(v, shift, 0) for v in x]
        x = [jnp.maximum(x[r], other[PEER_TOPK - 1 - r]) for r in range(PEER_TOPK)]
        dist = PEER_TOPK // 2
        while dist >= 1:
            for i in range(PEER_TOPK):
                if i & dist == 0:
                    x[i], x[i + dist] = jnp.maximum(x[i], x[i + dist]), jnp.minimum(x[i], x[i + dist])
            dist //= 2
    return x


def _route_kernel(xn_ref, wq_ref, keys_ref, n_ref, e1_ref, r2_ref, e2_ref, v_ref):
    q = _dot(xn_ref[...], wq_ref[...].astype(BF16)).astype(BF16)
    n_tiles = xn_ref.shape[0] // BLK
    for h in range(PEER_HEADS):
        s, tops = [], []
        for c in range(2):
            qs = q[:, (2 * h + c) * PEER_HALF:(2 * h + c + 1) * PEER_HALF]
            sc = _dot_nt(keys_ref[h, c], qs)
            s.append(sc)
            for lt in range(n_tiles):
                lanes = slice(lt * BLK, (lt + 1) * BLK)
                keys8 = [sc[8 * r:8 * r + 8, lanes] for r in range(N_KEYS // 8)]
                best = _top16_tile(keys8)
                for r in range(PEER_TOPK):
                    v_ref[c, r:r + 1, lanes] = best[r][0:1, :]
                if c == 1:
                    ranks = []
                    for x in keys8:
                        rank = jnp.full_like(x, float(PEER_TOPK))
                        for a in range(PEER_TOPK):
                            rank = jnp.where(x == best[a], float(a), rank)
                        ranks.append(rank)
                    r2 = jnp.concatenate(ranks, axis=0).astype(BF16)
                    e2 = jnp.exp(sc[:, lanes] - best[0][0:1, :]).astype(BF16)
                    r2_ref[h, lt] = pltpu.bitcast(r2, jnp.uint32)
                    e2_ref[h, lt] = pltpu.bitcast(e2, jnp.uint32)
            tops.append(v_ref[c])
        v1, v2 = tops
        blocks = [v1[0:1] + v2, v1[1:2] + v2[0:8]]
        blocks += [v1[a:a + 1] + v2[0:8] for a in range(2, 8)]
        blocks.append(v1[8:16] + v2[0:1])
        cand = jnp.concatenate(blocks, axis=0)
        top = v1[0:1] + v2[0:1]
        rem = cand
        for _ in range(PEER_TOPK - 1):
            m = jnp.max(rem, axis=0, keepdims=True)
            rem = jnp.where(rem == m, -jnp.inf, rem)
        tau = jnp.max(rem, axis=0, keepdims=True)
        z = jnp.sum(jnp.where(cand >= tau, jnp.exp(cand - top), 0.0), axis=0, keepdims=True)
        cnt = [jnp.sum(jnp.where(blocks[a] >= tau, 1.0, 0.0), axis=0, keepdims=True) for a in range(8)]
        cnt_hi = jnp.where(blocks[8] >= tau, 1.0, 0.0)
        n_sel = jnp.zeros_like(s[0])
        for a in range(PEER_TOPK):
            n_a = cnt[a] if a < 8 else cnt_hi[a - 8:a - 7]
            n_sel = jnp.where(s[0] == v1[a:a + 1], n_a, n_sel)
        n_ref[h] = n_sel
        e1_ref[h] = jnp.exp(s[0] - v1[0:1]) / z


def _route(xn, wq, keys):
    t = xn.shape[0]
    tm = min(t, 256)
    big = pl.BlockSpec((PEER_HEADS, N_KEYS, tm), lambda i: (0, 0, i))
    big_shape = jax.ShapeDtypeStruct((PEER_HEADS, N_KEYS, t), F32)
    tiled = pl.BlockSpec((PEER_HEADS, tm // BLK, N_KEYS // 2, BLK), lambda i: (0, i, 0, 0))
    tiled_shape = jax.ShapeDtypeStruct((PEER_HEADS, t // BLK, N_KEYS // 2, BLK), jnp.uint32)
    return pl.pallas_call(
        _route_kernel,
        grid=(t // tm,),
        in_specs=[
            pl.BlockSpec((tm, D_MODEL), lambda i: (i, 0)),
            pl.BlockSpec((D_MODEL, D_MODEL), lambda i: (0, 0), pipeline_mode=pl.Buffered(1)),
            pl.BlockSpec((PEER_HEADS, 2, N_KEYS, PEER_HALF), lambda i: (0, 0, 0, 0)),
        ],
        out_specs=[big, big, tiled, tiled],
        out_shape=[big_shape, big_shape, tiled_shape, tiled_shape],
        scratch_shapes=[pltpu.VMEM((2, PEER_TOPK, tm), F32)],
        compiler_params=_params("parallel"),
        name="peer_route",
    )(xn, wq, keys)


PEER_TB = 512
PEER_EB = 1024
PEER_JCH = 128


def _peer_kernel(xn_ref, u_ref, vt_ref, n_ref, e1_ref, r2_ref, e2_ref, o_ref, a_ref, w_ref):
    @pl.when(pl.program_id(1) == 0)
    def _():
        o_ref[...] = jnp.zeros_like(o_ref)

    a_ref[...] = _gelu_exact(_dot_nt(u_ref[...].astype(BF16), xn_ref[...])).astype(BF16)
    tb = xn_ref.shape[0]

    def per_key(ii, carry):
        row0 = pl.multiple_of(ii * N_KEYS, N_KEYS)
        n_rows = [n_ref[h, pl.ds(ii, 1), :] for h in range(PEER_HEADS)]
        e1_rows = [e1_ref[h, pl.ds(ii, 1), :] for h in range(PEER_HEADS)]
        for lt in range(tb // BLK):
            lanes = slice(lt * BLK, (lt + 1) * BLK)
            wide = lambda row: jnp.broadcast_to(row[:, lanes], (PEER_JCH, BLK)).astype(BF16)
            n_b = [wide(r) for r in n_rows]
            e1_b = [wide(r) for r in e1_rows]
            for j0 in range(0, N_KEYS, PEER_JCH):
                js = slice(j0 // 2, (j0 + PEER_JCH) // 2)
                acc = None
                for h in range(PEER_HEADS):
                    keep = pltpu.bitcast(r2_ref[h, lt, js, :], BF16) < n_b[h]
                    gate = jnp.where(keep, pltpu.bitcast(e2_ref[h, lt, js, :], BF16), 0.0) * e1_b[h]
                    acc = gate if acc is None else acc + gate
                rows = pl.ds(row0 + j0, PEER_JCH)
                w_ref[rows, lanes] = acc * a_ref[rows, lanes]
        return carry

    lax.fori_loop(0, PEER_EB // N_KEYS, per_key, 0)
    o_ref[...] += _dot(vt_ref[...], w_ref[...])


def _transpose_bf16_kernel(x_ref, o_ref):
    o_ref[...] = x_ref[...].T.astype(BF16)


def _transpose_bf16(x, rows=512):
    r, c = x.shape
    return pl.pallas_call(
        _transpose_bf16_kernel,
        grid=(r // rows,),
        in_specs=[pl.BlockSpec((rows, c), lambda i: (i, 0))],
        out_specs=pl.BlockSpec((c, rows), lambda i: (0, i)),
        out_shape=jax.ShapeDtypeStruct((c, r), BF16),
        compiler_params=_params("parallel"),
        name="transpose_v",
    )(x)


def _peer(xn, u, vt_b, n_sel, e1, r2, e2):
    t = xn.shape[0]
    tb = min(t, PEER_TB)
    assert tb == PEER_TB
    ni = PEER_EB // N_KEYS
    small = pl.BlockSpec((PEER_HEADS, ni, tb), lambda i, e: (0, e, i))
    big = pl.BlockSpec((PEER_HEADS, tb // BLK, N_KEYS // 2, BLK), lambda i, e: (0, i, 0, 0))
    return pl.pallas_call(
        _peer_kernel,
        grid=(t // tb, N_EXPERTS // PEER_EB),
        in_specs=[
            pl.BlockSpec((tb, D_MODEL), lambda i, e: (i, 0)),
            pl.BlockSpec((PEER_EB, D_MODEL), lambda i, e: (e, 0)),
            pl.BlockSpec((D_MODEL, PEER_EB), lambda i, e: (0, e)),
            small, small, big, big,
        ],
        out_specs=pl.BlockSpec((D_MODEL, tb), lambda i, e: (0, i)),
        out_shape=jax.ShapeDtypeStruct((D_MODEL, t), F32),
        scratch_shapes=[pltpu.VMEM((PEER_EB, tb), BF16), pltpu.VMEM((PEER_EB, tb), BF16)],
        compiler_params=_params("parallel", "arbitrary"),
        name="peer_experts",
    )(xn, u, vt_b, n_sel, e1, r2, e2)


def _final_kernel(h_ref, pt_ref, g_ref, o_ref):
    h = h_ref[...] + pt_ref[...].T
    ms = jnp.mean(h * h, axis=-1, keepdims=True)
    o_ref[...] = h * lax.rsqrt(ms + EPS) * g_ref[...]


def _final(h1, peer_t, gain):
    t = h1.shape[0]
    tm = min(t, 512)
    rows = pl.BlockSpec((tm, D_MODEL), lambda i: (i, 0))
    return pl.pallas_call(
        _final_kernel,
        grid=(t // tm,),
        in_specs=[rows, pl.BlockSpec((D_MODEL, tm), lambda i: (0, i)),
                  pl.BlockSpec((1, D_MODEL), lambda i: (0, 0))],
        out_specs=rows,
        out_shape=jax.ShapeDtypeStruct((t, D_MODEL), F32),
        compiler_params=_params("parallel"),
        name="final_norm",
    )(h1, peer_t, gain)


def _mixer(x2d, batch, seq, meta_tokens, rel_bias, ln_mix, w_in, sinks, conv_w, conv_b, dt_bias,
           a_log, d_skip, attn_norm_w, ssm_norm_w):
    nb = seq // BLK
    gain = ln_mix.reshape(1, D_MODEL)
    meta_pad = jnp.concatenate([jnp.zeros((BLK - N_META, D_MODEL), F32), meta_tokens.astype(F32)], axis=0)
    w_t = w_in.T.astype(BF16)
    proj = _inproj(x2d, gain, w_t)
    proj_meta = _inproj(meta_pad, gain, w_t)

    bucket, valid = _band_tables(nb)
    tab = rel_bias.astype(F32)
    bias = jnp.full((2, ATTN_HEADS) + bucket.shape[1:], NEG, F32)
    for b in range(REL_BUCKETS):
        bias = jnp.where((valid & (bucket == b))[:, None], tab[b][None, :, None, None], bias)
    ya = _attention(proj, proj_meta, sinks.astype(F32), bias, attn_norm_w.reshape(1, D_ATTN),
                    batch, nb, BF16)
    ys = _ssd(proj, proj_meta, conv_w, conv_b, dt_bias, a_log, d_skip,
              ssm_norm_w.reshape(1, SSM_D_INNER), batch, nb, BF16)
    return ya, ys


def kernel(x, meta_tokens, rel_bias, ln_mix, w_in, attn_sinks, conv_w, conv_b, dt_bias, a_log, d_skip,
           attn_norm_w, ssm_norm_w, w_out, ln_ffn, peer_wq, peer_keys, peer_u, peer_v, ln_final):
    batch, seq, _ = x.shape
    x2d = x.reshape(batch * seq, D_MODEL)
    ya, ys = _mixer(x2d, batch, seq, meta_tokens, rel_bias, ln_mix[0], w_in[0], attn_sinks[0],
                         conv_w[0], conv_b[0], dt_bias[0], a_log[0], d_skip[0], attn_norm_w[0],
                         ssm_norm_w[0])
    h1, xn = _outproj(ya, ys, x2d, w_out[0], ln_ffn[0].reshape(1, D_MODEL))
    n_sel, e1, r2, e2 = _route(xn, peer_wq[0], peer_keys[0].astype(BF16))
    peer_t = _peer(xn, peer_u[0], _transpose_bf16(peer_v[0]), n_sel, e1, r2, e2)
    out = _final(h1, peer_t, ln_final.reshape(1, D_MODEL))
    return out.reshape(batch, seq, D_MODEL)
```

```python
import functools

import jax
import jax.numpy as jnp
import numpy as np
from jax import lax
from jax.experimental import pallas as pl
from jax.experimental.pallas import tpu as pltpu

F32 = jnp.float32
BF16 = jnp.bfloat16

D_MODEL = 2048
N_META = 16
HEAD_DIM = 64
D_ATTN = 1024
ATTN_HEADS = 16
ATTN_KV_HEADS = 4
ATTN_GROUP = 4
D_KV = 256
WINDOW = 128
BLK = 128
REL_BUCKETS = 32
REL_MAX_DIST = 128
SSM_D_INNER = 1024
SSM_HEAD_DIM = 64
SSM_HEADS = 16
SSM_GROUPS = 2
SSM_HPG = 8
SSM_STATE = 128
CONV_WIDTH = 4
D_XBC = 1536
D_BC = 2 * SSM_GROUPS * SSM_STATE
PEER_HEADS = 8
PEER_TOPK = 16
N_KEYS = 128
N_EXPERTS = N_KEYS * N_KEYS
PEER_HALF = 128
EPS = 1e-6
NEG = -1e30

COL_Q = 0
COL_K = 1024
COL_V = 1280
COL_Z = 1536
COL_XS = 2560
COL_BC = 3584
COL_DT = 4096
D_IN = 4112
D_PROJ = 4224
HALF = 512

D_GROUP = SSM_D_INNER // SSM_GROUPS

SUBLANES = 8
VMEM_LIMIT = 56 * 1024 * 1024
ROW_TILE = 512


def _params(*sem):
    return pltpu.CompilerParams(dimension_semantics=sem, vmem_limit_bytes=VMEM_LIMIT)


def _dot(a, b):
    return jnp.dot(a, b, preferred_element_type=F32)


def _dot_nt(a, b):
    return lax.dot_general(a, b, (((1,), (1,)), ((), ())), preferred_element_type=F32)


def _split3(x):
    hi = x.astype(BF16)
    r = x - hi.astype(F32)
    mid = r.astype(BF16)
    lo = (r - mid.astype(F32)).astype(BF16)
    return hi, mid, lo


def _dot01_left(m01, x):
    hi, mid, lo = _split3(x)
    return _dot(m01, hi) + _dot(m01, mid) + _dot(m01, lo)


def _dot01_right(x, m01):
    hi, mid, lo = _split3(x)
    return _dot(hi, m01) + _dot(mid, m01) + _dot(lo, m01)


def _silu(x):
    h = 0.5 * x
    return h + h * jnp.tanh(h)


def _softplus(x):
    return jnp.maximum(x, 0.0) + jnp.log1p(jnp.exp(-jnp.abs(x)))


def _gelu_exact(x):
    return 0.5 * x * (1.0 + lax.erf(x * np.float32(np.sqrt(0.5))))


def _inproj_kernel(x_ref, g_ref, w_ref, o_ref):
    x = x_ref[...]
    ms = jnp.mean(x * x, axis=-1, keepdims=True)
    xn = (x * lax.rsqrt(ms + EPS) * g_ref[...]).astype(BF16)
    o_ref[:, :D_IN] = _dot_nt(xn, w_ref[...])
    o_ref[:, D_IN:] = jnp.zeros((x.shape[0], D_PROJ - D_IN), F32)


def _inproj(x2d, gain, w_t):
    m = x2d.shape[0]
    tm = min(m, ROW_TILE)
    return pl.pallas_call(
        _inproj_kernel,
        grid=(m // tm,),
        in_specs=[
            pl.BlockSpec((tm, D_MODEL), lambda i: (i, 0)),
            pl.BlockSpec((1, D_MODEL), lambda i: (0, 0)),
            pl.BlockSpec((D_IN, D_MODEL), lambda i: (0, 0), pipeline_mode=pl.Buffered(1)),
        ],
        out_specs=pl.BlockSpec((tm, D_PROJ), lambda i: (i, 0)),
        out_shape=jax.ShapeDtypeStruct((m, D_PROJ), F32),
        compiler_params=_params("parallel"),
        name="inproj",
    )(x2d, gain, w_t)


def _t5_bucket(dist):
    n = np.maximum(dist, 0)
    max_exact = REL_BUCKETS // 2
    large = max_exact + (np.log(np.maximum(n, 1) / max_exact) / np.log(REL_MAX_DIST / max_exact)
                         * (REL_BUCKETS - max_exact)).astype(np.int32)
    large = np.minimum(large, REL_BUCKETS - 1)
    return np.where(n < max_exact, n, large).astype(np.int32)


N_BAND = BLK + N_META


def _band_tables(nb):
    r = np.arange(BLK)[:, None]
    q = np.arange(BLK)[None, :]
    m = np.arange(N_META)[:, None]
    buckets, valids = [], []
    for n in range(nb):
        upper = r > q
        d_band = np.where(upper, q - r + BLK, q - r)
        d_meta = N_META + n * BLK + q - m
        assert (d_band[upper] < WINDOW).all() and (d_band >= 0).all() and (d_meta >= 0).all()
        buckets.append(_t5_bucket(np.concatenate([d_band, d_meta], axis=0)))
        valids.append(np.concatenate([~upper | (n > 0), np.ones((N_META, BLK), bool)], axis=0))
    for n in range(2, nb):
        assert (buckets[n] == buckets[1]).all() and (valids[n] == valids[1]).all()
    last = min(1, nb - 1)
    return np.stack([buckets[0], buckets[last]]), np.stack([valids[0], valids[last]])


ATTN_QB = 8


def _attn_kernel(n_qb, sink_ref, q_ref, kp_ref, ko_ref, km_ref, vp_ref, vo_ref, vmt_ref,
                 bias0_ref, bias1_ref, nw_ref, o_ref, yt_ref):
    upper = (lax.broadcasted_iota(jnp.int32, (BLK, BLK), 0)
             > lax.broadcasted_iota(jnp.int32, (BLK, BLK), 1))
    vmt = vmt_ref[...].astype(BF16)
    km = km_ref[...].astype(BF16)
    v_t = [vp_ref[...].T.astype(BF16)]
    k_b = [kp_ref[...].astype(BF16)]
    for s in range(n_qb):
        v_t.append(vo_ref[s * BLK:(s + 1) * BLK, :].T.astype(BF16))
        k_b.append(ko_ref[s * BLK:(s + 1) * BLK, :].astype(BF16))
    for s in range(n_qb):
        bias_ref = bias0_ref if s == 0 else bias1_ref
        qcols = slice(s * BLK, (s + 1) * BLK)
        q = (q_ref[qcols, :] * np.float32(HEAD_DIM ** -0.5)).astype(BF16)
        for j in range(ATTN_KV_HEADS):
            ks = slice(j * HEAD_DIM, (j + 1) * HEAD_DIM)
            heads = [j * ATTN_GROUP + g for g in range(ATTN_GROUP)]
            q4 = jnp.concatenate([q[:, h * HEAD_DIM:(h + 1) * HEAD_DIM] for h in heads], axis=0)
            lp = _dot_nt(k_b[s][:, ks], q4)
            lo = _dot_nt(k_b[s + 1][:, ks], q4)
            lm = _dot_nt(km[:, ks], q4)
            e_prev, e_own, e_meta, inv = [], [], [], []
            for g, h in enumerate(heads):
                cols = slice(g * BLK, (g + 1) * BLK)
                band = jnp.where(upper, lp[:, cols], lo[:, cols]) + bias_ref[0, h, 0:BLK, :]
                meta = lm[:, cols] + bias_ref[0, h, BLK:N_BAND, :]
                sink = sink_ref[h]
                mx = jnp.maximum(jnp.maximum(jnp.max(band, axis=0, keepdims=True),
                                             jnp.max(meta, axis=0, keepdims=True)), sink)
                eb = jnp.exp(band - mx)
                em = jnp.exp(meta - mx)
                denom = (jnp.sum(eb, axis=0, keepdims=True) + jnp.sum(em, axis=0, keepdims=True)
                         + jnp.exp(sink - mx))
                inv.append(1.0 / denom)
                e_prev.append(jnp.where(upper, eb, 0.0).astype(BF16))
                e_own.append(jnp.where(upper, 0.0, eb).astype(BF16))
                e_meta.append(em.astype(BF16))
            cat = lambda parts: jnp.concatenate(parts, axis=1)
            ot = (_dot(v_t[s][ks, :], cat(e_prev)) + _dot(v_t[s + 1][ks, :], cat(e_own))
                  + _dot(vmt[ks, :], cat(e_meta))) * cat(inv)
            for g, h in enumerate(heads):
                yt_ref[h * HEAD_DIM:(h + 1) * HEAD_DIM, qcols] = ot[:, g * BLK:(g + 1) * BLK]
    yt = yt_ref[...]
    ms = jnp.mean(yt * yt, axis=0, keepdims=True)
    o_ref[...] = ((yt * lax.rsqrt(ms + EPS)).T * nw_ref[...]).astype(o_ref.dtype)


def _attention(proj, proj_meta, sinks, bias, norm_w, batch, nb, ymix_dtype):
    rows = batch * nb * BLK
    kcol, vcol = COL_K // D_KV, COL_V // D_KV
    meta_blk = (BLK - N_META) // N_META
    vm_t = proj_meta[BLK - N_META:, COL_V:COL_V + D_KV].T

    n_qb = max(d for d in range(1, ATTN_QB + 1) if nb % d == 0)
    ns = nb // n_qb
    wide = n_qb * BLK

    def cur(col):
        return lambda b, n: (b * ns + n, col)

    def prev(col):
        return lambda b, n: (jnp.maximum((b * ns + n) * n_qb - 1, 0), col)

    bias_spec = lambda pick: pl.BlockSpec((1, ATTN_HEADS, N_BAND, BLK), lambda b, n: (pick(n), 0, 0, 0))
    return pl.pallas_call(
        functools.partial(_attn_kernel, n_qb),
        grid=(batch, ns),
        in_specs=[
            pl.BlockSpec(memory_space=pltpu.SMEM),
            pl.BlockSpec((wide, D_ATTN), cur(COL_Q // D_ATTN)),
            pl.BlockSpec((BLK, D_KV), prev(kcol)),
            pl.BlockSpec((wide, D_KV), cur(kcol)),
            pl.BlockSpec((N_META, D_KV), lambda b, n: (meta_blk, kcol)),
            pl.BlockSpec((BLK, D_KV), prev(vcol)),
            pl.BlockSpec((wide, D_KV), cur(vcol)),
            pl.BlockSpec((D_KV, N_META), lambda b, n: (0, 0)),
            bias_spec(lambda n: jnp.minimum(n, 1)),
            bias_spec(lambda n: 1),
            pl.BlockSpec((1, D_ATTN), lambda b, n: (0, 0)),
        ],
        out_specs=pl.BlockSpec((wide, D_ATTN), cur(0)),
        out_shape=jax.ShapeDtypeStruct((rows, D_ATTN), ymix_dtype),
        scratch_shapes=[pltpu.VMEM((D_ATTN, wide), F32)],
        compiler_params=_params("parallel", "arbitrary"),
        name="swa_attention",
    )(sinks, proj, proj, proj, proj_meta, proj, proj, vm_t, bias, bias, norm_w)


def _ssd_chunk(xs_raw, bc_raw, tail_xs, tail_bc, dt_raw, cw, cb, dtb, alog, row_mask, state):
    def conv(blk, tail, w, b):
        ext = jnp.concatenate([tail, blk], axis=0)
        acc = b + w[CONV_WIDTH - 1:CONV_WIDTH, :] * blk
        for back in range(1, CONV_WIDTH):
            shifted = pltpu.roll(ext, back, 0)[SUBLANES:, :]
            acc = acc + w[CONV_WIDTH - 1 - back:CONV_WIDTH - back, :] * shifted
        return _silu(acc)

    xs = conv(xs_raw, tail_xs, cw[:, :SSM_D_INNER], cb[:, :SSM_D_INNER])
    bc = conv(bc_raw, tail_bc, cw[:, SSM_D_INNER:], cb[:, SSM_D_INNER:])
    dt = _softplus(dt_raw + dtb)
    if row_mask is not None:
        xs = jnp.where(row_mask, xs, 0.0)
        bc = jnp.where(row_mask, bc, 0.0)
        dt = jnp.where(row_mask, dt, 0.0)
    a_neg = -jnp.exp(alog)
    d_a = dt * a_neg

    r = lax.broadcasted_iota(jnp.int32, (BLK, BLK), 0)
    c = lax.broadcasted_iota(jnp.int32, (BLK, BLK), 1)
    tri = r >= c
    cs = _dot01_left(tri.astype(BF16), d_a)
    cs_t = cs.T
    hh = lax.broadcasted_iota(jnp.int32, (BLK, SSM_D_INNER), 0)
    cc = lax.broadcasted_iota(jnp.int32, (BLK, SSM_D_INNER), 1)
    expand = (cc // SSM_HEAD_DIM == hh).astype(BF16)
    dt_rep = _dot01_right(dt, expand)
    ecs_rep = _dot01_right(jnp.exp(cs), expand)
    dec_rep = _dot01_right(jnp.exp(cs[BLK - 1:BLK, :] - cs), expand)

    xdt = xs * dt_rep
    xdtd = (xdt * dec_rep).astype(BF16)
    xdt_b = xdt.astype(BF16)
    chunk_decay = ecs_rep[BLK - 1:BLK, :]

    y_parts, new_state = [], []
    for g in range(SSM_GROUPS):
        b_g = bc[:, g * SSM_STATE:(g + 1) * SSM_STATE]
        c_g = bc[:, (SSM_GROUPS + g) * SSM_STATE:(SSM_GROUPS + g + 1) * SSM_STATE]
        cols = slice(g * D_GROUP, (g + 1) * D_GROUP)
        cb_g = _dot_nt(c_g.astype(BF16), b_g.astype(BF16))
        y_off = _dot(c_g.astype(BF16), state[g].astype(BF16)) * ecs_rep[:, cols]
        new_state.append(chunk_decay[:, cols] * state[g] + _dot(b_g.T.astype(BF16), xdtd[:, cols]))
        y_diag = []
        for hp in range(SSM_HPG):
            h = g * SSM_HPG + hp
            seg = cs[:, h:h + 1] - cs_t[h:h + 1, :]
            lmat = jnp.exp(jnp.where(tri, seg, -jnp.inf))
            m = (cb_g * lmat).astype(BF16)
            y_diag.append(_dot(m, xdt_b[:, h * SSM_HEAD_DIM:(h + 1) * SSM_HEAD_DIM]))
        y_parts.append(jnp.concatenate(y_diag, axis=1) + y_off)
    return jnp.concatenate(y_parts, axis=1), xs, new_state


def _halves(lo_ref, hi_ref):
    return jnp.concatenate([lo_ref[...], hi_ref[...]], axis=1)


def _ssd_meta_kernel(xs0_ref, xs1_ref, bc_ref, dt_ref, cw_ref, cb_ref, dtb_ref, alog_ref, st_ref):
    rows = lax.broadcasted_iota(jnp.int32, (BLK, 1), 0)
    zero_state = [jnp.zeros((SSM_STATE, D_GROUP), F32) for _ in range(SSM_GROUPS)]
    _, _, st = _ssd_chunk(_halves(xs0_ref, xs1_ref), bc_ref[...], jnp.zeros((SUBLANES, SSM_D_INNER), F32),
                          jnp.zeros((SUBLANES, D_BC), F32), dt_ref[...], cw_ref[...], cb_ref[...],
                          dtb_ref[...], alog_ref[...], rows >= BLK - N_META, zero_state)
    for g in range(SSM_GROUPS):
        st_ref[g] = st[g]


SSD_ROWS = 4


def _ssd_kernel(xs0_ref, xs1_ref, bc_ref, z0_ref, z1_ref, dt_ref, txs0_ref, txs1_ref, tbc_ref,
                mxs0_ref, mxs1_ref, mbc_ref, st0_ref,
                cw_ref, cb_ref, dtb_ref, alog_ref, dsk_ref, nw_ref, o_ref, st_ref):
    first = pl.program_id(1) == 0
    n_rows = o_ref.shape[0]

    @pl.when(first)
    def _():
        for r in range(n_rows):
            st_ref[r] = st0_ref[...]

    halves = lambda lo, hi, r: jnp.concatenate([lo[r], hi[r]], axis=1)
    for r in range(n_rows):
        tail_xs = jnp.where(first, _halves(mxs0_ref, mxs1_ref), halves(txs0_ref, txs1_ref, r))
        tail_bc = jnp.where(first, mbc_ref[...], tbc_ref[r])
        state = [st_ref[r, g] for g in range(SSM_GROUPS)]
        y, xs, new_state = _ssd_chunk(halves(xs0_ref, xs1_ref, r), bc_ref[r], tail_xs, tail_bc, dt_ref[r],
                                      cw_ref[...], cb_ref[...], dtb_ref[...], alog_ref[...], None, state)
        for g in range(SSM_GROUPS):
            st_ref[r, g] = new_state[g]
        y = y + xs * dsk_ref[...]
        yg = y * _silu(halves(z0_ref, z1_ref, r))
        outs = []
        for g in range(SSM_GROUPS):
            part = yg[:, g * D_GROUP:(g + 1) * D_GROUP]
            ms = jnp.mean(part * part, axis=-1, keepdims=True)
            outs.append(part * lax.rsqrt(ms + EPS))
        o_ref[r] = (jnp.concatenate(outs, axis=1) * nw_ref[...]).astype(o_ref.dtype)


def _pad_lanes(v, n=BLK):
    v = v.reshape(1, -1)
    return jnp.pad(v, ((0, 0), (0, n - v.shape[1])))


def _ssd(proj, proj_meta, conv_w, conv_b, dt_bias, a_log, d_skip, norm_w, batch, nc, ymix_dtype):
    rows = batch * nc * BLK
    cb = conv_b.reshape(1, D_XBC)
    dtb, alog = _pad_lanes(dt_bias), _pad_lanes(a_log)
    dsk = jnp.repeat(d_skip, SSM_HEAD_DIM).reshape(1, SSM_D_INNER)
    xs_c, z_c, bc_c, dt_c = COL_XS // HALF, COL_Z // HALF, COL_BC // D_BC, COL_DT // BLK
    full = lambda shape: pl.BlockSpec(shape, lambda *_: (0,) * len(shape))

    state0 = pl.pallas_call(
        _ssd_meta_kernel,
        grid=(1,),
        in_specs=[
            pl.BlockSpec((BLK, HALF), lambda i: (0, xs_c)),
            pl.BlockSpec((BLK, HALF), lambda i: (0, xs_c + 1)),
            pl.BlockSpec((BLK, D_BC), lambda i: (0, bc_c)),
            pl.BlockSpec((BLK, BLK), lambda i: (0, dt_c)),
            full((CONV_WIDTH, D_XBC)), full((1, D_XBC)), full((1, BLK)), full((1, BLK)),
        ],
        out_specs=full((SSM_GROUPS, SSM_STATE, D_GROUP)),
        out_shape=jax.ShapeDtypeStruct((SSM_GROUPS, SSM_STATE, D_GROUP), F32),
        compiler_params=_params("arbitrary"),
        name="ssd_meta_state",
    )(proj_meta, proj_meta, proj_meta, proj_meta, conv_w, cb, dtb, alog)

    n_rows = max(d for d in range(1, SSD_ROWS + 1) if batch % d == 0)
    proj3 = proj.reshape(batch, nc * BLK, D_PROJ)

    def cur(col, width):
        return pl.BlockSpec((n_rows, BLK, width), lambda b, c: (b, c, col))

    def tail(col, width):
        return pl.BlockSpec((n_rows, SUBLANES, width),
                            lambda b, c: (b, jnp.maximum(c * (BLK // SUBLANES) - 1, 0), col))

    def meta_tail(col, width):
        return pl.BlockSpec((SUBLANES, width), lambda b, c: (BLK // SUBLANES - 1, col))

    out = pl.pallas_call(
        _ssd_kernel,
        grid=(batch // n_rows, nc),
        in_specs=[
            cur(xs_c, HALF), cur(xs_c + 1, HALF), cur(bc_c, D_BC), cur(z_c, HALF), cur(z_c + 1, HALF),
            cur(dt_c, BLK),
            tail(xs_c, HALF), tail(xs_c + 1, HALF), tail(bc_c, D_BC),
            meta_tail(xs_c, HALF), meta_tail(xs_c + 1, HALF), meta_tail(bc_c, D_BC),
            full((SSM_GROUPS, SSM_STATE, D_GROUP)),
            full((CONV_WIDTH, D_XBC)), full((1, D_XBC)), full((1, BLK)), full((1, BLK)),
            full((1, SSM_D_INNER)), full((1, SSM_D_INNER)),
        ],
        out_specs=pl.BlockSpec((n_rows, BLK, SSM_D_INNER), lambda b, c: (b, c, 0)),
        out_shape=jax.ShapeDtypeStruct((batch, nc * BLK, SSM_D_INNER), ymix_dtype),
        scratch_shapes=[pltpu.VMEM((n_rows, SSM_GROUPS, SSM_STATE, D_GROUP), F32)],
        compiler_params=_params("parallel", "arbitrary"),
        name="ssd_mixer",
    )(proj3, proj3, proj3, proj3, proj3, proj3, proj3, proj3, proj3, proj_meta, proj_meta, proj_meta,
      state0, conv_w, cb, dtb, alog, dsk, norm_w)
    return out.reshape(rows, SSM_D_INNER)


def _outproj_kernel(ya_ref, ys_ref, x_ref, w_ref, g_ref, h_ref, xn_ref):
    y = jnp.concatenate([ya_ref[...], ys_ref[...]], axis=1).astype(BF16)
    h = x_ref[...] + _dot(y, w_ref[...].astype(BF16))
    h_ref[...] = h
    ms = jnp.mean(h * h, axis=-1, keepdims=True)
    xn_ref[...] = (h * lax.rsqrt(ms + EPS) * g_ref[...]).astype(BF16)


def _outproj(ya, ys, x2d, w_out, gain):
    m = x2d.shape[0]
    tm = min(m, ROW_TILE)
    return pl.pallas_call(
        _outproj_kernel,
        grid=(m // tm,),
        in_specs=[
            pl.BlockSpec((tm, D_ATTN), lambda i: (i, 0)),
            pl.BlockSpec((tm, SSM_D_INNER), lambda i: (i, 0)),
            pl.BlockSpec((tm, D_MODEL), lambda i: (i, 0)),
            pl.BlockSpec((D_MODEL, D_MODEL), lambda i: (0, 0), pipeline_mode=pl.Buffered(1)),
            pl.BlockSpec((1, D_MODEL), lambda i: (0, 0)),
        ],
        out_specs=[pl.BlockSpec((tm, D_MODEL), lambda i: (i, 0)),
                   pl.BlockSpec((tm, D_MODEL), lambda i: (i, 0))],
        out_shape=[jax.ShapeDtypeStruct((m, D_MODEL), F32),
                   jax.ShapeDtypeStruct((m, D_MODEL), BF16)],
        compiler_params=_params("parallel"),
        name="outproj",
    )(ya, ys, x2d, w_out, gain)


def _oddeven_sort_pairs(n):
    pairs = []
    p = 1
    while p < n:
        k = p
        while k >= 1:
            for j in range(k % p, n - k, 2 * k):
                for i in range(min(k, n - j - k)):
                    if (i + j) // (2 * p) == (i + j + k) // (2 * p):
                        pairs.append((i + j, i + j + k))
            k //= 2
        p *= 2
    return pairs


_SORT16 = _oddeven_sort_pairs(PEER_TOPK)


def _top16_tile(x):
    x = list(x)
    for i, j in _SORT16:
        x[i], x[j] = jnp.maximum(x[i], x[j]), jnp.minimum(x[i], x[j])
    for shift in (4, 2, 1):
        other = [pltpu.roll(v, shift, 0) for v in x]
        x = [jnp.maximum(x[r], other[PEER_TOPK - 1 - r]) for r in range(PEER_TOPK)]
        dist = PEER_TOPK // 2
        while dist >= 1:
            for i in range(PEER_TOPK):
                if i & dist == 0:
                    x[i], x[i + dist] = jnp.maximum(x[i], x[i + dist]), jnp.minimum(x[i], x[i + dist])
            dist //= 2
    return x


def _rank_in_top16(x, best):
    gt = lambda v: v > x
    pick = jnp.where
    b3 = gt(best[7])
    b2 = gt(pick(b3, best[11], best[3]))
    b1 = gt(pick(b3, pick(b2, best[13], best[9]), pick(b2, best[5], best[1])))
    b0 = gt(pick(b3, pick(b2, pick(b1, best[14], best[12]), pick(b1, best[10], best[8])),
                 pick(b2, pick(b1, best[6], best[4]), pick(b1, best[2], best[0]))))
    rank = pick(b3, 8.0, 0.0) + pick(b2, 4.0, 0.0) + pick(b1, 2.0, 0.0) + pick(b0, 1.0, 0.0)
    return pick(gt(best[PEER_TOPK - 1]), float(PEER_TOPK), rank)


def _route_kernel(xn_ref, wq_ref, keys_ref, n_ref, e1_ref, r2_ref, e2_ref, v_ref):
    q = _dot(xn_ref[...], wq_ref[...].astype(BF16)).astype(BF16)
    n_tiles = xn_ref.shape[0] // BLK
    for h in range(PEER_HEADS):
        s, tops = [], []
        for c in range(2):
            qs = q[:, (2 * h + c) * PEER_HALF:(2 * h + c + 1) * PEER_HALF]
            sc = _dot_nt(keys_ref[h, c], qs)
            s.append(sc)
            for lt in range(n_tiles):
                lanes = slice(lt * BLK, (lt + 1) * BLK)
                keys8 = [sc[SUBLANES * r:SUBLANES * (r + 1), lanes] for r in range(N_KEYS // SUBLANES)]
                best = _top16_tile(keys8)
                for r in range(PEER_TOPK):
                    v_ref[c, r:r + 1, lanes] = best[r][0:1, :]
                if c == 1:
                    ranks = [_rank_in_top16(x, best) for x in keys8]
                    r2 = jnp.concatenate(ranks, axis=0).astype(BF16)
                    e2 = jnp.exp(sc[:, lanes] - best[0][0:1, :]).astype(BF16)
                    r2_ref[h, lt] = pltpu.bitcast(r2, jnp.uint32)
                    e2_ref[h, lt] = pltpu.bitcast(e2, jnp.uint32)
            tops.append(v_ref[c])
        v1, v2 = tops
        blocks = [v1[0:1] + v2, v1[1:2] + v2[0:8]]
        blocks += [v1[a:a + 1] + v2[0:8] for a in range(2, 8)]
        blocks.append(v1[8:16] + v2[0:1])
        cand = jnp.concatenate(blocks, axis=0)
        top = v1[0:1] + v2[0:1]
        rem = cand
        for _ in range(PEER_TOPK - 1):
            m = jnp.max(rem, axis=0, keepdims=True)
            rem = jnp.where(rem == m, -jnp.inf, rem)
        tau = jnp.max(rem, axis=0, keepdims=True)
        z = jnp.sum(jnp.where(cand >= tau, jnp.exp(cand - top), 0.0), axis=0, keepdims=True)
        cnt = [jnp.sum(jnp.where(blocks[a] >= tau, 1.0, 0.0), axis=0, keepdims=True) for a in range(8)]
        cnt_hi = jnp.where(blocks[8] >= tau, 1.0, 0.0)
        n_sel = jnp.zeros_like(s[0])
        for a in range(PEER_TOPK):
            n_a = cnt[a] if a < 8 else cnt_hi[a - 8:a - 7]
            n_sel = jnp.where(s[0] == v1[a:a + 1], n_a, n_sel)
        n_ref[h] = n_sel
        e1_ref[h] = jnp.exp(s[0] - v1[0:1]) / z


def _route(xn, wq, keys):
    t = xn.shape[0]
    tm = min(t, 256)
    big = pl.BlockSpec((PEER_HEADS, N_KEYS, tm), lambda i: (0, 0, i))
    big_shape = jax.ShapeDtypeStruct((PEER_HEADS, N_KEYS, t), F32)
    tiled = pl.BlockSpec((PEER_HEADS, tm // BLK, N_KEYS // 2, BLK), lambda i: (0, i, 0, 0))
    tiled_shape = jax.ShapeDtypeStruct((PEER_HEADS, t // BLK, N_KEYS // 2, BLK), jnp.uint32)
    return pl.pallas_call(
        _route_kernel,
        grid=(t // tm,),
        in_specs=[
            pl.BlockSpec((tm, D_MODEL), lambda i: (i, 0)),
            pl.BlockSpec((D_MODEL, D_MODEL), lambda i: (0, 0), pipeline_mode=pl.Buffered(1)),
            pl.BlockSpec((PEER_HEADS, 2, N_KEYS, PEER_HALF), lambda i: (0, 0, 0, 0)),
        ],
        out_specs=[big, big, tiled, tiled],
        out_shape=[big_shape, big_shape, tiled_shape, tiled_shape],
        scratch_shapes=[pltpu.VMEM((2, PEER_TOPK, tm), F32)],
        compiler_params=_params("parallel"),
        name="peer_route",
    )(xn, wq, keys)


PEER_TB = 512
PEER_EB = 1024
PEER_JCH = 128


def _peer_kernel(xn_ref, u_ref, vt_ref, n_ref, e1_ref, r2_ref, e2_ref, o_ref, a_ref, w_ref):
    @pl.when(pl.program_id(1) == 0)
    def _():
        o_ref[...] = jnp.zeros_like(o_ref)

    a_ref[...] = _gelu_exact(_dot_nt(u_ref[...].astype(BF16), xn_ref[...])).astype(BF16)
    tb = xn_ref.shape[0]

    def per_key(ii, carry):
        row0 = pl.multiple_of(ii * N_KEYS, N_KEYS)
        n_rows = [n_ref[h, pl.ds(ii, 1), :] for h in range(PEER_HEADS)]
        e1_rows = [e1_ref[h, pl.ds(ii, 1), :] for h in range(PEER_HEADS)]
        for lt in range(tb // BLK):
            lanes = slice(lt * BLK, (lt + 1) * BLK)
            wide = lambda row: jnp.broadcast_to(row[:, lanes], (PEER_JCH, BLK)).astype(BF16)
            n_b = [wide(r) for r in n_rows]
            e1_b = [wide(r) for r in e1_rows]
            for j0 in range(0, N_KEYS, PEER_JCH):
                js = slice(j0 // 2, (j0 + PEER_JCH) // 2)
                acc = None
                for h in range(PEER_HEADS):
                    keep = pltpu.bitcast(r2_ref[h, lt, js, :], BF16) < n_b[h]
                    gate = jnp.where(keep, pltpu.bitcast(e2_ref[h, lt, js, :], BF16), 0.0) * e1_b[h]
                    acc = gate if acc is None else acc + gate
                rows = pl.ds(row0 + j0, PEER_JCH)
                w_ref[rows, lanes] = acc * a_ref[rows, lanes]
        return carry

    lax.fori_loop(0, PEER_EB // N_KEYS, per_key, 0)
    o_ref[...] += _dot(vt_ref[...], w_ref[...])


def _transpose_bf16_kernel(x_ref, o_ref):
    o_ref[...] = x_ref[...].T.astype(BF16)


def _transpose_bf16(x, rows=ROW_TILE):
    r, c = x.shape
    return pl.pallas_call(
        _transpose_bf16_kernel,
        grid=(r // rows,),
        in_specs=[pl.BlockSpec((rows, c), lambda i: (i, 0))],
        out_specs=pl.BlockSpec((c, rows), lambda i: (0, i)),
        out_shape=jax.ShapeDtypeStruct((c, r), BF16),
        compiler_params=_params("parallel"),
        name="transpose_v",
    )(x)


def _peer(xn, u, vt_b, n_sel, e1, r2, e2):
    t = xn.shape[0]
    tb = min(t, PEER_TB)
    assert tb == PEER_TB
    ni = PEER_EB // N_KEYS
    small = pl.BlockSpec((PEER_HEADS, ni, tb), lambda i, e: (0, e, i))
    big = pl.BlockSpec((PEER_HEADS, tb // BLK, N_KEYS // 2, BLK), lambda i, e: (0, i, 0, 0))
    return pl.pallas_call(
        _peer_kernel,
        grid=(t // tb, N_EXPERTS // PEER_EB),
        in_specs=[
            pl.BlockSpec((tb, D_MODEL), lambda i, e: (i, 0)),
            pl.BlockSpec((PEER_EB, D_MODEL), lambda i, e: (e, 0)),
            pl.BlockSpec((D_MODEL, PEER_EB), lambda i, e: (0, e)),
            small, small, big, big,
        ],
        out_specs=pl.BlockSpec((D_MODEL, tb), lambda i, e: (0, i)),
        out_shape=jax.ShapeDtypeStruct((D_MODEL, t), F32),
        scratch_shapes=[pltpu.VMEM((PEER_EB, tb), BF16), pltpu.VMEM((PEER_EB, tb), BF16)],
        compiler_params=_params("parallel", "arbitrary"),
        name="peer_experts",
    )(xn, u, vt_b, n_sel, e1, r2, e2)


def _final_kernel(h_ref, pt_ref, g_ref, o_ref):
    h = h_ref[...] + pt_ref[...].T
    ms = jnp.mean(h * h, axis=-1, keepdims=True)
    o_ref[...] = h * lax.rsqrt(ms + EPS) * g_ref[...]


def _final(h1, peer_t, gain):
    t = h1.shape[0]
    tm = min(t, ROW_TILE)
    rows = pl.BlockSpec((tm, D_MODEL), lambda i: (i, 0))
    return pl.pallas_call(
        _final_kernel,
        grid=(t // tm,),
        in_specs=[rows, pl.BlockSpec((D_MODEL, tm), lambda i: (0, i)),
                  pl.BlockSpec((1, D_MODEL), lambda i: (0, 0))],
        out_specs=rows,
        out_shape=jax.ShapeDtypeStruct((t, D_MODEL), F32),
        compiler_params=_params("parallel"),
        name="final_norm",
    )(h1, peer_t, gain)


def _mixer(x2d, batch, seq, meta_tokens, rel_bias, ln_mix, w_in, sinks, conv_w, conv_b, dt_bias,
           a_log, d_skip, attn_norm_w, ssm_norm_w):
    nb = seq // BLK
    gain = ln_mix.reshape(1, D_MODEL)
    meta_pad = jnp.concatenate([jnp.zeros((BLK - N_META, D_MODEL), F32), meta_tokens.astype(F32)], axis=0)
    w_t = w_in.T.astype(BF16)
    proj = _inproj(x2d, gain, w_t)
    proj_meta = _inproj(meta_pad, gain, w_t)

    bucket, valid = _band_tables(nb)
    tab = rel_bias.astype(F32)
    bias = jnp.full((2, ATTN_HEADS) + bucket.shape[1:], NEG, F32)
    for b in range(REL_BUCKETS):
        bias = jnp.where((valid & (bucket == b))[:, None], tab[b][None, :, None, None], bias)
    ya = _attention(proj, proj_meta, sinks.astype(F32), bias, attn_norm_w.reshape(1, D_ATTN),
                    batch, nb, BF16)
    ys = _ssd(proj, proj_meta, conv_w, conv_b, dt_bias, a_log, d_skip,
              ssm_norm_w.reshape(1, SSM_D_INNER), batch, nb, BF16)
    return ya, ys


def kernel(x, meta_tokens, rel_bias, ln_mix, w_in, attn_sinks, conv_w, conv_b, dt_bias, a_log, d_skip,
           attn_norm_w, ssm_norm_w, w_out, ln_ffn, peer_wq, peer_keys, peer_u, peer_v, ln_final):
    batch, seq, _ = x.shape
    x2d = x.reshape(batch * seq, D_MODEL)
    ya, ys = _mixer(x2d, batch, seq, meta_tokens, rel_bias, ln_mix[0], w_in[0], attn_sinks[0],
                         conv_w[0], conv_b[0], dt_bias[0], a_log[0], d_skip[0], attn_norm_w[0],
                         ssm_norm_w[0])
    h1, xn = _outproj(ya, ys, x2d, w_out[0], ln_ffn[0].reshape(1, D_MODEL))
    n_sel, e1, r2, e2 = _route(xn, peer_wq[0], peer_keys[0].astype(BF16))
    peer_t = _peer(xn, peer_u[0], _transpose_bf16(peer_v[0]), n_sel, e1, r2, e2)
    out = _final(h1, peer_t, ln_final.reshape(1, D_MODEL))
    return out.reshape(batch, seq, D_MODEL)
```

```python
import functools

import jax
import jax.numpy as jnp
import numpy as np
from jax import lax
from jax.experimental import pallas as pl
from jax.experimental.pallas import tpu as pltpu

F32 = jnp.float32
BF16 = jnp.bfloat16

D_MODEL = 2048
N_META = 16
HEAD_DIM = 64
D_ATTN = 1024
ATTN_HEADS = 16
ATTN_KV_HEADS = 4
ATTN_GROUP = 4
D_KV = 256
WINDOW = 128
BLK = 128
REL_BUCKETS = 32
REL_MAX_DIST = 128
SSM_D_INNER = 1024
SSM_HEAD_DIM = 64
SSM_HEADS = 16
SSM_GROUPS = 2
SSM_HPG = 8
SSM_STATE = 128
CONV_WIDTH = 4
D_XBC = 1536
D_BC = 2 * SSM_GROUPS * SSM_STATE
PEER_HEADS = 8
PEER_TOPK = 16
N_KEYS = 128
N_EXPERTS = N_KEYS * N_KEYS
PEER_HALF = 128
EPS = 1e-6
NEG = -1e30

COL_Q = 0
COL_K = 1024
COL_V = 1280
COL_Z = 1536
COL_XS = 2560
COL_BC = 3584
COL_DT = 4096
D_IN = 4112
D_PROJ = 4224
HALF = 512

D_GROUP = SSM_D_INNER // SSM_GROUPS

SUBLANES = 8
VMEM_LIMIT = 56 * 1024 * 1024
ROW_TILE = 512


def _params(*sem):
    return pltpu.CompilerParams(dimension_semantics=sem, vmem_limit_bytes=VMEM_LIMIT)


def _dot(a, b):
    return jnp.dot(a, b, preferred_element_type=F32)


def _dot_nt(a, b):
    return lax.dot_general(a, b, (((1,), (1,)), ((), ())), preferred_element_type=F32)


def _split3(x):
    hi = x.astype(BF16)
    r = x - hi.astype(F32)
    mid = r.astype(BF16)
    lo = (r - mid.astype(F32)).astype(BF16)
    return hi, mid, lo


def _dot01_left(m01, x):
    hi, mid, lo = _split3(x)
    return _dot(m01, hi) + _dot(m01, mid) + _dot(m01, lo)


def _dot01_right(x, m01):
    hi, mid, lo = _split3(x)
    return _dot(hi, m01) + _dot(mid, m01) + _dot(lo, m01)


def _silu(x):
    h = 0.5 * x
    return h + h * jnp.tanh(h)


def _softplus(x):
    return jnp.maximum(x, 0.0) + jnp.log1p(jnp.exp(-jnp.abs(x)))


def _gelu_exact(x):
    return 0.5 * x * (1.0 + lax.erf(x * np.float32(np.sqrt(0.5))))


def _inproj_kernel(x_ref, g_ref, w_ref, o_ref):
    x = x_ref[...]
    ms = jnp.mean(x * x, axis=-1, keepdims=True)
    xn = (x * lax.rsqrt(ms + EPS) * g_ref[...]).astype(BF16)
    o_ref[:, :D_IN] = _dot_nt(xn, w_ref[...])
    o_ref[:, D_IN:] = jnp.zeros((x.shape[0], D_PROJ - D_IN), F32)


def _inproj(x2d, gain, w_t):
    m = x2d.shape[0]
    tm = min(m, ROW_TILE)
    return pl.pallas_call(
        _inproj_kernel,
        grid=(m // tm,),
        in_specs=[
            pl.BlockSpec((tm, D_MODEL), lambda i: (i, 0)),
            pl.BlockSpec((1, D_MODEL), lambda i: (0, 0)),
            pl.BlockSpec((D_IN, D_MODEL), lambda i: (0, 0), pipeline_mode=pl.Buffered(1)),
        ],
        out_specs=pl.BlockSpec((tm, D_PROJ), lambda i: (i, 0)),
        out_shape=jax.ShapeDtypeStruct((m, D_PROJ), F32),
        compiler_params=_params("parallel"),
        name="inproj",
    )(x2d, gain, w_t)


def _t5_bucket(dist):
    n = np.maximum(dist, 0)
    max_exact = REL_BUCKETS // 2
    large = max_exact + (np.log(np.maximum(n, 1) / max_exact) / np.log(REL_MAX_DIST / max_exact)
                         * (REL_BUCKETS - max_exact)).astype(np.int32)
    large = np.minimum(large, REL_BUCKETS - 1)
    return np.where(n < max_exact, n, large).astype(np.int32)


N_BAND = BLK + N_META


def _band_tables(nb):
    r = np.arange(BLK)[:, None]
    q = np.arange(BLK)[None, :]
    m = np.arange(N_META)[:, None]
    buckets, valids = [], []
    for n in range(nb):
        upper = r > q
        d_band = np.where(upper, q - r + BLK, q - r)
        d_meta = N_META + n * BLK + q - m
        assert (d_band[upper] < WINDOW).all() and (d_band >= 0).all() and (d_meta >= 0).all()
        buckets.append(_t5_bucket(np.concatenate([d_band, d_meta], axis=0)))
        valids.append(np.concatenate([~upper | (n > 0), np.ones((N_META, BLK), bool)], axis=0))
    for n in range(2, nb):
        assert (buckets[n] == buckets[1]).all() and (valids[n] == valids[1]).all()
    last = min(1, nb - 1)
    return np.stack([buckets[0], buckets[last]]), np.stack([valids[0], valids[last]])


ATTN_QB = 8


def _attn_kernel(n_qb, sink_ref, q_ref, kp_ref, ko_ref, km_ref, vp_ref, vo_ref, vmt_ref,
                 bias0_ref, bias1_ref, nw_ref, o_ref, yt_ref):
    upper = (lax.broadcasted_iota(jnp.int32, (BLK, BLK), 0)
             > lax.broadcasted_iota(jnp.int32, (BLK, BLK), 1))
    vmt = vmt_ref[...].astype(BF16)
    km = km_ref[...].astype(BF16)
    v_t = [vp_ref[...].T.astype(BF16)]
    k_b = [kp_ref[...].astype(BF16)]
    for s in range(n_qb):
        v_t.append(vo_ref[s * BLK:(s + 1) * BLK, :].T.astype(BF16))
        k_b.append(ko_ref[s * BLK:(s + 1) * BLK, :].astype(BF16))
    for s in range(n_qb):
        bias_ref = bias0_ref if s == 0 else bias1_ref
        qcols = slice(s * BLK, (s + 1) * BLK)
        q = (q_ref[qcols, :] * np.float32(HEAD_DIM ** -0.5)).astype(BF16)
        for j in range(ATTN_KV_HEADS):
            ks = slice(j * HEAD_DIM, (j + 1) * HEAD_DIM)
            heads = [j * ATTN_GROUP + g for g in range(ATTN_GROUP)]
            q4 = jnp.concatenate([q[:, h * HEAD_DIM:(h + 1) * HEAD_DIM] for h in heads], axis=0)
            lp = _dot_nt(k_b[s][:, ks], q4)
            lo = _dot_nt(k_b[s + 1][:, ks], q4)
            lm = _dot_nt(km[:, ks], q4)
            e_prev, e_own, e_meta, inv = [], [], [], []
            for g, h in enumerate(heads):
                cols = slice(g * BLK, (g + 1) * BLK)
                band = jnp.where(upper, lp[:, cols], lo[:, cols]) + bias_ref[0, h, 0:BLK, :]
                meta = lm[:, cols] + bias_ref[0, h, BLK:N_BAND, :]
                sink = sink_ref[h]
                mx = jnp.maximum(jnp.maximum(jnp.max(band, axis=0, keepdims=True),
                                             jnp.max(meta, axis=0, keepdims=True)), sink)
                eb = jnp.exp(band - mx)
                em = jnp.exp(meta - mx)
                denom = (jnp.sum(eb, axis=0, keepdims=True) + jnp.sum(em, axis=0, keepdims=True)
                         + jnp.exp(sink - mx))
                inv.append(1.0 / denom)
                e_prev.append(jnp.where(upper, eb, 0.0).astype(BF16))
                e_own.append(jnp.where(upper, 0.0, eb).astype(BF16))
                e_meta.append(em.astype(BF16))
            cat = lambda parts: jnp.concatenate(parts, axis=1)
            ot = (_dot(v_t[s][ks, :], cat(e_prev)) + _dot(v_t[s + 1][ks, :], cat(e_own))
                  + _dot(vmt[ks, :], cat(e_meta))) * cat(inv)
            for g, h in enumerate(heads):
                yt_ref[s, h * HEAD_DIM:(h + 1) * HEAD_DIM, :] = ot[:, g * BLK:(g + 1) * BLK]
        yt = yt_ref[s]
        ms = jnp.mean(yt * yt, axis=0, keepdims=True)
        o_ref[qcols, :] = ((yt * lax.rsqrt(ms + EPS)).T * nw_ref[...]).astype(o_ref.dtype)


def _attention(proj, proj_meta, sinks, bias, norm_w, batch, nb, ymix_dtype):
    rows = batch * nb * BLK
    kcol, vcol = COL_K // D_KV, COL_V // D_KV
    meta_blk = (BLK - N_META) // N_META
    vm_t = proj_meta[BLK - N_META:, COL_V:COL_V + D_KV].T

    n_qb = max(d for d in range(1, ATTN_QB + 1) if nb % d == 0)
    ns = nb // n_qb
    wide = n_qb * BLK

    def cur(col):
        return lambda b, n: (b * ns + n, col)

    def prev(col):
        return lambda b, n: (jnp.maximum((b * ns + n) * n_qb - 1, 0), col)

    bias_spec = lambda pick: pl.BlockSpec((1, ATTN_HEADS, N_BAND, BLK), lambda b, n: (pick(n), 0, 0, 0))
    return pl.pallas_call(
        functools.partial(_attn_kernel, n_qb),
        grid=(batch, ns),
        in_specs=[
            pl.BlockSpec(memory_space=pltpu.SMEM),
            pl.BlockSpec((wide, D_ATTN), cur(COL_Q // D_ATTN)),
            pl.BlockSpec((BLK, D_KV), prev(kcol)),
            pl.BlockSpec((wide, D_KV), cur(kcol)),
            pl.BlockSpec((N_META, D_KV), lambda b, n: (meta_blk, kcol)),
            pl.BlockSpec((BLK, D_KV), prev(vcol)),
            pl.BlockSpec((wide, D_KV), cur(vcol)),
            pl.BlockSpec((D_KV, N_META), lambda b, n: (0, 0)),
            bias_spec(lambda n: jnp.minimum(n, 1)),
            bias_spec(lambda n: 1),
            pl.BlockSpec((1, D_ATTN), lambda b, n: (0, 0)),
        ],
        out_specs=pl.BlockSpec((wide, D_ATTN), cur(0)),
        out_shape=jax.ShapeDtypeStruct((rows, D_ATTN), ymix_dtype),
        scratch_shapes=[pltpu.VMEM((n_qb, D_ATTN, BLK), F32)],
        compiler_params=_params("parallel", "arbitrary"),
        name="swa_attention",
    )(sinks, proj, proj, proj, proj_meta, proj, proj, vm_t, bias, bias, norm_w)


def _ssd_chunk(xs_raw, bc_raw, tail_xs, tail_bc, dt_raw, cw, cb, dtb, alog, row_mask, state):
    def conv(blk, tail, w, b):
        ext = jnp.concatenate([tail, blk], axis=0)
        acc = b + w[CONV_WIDTH - 1:CONV_WIDTH, :] * blk
        for back in range(1, CONV_WIDTH):
            shifted = pltpu.roll(ext, back, 0)[SUBLANES:, :]
            acc = acc + w[CONV_WIDTH - 1 - back:CONV_WIDTH - back, :] * shifted
        return _silu(acc)

    xs = conv(xs_raw, tail_xs, cw[:, :SSM_D_INNER], cb[:, :SSM_D_INNER])
    bc = conv(bc_raw, tail_bc, cw[:, SSM_D_INNER:], cb[:, SSM_D_INNER:])
    dt = _softplus(dt_raw + dtb)
    if row_mask is not None:
        xs = jnp.where(row_mask, xs, 0.0)
        bc = jnp.where(row_mask, bc, 0.0)
        dt = jnp.where(row_mask, dt, 0.0)
    a_neg = -jnp.exp(alog)
    d_a = dt * a_neg

    r = lax.broadcasted_iota(jnp.int32, (BLK, BLK), 0)
    c = lax.broadcasted_iota(jnp.int32, (BLK, BLK), 1)
    tri = r >= c
    cs = _dot01_left(tri.astype(BF16), d_a)
    cs_t = cs.T
    hh = lax.broadcasted_iota(jnp.int32, (BLK, SSM_D_INNER), 0)
    cc = lax.broadcasted_iota(jnp.int32, (BLK, SSM_D_INNER), 1)
    expand = (cc // SSM_HEAD_DIM == hh).astype(BF16)
    dt_rep = _dot01_right(dt, expand)
    ecs_rep = _dot01_right(jnp.exp(cs), expand)
    dec_rep = _dot01_right(jnp.exp(cs[BLK - 1:BLK, :] - cs), expand)

    xdt = xs * dt_rep
    xdtd = (xdt * dec_rep).astype(BF16)
    xdt_b = xdt.astype(BF16)
    chunk_decay = ecs_rep[BLK - 1:BLK, :]

    y_parts, new_state = [], []
    for g in range(SSM_GROUPS):
        b_g = bc[:, g * SSM_STATE:(g + 1) * SSM_STATE]
        c_g = bc[:, (SSM_GROUPS + g) * SSM_STATE:(SSM_GROUPS + g + 1) * SSM_STATE]
        cols = slice(g * D_GROUP, (g + 1) * D_GROUP)
        cb_g = _dot_nt(c_g.astype(BF16), b_g.astype(BF16))
        y_off = _dot(c_g.astype(BF16), state[g].astype(BF16)) * ecs_rep[:, cols]
        new_state.append(chunk_decay[:, cols] * state[g] + _dot(b_g.T.astype(BF16), xdtd[:, cols]))
        y_diag = []
        for hp in range(SSM_HPG):
            h = g * SSM_HPG + hp
            seg = cs[:, h:h + 1] - cs_t[h:h + 1, :]
            lmat = jnp.exp(jnp.where(tri, seg, -jnp.inf))
            m = (cb_g * lmat).astype(BF16)
            y_diag.append(_dot(m, xdt_b[:, h * SSM_HEAD_DIM:(h + 1) * SSM_HEAD_DIM]))
        y_parts.append(jnp.concatenate(y_diag, axis=1) + y_off)
    return jnp.concatenate(y_parts, axis=1), xs, new_state


def _halves(lo_ref, hi_ref):
    return jnp.concatenate([lo_ref[...], hi_ref[...]], axis=1)


def _ssd_meta_kernel(xs0_ref, xs1_ref, bc_ref, dt_ref, cw_ref, cb_ref, dtb_ref, alog_ref, st_ref):
    rows = lax.broadcasted_iota(jnp.int32, (BLK, 1), 0)
    zero_state = [jnp.zeros((SSM_STATE, D_GROUP), F32) for _ in range(SSM_GROUPS)]
    _, _, st = _ssd_chunk(_halves(xs0_ref, xs1_ref), bc_ref[...], jnp.zeros((SUBLANES, SSM_D_INNER), F32),
                          jnp.zeros((SUBLANES, D_BC), F32), dt_ref[...], cw_ref[...], cb_ref[...],
                          dtb_ref[...], alog_ref[...], rows >= BLK - N_META, zero_state)
    for g in range(SSM_GROUPS):
        st_ref[g] = st[g]


SSD_ROWS = 4


def _ssd_kernel(xs0_ref, xs1_ref, bc_ref, z0_ref, z1_ref, dt_ref, txs0_ref, txs1_ref, tbc_ref,
                mxs0_ref, mxs1_ref, mbc_ref, st0_ref,
                cw_ref, cb_ref, dtb_ref, alog_ref, dsk_ref, nw_ref, o_ref, st_ref):
    first = pl.program_id(1) == 0
    n_rows = o_ref.shape[0]

    @pl.when(first)
    def _():
        for r in range(n_rows):
            st_ref[r] = st0_ref[...]

    halves = lambda lo, hi, r: jnp.concatenate([lo[r], hi[r]], axis=1)
    for r in range(n_rows):
        tail_xs = jnp.where(first, _halves(mxs0_ref, mxs1_ref), halves(txs0_ref, txs1_ref, r))
        tail_bc = jnp.where(first, mbc_ref[...], tbc_ref[r])
        state = [st_ref[r, g] for g in range(SSM_GROUPS)]
        y, xs, new_state = _ssd_chunk(halves(xs0_ref, xs1_ref, r), bc_ref[r], tail_xs, tail_bc, dt_ref[r],
                                      cw_ref[...], cb_ref[...], dtb_ref[...], alog_ref[...], None, state)
        for g in range(SSM_GROUPS):
            st_ref[r, g] = new_state[g]
        y = y + xs * dsk_ref[...]
        yg = y * _silu(halves(z0_ref, z1_ref, r))
        outs = []
        for g in range(SSM_GROUPS):
            part = yg[:, g * D_GROUP:(g + 1) * D_GROUP]
            ms = jnp.mean(part * part, axis=-1, keepdims=True)
            outs.append(part * lax.rsqrt(ms + EPS))
        o_ref[r] = (jnp.concatenate(outs, axis=1) * nw_ref[...]).astype(o_ref.dtype)


def _pad_lanes(v, n=BLK):
    v = v.reshape(1, -1)
    return jnp.pad(v, ((0, 0), (0, n - v.shape[1])))


def _ssd(proj, proj_meta, conv_w, conv_b, dt_bias, a_log, d_skip, norm_w, batch, nc, ymix_dtype):
    rows = batch * nc * BLK
    cb = conv_b.reshape(1, D_XBC)
    dtb, alog = _pad_lanes(dt_bias), _pad_lanes(a_log)
    dsk = jnp.repeat(d_skip, SSM_HEAD_DIM).reshape(1, SSM_D_INNER)
    xs_c, z_c, bc_c, dt_c = COL_XS // HALF, COL_Z // HALF, COL_BC // D_BC, COL_DT // BLK
    full = lambda shape: pl.BlockSpec(shape, lambda *_: (0,) * len(shape))

    state0 = pl.pallas_call(
        _ssd_meta_kernel,
        grid=(1,),
        in_specs=[
            pl.BlockSpec((BLK, HALF), lambda i: (0, xs_c)),
            pl.BlockSpec((BLK, HALF), lambda i: (0, xs_c + 1)),
            pl.BlockSpec((BLK, D_BC), lambda i: (0, bc_c)),
            pl.BlockSpec((BLK, BLK), lambda i: (0, dt_c)),
            full((CONV_WIDTH, D_XBC)), full((1, D_XBC)), full((1, BLK)), full((1, BLK)),
        ],
        out_specs=full((SSM_GROUPS, SSM_STATE, D_GROUP)),
        out_shape=jax.ShapeDtypeStruct((SSM_GROUPS, SSM_STATE, D_GROUP), F32),
        compiler_params=_params("arbitrary"),
        name="ssd_meta_state",
    )(proj_meta, proj_meta, proj_meta, proj_meta, conv_w, cb, dtb, alog)

    n_rows = max(d for d in range(1, SSD_ROWS + 1) if batch % d == 0)
    proj3 = proj.reshape(batch, nc * BLK, D_PROJ)

    def cur(col, width):
        return pl.BlockSpec((n_rows, BLK, width), lambda b, c: (b, c, col))

    def tail(col, width):
        return pl.BlockSpec((n_rows, SUBLANES, width),
                            lambda b, c: (b, jnp.maximum(c * (BLK // SUBLANES) - 1, 0), col))

    def meta_tail(col, width):
        return pl.BlockSpec((SUBLANES, width), lambda b, c: (BLK // SUBLANES - 1, col))

    out = pl.pallas_call(
        _ssd_kernel,
        grid=(batch // n_rows, nc),
        in_specs=[
            cur(xs_c, HALF), cur(xs_c + 1, HALF), cur(bc_c, D_BC), cur(z_c, HALF), cur(z_c + 1, HALF),
            cur(dt_c, BLK),
            tail(xs_c, HALF), tail(xs_c + 1, HALF), tail(bc_c, D_BC),
            meta_tail(xs_c, HALF), meta_tail(xs_c + 1, HALF), meta_tail(bc_c, D_BC),
            full((SSM_GROUPS, SSM_STATE, D_GROUP)),
            full((CONV_WIDTH, D_XBC)), full((1, D_XBC)), full((1, BLK)), full((1, BLK)),
            full((1, SSM_D_INNER)), full((1, SSM_D_INNER)),
        ],
        out_specs=pl.BlockSpec((n_rows, BLK, SSM_D_INNER), lambda b, c: (b, c, 0)),
        out_shape=jax.ShapeDtypeStruct((batch, nc * BLK, SSM_D_INNER), ymix_dtype),
        scratch_shapes=[pltpu.VMEM((n_rows, SSM_GROUPS, SSM_STATE, D_GROUP), F32)],
        compiler_params=_params("parallel", "arbitrary"),
        name="ssd_mixer",
    )(proj3, proj3, proj3, proj3, proj3, proj3, proj3, proj3, proj3, proj_meta, proj_meta, proj_meta,
      state0, conv_w, cb, dtb, alog, dsk, norm_w)
    return out.reshape(rows, SSM_D_INNER)


def _outproj_kernel(ya_ref, ys_ref, x_ref, w_ref, g_ref, h_ref, xn_ref):
    y = jnp.concatenate([ya_ref[...], ys_ref[...]], axis=1).astype(BF16)
    h = x_ref[...] + _dot(y, w_ref[...].astype(BF16))
    h_ref[...] = h
    ms = jnp.mean(h * h, axis=-1, keepdims=True)
    xn_ref[...] = (h * lax.rsqrt(ms + EPS) * g_ref[...]).astype(BF16)


def _outproj(ya, ys, x2d, w_out, gain):
    m = x2d.shape[0]
    tm = min(m, ROW_TILE)
    return pl.pallas_call(
        _outproj_kernel,
        grid=(m // tm,),
        in_specs=[
            pl.BlockSpec((tm, D_ATTN), lambda i: (i, 0)),
            pl.BlockSpec((tm, SSM_D_INNER), lambda i: (i, 0)),
            pl.BlockSpec((tm, D_MODEL), lambda i: (i, 0)),
            pl.BlockSpec((D_MODEL, D_MODEL), lambda i: (0, 0), pipeline_mode=pl.Buffered(1)),
            pl.BlockSpec((1, D_MODEL), lambda i: (0, 0)),
        ],
        out_specs=[pl.BlockSpec((tm, D_MODEL), lambda i: (i, 0)),
                   pl.BlockSpec((tm, D_MODEL), lambda i: (i, 0))],
        out_shape=[jax.ShapeDtypeStruct((m, D_MODEL), F32),
                   jax.ShapeDtypeStruct((m, D_MODEL), BF16)],
        compiler_params=_params("parallel"),
        name="outproj",
    )(ya, ys, x2d, w_out, gain)


def _oddeven_sort_pairs(n):
    pairs = []
    p = 1
    while p < n:
        k = p
        while k >= 1:
            for j in range(k % p, n - k, 2 * k):
                for i in range(min(k, n - j - k)):
                    if (i + j) // (2 * p) == (i + j + k) // (2 * p):
                        pairs.append((i + j, i + j + k))
            k //= 2
        p *= 2
    return pairs


_SORT16 = _oddeven_sort_pairs(PEER_TOPK)


def _top16_tile(x):
    x = list(x)
    for i, j in _SORT16:
        x[i], x[j] = jnp.maximum(x[i], x[j]), jnp.minimum(x[i], x[j])
    for shift in (4, 2, 1):
        other = [pltpu.roll(v, shift, 0) for v in x]
        x = [jnp.maximum(x[r], other[PEER_TOPK - 1 - r]) for r in range(PEER_TOPK)]
        dist = PEER_TOPK // 2
        while dist >= 1:
            for i in range(PEER_TOPK):
                if i & dist == 0:
                    x[i], x[i + dist] = jnp.maximum(x[i], x[i + dist]), jnp.minimum(x[i], x[i + dist])
            dist //= 2
    return x


def _rank_in_top16(x, best):
    gt = lambda v: v > x
    pick = jnp.where
    b3 = gt(best[7])
    b2 = gt(pick(b3, best[11], best[3]))
    b1 = gt(pick(b3, pick(b2, best[13], best[9]), pick(b2, best[5], best[1])))
    b0 = gt(pick(b3, pick(b2, pick(b1, best[14], best[12]), pick(b1, best[10], best[8])),
                 pick(b2, pick(b1, best[6], best[4]), pick(b1, best[2], best[0]))))
    rank = pick(b3, 8.0, 0.0) + pick(b2, 4.0, 0.0) + pick(b1, 2.0, 0.0) + pick(b0, 1.0, 0.0)
    return pick(gt(best[PEER_TOPK - 1]), float(PEER_TOPK), rank)


def _route_kernel(xn_ref, wq_ref, keys_ref, n_ref, e1_ref, r2_ref, e2_ref, v_ref):
    q = _dot(xn_ref[...], wq_ref[...].astype(BF16)).astype(BF16)
    n_tiles = xn_ref.shape[0] // BLK
    for h in range(PEER_HEADS):
        s, tops = [], []
        for c in range(2):
            qs = q[:, (2 * h + c) * PEER_HALF:(2 * h + c + 1) * PEER_HALF]
            sc = _dot_nt(keys_ref[h, c], qs)
            s.append(sc)
            for lt in range(n_tiles):
                lanes = slice(lt * BLK, (lt + 1) * BLK)
                keys8 = [sc[SUBLANES * r:SUBLANES * (r + 1), lanes] for r in range(N_KEYS // SUBLANES)]
                best = _top16_tile(keys8)
                for r in range(PEER_TOPK):
                    v_ref[c, r:r + 1, lanes] = best[r][0:1, :]
                if c == 1:
                    ranks = [_rank_in_top16(x, best) for x in keys8]
                    r2 = jnp.concatenate(ranks, axis=0).astype(BF16)
                    e2 = jnp.exp(sc[:, lanes] - best[0][0:1, :]).astype(BF16)
                    r2_ref[h, lt] = pltpu.bitcast(r2, jnp.uint32)
                    e2_ref[h, lt] = pltpu.bitcast(e2, jnp.uint32)
            tops.append(v_ref[c])
        v1, v2 = tops
        blocks = [v1[0:1] + v2, v1[1:2] + v2[0:8]]
        blocks += [v1[a:a + 1] + v2[0:8] for a in range(2, 8)]
        blocks.append(v1[8:16] + v2[0:1])
        cand = jnp.concatenate(blocks, axis=0)
        top = v1[0:1] + v2[0:1]
        rem = cand
        for _ in range(PEER_TOPK - 1):
            m = jnp.max(rem, axis=0, keepdims=True)
            rem = jnp.where(rem == m, -jnp.inf, rem)
        tau = jnp.max(rem, axis=0, keepdims=True)
        z = jnp.sum(jnp.where(cand >= tau, jnp.exp(cand - top), 0.0), axis=0, keepdims=True)
        cnt = [jnp.sum(jnp.where(blocks[a] >= tau, 1.0, 0.0), axis=0, keepdims=True) for a in range(8)]
        cnt_hi = jnp.where(blocks[8] >= tau, 1.0, 0.0)
        n_sel = jnp.zeros_like(s[0])
        for a in range(PEER_TOPK):
            n_a = cnt[a] if a < 8 else cnt_hi[a - 8:a - 7]
            n_sel = jnp.where(s[0] == v1[a:a + 1], n_a, n_sel)
        n_ref[h] = n_sel
        e1_ref[h] = jnp.exp(s[0] - v1[0:1]) / z


def _route(xn, wq, keys):
    t = xn.shape[0]
    tm = min(t, 256)
    big = pl.BlockSpec((PEER_HEADS, N_KEYS, tm), lambda i: (0, 0, i))
    big_shape = jax.ShapeDtypeStruct((PEER_HEADS, N_KEYS, t), F32)
    tiled = pl.BlockSpec((PEER_HEADS, tm // BLK, N_KEYS // 2, BLK), lambda i: (0, i, 0, 0))
    tiled_shape = jax.ShapeDtypeStruct((PEER_HEADS, t // BLK, N_KEYS // 2, BLK), jnp.uint32)
    return pl.pallas_call(
        _route_kernel,
        grid=(t // tm,),
        in_specs=[
            pl.BlockSpec((tm, D_MODEL), lambda i: (i, 0)),
            pl.BlockSpec((D_MODEL, D_MODEL), lambda i: (0, 0), pipeline_mode=pl.Buffered(1)),
            pl.BlockSpec((PEER_HEADS, 2, N_KEYS, PEER_HALF), lambda i: (0, 0, 0, 0)),
        ],
        out_specs=[big, big, tiled, tiled],
        out_shape=[big_shape, big_shape, tiled_shape, tiled_shape],
        scratch_shapes=[pltpu.VMEM((2, PEER_TOPK, tm), F32)],
        compiler_params=_params("parallel"),
        name="peer_route",
    )(xn, wq, keys)


PEER_TB = 512
PEER_EB = 1024
PEER_JCH = 128


def _peer_kernel(xn_ref, u_ref, vt_ref, n_ref, e1_ref, r2_ref, e2_ref, o_ref, a_ref, w_ref):
    @pl.when(pl.program_id(1) == 0)
    def _():
        o_ref[...] = jnp.zeros_like(o_ref)

    a_ref[...] = _gelu_exact(_dot_nt(u_ref[...].astype(BF16), xn_ref[...])).astype(BF16)
    tb = xn_ref.shape[0]

    def per_key(ii, carry):
        row0 = pl.multiple_of(ii * N_KEYS, N_KEYS)
        n_rows = [n_ref[h, pl.ds(ii, 1), :] for h in range(PEER_HEADS)]
        e1_rows = [e1_ref[h, pl.ds(ii, 1), :] for h in range(PEER_HEADS)]
        for lt in range(tb // BLK):
            lanes = slice(lt * BLK, (lt + 1) * BLK)
            wide = lambda row: jnp.broadcast_to(row[:, lanes], (PEER_JCH, BLK)).astype(BF16)
            n_b = [wide(r) for r in n_rows]
            e1_b = [wide(r) for r in e1_rows]
            for j0 in range(0, N_KEYS, PEER_JCH):
                js = slice(j0 // 2, (j0 + PEER_JCH) // 2)
                acc = None
                for h in range(PEER_HEADS):
                    keep = pltpu.bitcast(r2_ref[h, lt, js, :], BF16) < n_b[h]
                    gate = jnp.where(keep, pltpu.bitcast(e2_ref[h, lt, js, :], BF16), 0.0) * e1_b[h]
                    acc = gate if acc is None else acc + gate
                rows = pl.ds(row0 + j0, PEER_JCH)
                w_ref[rows, lanes] = acc * a_ref[rows, lanes]
        return carry

    lax.fori_loop(0, PEER_EB // N_KEYS, per_key, 0)
    o_ref[...] += _dot(vt_ref[...], w_ref[...])


def _transpose_bf16_kernel(x_ref, o_ref):
    o_ref[...] = x_ref[...].T.astype(BF16)


def _transpose_bf16(x, rows=ROW_TILE):
    r, c = x.shape
    return pl.pallas_call(
        _transpose_bf16_kernel,
        grid=(r // rows,),
        in_specs=[pl.BlockSpec((rows, c), lambda i: (i, 0))],
        out_specs=pl.BlockSpec((c, rows), lambda i: (0, i)),
        out_shape=jax.ShapeDtypeStruct((c, r), BF16),
        compiler_params=_params("parallel"),
        name="transpose_v",
    )(x)


def _peer(xn, u, vt_b, n_sel, e1, r2, e2):
    t = xn.shape[0]
    tb = min(t, PEER_TB)
    assert tb == PEER_TB
    ni = PEER_EB // N_KEYS
    small = pl.BlockSpec((PEER_HEADS, ni, tb), lambda i, e: (0, e, i))
    big = pl.BlockSpec((PEER_HEADS, tb // BLK, N_KEYS // 2, BLK), lambda i, e: (0, i, 0, 0))
    return pl.pallas_call(
        _peer_kernel,
        grid=(t // tb, N_EXPERTS // PEER_EB),
        in_specs=[
            pl.BlockSpec((tb, D_MODEL), lambda i, e: (i, 0)),
            pl.BlockSpec((PEER_EB, D_MODEL), lambda i, e: (e, 0)),
            pl.BlockSpec((D_MODEL, PEER_EB), lambda i, e: (0, e)),
            small, small, big, big,
        ],
        out_specs=pl.BlockSpec((D_MODEL, tb), lambda i, e: (0, i)),
        out_shape=jax.ShapeDtypeStruct((D_MODEL, t), F32),
        scratch_shapes=[pltpu.VMEM((PEER_EB, tb), BF16), pltpu.VMEM((PEER_EB, tb), BF16)],
        compiler_params=_params("parallel", "arbitrary"),
        name="peer_experts",
    )(xn, u, vt_b, n_sel, e1, r2, e2)


def _final_kernel(h_ref, pt_ref, g_ref, o_ref):
    h = h_ref[...] + pt_ref[...].T
    ms = jnp.mean(h * h, axis=-1, keepdims=True)
    o_ref[...] = h * lax.rsqrt(ms + EPS) * g_ref[...]


def _final(h1, peer_t, gain):
    t = h1.shape[0]
    tm = min(t, ROW_TILE)
    rows = pl.BlockSpec((tm, D_MODEL), lambda i: (i, 0))
    return pl.pallas_call(
        _final_kernel,
        grid=(t // tm,),
        in_specs=[rows, pl.BlockSpec((D_MODEL, tm), lambda i: (0, i)),
                  pl.BlockSpec((1, D_MODEL), lambda i: (0, 0))],
        out_specs=rows,
        out_shape=jax.ShapeDtypeStruct((t, D_MODEL), F32),
        compiler_params=_params("parallel"),
        name="final_norm",
    )(h1, peer_t, gain)


def _mixer(x2d, batch, seq, meta_tokens, rel_bias, ln_mix, w_in, sinks, conv_w, conv_b, dt_bias,
           a_log, d_skip, attn_norm_w, ssm_norm_w):
    nb = seq // BLK
    gain = ln_mix.reshape(1, D_MODEL)
    meta_pad = jnp.concatenate([jnp.zeros((BLK - N_META, D_MODEL), F32), meta_tokens.astype(F32)], axis=0)
    w_t = w_in.T.astype(BF16)
    proj = _inproj(x2d, gain, w_t)
    proj_meta = _inproj(meta_pad, gain, w_t)

    bucket, valid = _band_tables(nb)
    tab = rel_bias.astype(F32)
    bias = jnp.full((2, ATTN_HEADS) + bucket.shape[1:], NEG, F32)
    for b in range(REL_BUCKETS):
        bias = jnp.where((valid & (bucket == b))[:, None], tab[b][None, :, None, None], bias)
    ya = _attention(proj, proj_meta, sinks.astype(F32), bias, attn_norm_w.reshape(1, D_ATTN),
                    batch, nb, BF16)
    ys = _ssd(proj, proj_meta, conv_w, conv_b, dt_bias, a_log, d_skip,
              ssm_norm_w.reshape(1, SSM_D_INNER), batch, nb, BF16)
    return ya, ys


def kernel(x, meta_tokens, rel_bias, ln_mix, w_in, attn_sinks, conv_w, conv_b, dt_bias, a_log, d_skip,
           attn_norm_w, ssm_norm_w, w_out, ln_ffn, peer_wq, peer_keys, peer_u, peer_v, ln_final):
    batch, seq, _ = x.shape
    x2d = x.reshape(batch * seq, D_MODEL)
    ya, ys = _mixer(x2d, batch, seq, meta_tokens, rel_bias, ln_mix[0], w_in[0], attn_sinks[0],
                         conv_w[0], conv_b[0], dt_bias[0], a_log[0], d_skip[0], attn_norm_w[0],
                         ssm_norm_w[0])
    h1, xn = _outproj(ya, ys, x2d, w_out[0], ln_ffn[0].reshape(1, D_MODEL))
    n_sel, e1, r2, e2 = _route(xn, peer_wq[0], peer_keys[0].astype(BF16))
    peer_t = _peer(xn, peer_u[0], _transpose_bf16(peer_v[0]), n_sel, e1, r2, e2)
    out = _final(h1, peer_t, ln_final.reshape(1, D_MODEL))
    return out.reshape(batch, seq, D_MODEL)
```

```python
import functools

import jax
import jax.numpy as jnp
import numpy as np
from jax import lax
from jax.experimental import pallas as pl
from jax.experimental.pallas import tpu as pltpu

F32 = jnp.float32
BF16 = jnp.bfloat16

D_MODEL = 2048
N_META = 16
HEAD_DIM = 64
D_ATTN = 1024
ATTN_HEADS = 16
ATTN_KV_HEADS = 4
ATTN_GROUP = 4
D_KV = 256
WINDOW = 128
BLK = 128
REL_BUCKETS = 32
REL_MAX_DIST = 128
SSM_D_INNER = 1024
SSM_HEAD_DIM = 64
SSM_HEADS = 16
SSM_GROUPS = 2
SSM_HPG = 8
SSM_STATE = 128
CONV_WIDTH = 4
D_XBC = 1536
D_BC = 2 * SSM_GROUPS * SSM_STATE
PEER_HEADS = 8
PEER_TOPK = 16
N_KEYS = 128
N_EXPERTS = N_KEYS * N_KEYS
PEER_HALF = 128
EPS = 1e-6
NEG = -1e30

COL_Q = 0
COL_K = 1024
COL_V = 1280
COL_Z = 1536
COL_XS = 2560
COL_BC = 3584
COL_DT = 4096
D_IN = 4112
D_PROJ = 4224
HALF = 512

D_GROUP = SSM_D_INNER // SSM_GROUPS

SUBLANES = 8
VMEM_LIMIT = 56 * 1024 * 1024
ROW_TILE = 512


def _params(*sem):
    return pltpu.CompilerParams(dimension_semantics=sem, vmem_limit_bytes=VMEM_LIMIT)


def _dot(a, b):
    return jnp.dot(a, b, preferred_element_type=F32)


def _dot_nt(a, b):
    return lax.dot_general(a, b, (((1,), (1,)), ((), ())), preferred_element_type=F32)


def _split3(x):
    hi = x.astype(BF16)
    r = x - hi.astype(F32)
    mid = r.astype(BF16)
    lo = (r - mid.astype(F32)).astype(BF16)
    return hi, mid, lo


def _dot01_left(m01, x):
    hi, mid, lo = _split3(x)
    return _dot(m01, hi) + _dot(m01, mid) + _dot(m01, lo)


def _dot01_right(x, m01):
    hi, mid, lo = _split3(x)
    return _dot(hi, m01) + _dot(mid, m01) + _dot(lo, m01)


def _silu(x):
    h = 0.5 * x
    return h + h * jnp.tanh(h)


def _softplus(x):
    return jnp.maximum(x, 0.0) + jnp.log1p(jnp.exp(-jnp.abs(x)))


def _gelu_exact(x):
    return 0.5 * x * (1.0 + lax.erf(x * np.float32(np.sqrt(0.5))))


def _inproj_kernel(x_ref, g_ref, w_ref, o_ref):
    x = x_ref[...]
    ms = jnp.mean(x * x, axis=-1, keepdims=True)
    xn = (x * lax.rsqrt(ms + EPS) * g_ref[...]).astype(BF16)
    o_ref[:, :D_IN] = _dot_nt(xn, w_ref[...])
    o_ref[:, D_IN:] = jnp.zeros((x.shape[0], D_PROJ - D_IN), F32)


def _inproj(x2d, gain, w_t):
    m = x2d.shape[0]
    tm = min(m, ROW_TILE)
    return pl.pallas_call(
        _inproj_kernel,
        grid=(m // tm,),
        in_specs=[
            pl.BlockSpec((tm, D_MODEL), lambda i: (i, 0)),
            pl.BlockSpec((1, D_MODEL), lambda i: (0, 0)),
            pl.BlockSpec((D_IN, D_MODEL), lambda i: (0, 0), pipeline_mode=pl.Buffered(1)),
        ],
        out_specs=pl.BlockSpec((tm, D_PROJ), lambda i: (i, 0)),
        out_shape=jax.ShapeDtypeStruct((m, D_PROJ), F32),
        compiler_params=_params("parallel"),
        name="inproj",
    )(x2d, gain, w_t)


def _t5_bucket(dist):
    n = np.maximum(dist, 0)
    max_exact = REL_BUCKETS // 2
    large = max_exact + (np.log(np.maximum(n, 1) / max_exact) / np.log(REL_MAX_DIST / max_exact)
                         * (REL_BUCKETS - max_exact)).astype(np.int32)
    large = np.minimum(large, REL_BUCKETS - 1)
    return np.where(n < max_exact, n, large).astype(np.int32)


N_BAND = BLK + N_META


def _band_tables(nb):
    r = np.arange(BLK)[:, None]
    q = np.arange(BLK)[None, :]
    m = np.arange(N_META)[:, None]
    buckets, valids = [], []
    for n in range(nb):
        upper = r > q
        d_band = np.where(upper, q - r + BLK, q - r)
        d_meta = N_META + n * BLK + q - m
        assert (d_band[upper] < WINDOW).all() and (d_band >= 0).all() and (d_meta >= 0).all()
        buckets.append(_t5_bucket(np.concatenate([d_band, d_meta], axis=0)))
        valids.append(np.concatenate([~upper | (n > 0), np.ones((N_META, BLK), bool)], axis=0))
    for n in range(2, nb):
        assert (buckets[n] == buckets[1]).all() and (valids[n] == valids[1]).all()
    last = min(1, nb - 1)
    return np.stack([buckets[0], buckets[last]]), np.stack([valids[0], valids[last]])


ATTN_QB = 8


def _attn_kernel(n_qb, sink_ref, q_ref, kp_ref, ko_ref, km_ref, vp_ref, vo_ref, vmt_ref,
                 bias0_ref, bias1_ref, nw_ref, o_ref, yt_ref):
    upper = (lax.broadcasted_iota(jnp.int32, (BLK, BLK), 0)
             > lax.broadcasted_iota(jnp.int32, (BLK, BLK), 1))
    vmt = vmt_ref[...].astype(BF16)
    km = km_ref[...].astype(BF16)
    v_t = [vp_ref[...].T.astype(BF16)]
    k_b = [kp_ref[...].astype(BF16)]
    for s in range(n_qb):
        v_t.append(vo_ref[s * BLK:(s + 1) * BLK, :].T.astype(BF16))
        k_b.append(ko_ref[s * BLK:(s + 1) * BLK, :].astype(BF16))
    for s in range(n_qb):
        bias_ref = bias0_ref if s == 0 else bias1_ref
        qcols = slice(s * BLK, (s + 1) * BLK)
        q = (q_ref[qcols, :] * np.float32(HEAD_DIM ** -0.5)).astype(BF16)
        for j in range(ATTN_KV_HEADS):
            ks = slice(j * HEAD_DIM, (j + 1) * HEAD_DIM)
            heads = [j * ATTN_GROUP + g for g in range(ATTN_GROUP)]
            q4 = jnp.concatenate([q[:, h * HEAD_DIM:(h + 1) * HEAD_DIM] for h in heads], axis=0)
            lp = _dot_nt(k_b[s][:, ks], q4)
            lo = _dot_nt(k_b[s + 1][:, ks], q4)
            lm = _dot_nt(km[:, ks], q4)
            e_prev, e_own, e_meta, inv = [], [], [], []
            for g, h in enumerate(heads):
                cols = slice(g * BLK, (g + 1) * BLK)
                band = jnp.where(upper, lp[:, cols], lo[:, cols]) + bias_ref[0, h, 0:BLK, :]
                meta = lm[:, cols] + bias_ref[0, h, BLK:N_BAND, :]
                sink = sink_ref[h]
                mx = jnp.maximum(jnp.maximum(jnp.max(band, axis=0, keepdims=True),
                                             jnp.max(meta, axis=0, keepdims=True)), sink)
                eb = jnp.exp(band - mx)
                em = jnp.exp(meta - mx)
                denom = (jnp.sum(eb, axis=0, keepdims=True) + jnp.sum(em, axis=0, keepdims=True)
                         + jnp.exp(sink - mx))
                inv.append(1.0 / denom)
                e_prev.append(jnp.where(upper, eb, 0.0).astype(BF16))
                e_own.append(jnp.where(upper, 0.0, eb).astype(BF16))
                e_meta.append(em.astype(BF16))
            cat = lambda parts: jnp.concatenate(parts, axis=1)
            ot = (_dot(v_t[s][ks, :], cat(e_prev)) + _dot(v_t[s + 1][ks, :], cat(e_own))
                  + _dot(vmt[ks, :], cat(e_meta))) * cat(inv)
            for g, h in enumerate(heads):
                yt_ref[h * HEAD_DIM:(h + 1) * HEAD_DIM, qcols] = ot[:, g * BLK:(g + 1) * BLK]
    yt = yt_ref[...]
    ms = jnp.mean(yt * yt, axis=0, keepdims=True)
    o_ref[...] = ((yt * lax.rsqrt(ms + EPS)).T * nw_ref[...]).astype(o_ref.dtype)


def _attention(proj, proj_meta, sinks, bias, norm_w, batch, nb, ymix_dtype):
    rows = batch * nb * BLK
    kcol, vcol = COL_K // D_KV, COL_V // D_KV
    meta_blk = (BLK - N_META) // N_META
    vm_t = proj_meta[BLK - N_META:, COL_V:COL_V + D_KV].T

    n_qb = max(d for d in range(1, ATTN_QB + 1) if nb % d == 0)
    ns = nb // n_qb
    wide = n_qb * BLK

    def cur(col):
        return lambda b, n: (b * ns + n, col)

    def prev(col):
        return lambda b, n: (jnp.maximum((b * ns + n) * n_qb - 1, 0), col)

    bias_spec = lambda pick: pl.BlockSpec((1, ATTN_HEADS, N_BAND, BLK), lambda b, n: (pick(n), 0, 0, 0))
    return pl.pallas_call(
        functools.partial(_attn_kernel, n_qb),
        grid=(batch, ns),
        in_specs=[
            pl.BlockSpec(memory_space=pltpu.SMEM),
            pl.BlockSpec((wide, D_ATTN), cur(COL_Q // D_ATTN)),
            pl.BlockSpec((BLK, D_KV), prev(kcol)),
            pl.BlockSpec((wide, D_KV), cur(kcol)),
            pl.BlockSpec((N_META, D_KV), lambda b, n: (meta_blk, kcol)),
            pl.BlockSpec((BLK, D_KV), prev(vcol)),
            pl.BlockSpec((wide, D_KV), cur(vcol)),
            pl.BlockSpec((D_KV, N_META), lambda b, n: (0, 0)),
            bias_spec(lambda n: jnp.minimum(n, 1)),
            bias_spec(lambda n: 1),
            pl.BlockSpec((1, D_ATTN), lambda b, n: (0, 0)),
        ],
        out_specs=pl.BlockSpec((wide, D_ATTN), cur(0)),
        out_shape=jax.ShapeDtypeStruct((rows, D_ATTN), ymix_dtype),
        scratch_shapes=[pltpu.VMEM((D_ATTN, wide), F32)],
        compiler_params=_params("parallel", "arbitrary"),
        name="swa_attention",
    )(sinks, proj, proj, proj, proj_meta, proj, proj, vm_t, bias, bias, norm_w)


def _ssd_chunk(xs_raw, bc_raw, tail_xs, tail_bc, dt_raw, cw, cb, dtb, alog, row_mask, state):
    def conv(blk, tail, w, b):
        ext = jnp.concatenate([tail, blk], axis=0)
        acc = b + w[CONV_WIDTH - 1:CONV_WIDTH, :] * blk
        for back in range(1, CONV_WIDTH):
            shifted = pltpu.roll(ext, back, 0)[SUBLANES:, :]
            acc = acc + w[CONV_WIDTH - 1 - back:CONV_WIDTH - back, :] * shifted
        return _silu(acc)

    xs = conv(xs_raw, tail_xs, cw[:, :SSM_D_INNER], cb[:, :SSM_D_INNER])
    bc = conv(bc_raw, tail_bc, cw[:, SSM_D_INNER:], cb[:, SSM_D_INNER:])
    dt = _softplus(dt_raw + dtb)
    if row_mask is not None:
        xs = jnp.where(row_mask, xs, 0.0)
        bc = jnp.where(row_mask, bc, 0.0)
        dt = jnp.where(row_mask, dt, 0.0)
    a_neg = -jnp.exp(alog)
    d_a = dt * a_neg

    r = lax.broadcasted_iota(jnp.int32, (BLK, BLK), 0)
    c = lax.broadcasted_iota(jnp.int32, (BLK, BLK), 1)
    tri = r >= c
    cs = _dot01_left(tri.astype(BF16), d_a)
    cs_t = cs.T
    hh = lax.broadcasted_iota(jnp.int32, (BLK, SSM_D_INNER), 0)
    cc = lax.broadcasted_iota(jnp.int32, (BLK, SSM_D_INNER), 1)
    expand = (cc // SSM_HEAD_DIM == hh).astype(BF16)
    dt_rep = _dot01_right(dt, expand)
    ecs_rep = _dot01_right(jnp.exp(cs), expand)
    dec_rep = _dot01_right(jnp.exp(cs[BLK - 1:BLK, :] - cs), expand)

    xdt = xs * dt_rep
    xdtd = (xdt * dec_rep).astype(BF16)
    xdt_b = xdt.astype(BF16)
    chunk_decay = ecs_rep[BLK - 1:BLK, :]

    y_parts, new_state = [], []
    for g in range(SSM_GROUPS):
        b_g = bc[:, g * SSM_STATE:(g + 1) * SSM_STATE]
        c_g = bc[:, (SSM_GROUPS + g) * SSM_STATE:(SSM_GROUPS + g + 1) * SSM_STATE]
        cols = slice(g * D_GROUP, (g + 1) * D_GROUP)
        cb_g = _dot_nt(c_g.astype(BF16), b_g.astype(BF16))
        y_off = _dot(c_g.astype(BF16), state[g].astype(BF16)) * ecs_rep[:, cols]
        new_state.append(chunk_decay[:, cols] * state[g] + _dot(b_g.T.astype(BF16), xdtd[:, cols]))
        y_diag = []
        for hp in range(SSM_HPG):
            h = g * SSM_HPG + hp
            seg = cs[:, h:h + 1] - cs_t[h:h + 1, :]
            lmat = jnp.exp(jnp.where(tri, seg, -jnp.inf))
            m = (cb_g * lmat).astype(BF16)
            y_diag.append(_dot(m, xdt_b[:, h * SSM_HEAD_DIM:(h + 1) * SSM_HEAD_DIM]))
        y_parts.append(jnp.concatenate(y_diag, axis=1) + y_off)
    return jnp.concatenate(y_parts, axis=1), xs, new_state


def _halves(lo_ref, hi_ref):
    return jnp.concatenate([lo_ref[...], hi_ref[...]], axis=1)


def _ssd_meta_kernel(xs0_ref, xs1_ref, bc_ref, dt_ref, cw_ref, cb_ref, dtb_ref, alog_ref, st_ref):
    rows = lax.broadcasted_iota(jnp.int32, (BLK, 1), 0)
    zero_state = [jnp.zeros((SSM_STATE, D_GROUP), F32) for _ in range(SSM_GROUPS)]
    _, _, st = _ssd_chunk(_halves(xs0_ref, xs1_ref), bc_ref[...], jnp.zeros((SUBLANES, SSM_D_INNER), F32),
                          jnp.zeros((SUBLANES, D_BC), F32), dt_ref[...], cw_ref[...], cb_ref[...],
                          dtb_ref[...], alog_ref[...], rows >= BLK - N_META, zero_state)
    for g in range(SSM_GROUPS):
        st_ref[g] = st[g]


SSD_ROWS = 4


def _ssd_kernel(xs0_ref, xs1_ref, bc_ref, z0_ref, z1_ref, dt_ref, txs0_ref, txs1_ref, tbc_ref,
                mxs0_ref, mxs1_ref, mbc_ref, st0_ref,
                cw_ref, cb_ref, dtb_ref, alog_ref, dsk_ref, nw_ref, o_ref, st_ref):
    first = pl.program_id(1) == 0
    n_rows = o_ref.shape[0]

    @pl.when(first)
    def _():
        for r in range(n_rows):
            st_ref[r] = st0_ref[...]

    halves = lambda lo, hi, r: jnp.concatenate([lo[r], hi[r]], axis=1)
    for r in range(n_rows):
        tail_xs = jnp.where(first, _halves(mxs0_ref, mxs1_ref), halves(txs0_ref, txs1_ref, r))
        tail_bc = jnp.where(first, mbc_ref[...], tbc_ref[r])
        state = [st_ref[r, g] for g in range(SSM_GROUPS)]
        y, xs, new_state = _ssd_chunk(halves(xs0_ref, xs1_ref, r), bc_ref[r], tail_xs, tail_bc, dt_ref[r],
                                      cw_ref[...], cb_ref[...], dtb_ref[...], alog_ref[...], None, state)
        for g in range(SSM_GROUPS):
            st_ref[r, g] = new_state[g]
        y = y + xs * dsk_ref[...]
        yg = y * _silu(halves(z0_ref, z1_ref, r))
        outs = []
        for g in range(SSM_GROUPS):
            part = yg[:, g * D_GROUP:(g + 1) * D_GROUP]
            ms = jnp.mean(part * part, axis=-1, keepdims=True)
            outs.append(part * lax.rsqrt(ms + EPS))
        o_ref[r] = (jnp.concatenate(outs, axis=1) * nw_ref[...]).astype(o_ref.dtype)


def _pad_lanes(v, n=BLK):
    v = v.reshape(1, -1)
    return jnp.pad(v, ((0, 0), (0, n - v.shape[1])))


def _ssd(proj, proj_meta, conv_w, conv_b, dt_bias, a_log, d_skip, norm_w, batch, nc, ymix_dtype):
    rows = batch * nc * BLK
    cb = conv_b.reshape(1, D_XBC)
    dtb, alog = _pad_lanes(dt_bias), _pad_lanes(a_log)
    dsk = jnp.repeat(d_skip, SSM_HEAD_DIM).reshape(1, SSM_D_INNER)
    xs_c, z_c, bc_c, dt_c = COL_XS // HALF, COL_Z // HALF, COL_BC // D_BC, COL_DT // BLK
    full = lambda shape: pl.BlockSpec(shape, lambda *_: (0,) * len(shape))

    state0 = pl.pallas_call(
        _ssd_meta_kernel,
        grid=(1,),
        in_specs=[
            pl.BlockSpec((BLK, HALF), lambda i: (0, xs_c)),
            pl.BlockSpec((BLK, HALF), lambda i: (0, xs_c + 1)),
            pl.BlockSpec((BLK, D_BC), lambda i: (0, bc_c)),
            pl.BlockSpec((BLK, BLK), lambda i: (0, dt_c)),
            full((CONV_WIDTH, D_XBC)), full((1, D_XBC)), full((1, BLK)), full((1, BLK)),
        ],
        out_specs=full((SSM_GROUPS, SSM_STATE, D_GROUP)),
        out_shape=jax.ShapeDtypeStruct((SSM_GROUPS, SSM_STATE, D_GROUP), F32),
        compiler_params=_params("arbitrary"),
        name="ssd_meta_state",
    )(proj_meta, proj_meta, proj_meta, proj_meta, conv_w, cb, dtb, alog)

    n_rows = max(d for d in range(1, SSD_ROWS + 1) if batch % d == 0)
    proj3 = proj.reshape(batch, nc * BLK, D_PROJ)

    def cur(col, width):
        return pl.BlockSpec((n_rows, BLK, width), lambda b, c: (b, c, col))

    def tail(col, width):
        return pl.BlockSpec((n_rows, SUBLANES, width),
                            lambda b, c: (b, jnp.maximum(c * (BLK // SUBLANES) - 1, 0), col))

    def meta_tail(col, width):
        return pl.BlockSpec((SUBLANES, width), lambda b, c: (BLK // SUBLANES - 1, col))

    out = pl.pallas_call(
        _ssd_kernel,
        grid=(batch // n_rows, nc),
        in_specs=[
            cur(xs_c, HALF), cur(xs_c + 1, HALF), cur(bc_c, D_BC), cur(z_c, HALF), cur(z_c + 1, HALF),
            cur(dt_c, BLK),
            tail(xs_c, HALF), tail(xs_c + 1, HALF), tail(bc_c, D_BC),
            meta_tail(xs_c, HALF), meta_tail(xs_c + 1, HALF), meta_tail(bc_c, D_BC),
            full((SSM_GROUPS, SSM_STATE, D_GROUP)),
            full((CONV_WIDTH, D_XBC)), full((1, D_XBC)), full((1, BLK)), full((1, BLK)),
            full((1, SSM_D_INNER)), full((1, SSM_D_INNER)),
        ],
        out_specs=pl.BlockSpec((n_rows, BLK, SSM_D_INNER), lambda b, c: (b, c, 0)),
        out_shape=jax.ShapeDtypeStruct((batch, nc * BLK, SSM_D_INNER), ymix_dtype),
        scratch_shapes=[pltpu.VMEM((n_rows, SSM_GROUPS, SSM_STATE, D_GROUP), F32)],
        compiler_params=_params("parallel", "arbitrary"),
        name="ssd_mixer",
    )(proj3, proj3, proj3, proj3, proj3, proj3, proj3, proj3, proj3, proj_meta, proj_meta, proj_meta,
      state0, conv_w, cb, dtb, alog, dsk, norm_w)
    return out.reshape(rows, SSM_D_INNER)


def _outproj_kernel(ya_ref, ys_ref, x_ref, w_ref, g_ref, h_ref, xn_ref):
    y = jnp.concatenate([ya_ref[...], ys_ref[...]], axis=1).astype(BF16)
    h = x_ref[...] + _dot(y, w_ref[...].astype(BF16))
    h_ref[...] = h
    ms = jnp.mean(h * h, axis=-1, keepdims=True)
    xn_ref[...] = (h * lax.rsqrt(ms + EPS) * g_ref[...]).astype(BF16)


def _outproj(ya, ys, x2d, w_out, gain):
    m = x2d.shape[0]
    tm = min(m, ROW_TILE)
    return pl.pallas_call(
        _outproj_kernel,
        grid=(m // tm,),
        in_specs=[
            pl.BlockSpec((tm, D_ATTN), lambda i: (i, 0)),
            pl.BlockSpec((tm, SSM_D_INNER), lambda i: (i, 0)),
            pl.BlockSpec((tm, D_MODEL), lambda i: (i, 0)),
            pl.BlockSpec((D_MODEL, D_MODEL), lambda i: (0, 0), pipeline_mode=pl.Buffered(1)),
            pl.BlockSpec((1, D_MODEL), lambda i: (0, 0)),
        ],
        out_specs=[pl.BlockSpec((tm, D_MODEL), lambda i: (i, 0)),
                   pl.BlockSpec((tm, D_MODEL), lambda i: (i, 0))],
        out_shape=[jax.ShapeDtypeStruct((m, D_MODEL), F32),
                   jax.ShapeDtypeStruct((m, D_MODEL), BF16)],
        compiler_params=_params("parallel"),
        name="outproj",
    )(ya, ys, x2d, w_out, gain)


def _oddeven_sort_pairs(n):
    pairs = []
    p = 1
    while p < n:
        k = p
        while k >= 1:
            for j in range(k % p, n - k, 2 * k):
                for i in range(min(k, n - j - k)):
                    if (i + j) // (2 * p) == (i + j + k) // (2 * p):
                        pairs.append((i + j, i + j + k))
            k //= 2
        p *= 2
    return pairs


_SORT16 = _oddeven_sort_pairs(PEER_TOPK)


def _top16_tile(x):
    x = list(x)
    for i, j in _SORT16:
        x[i], x[j] = jnp.maximum(x[i], x[j]), jnp.minimum(x[i], x[j])
    for shift in (4, 2, 1):
        other = [pltpu.roll(v, shift, 0) for v in x]
        x = [jnp.maximum(x[r], other[PEER_TOPK - 1 - r]) for r in range(PEER_TOPK)]
        dist = PEER_TOPK // 2
        while dist >= 1:
            for i in range(PEER_TOPK):
                if i & dist == 0:
                    x[i], x[i + dist] = jnp.maximum(x[i], x[i + dist]), jnp.minimum(x[i], x[i + dist])
            dist //= 2
    return x


def _rank_in_top16(x, best):
    gt = lambda v: v > x
    pick = jnp.where
    b3 = gt(best[7])
    b2 = gt(pick(b3, best[11], best[3]))
    b1 = gt(pick(b3, pick(b2, best[13], best[9]), pick(b2, best[5], best[1])))
    b0 = gt(pick(b3, pick(b2, pick(b1, best[14], best[12]), pick(b1, best[10], best[8])),
                 pick(b2, pick(b1, best[6], best[4]), pick(b1, best[2], best[0]))))
    rank = pick(b3, 8.0, 0.0) + pick(b2, 4.0, 0.0) + pick(b1, 2.0, 0.0) + pick(b0, 1.0, 0.0)
    return pick(gt(best[PEER_TOPK - 1]), float(PEER_TOPK), rank)


def _route_kernel(xn_ref, wq_ref, keys_ref, n_ref, e1_ref, r2_ref, e2_ref, v_ref):
    q = _dot(xn_ref[...], wq_ref[...].astype(BF16)).astype(BF16)
    n_tiles = xn_ref.shape[0] // BLK
    for h in range(PEER_HEADS):
        s, tops = [], []
        for c in range(2):
            qs = q[:, (2 * h + c) * PEER_HALF:(2 * h + c + 1) * PEER_HALF]
            sc = _dot_nt(keys_ref[h, c], qs)
            s.append(sc)
            for lt in range(n_tiles):
                lanes = slice(lt * BLK, (lt + 1) * BLK)
                keys8 = [sc[SUBLANES * r:SUBLANES * (r + 1), lanes] for r in range(N_KEYS // SUBLANES)]
                best = _top16_tile(keys8)
                for r in range(PEER_TOPK):
                    v_ref[c, r:r + 1, lanes] = best[r][0:1, :]
                if c == 1:
                    ranks = [_rank_in_top16(x, best) for x in keys8]
                    r2 = jnp.concatenate(ranks, axis=0).astype(BF16)
                    e2 = jnp.exp(sc[:, lanes] - best[0][0:1, :]).astype(BF16)
                    r2_ref[h, lt] = pltpu.bitcast(r2, jnp.uint32)
                    e2_ref[h, lt] = pltpu.bitcast(e2, jnp.uint32)
            tops.append(v_ref[c])
        v1, v2 = tops
        blocks = [v1[0:1] + v2, v1[1:2] + v2[0:8]]
        blocks += [v1[a:a + 1] + v2[0:8] for a in range(2, 8)]
        blocks.append(v1[8:16] + v2[0:1])
        cand = jnp.concatenate(blocks, axis=0)
        top = v1[0:1] + v2[0:1]
        rem = cand
        for _ in range(PEER_TOPK - 1):
            m = jnp.max(rem, axis=0, keepdims=True)
            rem = jnp.where(rem == m, -jnp.inf, rem)
        tau = jnp.max(rem, axis=0, keepdims=True)
        z = jnp.sum(jnp.where(cand >= tau, jnp.exp(cand - top), 0.0), axis=0, keepdims=True)
        cnt = [jnp.sum(jnp.where(blocks[a] >= tau, 1.0, 0.0), axis=0, keepdims=True) for a in range(8)]
        cnt_hi = jnp.where(blocks[8] >= tau, 1.0, 0.0)
        n_sel = jnp.zeros_like(s[0])
        for a in range(PEER_TOPK):
            n_a = cnt[a] if a < 8 else cnt_hi[a - 8:a - 7]
            n_sel = jnp.where(s[0] == v1[a:a + 1], n_a, n_sel)
        n_ref[h] = n_sel
        e1_ref[h] = jnp.exp(s[0] - v1[0:1]) / z


def _route(xn, wq, keys):
    t = xn.shape[0]
    tm = min(t, 256)
    big = pl.BlockSpec((PEER_HEADS, N_KEYS, tm), lambda i: (0, 0, i))
    big_shape = jax.ShapeDtypeStruct((PEER_HEADS, N_KEYS, t), F32)
    tiled = pl.BlockSpec((PEER_HEADS, tm // BLK, N_KEYS // 2, BLK), lambda i: (0, i, 0, 0))
    tiled_shape = jax.ShapeDtypeStruct((PEER_HEADS, t // BLK, N_KEYS // 2, BLK), jnp.uint32)
    return pl.pallas_call(
        _route_kernel,
        grid=(t // tm,),
        in_specs=[
            pl.BlockSpec((tm, D_MODEL), lambda i: (i, 0)),
            pl.BlockSpec((D_MODEL, D_MODEL), lambda i: (0, 0), pipeline_mode=pl.Buffered(1)),
            pl.BlockSpec((PEER_HEADS, 2, N_KEYS, PEER_HALF), lambda i: (0, 0, 0, 0)),
        ],
        out_specs=[big, big, tiled, tiled],
        out_shape=[big_shape, big_shape, tiled_shape, tiled_shape],
        scratch_shapes=[pltpu.VMEM((2, PEER_TOPK, tm), F32)],
        compiler_params=_params("parallel"),
        name="peer_route",
    )(xn, wq, keys)


PEER_TB = 512
PEER_EB = 1024
PEER_JCH = 128


def _peer_kernel(xn_ref, u_ref, vt_ref, n_ref, e1_ref, r2_ref, e2_ref, h_ref, g_ref, o_ref,
                 a_ref, w_ref, acc_ref):
    @pl.when(pl.program_id(1) == 0)
    def _():
        acc_ref[...] = jnp.zeros_like(acc_ref)

    a_ref[...] = _gelu_exact(_dot_nt(u_ref[...].astype(BF16), xn_ref[...])).astype(BF16)
    tb = xn_ref.shape[0]

    def per_key(ii, carry):
        row0 = pl.multiple_of(ii * N_KEYS, N_KEYS)
        n_rows = [n_ref[h, pl.ds(ii, 1), :] for h in range(PEER_HEADS)]
        e1_rows = [e1_ref[h, pl.ds(ii, 1), :] for h in range(PEER_HEADS)]
        for lt in range(tb // BLK):
            lanes = slice(lt * BLK, (lt + 1) * BLK)
            wide = lambda row: jnp.broadcast_to(row[:, lanes], (PEER_JCH, BLK)).astype(BF16)
            n_b = [wide(r) for r in n_rows]
            e1_b = [wide(r) for r in e1_rows]
            for j0 in range(0, N_KEYS, PEER_JCH):
                js = slice(j0 // 2, (j0 + PEER_JCH) // 2)
                acc = None
                for h in range(PEER_HEADS):
                    keep = pltpu.bitcast(r2_ref[h, lt, js, :], BF16) < n_b[h]
                    gate = jnp.where(keep, pltpu.bitcast(e2_ref[h, lt, js, :], BF16), 0.0) * e1_b[h]
                    acc = gate if acc is None else acc + gate
                rows = pl.ds(row0 + j0, PEER_JCH)
                w_ref[rows, lanes] = acc * a_ref[rows, lanes]
        return carry

    lax.fori_loop(0, PEER_EB // N_KEYS, per_key, 0)
    acc_ref[...] += _dot(vt_ref[...], w_ref[...])

    @pl.when(pl.program_id(1) == pl.num_programs(1) - 1)
    def _():
        h = h_ref[...] + acc_ref[...].T
        ms = jnp.mean(h * h, axis=-1, keepdims=True)
        o_ref[...] = h * lax.rsqrt(ms + EPS) * g_ref[...]


def _transpose_bf16_kernel(x_ref, o_ref):
    o_ref[...] = x_ref[...].T.astype(BF16)


def _transpose_bf16(x, rows=ROW_TILE):
    r, c = x.shape
    return pl.pallas_call(
        _transpose_bf16_kernel,
        grid=(r // rows,),
        in_specs=[pl.BlockSpec((rows, c), lambda i: (i, 0))],
        out_specs=pl.BlockSpec((c, rows), lambda i: (0, i)),
        out_shape=jax.ShapeDtypeStruct((c, r), BF16),
        compiler_params=_params("parallel"),
        name="transpose_v",
    )(x)


PEER_VMEM_LIMIT = 59 * 1024 * 1024


def _peer(xn, u, vt_b, n_sel, e1, r2, e2, h1, gain):
    t = xn.shape[0]
    tb = min(t, PEER_TB)
    assert tb == PEER_TB
    ni = PEER_EB // N_KEYS
    small = pl.BlockSpec((PEER_HEADS, ni, tb), lambda i, e: (0, e, i))
    big = pl.BlockSpec((PEER_HEADS, tb // BLK, N_KEYS // 2, BLK), lambda i, e: (0, i, 0, 0))
    return pl.pallas_call(
        _peer_kernel,
        grid=(t // tb, N_EXPERTS // PEER_EB),
        in_specs=[
            pl.BlockSpec((tb, D_MODEL), lambda i, e: (i, 0)),
            pl.BlockSpec((PEER_EB, D_MODEL), lambda i, e: (e, 0)),
            pl.BlockSpec((D_MODEL, PEER_EB), lambda i, e: (0, e)),
            small, small, big, big,
            pl.BlockSpec((tb, D_MODEL), lambda i, e: (i, 0), pipeline_mode=pl.Buffered(1)),
            pl.BlockSpec((1, D_MODEL), lambda i, e: (0, 0)),
        ],
        out_specs=pl.BlockSpec((tb, D_MODEL), lambda i, e: (i, 0)),
        out_shape=jax.ShapeDtypeStruct((t, D_MODEL), F32),
        scratch_shapes=[pltpu.VMEM((PEER_EB, tb), BF16), pltpu.VMEM((PEER_EB, tb), BF16),
                        pltpu.VMEM((D_MODEL, tb), F32)],
        compiler_params=pltpu.CompilerParams(dimension_semantics=("parallel", "arbitrary"),
                                             vmem_limit_bytes=PEER_VMEM_LIMIT),
        name="peer_experts",
    )(xn, u, vt_b, n_sel, e1, r2, e2, h1, gain)


def _final_kernel(h_ref, pt_ref, g_ref, o_ref):
    h = h_ref[...] + pt_ref[...].T
    ms = jnp.mean(h * h, axis=-1, keepdims=True)
    o_ref[...] = h * lax.rsqrt(ms + EPS) * g_ref[...]


def _final(h1, peer_t, gain):
    t = h1.shape[0]
    tm = min(t, ROW_TILE)
    rows = pl.BlockSpec((tm, D_MODEL), lambda i: (i, 0))
    return pl.pallas_call(
        _final_kernel,
        grid=(t // tm,),
        in_specs=[rows, pl.BlockSpec((D_MODEL, tm), lambda i: (0, i)),
                  pl.BlockSpec((1, D_MODEL), lambda i: (0, 0))],
        out_specs=rows,
        out_shape=jax.ShapeDtypeStruct((t, D_MODEL), F32),
        compiler_params=_params("parallel"),
        name="final_norm",
    )(h1, peer_t, gain)


def _mixer(x2d, batch, seq, meta_tokens, rel_bias, ln_mix, w_in, sinks, conv_w, conv_b, dt_bias,
           a_log, d_skip, attn_norm_w, ssm_norm_w):
    nb = seq // BLK
    gain = ln_mix.reshape(1, D_MODEL)
    meta_pad = jnp.concatenate([jnp.zeros((BLK - N_META, D_MODEL), F32), meta_tokens.astype(F32)], axis=0)
    w_t = w_in.T.astype(BF16)
    proj = _inproj(x2d, gain, w_t)
    proj_meta = _inproj(meta_pad, gain, w_t)

    bucket, valid = _band_tables(nb)
    tab = rel_bias.astype(F32)
    bias = jnp.full((2, ATTN_HEADS) + bucket.shape[1:], NEG, F32)
    for b in range(REL_BUCKETS):
        bias = jnp.where((valid & (bucket == b))[:, None], tab[b][None, :, None, None], bias)
    ya = _attention(proj, proj_meta, sinks.astype(F32), bias, attn_norm_w.reshape(1, D_ATTN),
                    batch, nb, BF16)
    ys = _ssd(proj, proj_meta, conv_w, conv_b, dt_bias, a_log, d_skip,
              ssm_norm_w.reshape(1, SSM_D_INNER), batch, nb, BF16)
    return ya, ys


def kernel(x, meta_tokens, rel_bias, ln_mix, w_in, attn_sinks, conv_w, conv_b, dt_bias, a_log, d_skip,
           attn_norm_w, ssm_norm_w, w_out, ln_ffn, peer_wq, peer_keys, peer_u, peer_v, ln_final):
    batch, seq, _ = x.shape
    x2d = x.reshape(batch * seq, D_MODEL)
    ya, ys = _mixer(x2d, batch, seq, meta_tokens, rel_bias, ln_mix[0], w_in[0], attn_sinks[0],
                         conv_w[0], conv_b[0], dt_bias[0], a_log[0], d_skip[0], attn_norm_w[0],
                         ssm_norm_w[0])
    h1, xn = _outproj(ya, ys, x2d, w_out[0], ln_ffn[0].reshape(1, D_MODEL))
    n_sel, e1, r2, e2 = _route(xn, peer_wq[0], peer_keys[0].astype(BF16))
    out = _peer(xn, peer_u[0], _transpose_bf16(peer_v[0]), n_sel, e1, r2, e2,
                h1, ln_final.reshape(1, D_MODEL))
    return out.reshape(batch, seq, D_MODEL)
```

```python
import functools

import jax
import jax.numpy as jnp
import numpy as np
from jax import lax
from jax.experimental import pallas as pl
from jax.experimental.pallas import tpu as pltpu

F32 = jnp.float32
BF16 = jnp.bfloat16

D_MODEL = 2048
N_META = 16
HEAD_DIM = 64
D_ATTN = 1024
ATTN_HEADS = 16
ATTN_KV_HEADS = 4
ATTN_GROUP = 4
D_KV = 256
WINDOW = 128
BLK = 128
REL_BUCKETS = 32
REL_MAX_DIST = 128
SSM_D_INNER = 1024
SSM_HEAD_DIM = 64
SSM_HEADS = 16
SSM_GROUPS = 2
SSM_HPG = 8
SSM_STATE = 128
CONV_WIDTH = 4
D_XBC = 1536
D_BC = 2 * SSM_GROUPS * SSM_STATE
PEER_HEADS = 8
PEER_TOPK = 16
N_KEYS = 128
N_EXPERTS = N_KEYS * N_KEYS
PEER_HALF = 128
EPS = 1e-6
NEG = -1e30

COL_Q = 0
COL_K = 1024
COL_V = 1280
COL_Z = 1536
COL_XS = 2560
COL_BC = 3584
COL_DT = 4096
D_IN = 4112
D_PROJ = 4224
HALF = 512

D_GROUP = SSM_D_INNER // SSM_GROUPS

SUBLANES = 8
VMEM_LIMIT = 56 * 1024 * 1024
ROW_TILE = 512


def _params(*sem):
    return pltpu.CompilerParams(dimension_semantics=sem, vmem_limit_bytes=VMEM_LIMIT)


def _dot(a, b):
    return jnp.dot(a, b, preferred_element_type=F32)


def _dot_nt(a, b):
    return lax.dot_general(a, b, (((1,), (1,)), ((), ())), preferred_element_type=F32)


def _split3(x):
    hi = x.astype(BF16)
    r = x - hi.astype(F32)
    mid = r.astype(BF16)
    lo = (r - mid.astype(F32)).astype(BF16)
    return hi, mid, lo


def _dot01_left(m01, x):
    hi, mid, lo = _split3(x)
    return _dot(m01, hi) + _dot(m01, mid) + _dot(m01, lo)


def _dot01_right(x, m01):
    hi, mid, lo = _split3(x)
    return _dot(hi, m01) + _dot(mid, m01) + _dot(lo, m01)


def _silu(x):
    h = 0.5 * x
    return h + h * jnp.tanh(h)


def _softplus(x):
    return jnp.maximum(x, 0.0) + jnp.log1p(jnp.exp(-jnp.abs(x)))


def _gelu_exact(x):
    return 0.5 * x * (1.0 + lax.erf(x * np.float32(np.sqrt(0.5))))


def _inproj_kernel(x_ref, g_ref, w_ref, o_ref):
    x = x_ref[...]
    ms = jnp.mean(x * x, axis=-1, keepdims=True)
    xn = (x * lax.rsqrt(ms + EPS) * g_ref[...]).astype(BF16)
    o_ref[:, :D_IN] = _dot_nt(xn, w_ref[...])
    o_ref[:, D_IN:] = jnp.zeros((x.shape[0], D_PROJ - D_IN), F32)


def _inproj(x2d, gain, w_t):
    m = x2d.shape[0]
    tm = min(m, ROW_TILE)
    return pl.pallas_call(
        _inproj_kernel,
        grid=(m // tm,),
        in_specs=[
            pl.BlockSpec((tm, D_MODEL), lambda i: (i, 0)),
            pl.BlockSpec((1, D_MODEL), lambda i: (0, 0)),
            pl.BlockSpec((D_IN, D_MODEL), lambda i: (0, 0), pipeline_mode=pl.Buffered(1)),
        ],
        out_specs=pl.BlockSpec((tm, D_PROJ), lambda i: (i, 0)),
        out_shape=jax.ShapeDtypeStruct((m, D_PROJ), F32),
        compiler_params=_params("parallel"),
        name="inproj",
    )(x2d, gain, w_t)


def _t5_bucket(dist):
    n = np.maximum(dist, 0)
    max_exact = REL_BUCKETS // 2
    large = max_exact + (np.log(np.maximum(n, 1) / max_exact) / np.log(REL_MAX_DIST / max_exact)
                         * (REL_BUCKETS - max_exact)).astype(np.int32)
    large = np.minimum(large, REL_BUCKETS - 1)
    return np.where(n < max_exact, n, large).astype(np.int32)


N_BAND = BLK + N_META


def _band_tables(nb):
    r = np.arange(BLK)[:, None]
    q = np.arange(BLK)[None, :]
    m = np.arange(N_META)[:, None]
    buckets, valids = [], []
    for n in range(nb):
        upper = r > q
        d_band = np.where(upper, q - r + BLK, q - r)
        d_meta = N_META + n * BLK + q - m
        assert (d_band[upper] < WINDOW).all() and (d_band >= 0).all() and (d_meta >= 0).all()
        buckets.append(_t5_bucket(np.concatenate([d_band, d_meta], axis=0)))
        valids.append(np.concatenate([~upper | (n > 0), np.ones((N_META, BLK), bool)], axis=0))
    for n in range(2, nb):
        assert (buckets[n] == buckets[1]).all() and (valids[n] == valids[1]).all()
    last = min(1, nb - 1)
    return np.stack([buckets[0], buckets[last]]), np.stack([valids[0], valids[last]])


ATTN_QB = 8


def _attn_kernel(n_qb, sink_ref, q_ref, kp_ref, ko_ref, km_ref, vp_ref, vo_ref, vmt_ref,
                 bias0_ref, bias1_ref, nw_ref, o_ref, yt_ref):
    upper = (lax.broadcasted_iota(jnp.int32, (BLK, BLK), 0)
             > lax.broadcasted_iota(jnp.int32, (BLK, BLK), 1))
    vmt = vmt_ref[...].astype(BF16)
    km = km_ref[...].astype(BF16)
    v_t = [vp_ref[...].T.astype(BF16)]
    k_b = [kp_ref[...].astype(BF16)]
    for s in range(n_qb):
        v_t.append(vo_ref[s * BLK:(s + 1) * BLK, :].T.astype(BF16))
        k_b.append(ko_ref[s * BLK:(s + 1) * BLK, :].astype(BF16))
    for s in range(n_qb):
        bias_ref = bias0_ref if s == 0 else bias1_ref
        qcols = slice(s * BLK, (s + 1) * BLK)
        q = (q_ref[qcols, :] * np.float32(HEAD_DIM ** -0.5)).astype(BF16)
        for j in range(ATTN_KV_HEADS):
            ks = slice(j * HEAD_DIM, (j + 1) * HEAD_DIM)
            heads = [j * ATTN_GROUP + g for g in range(ATTN_GROUP)]
            q4 = jnp.concatenate([q[:, h * HEAD_DIM:(h + 1) * HEAD_DIM] for h in heads], axis=0)
            lp = _dot_nt(k_b[s][:, ks], q4)
            lo = _dot_nt(k_b[s + 1][:, ks], q4)
            lm = _dot_nt(km[:, ks], q4)
            e_prev, e_own, e_meta, inv = [], [], [], []
            for g, h in enumerate(heads):
                cols = slice(g * BLK, (g + 1) * BLK)
                band = jnp.where(upper, lp[:, cols], lo[:, cols]) + bias_ref[0, h, 0:BLK, :]
                meta = lm[:, cols] + bias_ref[0, h, BLK:N_BAND, :]
                sink = sink_ref[h]
                mx = jnp.maximum(jnp.maximum(jnp.max(band, axis=0, keepdims=True),
                                             jnp.max(meta, axis=0, keepdims=True)), sink)
                eb = jnp.exp(band - mx)
                em = jnp.exp(meta - mx)
                denom = (jnp.sum(eb, axis=0, keepdims=True) + jnp.sum(em, axis=0, keepdims=True)
                         + jnp.exp(sink - mx))
                inv.append(1.0 / denom)
                e_prev.append(jnp.where(upper, eb, 0.0).astype(BF16))
                e_own.append(jnp.where(upper, 0.0, eb).astype(BF16))
                e_meta.append(em.astype(BF16))
            cat = lambda parts: jnp.concatenate(parts, axis=1)
            ot = (_dot(v_t[s][ks, :], cat(e_prev)) + _dot(v_t[s + 1][ks, :], cat(e_own))
                  + _dot(vmt[ks, :], cat(e_meta))) * cat(inv)
            for g, h in enumerate(heads):
                yt_ref[h * HEAD_DIM:(h + 1) * HEAD_DIM, qcols] = ot[:, g * BLK:(g + 1) * BLK]
    yt = yt_ref[...]
    ms = jnp.mean(yt * yt, axis=0, keepdims=True)
    o_ref[...] = ((yt * lax.rsqrt(ms + EPS)).T * nw_ref[...]).astype(o_ref.dtype)


def _attention(proj, proj_meta, sinks, bias, norm_w, batch, nb, ymix_dtype):
    rows = batch * nb * BLK
    kcol, vcol = COL_K // D_KV, COL_V // D_KV
    meta_blk = (BLK - N_META) // N_META
    vm_t = proj_meta[BLK - N_META:, COL_V:COL_V + D_KV].T

    n_qb = max(d for d in range(1, ATTN_QB + 1) if nb % d == 0)
    ns = nb // n_qb
    wide = n_qb * BLK

    def cur(col):
        return lambda b, n: (b * ns + n, col)

    def prev(col):
        return lambda b, n: (jnp.maximum((b * ns + n) * n_qb - 1, 0), col)

    bias_spec = lambda pick: pl.BlockSpec((1, ATTN_HEADS, N_BAND, BLK), lambda b, n: (pick(n), 0, 0, 0))
    return pl.pallas_call(
        functools.partial(_attn_kernel, n_qb),
        grid=(batch, ns),
        in_specs=[
            pl.BlockSpec(memory_space=pltpu.SMEM),
            pl.BlockSpec((wide, D_ATTN), cur(COL_Q // D_ATTN)),
            pl.BlockSpec((BLK, D_KV), prev(kcol)),
            pl.BlockSpec((wide, D_KV), cur(kcol)),
            pl.BlockSpec((N_META, D_KV), lambda b, n: (meta_blk, kcol)),
            pl.BlockSpec((BLK, D_KV), prev(vcol)),
            pl.BlockSpec((wide, D_KV), cur(vcol)),
            pl.BlockSpec((D_KV, N_META), lambda b, n: (0, 0)),
            bias_spec(lambda n: jnp.minimum(n, 1)),
            bias_spec(lambda n: 1),
            pl.BlockSpec((1, D_ATTN), lambda b, n: (0, 0)),
        ],
        out_specs=pl.BlockSpec((wide, D_ATTN), cur(0)),
        out_shape=jax.ShapeDtypeStruct((rows, D_ATTN), ymix_dtype),
        scratch_shapes=[pltpu.VMEM((D_ATTN, wide), F32)],
        compiler_params=_params("parallel", "arbitrary"),
        name="swa_attention",
    )(sinks, proj, proj, proj, proj_meta, proj, proj, vm_t, bias, bias, norm_w)


def _ssd_chunk(xs_raw, bc_raw, tail_xs, tail_bc, dt_raw, cw, cb, dtb, alog, row_mask, state):
    def conv(blk, tail, w, b):
        ext = jnp.concatenate([tail, blk], axis=0)
        acc = b + w[CONV_WIDTH - 1:CONV_WIDTH, :] * blk
        for back in range(1, CONV_WIDTH):
            shifted = pltpu.roll(ext, back, 0)[SUBLANES:, :]
            acc = acc + w[CONV_WIDTH - 1 - back:CONV_WIDTH - back, :] * shifted
        return _silu(acc)

    xs = conv(xs_raw, tail_xs, cw[:, :SSM_D_INNER], cb[:, :SSM_D_INNER])
    bc = conv(bc_raw, tail_bc, cw[:, SSM_D_INNER:], cb[:, SSM_D_INNER:])
    dt = _softplus(dt_raw + dtb)
    if row_mask is not None:
        xs = jnp.where(row_mask, xs, 0.0)
        bc = jnp.where(row_mask, bc, 0.0)
        dt = jnp.where(row_mask, dt, 0.0)
    a_neg = -jnp.exp(alog)
    d_a = dt * a_neg

    r = lax.broadcasted_iota(jnp.int32, (BLK, BLK), 0)
    c = lax.broadcasted_iota(jnp.int32, (BLK, BLK), 1)
    tri = r >= c
    cs = _dot01_left(tri.astype(BF16), d_a)
    cs_t = cs.T
    hh = lax.broadcasted_iota(jnp.int32, (BLK, SSM_D_INNER), 0)
    cc = lax.broadcasted_iota(jnp.int32, (BLK, SSM_D_INNER), 1)
    expand = (cc // SSM_HEAD_DIM == hh).astype(BF16)
    dt_rep = _dot01_right(dt, expand)
    ecs_rep = _dot01_right(jnp.exp(cs), expand)
    dec_rep = _dot01_right(jnp.exp(cs[BLK - 1:BLK, :] - cs), expand)

    xdt = xs * dt_rep
    xdtd = (xdt * dec_rep).astype(BF16)
    xdt_b = xdt.astype(BF16)
    chunk_decay = ecs_rep[BLK - 1:BLK, :]

    y_parts, new_state = [], []
    for g in range(SSM_GROUPS):
        b_g = bc[:, g * SSM_STATE:(g + 1) * SSM_STATE]
        c_g = bc[:, (SSM_GROUPS + g) * SSM_STATE:(SSM_GROUPS + g + 1) * SSM_STATE]
        cols = slice(g * D_GROUP, (g + 1) * D_GROUP)
        cb_g = _dot_nt(c_g.astype(BF16), b_g.astype(BF16))
        y_off = _dot(c_g.astype(BF16), state[g].astype(BF16)) * ecs_rep[:, cols]
        new_state.append(chunk_decay[:, cols] * state[g] + _dot(b_g.T.astype(BF16), xdtd[:, cols]))
        y_diag = []
        for hp in range(SSM_HPG):
            h = g * SSM_HPG + hp
            seg = cs[:, h:h + 1] - cs_t[h:h + 1, :]
            lmat = jnp.exp(jnp.where(tri, seg, -jnp.inf))
            m = (cb_g * lmat).astype(BF16)
            y_diag.append(_dot(m, xdt_b[:, h * SSM_HEAD_DIM:(h + 1) * SSM_HEAD_DIM]))
        y_parts.append(jnp.concatenate(y_diag, axis=1) + y_off)
    return jnp.concatenate(y_parts, axis=1), xs, new_state


def _halves(lo_ref, hi_ref):
    return jnp.concatenate([lo_ref[...], hi_ref[...]], axis=1)


def _ssd_meta_kernel(xs0_ref, xs1_ref, bc_ref, dt_ref, cw_ref, cb_ref, dtb_ref, alog_ref, st_ref):
    rows = lax.broadcasted_iota(jnp.int32, (BLK, 1), 0)
    zero_state = [jnp.zeros((SSM_STATE, D_GROUP), F32) for _ in range(SSM_GROUPS)]
    _, _, st = _ssd_chunk(_halves(xs0_ref, xs1_ref), bc_ref[...], jnp.zeros((SUBLANES, SSM_D_INNER), F32),
                          jnp.zeros((SUBLANES, D_BC), F32), dt_ref[...], cw_ref[...], cb_ref[...],
                          dtb_ref[...], alog_ref[...], rows >= BLK - N_META, zero_state)
    for g in range(SSM_GROUPS):
        st_ref[g] = st[g]


SSD_ROWS = 4


def _ssd_kernel(xs0_ref, xs1_ref, bc_ref, z0_ref, z1_ref, dt_ref, txs0_ref, txs1_ref, tbc_ref,
                mxs0_ref, mxs1_ref, mbc_ref, st0_ref,
                cw_ref, cb_ref, dtb_ref, alog_ref, dsk_ref, nw_ref, o_ref, st_ref):
    first = pl.program_id(1) == 0
    n_rows = o_ref.shape[0]

    @pl.when(first)
    def _():
        for r in range(n_rows):
            st_ref[r] = st0_ref[...]

    halves = lambda lo, hi, r: jnp.concatenate([lo[r], hi[r]], axis=1)
    for r in range(n_rows):
        tail_xs = jnp.where(first, _halves(mxs0_ref, mxs1_ref), halves(txs0_ref, txs1_ref, r))
        tail_bc = jnp.where(first, mbc_ref[...], tbc_ref[r])
        state = [st_ref[r, g] for g in range(SSM_GROUPS)]
        y, xs, new_state = _ssd_chunk(halves(xs0_ref, xs1_ref, r), bc_ref[r], tail_xs, tail_bc, dt_ref[r],
                                      cw_ref[...], cb_ref[...], dtb_ref[...], alog_ref[...], None, state)
        for g in range(SSM_GROUPS):
            st_ref[r, g] = new_state[g]
        y = y + xs * dsk_ref[...]
        yg = y * _silu(halves(z0_ref, z1_ref, r))
        outs = []
        for g in range(SSM_GROUPS):
            part = yg[:, g * D_GROUP:(g + 1) * D_GROUP]
            ms = jnp.mean(part * part, axis=-1, keepdims=True)
            outs.append(part * lax.rsqrt(ms + EPS))
        o_ref[r] = (jnp.concatenate(outs, axis=1) * nw_ref[...]).astype(o_ref.dtype)


def _pad_lanes(v, n=BLK):
    v = v.reshape(1, -1)
    return jnp.pad(v, ((0, 0), (0, n - v.shape[1])))


def _ssd(proj, proj_meta, conv_w, conv_b, dt_bias, a_log, d_skip, norm_w, batch, nc, ymix_dtype):
    rows = batch * nc * BLK
    cb = conv_b.reshape(1, D_XBC)
    dtb, alog = _pad_lanes(dt_bias), _pad_lanes(a_log)
    dsk = jnp.repeat(d_skip, SSM_HEAD_DIM).reshape(1, SSM_D_INNER)
    xs_c, z_c, bc_c, dt_c = COL_XS // HALF, COL_Z // HALF, COL_BC // D_BC, COL_DT // BLK
    full = lambda shape: pl.BlockSpec(shape, lambda *_: (0,) * len(shape))

    state0 = pl.pallas_call(
        _ssd_meta_kernel,
        grid=(1,),
        in_specs=[
            pl.BlockSpec((BLK, HALF), lambda i: (0, xs_c)),
            pl.BlockSpec((BLK, HALF), lambda i: (0, xs_c + 1)),
            pl.BlockSpec((BLK, D_BC), lambda i: (0, bc_c)),
            pl.BlockSpec((BLK, BLK), lambda i: (0, dt_c)),
            full((CONV_WIDTH, D_XBC)), full((1, D_XBC)), full((1, BLK)), full((1, BLK)),
        ],
        out_specs=full((SSM_GROUPS, SSM_STATE, D_GROUP)),
        out_shape=jax.ShapeDtypeStruct((SSM_GROUPS, SSM_STATE, D_GROUP), F32),
        compiler_params=_params("arbitrary"),
        name="ssd_meta_state",
    )(proj_meta, proj_meta, proj_meta, proj_meta, conv_w, cb, dtb, alog)

    n_rows = max(d for d in range(1, SSD_ROWS + 1) if batch % d == 0)
    proj3 = proj.reshape(batch, nc * BLK, D_PROJ)

    def cur(col, width):
        return pl.BlockSpec((n_rows, BLK, width), lambda b, c: (b, c, col))

    def tail(col, width):
        return pl.BlockSpec((n_rows, SUBLANES, width),
                            lambda b, c: (b, jnp.maximum(c * (BLK // SUBLANES) - 1, 0), col))

    def meta_tail(col, width):
        return pl.BlockSpec((SUBLANES, width), lambda b, c: (BLK // SUBLANES - 1, col))

    out = pl.pallas_call(
        _ssd_kernel,
        grid=(batch // n_rows, nc),
        in_specs=[
            cur(xs_c, HALF), cur(xs_c + 1, HALF), cur(bc_c, D_BC), cur(z_c, HALF), cur(z_c + 1, HALF),
            cur(dt_c, BLK),
            tail(xs_c, HALF), tail(xs_c + 1, HALF), tail(bc_c, D_BC),
            meta_tail(xs_c, HALF), meta_tail(xs_c + 1, HALF), meta_tail(bc_c, D_BC),
            full((SSM_GROUPS, SSM_STATE, D_GROUP)),
            full((CONV_WIDTH, D_XBC)), full((1, D_XBC)), full((1, BLK)), full((1, BLK)),
            full((1, SSM_D_INNER)), full((1, SSM_D_INNER)),
        ],
        out_specs=pl.BlockSpec((n_rows, BLK, SSM_D_INNER), lambda b, c: (b, c, 0)),
        out_shape=jax.ShapeDtypeStruct((batch, nc * BLK, SSM_D_INNER), ymix_dtype),
        scratch_shapes=[pltpu.VMEM((n_rows, SSM_GROUPS, SSM_STATE, D_GROUP), F32)],
        compiler_params=_params("parallel", "arbitrary"),
        name="ssd_mixer",
    )(proj3, proj3, proj3, proj3, proj3, proj3, proj3, proj3, proj3, proj_meta, proj_meta, proj_meta,
      state0, conv_w, cb, dtb, alog, dsk, norm_w)
    return out.reshape(rows, SSM_D_INNER)


def _outproj_kernel(ya_ref, ys_ref, x_ref, w_ref, g_ref, h_ref, xn_ref):
    y = jnp.concatenate([ya_ref[...], ys_ref[...]], axis=1).astype(BF16)
    h = x_ref[...] + _dot(y, w_ref[...].astype(BF16))
    h_ref[...] = h
    ms = jnp.mean(h * h, axis=-1, keepdims=True)
    xn_ref[...] = (h * lax.rsqrt(ms + EPS) * g_ref[...]).astype(BF16)


def _outproj(ya, ys, x2d, w_out, gain):
    m = x2d.shape[0]
    tm = min(m, ROW_TILE)
    return pl.pallas_call(
        _outproj_kernel,
        grid=(m // tm,),
        in_specs=[
            pl.BlockSpec((tm, D_ATTN), lambda i: (i, 0)),
            pl.BlockSpec((tm, SSM_D_INNER), lambda i: (i, 0)),
            pl.BlockSpec((tm, D_MODEL), lambda i: (i, 0)),
            pl.BlockSpec((D_MODEL, D_MODEL), lambda i: (0, 0), pipeline_mode=pl.Buffered(1)),
            pl.BlockSpec((1, D_MODEL), lambda i: (0, 0)),
        ],
        out_specs=[pl.BlockSpec((tm, D_MODEL), lambda i: (i, 0)),
                   pl.BlockSpec((tm, D_MODEL), lambda i: (i, 0))],
        out_shape=[jax.ShapeDtypeStruct((m, D_MODEL), F32),
                   jax.ShapeDtypeStruct((m, D_MODEL), BF16)],
        compiler_params=_params("parallel"),
        name="outproj",
    )(ya, ys, x2d, w_out, gain)


def _oddeven_sort_pairs(n):
    pairs = []
    p = 1
    while p < n:
        k = p
        while k >= 1:
            for j in range(k % p, n - k, 2 * k):
                for i in range(min(k, n - j - k)):
                    if (i + j) // (2 * p) == (i + j + k) // (2 * p):
                        pairs.append((i + j, i + j + k))
            k //= 2
        p *= 2
    return pairs


_SORT16 = _oddeven_sort_pairs(PEER_TOPK)


def _top16_tile(x):
    x = list(x)
    for i, j in _SORT16:
        x[i], x[j] = jnp.maximum(x[i], x[j]), jnp.minimum(x[i], x[j])
    for shift in (4, 2, 1):
        other = [pltpu.roll(v, shift, 0) for v in x]
        x = [jnp.maximum(x[r], other[PEER_TOPK - 1 - r]) for r in range(PEER_TOPK)]
        dist = PEER_TOPK // 2
        while dist >= 1:
            for i in range(PEER_TOPK):
                if i & dist == 0:
                    x[i], x[i + dist] = jnp.maximum(x[i], x[i + dist]), jnp.minimum(x[i], x[i + dist])
            dist //= 2
    return x


def _rank_in_top16(x, best):
    gt = lambda v: v > x
    pick = jnp.where
    b3 = gt(best[7])
    b2 = gt(pick(b3, best[11], best[3]))
    b1 = gt(pick(b3, pick(b2, best[13], best[9]), pick(b2, best[5], best[1])))
    b0 = gt(pick(b3, pick(b2, pick(b1, best[14], best[12]), pick(b1, best[10], best[8])),
                 pick(b2, pick(b1, best[6], best[4]), pick(b1, best[2], best[0]))))
    rank = pick(b3, 8.0, 0.0) + pick(b2, 4.0, 0.0) + pick(b1, 2.0, 0.0) + pick(b0, 1.0, 0.0)
    return pick(gt(best[PEER_TOPK - 1]), float(PEER_TOPK), rank)


def _route_kernel(xn_ref, wq_ref, keys_ref, n_ref, e1_ref, r2_ref, e2_ref, v_ref):
    q = _dot(xn_ref[...], wq_ref[...].astype(BF16)).astype(BF16)
    n_tiles = xn_ref.shape[0] // BLK
    for h in range(PEER_HEADS):
        s, tops = [], []
        for c in range(2):
            qs = q[:, (2 * h + c) * PEER_HALF:(2 * h + c + 1) * PEER_HALF]
            sc = _dot_nt(keys_ref[h, c], qs)
            s.append(sc)
            for lt in range(n_tiles):
                lanes = slice(lt * BLK, (lt + 1) * BLK)
                keys8 = [sc[SUBLANES * r:SUBLANES * (r + 1), lanes] for r in range(N_KEYS // SUBLANES)]
                best = _top16_tile(keys8)
                for r in range(PEER_TOPK):
                    v_ref[c, r:r + 1, lanes] = best[r][0:1, :]
                if c == 1:
                    ranks = [_rank_in_top16(x, best) for x in keys8]
                    r2 = jnp.concatenate(ranks, axis=0).astype(BF16)
                    e2 = jnp.exp(sc[:, lanes] - best[0][0:1, :]).astype(BF16)
                    r2_ref[h, lt] = pltpu.bitcast(r2, jnp.uint32)
                    e2_ref[h, lt] = pltpu.bitcast(e2, jnp.uint32)
            tops.append(v_ref[c])
        v1, v2 = tops
        blocks = [v1[0:1] + v2, v1[1:2] + v2[0:8]]
        blocks += [v1[a:a + 1] + v2[0:8] for a in range(2, 8)]
        blocks.append(v1[8:16] + v2[0:1])
        cand = jnp.concatenate(blocks, axis=0)
        top = v1[0:1] + v2[0:1]
        rem = cand
        for _ in range(PEER_TOPK - 1):
            m = jnp.max(rem, axis=0, keepdims=True)
            rem = jnp.where(rem == m, -jnp.inf, rem)
        tau = jnp.max(rem, axis=0, keepdims=True)
        z = jnp.sum(jnp.where(cand >= tau, jnp.exp(cand - top), 0.0), axis=0, keepdims=True)
        cnt = [jnp.sum(jnp.where(blocks[a] >= tau, 1.0, 0.0), axis=0, keepdims=True) for a in range(8)]
        cnt_hi = jnp.where(blocks[8] >= tau, 1.0, 0.0)
        n_sel = jnp.zeros_like(s[0])
        for a in range(PEER_TOPK):
            n_a = cnt[a] if a < 8 else cnt_hi[a - 8:a - 7]
            n_sel = jnp.where(s[0] == v1[a:a + 1], n_a, n_sel)
        n_ref[h] = n_sel
        e1_ref[h] = jnp.exp(s[0] - v1[0:1]) / z


def _route(xn, wq, keys):
    t = xn.shape[0]
    tm = min(t, 256)
    big = pl.BlockSpec((PEER_HEADS, N_KEYS, tm), lambda i: (0, 0, i))
    big_shape = jax.ShapeDtypeStruct((PEER_HEADS, N_KEYS, t), F32)
    tiled = pl.BlockSpec((PEER_HEADS, tm // BLK, N_KEYS // 2, BLK), lambda i: (0, i, 0, 0))
    tiled_shape = jax.ShapeDtypeStruct((PEER_HEADS, t // BLK, N_KEYS // 2, BLK), jnp.uint32)
    return pl.pallas_call(
        _route_kernel,
        grid=(t // tm,),
        in_specs=[
            pl.BlockSpec((tm, D_MODEL), lambda i: (i, 0)),
            pl.BlockSpec((D_MODEL, D_MODEL), lambda i: (0, 0), pipeline_mode=pl.Buffered(1)),
            pl.BlockSpec((PEER_HEADS, 2, N_KEYS, PEER_HALF), lambda i: (0, 0, 0, 0)),
        ],
        out_specs=[big, big, tiled, tiled],
        out_shape=[big_shape, big_shape, tiled_shape, tiled_shape],
        scratch_shapes=[pltpu.VMEM((2, PEER_TOPK, tm), F32)],
        compiler_params=_params("parallel"),
        name="peer_route",
    )(xn, wq, keys)


PEER_TB = 512
PEER_EB = 1024
PEER_JCH = 128


def _peer_kernel(xn_ref, u_ref, n_ref, e1_ref, r2_ref, e2_ref, w_ref, a_ref):
    a_ref[...] = _gelu_exact(_dot_nt(u_ref[...].astype(BF16), xn_ref[...])).astype(BF16)
    tb = xn_ref.shape[0]

    def per_key(ii, carry):
        row0 = pl.multiple_of(ii * N_KEYS, N_KEYS)
        n_rows = [n_ref[h, pl.ds(ii, 1), :] for h in range(PEER_HEADS)]
        e1_rows = [e1_ref[h, pl.ds(ii, 1), :] for h in range(PEER_HEADS)]
        for lt in range(tb // BLK):
            lanes = slice(lt * BLK, (lt + 1) * BLK)
            wide = lambda row: jnp.broadcast_to(row[:, lanes], (PEER_JCH, BLK)).astype(BF16)
            n_b = [wide(r) for r in n_rows]
            e1_b = [wide(r) for r in e1_rows]
            for j0 in range(0, N_KEYS, PEER_JCH):
                js = slice(j0 // 2, (j0 + PEER_JCH) // 2)
                acc = None
                for h in range(PEER_HEADS):
                    keep = pltpu.bitcast(r2_ref[h, lt, js, :], BF16) < n_b[h]
                    gate = jnp.where(keep, pltpu.bitcast(e2_ref[h, lt, js, :], BF16), 0.0) * e1_b[h]
                    acc = gate if acc is None else acc + gate
                rows = pl.ds(row0 + j0, PEER_JCH)
                w_ref[rows, lanes] = acc * a_ref[rows, lanes]
        return carry

    lax.fori_loop(0, PEER_EB // N_KEYS, per_key, 0)


PEER_DOWN_TB = 1024


def _peer_down_kernel(vt_ref, w_ref, o_ref):
    @pl.when(pl.program_id(1) == 0)
    def _():
        o_ref[...] = jnp.zeros_like(o_ref)

    o_ref[...] += _dot(vt_ref[...], w_ref[...])


def _transpose_bf16_kernel(x_ref, o_ref):
    o_ref[...] = x_ref[...].T.astype(BF16)


def _transpose_bf16(x, rows=ROW_TILE):
    r, c = x.shape
    return pl.pallas_call(
        _transpose_bf16_kernel,
        grid=(r // rows,),
        in_specs=[pl.BlockSpec((rows, c), lambda i: (i, 0))],
        out_specs=pl.BlockSpec((c, rows), lambda i: (0, i)),
        out_shape=jax.ShapeDtypeStruct((c, r), BF16),
        compiler_params=_params("parallel"),
        name="transpose_v",
    )(x)


def _peer(xn, u, vt_b, n_sel, e1, r2, e2):
    t = xn.shape[0]
    tb = min(t, PEER_TB)
    assert tb == PEER_TB
    ni = PEER_EB // N_KEYS
    small = pl.BlockSpec((PEER_HEADS, ni, tb), lambda i, e: (0, e, i))
    big = pl.BlockSpec((PEER_HEADS, tb // BLK, N_KEYS // 2, BLK), lambda i, e: (0, i, 0, 0))
    w = pl.pallas_call(
        _peer_kernel,
        grid=(t // tb, N_EXPERTS // PEER_EB),
        in_specs=[
            pl.BlockSpec((tb, D_MODEL), lambda i, e: (i, 0)),
            pl.BlockSpec((PEER_EB, D_MODEL), lambda i, e: (e, 0)),
            small, small, big, big,
        ],
        out_specs=pl.BlockSpec((PEER_EB, tb), lambda i, e: (e, i)),
        out_shape=jax.ShapeDtypeStruct((N_EXPERTS, t), BF16),
        scratch_shapes=[pltpu.VMEM((PEER_EB, tb), BF16)],
        compiler_params=_params("parallel", "arbitrary"),
        name="peer_experts",
    )(xn, u, n_sel, e1, r2, e2)
    td = min(t, PEER_DOWN_TB)
    return pl.pallas_call(
        _peer_down_kernel,
        grid=(t // td, N_EXPERTS // PEER_EB),
        in_specs=[
            pl.BlockSpec((D_MODEL, PEER_EB), lambda i, e: (0, e)),
            pl.BlockSpec((PEER_EB, td), lambda i, e: (e, i)),
        ],
        out_specs=pl.BlockSpec((D_MODEL, td), lambda i, e: (0, i)),
        out_shape=jax.ShapeDtypeStruct((D_MODEL, t), F32),
        compiler_params=_params("parallel", "arbitrary"),
        name="peer_down",
    )(vt_b, w)


def _final_kernel(h_ref, pt_ref, g_ref, o_ref):
    h = h_ref[...] + pt_ref[...].T
    ms = jnp.mean(h * h, axis=-1, keepdims=True)
    o_ref[...] = h * lax.rsqrt(ms + EPS) * g_ref[...]


def _final(h1, peer_t, gain):
    t = h1.shape[0]
    tm = min(t, ROW_TILE)
    rows = pl.BlockSpec((tm, D_MODEL), lambda i: (i, 0))
    return pl.pallas_call(
        _final_kernel,
        grid=(t // tm,),
        in_specs=[rows, pl.BlockSpec((D_MODEL, tm), lambda i: (0, i)),
                  pl.BlockSpec((1, D_MODEL), lambda i: (0, 0))],
        out_specs=rows,
        out_shape=jax.ShapeDtypeStruct((t, D_MODEL), F32),
        compiler_params=_params("parallel"),
        name="final_norm",
    )(h1, peer_t, gain)


def _mixer(x2d, batch, seq, meta_tokens, rel_bias, ln_mix, w_in, sinks, conv_w, conv_b, dt_bias,
           a_log, d_skip, attn_norm_w, ssm_norm_w):
    nb = seq // BLK
    gain = ln_mix.reshape(1, D_MODEL)
    meta_pad = jnp.concatenate([jnp.zeros((BLK - N_META, D_MODEL), F32), meta_tokens.astype(F32)], axis=0)
    w_t = w_in.T.astype(BF16)
    proj = _inproj(x2d, gain, w_t)
    proj_meta = _inproj(meta_pad, gain, w_t)

    bucket, valid = _band_tables(nb)
    tab = rel_bias.astype(F32)
    bias = jnp.full((2, ATTN_HEADS) + bucket.shape[1:], NEG, F32)
    for b in range(REL_BUCKETS):
        bias = jnp.where((valid & (bucket == b))[:, None], tab[b][None, :, None, None], bias)
    ya = _attention(proj, proj_meta, sinks.astype(F32), bias, attn_norm_w.reshape(1, D_ATTN),
                    batch, nb, BF16)
    ys = _ssd(proj, proj_meta, conv_w, conv_b, dt_bias, a_log, d_skip,
              ssm_norm_w.reshape(1, SSM_D_INNER), batch, nb, BF16)
    return ya, ys


def kernel(x, meta_tokens, rel_bias, ln_mix, w_in, attn_sinks, conv_w, conv_b, dt_bias, a_log, d_skip,
           attn_norm_w, ssm_norm_w, w_out, ln_ffn, peer_wq, peer_keys, peer_u, peer_v, ln_final):
    batch, seq, _ = x.shape
    x2d = x.reshape(batch * seq, D_MODEL)
    ya, ys = _mixer(x2d, batch, seq, meta_tokens, rel_bias, ln_mix[0], w_in[0], attn_sinks[0],
                         conv_w[0], conv_b[0], dt_bias[0], a_log[0], d_skip[0], attn_norm_w[0],
                         ssm_norm_w[0])
    h1, xn = _outproj(ya, ys, x2d, w_out[0], ln_ffn[0].reshape(1, D_MODEL))
    n_sel, e1, r2, e2 = _route(xn, peer_wq[0], peer_keys[0].astype(BF16))
    peer_t = _peer(xn, peer_u[0], _transpose_bf16(peer_v[0]), n_sel, e1, r2, e2)
    out = _final(h1, peer_t, ln_final.reshape(1, D_MODEL))
    return out.reshape(batch, seq, D_MODEL)
```

```python
import functools

import jax
import jax.numpy as jnp
import numpy as np
from jax import lax
from jax.experimental import pallas as pl
from jax.experimental.pallas import tpu as pltpu

F32 = jnp.float32
BF16 = jnp.bfloat16

D_MODEL = 2048
N_META = 16
HEAD_DIM = 64
D_ATTN = 1024
ATTN_HEADS = 16
ATTN_KV_HEADS = 4
ATTN_GROUP = 4
D_KV = 256
WINDOW = 128
BLK = 128
REL_BUCKETS = 32
REL_MAX_DIST = 128
SSM_D_INNER = 1024
SSM_HEAD_DIM = 64
SSM_HEADS = 16
SSM_GROUPS = 2
SSM_HPG = 8
SSM_STATE = 128
CONV_WIDTH = 4
D_XBC = 1536
D_BC = 2 * SSM_GROUPS * SSM_STATE
PEER_HEADS = 8
PEER_TOPK = 16
N_KEYS = 128
N_EXPERTS = N_KEYS * N_KEYS
PEER_HALF = 128
EPS = 1e-6
NEG = -1e30

COL_Q = 0
COL_K = 1024
COL_V = 1280
COL_Z = 1536
COL_XS = 2560
COL_BC = 3584
COL_DT = 4096
D_IN = 4112
D_PROJ = 4224
HALF = 512

D_GROUP = SSM_D_INNER // SSM_GROUPS

SUBLANES = 8
VMEM_LIMIT = 56 * 1024 * 1024
ROW_TILE = 512


def _params(*sem):
    return pltpu.CompilerParams(dimension_semantics=sem, vmem_limit_bytes=VMEM_LIMIT)


def _dot(a, b):
    return jnp.dot(a, b, preferred_element_type=F32)


def _dot_nt(a, b):
    return lax.dot_general(a, b, (((1,), (1,)), ((), ())), preferred_element_type=F32)


def _split3(x):
    hi = x.astype(BF16)
    r = x - hi.astype(F32)
    mid = r.astype(BF16)
    lo = (r - mid.astype(F32)).astype(BF16)
    return hi, mid, lo


def _dot01_left(m01, x):
    hi, mid, lo = _split3(x)
    return _dot(m01, hi) + _dot(m01, mid) + _dot(m01, lo)


def _dot01_right(x, m01):
    hi, mid, lo = _split3(x)
    return _dot(hi, m01) + _dot(mid, m01) + _dot(lo, m01)


def _silu(x):
    h = 0.5 * x
    return h + h * jnp.tanh(h)


def _softplus(x):
    return jnp.maximum(x, 0.0) + jnp.log1p(jnp.exp(-jnp.abs(x)))


def _gelu_exact(x):
    return 0.5 * x * (1.0 + lax.erf(x * np.float32(np.sqrt(0.5))))


def _inproj_kernel(x_ref, g_ref, w_ref, o_ref):
    x = x_ref[...]
    ms = jnp.mean(x * x, axis=-1, keepdims=True)
    xn = (x * lax.rsqrt(ms + EPS) * g_ref[...]).astype(BF16)
    o_ref[:, :D_IN] = _dot_nt(xn, w_ref[...])
    o_ref[:, D_IN:] = jnp.zeros((x.shape[0], D_PROJ - D_IN), F32)


def _inproj(x2d, gain, w_t):
    m = x2d.shape[0]
    tm = min(m, ROW_TILE)
    return pl.pallas_call(
        _inproj_kernel,
        grid=(m // tm,),
        in_specs=[
            pl.BlockSpec((tm, D_MODEL), lambda i: (i, 0)),
            pl.BlockSpec((1, D_MODEL), lambda i: (0, 0)),
            pl.BlockSpec((D_IN, D_MODEL), lambda i: (0, 0), pipeline_mode=pl.Buffered(1)),
        ],
        out_specs=pl.BlockSpec((tm, D_PROJ), lambda i: (i, 0)),
        out_shape=jax.ShapeDtypeStruct((m, D_PROJ), F32),
        compiler_params=_params("parallel"),
        name="inproj",
    )(x2d, gain, w_t)


def _t5_bucket(dist):
    n = np.maximum(dist, 0)
    max_exact = REL_BUCKETS // 2
    large = max_exact + (np.log(np.maximum(n, 1) / max_exact) / np.log(REL_MAX_DIST / max_exact)
                         * (REL_BUCKETS - max_exact)).astype(np.int32)
    large = np.minimum(large, REL_BUCKETS - 1)
    return np.where(n < max_exact, n, large).astype(np.int32)


N_BAND = BLK + N_META


def _band_tables(nb):
    r = np.arange(BLK)[:, None]
    q = np.arange(BLK)[None, :]
    m = np.arange(N_META)[:, None]
    buckets, valids = [], []
    for n in range(nb):
        upper = r > q
        d_band = np.where(upper, q - r + BLK, q - r)
        d_meta = N_META + n * BLK + q - m
        assert (d_band[upper] < WINDOW).all() and (d_band >= 0).all() and (d_meta >= 0).all()
        buckets.append(_t5_bucket(np.concatenate([d_band, d_meta], axis=0)))
        valids.append(np.concatenate([~upper | (n > 0), np.ones((N_META, BLK), bool)], axis=0))
    for n in range(2, nb):
        assert (buckets[n] == buckets[1]).all() and (valids[n] == valids[1]).all()
    last = min(1, nb - 1)
    return np.stack([buckets[0], buckets[last]]), np.stack([valids[0], valids[last]])


ATTN_QB = 8


def _attn_kernel(n_qb, sink_ref, q_ref, kp_ref, ko_ref, km_ref, vp_ref, vo_ref, vmt_ref,
                 bias0_ref, bias1_ref, nw_ref, o_ref, yt_ref):
    upper = (lax.broadcasted_iota(jnp.int32, (BLK, BLK), 0)
             > lax.broadcasted_iota(jnp.int32, (BLK, BLK), 1))
    vmt = vmt_ref[...].astype(BF16)
    km = km_ref[...].astype(BF16)
    v_t = [vp_ref[...].T.astype(BF16)]
    k_b = [kp_ref[...].astype(BF16)]
    for s in range(n_qb):
        v_t.append(vo_ref[s * BLK:(s + 1) * BLK, :].T.astype(BF16))
        k_b.append(ko_ref[s * BLK:(s + 1) * BLK, :].astype(BF16))
    for s in range(n_qb):
        bias_ref = bias0_ref if s == 0 else bias1_ref
        qcols = slice(s * BLK, (s + 1) * BLK)
        q = (q_ref[qcols, :] * np.float32(HEAD_DIM ** -0.5)).astype(BF16)
        for j in range(ATTN_KV_HEADS):
            ks = slice(j * HEAD_DIM, (j + 1) * HEAD_DIM)
            heads = [j * ATTN_GROUP + g for g in range(ATTN_GROUP)]
            q4 = jnp.concatenate([q[:, h * HEAD_DIM:(h + 1) * HEAD_DIM] for h in heads], axis=0)
            lp = _dot_nt(k_b[s][:, ks], q4)
            lo = _dot_nt(k_b[s + 1][:, ks], q4)
            lm = _dot_nt(km[:, ks], q4)
            e_prev, e_own, e_meta, inv = [], [], [], []
            for g, h in enumerate(heads):
                cols = slice(g * BLK, (g + 1) * BLK)
                band = jnp.where(upper, lp[:, cols], lo[:, cols]) + bias_ref[0, h, 0:BLK, :]
                meta = lm[:, cols] + bias_ref[0, h, BLK:N_BAND, :]
                sink = sink_ref[h]
                mx = jnp.maximum(jnp.maximum(jnp.max(band, axis=0, keepdims=True),
                                             jnp.max(meta, axis=0, keepdims=True)), sink)
                eb = jnp.exp(band - mx)
                em = jnp.exp(meta - mx)
                denom = (jnp.sum(eb, axis=0, keepdims=True) + jnp.sum(em, axis=0, keepdims=True)
                         + jnp.exp(sink - mx))
                inv.append(1.0 / denom)
                e_prev.append(jnp.where(upper, eb, 0.0).astype(BF16))
                e_own.append(jnp.where(upper, 0.0, eb).astype(BF16))
                e_meta.append(em.astype(BF16))
            cat = lambda parts: jnp.concatenate(parts, axis=1)
            ot = (_dot(v_t[s][ks, :], cat(e_prev)) + _dot(v_t[s + 1][ks, :], cat(e_own))
                  + _dot(vmt[ks, :], cat(e_meta))) * cat(inv)
            for g, h in enumerate(heads):
                yt_ref[h * HEAD_DIM:(h + 1) * HEAD_DIM, qcols] = ot[:, g * BLK:(g + 1) * BLK]
    yt = yt_ref[...]
    ms = jnp.mean(yt * yt, axis=0, keepdims=True)
    o_ref[...] = ((yt * lax.rsqrt(ms + EPS)).T * nw_ref[...]).astype(o_ref.dtype)


def _attention(proj, proj_meta, sinks, bias, norm_w, batch, nb, ymix_dtype):
    rows = batch * nb * BLK
    kcol, vcol = COL_K // D_KV, COL_V // D_KV
    meta_blk = (BLK - N_META) // N_META
    vm_t = proj_meta[BLK - N_META:, COL_V:COL_V + D_KV].T

    n_qb = max(d for d in range(1, ATTN_QB + 1) if nb % d == 0)
    ns = nb // n_qb
    wide = n_qb * BLK

    def cur(col):
        return lambda b, n: (b * ns + n, col)

    def prev(col):
        return lambda b, n: (jnp.maximum((b * ns + n) * n_qb - 1, 0), col)

    bias_spec = lambda pick: pl.BlockSpec((1, ATTN_HEADS, N_BAND, BLK), lambda b, n: (pick(n), 0, 0, 0))
    return pl.pallas_call(
        functools.partial(_attn_kernel, n_qb),
        grid=(batch, ns),
        in_specs=[
            pl.BlockSpec(memory_space=pltpu.SMEM),
            pl.BlockSpec((wide, D_ATTN), cur(COL_Q // D_ATTN)),
            pl.BlockSpec((BLK, D_KV), prev(kcol)),
            pl.BlockSpec((wide, D_KV), cur(kcol)),
            pl.BlockSpec((N_META, D_KV), lambda b, n: (meta_blk, kcol)),
            pl.BlockSpec((BLK, D_KV), prev(vcol)),
            pl.BlockSpec((wide, D_KV), cur(vcol)),
            pl.BlockSpec((D_KV, N_META), lambda b, n: (0, 0)),
            bias_spec(lambda n: jnp.minimum(n, 1)),
            bias_spec(lambda n: 1),
            pl.BlockSpec((1, D_ATTN), lambda b, n: (0, 0)),
        ],
        out_specs=pl.BlockSpec((wide, D_ATTN), cur(0)),
        out_shape=jax.ShapeDtypeStruct((rows, D_ATTN), ymix_dtype),
        scratch_shapes=[pltpu.VMEM((D_ATTN, wide), F32)],
        compiler_params=_params("parallel", "arbitrary"),
        name="swa_attention",
    )(sinks, proj, proj, proj, proj_meta, proj, proj, vm_t, bias, bias, norm_w)


def _ssd_chunk(xs_raw, bc_raw, tail_xs, tail_bc, dt_raw, cw, cb, dtb, alog, row_mask, state):
    def conv(blk, tail, w, b):
        ext = jnp.concatenate([tail, blk], axis=0)
        acc = b + w[CONV_WIDTH - 1:CONV_WIDTH, :] * blk
        for back in range(1, CONV_WIDTH):
            shifted = pltpu.roll(ext, back, 0)[SUBLANES:, :]
            acc = acc + w[CONV_WIDTH - 1 - back:CONV_WIDTH - back, :] * shifted
        return _silu(acc)

    xs = conv(xs_raw, tail_xs, cw[:, :SSM_D_INNER], cb[:, :SSM_D_INNER])
    bc = conv(bc_raw, tail_bc, cw[:, SSM_D_INNER:], cb[:, SSM_D_INNER:])
    dt = _softplus(dt_raw + dtb)
    if row_mask is not None:
        xs = jnp.where(row_mask, xs, 0.0)
        bc = jnp.where(row_mask, bc, 0.0)
        dt = jnp.where(row_mask, dt, 0.0)
    a_neg = -jnp.exp(alog)
    d_a = dt * a_neg

    r = lax.broadcasted_iota(jnp.int32, (BLK, BLK), 0)
    c = lax.broadcasted_iota(jnp.int32, (BLK, BLK), 1)
    tri = r >= c
    cs = _dot01_left(tri.astype(BF16), d_a)
    cs_t = cs.T
    hh = lax.broadcasted_iota(jnp.int32, (BLK, SSM_D_INNER), 0)
    cc = lax.broadcasted_iota(jnp.int32, (BLK, SSM_D_INNER), 1)
    expand = (cc // SSM_HEAD_DIM == hh).astype(BF16)
    dt_rep = _dot01_right(dt, expand)
    ecs_rep = _dot01_right(jnp.exp(cs), expand)
    dec_rep = _dot01_right(jnp.exp(cs[BLK - 1:BLK, :] - cs), expand)

    xdt = xs * dt_rep
    xdtd = (xdt * dec_rep).astype(BF16)
    xdt_b = xdt.astype(BF16)
    chunk_decay = ecs_rep[BLK - 1:BLK, :]

    y_parts, new_state = [], []
    for g in range(SSM_GROUPS):
        b_g = bc[:, g * SSM_STATE:(g + 1) * SSM_STATE]
        c_g = bc[:, (SSM_GROUPS + g) * SSM_STATE:(SSM_GROUPS + g + 1) * SSM_STATE]
        cols = slice(g * D_GROUP, (g + 1) * D_GROUP)
        cb_g = _dot_nt(c_g.astype(BF16), b_g.astype(BF16))
        y_off = _dot(c_g.astype(BF16), state[g].astype(BF16)) * ecs_rep[:, cols]
        new_state.append(chunk_decay[:, cols] * state[g] + _dot(b_g.T.astype(BF16), xdtd[:, cols]))
        y_diag = []
        for hp in range(SSM_HPG):
            h = g * SSM_HPG + hp
            seg = cs[:, h:h + 1] - cs_t[h:h + 1, :]
            lmat = jnp.exp(jnp.where(tri, seg, -jnp.inf))
            m = (cb_g * lmat).astype(BF16)
            y_diag.append(_dot(m, xdt_b[:, h * SSM_HEAD_DIM:(h + 1) * SSM_HEAD_DIM]))
        y_parts.append(jnp.concatenate(y_diag, axis=1) + y_off)
    return jnp.concatenate(y_parts, axis=1), xs, new_state


def _halves(lo_ref, hi_ref):
    return jnp.concatenate([lo_ref[...], hi_ref[...]], axis=1)


def _ssd_meta_kernel(xs0_ref, xs1_ref, bc_ref, dt_ref, cw_ref, cb_ref, dtb_ref, alog_ref, st_ref):
    rows = lax.broadcasted_iota(jnp.int32, (BLK, 1), 0)
    zero_state = [jnp.zeros((SSM_STATE, D_GROUP), F32) for _ in range(SSM_GROUPS)]
    _, _, st = _ssd_chunk(_halves(xs0_ref, xs1_ref), bc_ref[...], jnp.zeros((SUBLANES, SSM_D_INNER), F32),
                          jnp.zeros((SUBLANES, D_BC), F32), dt_ref[...], cw_ref[...], cb_ref[...],
                          dtb_ref[...], alog_ref[...], rows >= BLK - N_META, zero_state)
    for g in range(SSM_GROUPS):
        st_ref[g] = st[g]


SSD_ROWS = 4


def _ssd_kernel(xs0_ref, xs1_ref, bc_ref, z0_ref, z1_ref, dt_ref, txs0_ref, txs1_ref, tbc_ref,
                mxs0_ref, mxs1_ref, mbc_ref, st0_ref,
                cw_ref, cb_ref, dtb_ref, alog_ref, dsk_ref, nw_ref, o_ref, st_ref):
    first = pl.program_id(1) == 0
    n_rows = o_ref.shape[0]

    @pl.when(first)
    def _():
        for r in range(n_rows):
            st_ref[r] = st0_ref[...]

    halves = lambda lo, hi, r: jnp.concatenate([lo[r], hi[r]], axis=1)
    for r in range(n_rows):
        tail_xs = jnp.where(first, _halves(mxs0_ref, mxs1_ref), halves(txs0_ref, txs1_ref, r))
        tail_bc = jnp.where(first, mbc_ref[...], tbc_ref[r])
        state = [st_ref[r, g] for g in range(SSM_GROUPS)]
        y, xs, new_state = _ssd_chunk(halves(xs0_ref, xs1_ref, r), bc_ref[r], tail_xs, tail_bc, dt_ref[r],
                                      cw_ref[...], cb_ref[...], dtb_ref[...], alog_ref[...], None, state)
        for g in range(SSM_GROUPS):
            st_ref[r, g] = new_state[g]
        y = y + xs * dsk_ref[...]
        yg = y * _silu(halves(z0_ref, z1_ref, r))
        outs = []
        for g in range(SSM_GROUPS):
            part = yg[:, g * D_GROUP:(g + 1) * D_GROUP]
            ms = jnp.mean(part * part, axis=-1, keepdims=True)
            outs.append(part * lax.rsqrt(ms + EPS))
        o_ref[r] = (jnp.concatenate(outs, axis=1) * nw_ref[...]).astype(o_ref.dtype)


def _pad_lanes(v, n=BLK):
    v = v.reshape(1, -1)
    return jnp.pad(v, ((0, 0), (0, n - v.shape[1])))


def _ssd(proj, proj_meta, conv_w, conv_b, dt_bias, a_log, d_skip, norm_w, batch, nc, ymix_dtype):
    rows = batch * nc * BLK
    cb = conv_b.reshape(1, D_XBC)
    dtb, alog = _pad_lanes(dt_bias), _pad_lanes(a_log)
    dsk = jnp.repeat(d_skip, SSM_HEAD_DIM).reshape(1, SSM_D_INNER)
    xs_c, z_c, bc_c, dt_c = COL_XS // HALF, COL_Z // HALF, COL_BC // D_BC, COL_DT // BLK
    full = lambda shape: pl.BlockSpec(shape, lambda *_: (0,) * len(shape))

    state0 = pl.pallas_call(
        _ssd_meta_kernel,
        grid=(1,),
        in_specs=[
            pl.BlockSpec((BLK, HALF), lambda i: (0, xs_c)),
            pl.BlockSpec((BLK, HALF), lambda i: (0, xs_c + 1)),
            pl.BlockSpec((BLK, D_BC), lambda i: (0, bc_c)),
            pl.BlockSpec((BLK, BLK), lambda i: (0, dt_c)),
            full((CONV_WIDTH, D_XBC)), full((1, D_XBC)), full((1, BLK)), full((1, BLK)),
        ],
        out_specs=full((SSM_GROUPS, SSM_STATE, D_GROUP)),
        out_shape=jax.ShapeDtypeStruct((SSM_GROUPS, SSM_STATE, D_GROUP), F32),
        compiler_params=_params("arbitrary"),
        name="ssd_meta_state",
    )(proj_meta, proj_meta, proj_meta, proj_meta, conv_w, cb, dtb, alog)

    n_rows = max(d for d in range(1, SSD_ROWS + 1) if batch % d == 0)
    proj3 = proj.reshape(batch, nc * BLK, D_PROJ)

    def cur(col, width):
        return pl.BlockSpec((n_rows, BLK, width), lambda b, c: (b, c, col))

    def tail(col, width):
        return pl.BlockSpec((n_rows, SUBLANES, width),
                            lambda b, c: (b, jnp.maximum(c * (BLK // SUBLANES) - 1, 0), col))

    def meta_tail(col, width):
        return pl.BlockSpec((SUBLANES, width), lambda b, c: (BLK // SUBLANES - 1, col))

    out = pl.pallas_call(
        _ssd_kernel,
        grid=(batch // n_rows, nc),
        in_specs=[
            cur(xs_c, HALF), cur(xs_c + 1, HALF), cur(bc_c, D_BC), cur(z_c, HALF), cur(z_c + 1, HALF),
            cur(dt_c, BLK),
            tail(xs_c, HALF), tail(xs_c + 1, HALF), tail(bc_c, D_BC),
            meta_tail(xs_c, HALF), meta_tail(xs_c + 1, HALF), meta_tail(bc_c, D_BC),
            full((SSM_GROUPS, SSM_STATE, D_GROUP)),
            full((CONV_WIDTH, D_XBC)), full((1, D_XBC)), full((1, BLK)), full((1, BLK)),
            full((1, SSM_D_INNER)), full((1, SSM_D_INNER)),
        ],
        out_specs=pl.BlockSpec((n_rows, BLK, SSM_D_INNER), lambda b, c: (b, c, 0)),
        out_shape=jax.ShapeDtypeStruct((batch, nc * BLK, SSM_D_INNER), ymix_dtype),
        scratch_shapes=[pltpu.VMEM((n_rows, SSM_GROUPS, SSM_STATE, D_GROUP), F32)],
        compiler_params=_params("parallel", "arbitrary"),
        name="ssd_mixer",
    )(proj3, proj3, proj3, proj3, proj3, proj3, proj3, proj3, proj3, proj_meta, proj_meta, proj_meta,
      state0, conv_w, cb, dtb, alog, dsk, norm_w)
    return out.reshape(rows, SSM_D_INNER)


def _outproj_kernel(ya_ref, ys_ref, x_ref, w_ref, g_ref, h_ref, xn_ref):
    y = jnp.concatenate([ya_ref[...], ys_ref[...]], axis=1).astype(BF16)
    h = x_ref[...] + _dot(y, w_ref[...].astype(BF16))
    h_ref[...] = h
    ms = jnp.mean(h * h, axis=-1, keepdims=True)
    xn_ref[...] = (h * lax.rsqrt(ms + EPS) * g_ref[...]).astype(BF16)


def _outproj(ya, ys, x2d, w_out, gain):
    m = x2d.shape[0]
    tm = min(m, ROW_TILE)
    return pl.pallas_call(
        _outproj_kernel,
        grid=(m // tm,),
        in_specs=[
            pl.BlockSpec((tm, D_ATTN), lambda i: (i, 0)),
            pl.BlockSpec((tm, SSM_D_INNER), lambda i: (i, 0)),
            pl.BlockSpec((tm, D_MODEL), lambda i: (i, 0)),
            pl.BlockSpec((D_MODEL, D_MODEL), lambda i: (0, 0), pipeline_mode=pl.Buffered(1)),
            pl.BlockSpec((1, D_MODEL), lambda i: (0, 0)),
        ],
        out_specs=[pl.BlockSpec((tm, D_MODEL), lambda i: (i, 0)),
                   pl.BlockSpec((tm, D_MODEL), lambda i: (i, 0))],
        out_shape=[jax.ShapeDtypeStruct((m, D_MODEL), F32),
                   jax.ShapeDtypeStruct((m, D_MODEL), BF16)],
        compiler_params=_params("parallel"),
        name="outproj",
    )(ya, ys, x2d, w_out, gain)


def _oddeven_sort_pairs(n):
    pairs = []
    p = 1
    while p < n:
        k = p
        while k >= 1:
            for j in range(k % p, n - k, 2 * k):
                for i in range(min(k, n - j - k)):
                    if (i + j) // (2 * p) == (i + j + k) // (2 * p):
                        pairs.append((i + j, i + j + k))
            k //= 2
        p *= 2
    return pairs


_SORT16 = _oddeven_sort_pairs(PEER_TOPK)


def _top16_tile(x):
    x = list(x)
    for i, j in _SORT16:
        x[i], x[j] = jnp.maximum(x[i], x[j]), jnp.minimum(x[i], x[j])
    for shift in (4, 2, 1):
        other = [pltpu.roll(v, shift, 0) for v in x]
        x = [jnp.maximum(x[r], other[PEER_TOPK - 1 - r]) for r in range(PEER_TOPK)]
        dist = PEER_TOPK // 2
        while dist >= 1:
            for i in range(PEER_TOPK):
                if i & dist == 0:
                    x[i], x[i + dist] = jnp.maximum(x[i], x[i + dist]), jnp.minimum(x[i], x[i + dist])
            dist //= 2
    return x


def _rank_in_top16(x, best):
    gt = lambda v: v > x
    pick = jnp.where
    b3 = gt(best[7])
    b2 = gt(pick(b3, best[11], best[3]))
    b1 = gt(pick(b3, pick(b2, best[13], best[9]), pick(b2, best[5], best[1])))
    b0 = gt(pick(b3, pick(b2, pick(b1, best[14], best[12]), pick(b1, best[10], best[8])),
                 pick(b2, pick(b1, best[6], best[4]), pick(b1, best[2], best[0]))))
    rank = pick(b3, 8.0, 0.0) + pick(b2, 4.0, 0.0) + pick(b1, 2.0, 0.0) + pick(b0, 1.0, 0.0)
    return pick(gt(best[PEER_TOPK - 1]), float(PEER_TOPK), rank)


def _route_kernel(xn_ref, wq_ref, keys_ref, n_ref, e1_ref, r2_ref, e2_ref, v_ref):
    q = _dot(xn_ref[...], wq_ref[...].astype(BF16)).astype(BF16)
    n_tiles = xn_ref.shape[0] // BLK
    for h in range(PEER_HEADS):
        s, tops = [], []
        for c in range(2):
            qs = q[:, (2 * h + c) * PEER_HALF:(2 * h + c + 1) * PEER_HALF]
            sc = _dot_nt(keys_ref[h, c], qs)
            s.append(sc)
            for lt in range(n_tiles):
                lanes = slice(lt * BLK, (lt + 1) * BLK)
                keys8 = [sc[SUBLANES * r:SUBLANES * (r + 1), lanes] for r in range(N_KEYS // SUBLANES)]
                best = _top16_tile(keys8)
                for r in range(PEER_TOPK):
                    v_ref[c, r:r + 1, lanes] = best[r][0:1, :]
                if c == 1:
                    ranks = [_rank_in_top16(x, best) for x in keys8]
                    r2 = jnp.concatenate(ranks, axis=0).astype(BF16)
                    e2 = jnp.exp(sc[:, lanes] - best[0][0:1, :]).astype(BF16)
                    r2_ref[h, lt] = pltpu.bitcast(r2, jnp.uint32)
                    e2_ref[h, lt] = pltpu.bitcast(e2, jnp.uint32)
            tops.append(v_ref[c])
        v1, v2 = tops
        blocks = [v1[0:1] + v2, v1[1:2] + v2[0:8]]
        blocks += [v1[a:a + 1] + v2[0:8] for a in range(2, 8)]
        blocks.append(v1[8:16] + v2[0:1])
        cand = jnp.concatenate(blocks, axis=0)
        top = v1[0:1] + v2[0:1]
        rem = cand
        for _ in range(PEER_TOPK - 1):
            m = jnp.max(rem, axis=0, keepdims=True)
            rem = jnp.where(rem == m, -jnp.inf, rem)
        tau = jnp.max(rem, axis=0, keepdims=True)
        z = jnp.sum(jnp.where(cand >= tau, jnp.exp(cand - top), 0.0), axis=0, keepdims=True)
        cnt = [jnp.sum(jnp.where(blocks[a] >= tau, 1.0, 0.0), axis=0, keepdims=True) for a in range(8)]
        cnt_hi = jnp.where(blocks[8] >= tau, 1.0, 0.0)
        n_sel = jnp.zeros_like(s[0])
        for a in range(PEER_TOPK):
            n_a = cnt[a] if a < 8 else cnt_hi[a - 8:a - 7]
            n_sel = jnp.where(s[0] == v1[a:a + 1], n_a, n_sel)
        n_ref[h] = n_sel
        e1_ref[h] = jnp.exp(s[0] - v1[0:1]) / z


def _route(xn, wq, keys):
    t = xn.shape[0]
    tm = min(t, 256)
    big = pl.BlockSpec((PEER_HEADS, N_KEYS, tm), lambda i: (0, 0, i))
    big_shape = jax.ShapeDtypeStruct((PEER_HEADS, N_KEYS, t), F32)
    tiled = pl.BlockSpec((PEER_HEADS, tm // BLK, N_KEYS // 2, BLK), lambda i: (0, i, 0, 0))
    tiled_shape = jax.ShapeDtypeStruct((PEER_HEADS, t // BLK, N_KEYS // 2, BLK), jnp.uint32)
    return pl.pallas_call(
        _route_kernel,
        grid=(t // tm,),
        in_specs=[
            pl.BlockSpec((tm, D_MODEL), lambda i: (i, 0)),
            pl.BlockSpec((D_MODEL, D_MODEL), lambda i: (0, 0), pipeline_mode=pl.Buffered(1)),
            pl.BlockSpec((PEER_HEADS, 2, N_KEYS, PEER_HALF), lambda i: (0, 0, 0, 0)),
        ],
        out_specs=[big, big, tiled, tiled],
        out_shape=[big_shape, big_shape, tiled_shape, tiled_shape],
        scratch_shapes=[pltpu.VMEM((2, PEER_TOPK, tm), F32)],
        compiler_params=_params("parallel"),
        name="peer_route",
    )(xn, wq, keys)


PEER_TB = 1024
PEER_EB = 1024
PEER_JCH = 128


def _peer_kernel(xn_ref, u_ref, n_ref, e1_ref, r2_ref, e2_ref, w_ref, a_ref):
    a_ref[...] = _gelu_exact(_dot_nt(u_ref[...].astype(BF16), xn_ref[...])).astype(BF16)
    tb = xn_ref.shape[0]

    def per_key(ii, carry):
        row0 = pl.multiple_of(ii * N_KEYS, N_KEYS)
        n_rows = [n_ref[h, pl.ds(ii, 1), :] for h in range(PEER_HEADS)]
        e1_rows = [e1_ref[h, pl.ds(ii, 1), :] for h in range(PEER_HEADS)]
        for lt in range(tb // BLK):
            lanes = slice(lt * BLK, (lt + 1) * BLK)
            wide = lambda row: jnp.broadcast_to(row[:, lanes], (PEER_JCH, BLK)).astype(BF16)
            n_b = [wide(r) for r in n_rows]
            e1_b = [wide(r) for r in e1_rows]
            for j0 in range(0, N_KEYS, PEER_JCH):
                js = slice(j0 // 2, (j0 + PEER_JCH) // 2)
                acc = None
                for h in range(PEER_HEADS):
                    keep = pltpu.bitcast(r2_ref[h, lt, js, :], BF16) < n_b[h]
                    gate = jnp.where(keep, pltpu.bitcast(e2_ref[h, lt, js, :], BF16), 0.0) * e1_b[h]
                    acc = gate if acc is None else acc + gate
                rows = pl.ds(row0 + j0, PEER_JCH)
                w_ref[rows, lanes] = acc * a_ref[rows, lanes]
        return carry

    lax.fori_loop(0, PEER_EB // N_KEYS, per_key, 0)


PEER_DOWN_TB = 1024


def _peer_down_kernel(vt_ref, w_ref, o_ref):
    @pl.when(pl.program_id(1) == 0)
    def _():
        o_ref[...] = jnp.zeros_like(o_ref)

    o_ref[...] += _dot(vt_ref[...], w_ref[...])


def _transpose_bf16_kernel(x_ref, o_ref):
    o_ref[...] = x_ref[...].T.astype(BF16)


def _transpose_bf16(x, rows=ROW_TILE):
    r, c = x.shape
    return pl.pallas_call(
        _transpose_bf16_kernel,
        grid=(r // rows,),
        in_specs=[pl.BlockSpec((rows, c), lambda i: (i, 0))],
        out_specs=pl.BlockSpec((c, rows), lambda i: (0, i)),
        out_shape=jax.ShapeDtypeStruct((c, r), BF16),
        compiler_params=_params("parallel"),
        name="transpose_v",
    )(x)


def _peer(xn, u, vt_b, n_sel, e1, r2, e2):
    t = xn.shape[0]
    tb = min(t, PEER_TB)
    assert tb == PEER_TB
    ni = PEER_EB // N_KEYS
    small = pl.BlockSpec((PEER_HEADS, ni, tb), lambda i, e: (0, e, i))
    big = pl.BlockSpec((PEER_HEADS, tb // BLK, N_KEYS // 2, BLK), lambda i, e: (0, i, 0, 0))
    w = pl.pallas_call(
        _peer_kernel,
        grid=(t // tb, N_EXPERTS // PEER_EB),
        in_specs=[
            pl.BlockSpec((tb, D_MODEL), lambda i, e: (i, 0)),
            pl.BlockSpec((PEER_EB, D_MODEL), lambda i, e: (e, 0)),
            small, small, big, big,
        ],
        out_specs=pl.BlockSpec((PEER_EB, tb), lambda i, e: (e, i)),
        out_shape=jax.ShapeDtypeStruct((N_EXPERTS, t), BF16),
        scratch_shapes=[pltpu.VMEM((PEER_EB, tb), BF16)],
        compiler_params=_params("parallel", "arbitrary"),
        name="peer_experts",
    )(xn, u, n_sel, e1, r2, e2)
    td = min(t, PEER_DOWN_TB)
    return pl.pallas_call(
        _peer_down_kernel,
        grid=(t // td, N_EXPERTS // PEER_EB),
        in_specs=[
            pl.BlockSpec((D_MODEL, PEER_EB), lambda i, e: (0, e)),
            pl.BlockSpec((PEER_EB, td), lambda i, e: (e, i)),
        ],
        out_specs=pl.BlockSpec((D_MODEL, td), lambda i, e: (0, i)),
        out_shape=jax.ShapeDtypeStruct((D_MODEL, t), F32),
        compiler_params=_params("parallel", "arbitrary"),
        name="peer_down",
    )(vt_b, w)


def _final_kernel(h_ref, pt_ref, g_ref, o_ref):
    h = h_ref[...] + pt_ref[...].T
    ms = jnp.mean(h * h, axis=-1, keepdims=True)
    o_ref[...] = h * lax.rsqrt(ms + EPS) * g_ref[...]


def _final(h1, peer_t, gain):
    t = h1.shape[0]
    tm = min(t, ROW_TILE)
    rows = pl.BlockSpec((tm, D_MODEL), lambda i: (i, 0))
    return pl.pallas_call(
        _final_kernel,
        grid=(t // tm,),
        in_specs=[rows, pl.BlockSpec((D_MODEL, tm), lambda i: (0, i)),
                  pl.BlockSpec((1, D_MODEL), lambda i: (0, 0))],
        out_specs=rows,
        out_shape=jax.ShapeDtypeStruct((t, D_MODEL), F32),
        compiler_params=_params("parallel"),
        name="final_norm",
    )(h1, peer_t, gain)


def _mixer(x2d, batch, seq, meta_tokens, rel_bias, ln_mix, w_in, sinks, conv_w, conv_b, dt_bias,
           a_log, d_skip, attn_norm_w, ssm_norm_w):
    nb = seq // BLK
    gain = ln_mix.reshape(1, D_MODEL)
    meta_pad = jnp.concatenate([jnp.zeros((BLK - N_META, D_MODEL), F32), meta_tokens.astype(F32)], axis=0)
    w_t = w_in.T.astype(BF16)
    proj = _inproj(x2d, gain, w_t)
    proj_meta = _inproj(meta_pad, gain, w_t)

    bucket, valid = _band_tables(nb)
    tab = rel_bias.astype(F32)
    bias = jnp.full((2, ATTN_HEADS) + bucket.shape[1:], NEG, F32)
    for b in range(REL_BUCKETS):
        bias = jnp.where((valid & (bucket == b))[:, None], tab[b][None, :, None, None], bias)
    ya = _attention(proj, proj_meta, sinks.astype(F32), bias, attn_norm_w.reshape(1, D_ATTN),
                    batch, nb, BF16)
    ys = _ssd(proj, proj_meta, conv_w, conv_b, dt_bias, a_log, d_skip,
              ssm_norm_w.reshape(1, SSM_D_INNER), batch, nb, BF16)
    return ya, ys


def kernel(x, meta_tokens, rel_bias, ln_mix, w_in, attn_sinks, conv_w, conv_b, dt_bias, a_log, d_skip,
           attn_norm_w, ssm_norm_w, w_out, ln_ffn, peer_wq, peer_keys, peer_u, peer_v, ln_final):
    batch, seq, _ = x.shape
    x2d = x.reshape(batch * seq, D_MODEL)
    ya, ys = _mixer(x2d, batch, seq, meta_tokens, rel_bias, ln_mix[0], w_in[0], attn_sinks[0],
                         conv_w[0], conv_b[0], dt_bias[0], a_log[0], d_skip[0], attn_norm_w[0],
                         ssm_norm_w[0])
    h1, xn = _outproj(ya, ys, x2d, w_out[0], ln_ffn[0].reshape(1, D_MODEL))
    n_sel, e1, r2, e2 = _route(xn, peer_wq[0], peer_keys[0].astype(BF16))
    peer_t = _peer(xn, peer_u[0], _transpose_bf16(peer_v[0]), n_sel, e1, r2, e2)
    out = _final(h1, peer_t, ln_final.reshape(1, D_MODEL))
    return out.reshape(batch, seq, D_MODEL)
```
